```python
import math
import jax
import jax.numpy as jnp
from jax import lax
import numpy as np

D_MODEL = 1024
BATCH = 16
SEQ = 2048
DEPTH = 2
DEC_BATCH = 2
DEC_SEQ = 16384
PAST_LEN = 128

BRANCH_W = 512
N_BRANCH = 3
HY_W = BRANCH_W
HY_ORDER = 2
HY_SHORT = 3
HY_EMB = 33
HY_FILTER_HIDDEN = 64
HY_FAST_PCT = 0.3
HY_SLOW_PCT = 1.5
HY_TARGET = 1e-2
MLA_HEADS = 4
MLA_NOPE = 128
MLA_ROPE = 64
MLA_V = 128
MLA_Q_LORA = 256
MLA_KV_LORA = 128
ROPE_THETA = 10000.0
Q_BLOCK = 128
GDN_HEADS = 4
GDN_DK = 128
GDN_DV = 128
GDN_CONV = 3
GDN_CHUNK = 64
D_FF = -(-8 * D_MODEL // (3 * 256)) * 256
NORM_EPS = 1e-6

IN_SIZES = (
    (HY_ORDER + 1) * HY_W,
    MLA_Q_LORA,
    MLA_KV_LORA + MLA_ROPE,
    GDN_HEADS * (2 * GDN_DK + GDN_DV),
    GDN_HEADS * GDN_DV,
    2 * GDN_HEADS,
    2 * GDN_HEADS,
    N_BRANCH * D_MODEL,
)
D_IN = sum(IN_SIZES)

kernel_name = "hyena_mla_gdn_parallel_encoder"


def rms_norm(x, g):
    xf = x.astype(jnp.float32)
    y = xf * lax.rsqrt(jnp.mean(xf * xf, axis=-1, keepdims=True) + NORM_EPS)
    return (y * g.astype(jnp.float32)).astype(x.dtype)


def centred_dwconv(x, w):
    k = w.shape[0]
    p = k // 2
    L = x.shape[1]
    xp = jnp.pad(x, ((0, 0), (p, p), (0, 0)))
    out = xp[:, 0:L] * w[0]
    for i in range(1, k):
        out = out + xp[:, i:i + L] * w[i]
    return out


def hyena_filter_bank(L, w1, b1, freq, w2, b2, w3):
    f32 = jnp.float32
    t = jnp.linspace(0.0, 1.0, L, dtype=f32)[:, None]
    bands = (HY_EMB - 1) // 2
    wpos = (2.0 * math.pi / L) * jnp.arange(L, dtype=f32)[:, None]
    fr = jnp.linspace(1e-4, bands - 1, bands, dtype=f32)[None, :]
    feats = jnp.concatenate([t, jnp.cos(fr * wpos), -jnp.sin(fr * wpos)], axis=-1)
    freq = freq.astype(f32)
    h = jnp.sin(freq * (feats @ w1.astype(f32) + b1.astype(f32)))
    h = jnp.sin(freq * (h @ w2.astype(f32) + b2.astype(f32)))
    h = (h @ w3.astype(f32)).reshape(L, HY_ORDER, 2, HY_W)
    deltas = jnp.abs(jnp.linspace(math.log(HY_TARGET) / HY_SLOW_PCT,
                                  math.log(HY_TARGET) / HY_FAST_PCT, HY_W, dtype=f32))
    h = h * jnp.exp(-t * deltas)[:, None, None, :]
    fwd = h[:, :, 0]
    bwd = h[1:, :, 1][::-1]
    kern = jnp.concatenate([fwd, jnp.zeros((1, HY_ORDER, HY_W), f32), bwd], axis=0)
    kern = kern / jnp.sum(jnp.abs(kern), axis=0, keepdims=True)
    return jnp.fft.rfft(kern, axis=0)


def hyena_mixer(u, conv_w, conv_b, w1, b1, freq, w2, b2, w3, skip):
    L = u.shape[1]
    u = (centred_dwconv(u, conv_w) + conv_b).astype(jnp.float32)
    x1, x2, v = jnp.split(u, 3, axis=-1)
    kf = hyena_filter_bank(L, w1, b1, freq, w2, b2, w3)
    z = v
    for o, gate in enumerate((x1, x2)):
        zf = jnp.fft.rfft(z, n=2 * L, axis=1)
        conv = jnp.fft.irfft(zf * kf[None, :, o], n=2 * L, axis=1)[:, :L]
        z = gate * (conv + skip[o].astype(jnp.float32) * z)
    return z


def rope_tables(L):
    half = MLA_ROPE // 2
    inv = ROPE_THETA ** (-jnp.arange(half, dtype=jnp.float32) / half)
    ang = jnp.arange(L, dtype=jnp.float32)[:, None] * inv[None, :]
    return jnp.cos(ang), jnp.sin(ang)


def apply_rope(x, cos, sin):
    x1, x2 = jnp.split(x, 2, axis=-1)
    return jnp.concatenate([x1 * cos - x2 * sin, x2 * cos + x1 * sin], axis=-1).astype(x.dtype)


def mla_mixer(q_lat, kv_lat, q_norm, wq_b, kv_norm, wkv_b):
    B, L, _ = q_lat.shape
    q = (rms_norm(q_lat, q_norm) @ wq_b).reshape(B, L, MLA_HEADS, MLA_NOPE + MLA_ROPE)
    q_nope, q_pe = q[..., :MLA_NOPE], q[..., MLA_NOPE:]
    c_kv, k_pe = kv_lat[..., :MLA_KV_LORA], kv_lat[..., MLA_KV_LORA:]
    kv = (rms_norm(c_kv, kv_norm) @ wkv_b).reshape(B, L, MLA_HEADS, MLA_NOPE + MLA_V)
    k_nope, v = kv[..., :MLA_NOPE], kv[..., MLA_NOPE:]
    cos, sin = rope_tables(L)
    q_pe = apply_rope(q_pe, cos[:, None, :], sin[:, None, :])
    k_pe = apply_rope(k_pe, cos, sin)
    scale = (MLA_NOPE + MLA_ROPE) ** -0.5
    nblk = L // Q_BLOCK
    qn_b = jnp.moveaxis(q_nope.reshape(B, nblk, Q_BLOCK, MLA_HEADS, MLA_NOPE), 1, 0)
    qp_b = jnp.moveaxis(q_pe.reshape(B, nblk, Q_BLOCK, MLA_HEADS, MLA_ROPE), 1, 0)

    def attend(args):
        qn, qp = args
        s = (jnp.einsum('bqhd,bkhd->bhqk', qn, k_nope)
             + jnp.einsum('bqhr,bkr->bhqk', qp, k_pe))
        p = jax.nn.softmax(s.astype(jnp.float32) * scale, axis=-1).astype(v.dtype)
        return jnp.einsum('bhqk,bkhd->bqhd', p, v)

    o = lax.map(attend, (qn_b, qp_b))
    return jnp.moveaxis(o, 0, 1).reshape(B, L, MLA_HEADS * MLA_V)


def l2_normalise(x):
    return x * lax.rsqrt(jnp.sum(x * x, axis=-1, keepdims=True) + NORM_EPS)


def chunk_gated_delta(q, k, v, g, beta):
    B, L, H, DK = q.shape
    DV = v.shape[-1]
    C = GDN_CHUNK
    N = L // C

    def chunks(t):
        return jnp.swapaxes(t.reshape(B, N, C, H, t.shape[-1]), 2, 3)

    q, k, v = chunks(q), chunks(k), chunks(v)
    g = jnp.swapaxes(g.reshape(B, N, C, H), 2, 3)
    beta = jnp.swapaxes(beta.reshape(B, N, C, H), 2, 3)
    gc = jnp.cumsum(g, axis=-1)
    idx = jnp.arange(C)
    causal = idx[:, None] >= idx[None, :]
    strict = idx[:, None] > idx[None, :]
    decay_mask = jnp.exp(jnp.where(causal, gc[..., :, None] - gc[..., None, :], -jnp.inf))
    kb = k * beta[..., None]
    vb = v * beta[..., None]
    a_strict = jnp.where(strict, jnp.einsum('bnhid,bnhjd->bnhij', kb, k) * decay_mask, 0.0)
    m = a_strict + jnp.eye(C, dtype=q.dtype)
    rhs = jnp.concatenate([vb, kb * jnp.exp(gc)[..., None]], axis=-1)
    sol = lax.linalg.triangular_solve(m, rhs, left_side=True, lower=True, unit_diagonal=True)
    u, w = sol[..., :DV], sol[..., DV:]
    attn_qk = jnp.einsum('bnhid,bnhjd->bnhij', q, k) * decay_mask
    q_dec = q * jnp.exp(gc)[..., None]
    k_dec = k * jnp.exp(gc[..., -1:] - gc)[..., None]
    chunk_decay = jnp.exp(gc[..., -1])

    def step(S, xs):
        w_c, u_c, qd, kd, aqk, dec = xs
        v_new = u_c - jnp.einsum('bhcd,bhde->bhce', w_c, S)
        o = jnp.einsum('bhcd,bhde->bhce', qd, S) + jnp.einsum('bhij,bhje->bhie', aqk, v_new)
        S = S * dec[..., None, None] + jnp.einsum('bhcd,bhce->bhde', kd, v_new)
        return S, o

    xs = tuple(jnp.moveaxis(t, 1, 0) for t in (w, u, q_dec, k_dec, attn_qk, chunk_decay))
    S0 = jnp.zeros((B, H, DK, DV), q.dtype)
    _, o = lax.scan(step, S0, xs)
    return jnp.transpose(o, (1, 0, 3, 2, 4)).reshape(B, L, H, DV)


def gdn_mixer(qkv, z, b_raw, a_raw, conv_w, a_log, dt_bias, out_norm):
    B, L, _ = qkv.shape
    f32 = jnp.float32
    qkv = jax.nn.silu(centred_dwconv(qkv, conv_w).astype(f32))
    nqk = GDN_HEADS * GDN_DK
    q = l2_normalise(qkv[..., :nqk].reshape(B, L, GDN_HEADS, GDN_DK)) * (GDN_DK ** -0.5)
    k = l2_normalise(qkv[..., nqk:2 * nqk].reshape(B, L, GDN_HEADS, GDN_DK))
    v = qkv[..., 2 * nqk:].reshape(B, L, GDN_HEADS, GDN_DV)
    beta = jax.nn.sigmoid(b_raw.astype(f32)).reshape(B, L, 2, GDN_HEADS)
    g = (-jnp.exp(a_log.astype(f32))
         * jax.nn.softplus(a_raw.astype(f32).reshape(B, L, 2, GDN_HEADS) + dt_bias.astype(f32)))
    o_f = chunk_gated_delta(q, k, v, g[:, :, 0], beta[:, :, 0])
    flip = lambda t: jnp.flip(t, axis=1)
    o_b = flip(chunk_gated_delta(flip(q), flip(k), flip(v), flip(g[:, :, 1]), flip(beta[:, :, 1])))
    o = rms_norm(o_f + o_b, out_norm) * jax.nn.silu(z.astype(f32).reshape(B, L, GDN_HEADS, GDN_DV))
    return o.reshape(B, L, GDN_HEADS * GDN_DV)


def trunk_layer(x, norm_mix_pre, norm_mix_post, norm_ffn_pre, norm_ffn_post, w_in,
                hy_conv_w, hy_conv_b, hy_ffn_w1, hy_ffn_b1, hy_sin_freq, hy_ffn_w2, hy_ffn_b2,
                hy_ffn_w3, hy_skip, mla_q_norm, mla_wq_b, mla_kv_norm, mla_wkv_b,
                gdn_conv_w, gdn_a_log, gdn_dt_bias, gdn_out_norm,
                w_branch, w_out, w_gate, w_up, w_down):
    B, L, _ = x.shape
    dt = x.dtype
    h = rms_norm(x, norm_mix_pre)
    proj = h @ w_in
    points = [int(p) for p in np.cumsum(IN_SIZES)[:-1]]
    hy_in, mla_q, mla_kv, gdn_qkv, gdn_z, gdn_b, gdn_a, gate_logits = jnp.split(proj, points, axis=-1)
    o_hy = hyena_mixer(hy_in, hy_conv_w, hy_conv_b, hy_ffn_w1, hy_ffn_b1, hy_sin_freq,
                       hy_ffn_w2, hy_ffn_b2, hy_ffn_w3, hy_skip)
    o_mla = mla_mixer(mla_q, mla_kv, mla_q_norm, mla_wq_b, mla_kv_norm, mla_wkv_b)
    o_gdn = gdn_mixer(gdn_qkv, gdn_z, gdn_b, gdn_a, gdn_conv_w, gdn_a_log, gdn_dt_bias, gdn_out_norm)
    gates = jax.nn.sigmoid(gate_logits.astype(jnp.float32)).astype(dt).reshape(B, L, N_BRANCH, D_MODEL)
    branches = (o_hy, o_mla, o_gdn)
    merged = gates[:, :, 0] * (branches[0].astype(dt) @ w_branch[0])
    for i in range(1, N_BRANCH):
        merged = merged + gates[:, :, i] * (branches[i].astype(dt) @ w_branch[i])
    x = x + rms_norm(merged @ w_out, norm_mix_post)
    h = rms_norm(x, norm_ffn_pre)
    f = (jax.nn.silu(h @ w_gate) * (h @ w_up)) @ w_down
    return x + rms_norm(f, norm_ffn_post)


def setup_inputs(seed: int = 0) -> dict:
    key = jax.random.key(seed)
    ks = iter(jax.random.split(key, 40))
    f32 = jnp.float32

    def nrm(shape, scale):
        return scale * jax.random.normal(next(ks), shape, f32)

    def gain(shape):
        return 1.0 + 0.05 * jax.random.normal(next(ks), shape, f32)

    D = D_MODEL
    x_prompt = nrm((BATCH, SEQ, D), 1.0)
    x_sample = nrm((DEC_BATCH, DEC_SEQ, D), 1.0)
    a_log = jnp.log(jax.random.uniform(next(ks), (DEPTH, 2, GDN_HEADS), f32, 1.0, 16.0))
    dt0 = jnp.exp(jax.random.uniform(next(ks), (DEPTH, 2, GDN_HEADS), f32,
                                     math.log(1e-3), math.log(1e-1)))
    dt_bias = dt0 + jnp.log(-jnp.expm1(-dt0))
    return {
        "x_prompt": x_prompt,
        "x_sample": x_sample,
        "norm_mix_pre": gain((DEPTH, D)),
        "norm_mix_post": gain((DEPTH, D)),
        "norm_ffn_pre": gain((DEPTH, D)),
        "norm_ffn_post": gain((DEPTH, D)),
        "w_in": nrm((DEPTH, D, D_IN), D ** -0.5),
        "hy_conv_w": nrm((DEPTH, HY_SHORT, (HY_ORDER + 1) * HY_W), HY_SHORT ** -0.5),
        "hy_conv_b": nrm((DEPTH, (HY_ORDER + 1) * HY_W), 0.02),
        "hy_ffn_w1": nrm((DEPTH, HY_EMB, HY_FILTER_HIDDEN), HY_EMB ** -0.5),
        "hy_ffn_b1": nrm((DEPTH, HY_FILTER_HIDDEN), 0.02),
        "hy_sin_freq": gain((DEPTH, HY_FILTER_HIDDEN)),
        "hy_ffn_w2": nrm((DEPTH, HY_FILTER_HIDDEN, HY_FILTER_HIDDEN), HY_FILTER_HIDDEN ** -0.5),
        "hy_ffn_b2": nrm((DEPTH, HY_FILTER_HIDDEN), 0.02),
        "hy_ffn_w3": nrm((DEPTH, HY_FILTER_HIDDEN, HY_ORDER * 2 * HY_W), HY_FILTER_HIDDEN ** -0.5),
        "hy_skip": nrm((DEPTH, HY_ORDER, HY_W), 0.5),
        "mla_q_norm": gain((DEPTH, MLA_Q_LORA)),
        "mla_wq_b": nrm((DEPTH, MLA_Q_LORA, MLA_HEADS * (MLA_NOPE + MLA_ROPE)), MLA_Q_LORA ** -0.5),
        "mla_kv_norm": gain((DEPTH, MLA_KV_LORA)),
        "mla_wkv_b": nrm((DEPTH, MLA_KV_LORA, MLA_HEADS * (MLA_NOPE + MLA_V)), MLA_KV_LORA ** -0.5),
        "gdn_conv_w": nrm((DEPTH, GDN_CONV, GDN_HEADS * (2 * GDN_DK + GDN_DV)), GDN_CONV ** -0.5),
        "gdn_a_log": a_log,
        "gdn_dt_bias": dt_bias,
        "gdn_out_norm": gain((DEPTH, GDN_DV)),
        "w_branch": nrm((DEPTH, N_BRANCH, BRANCH_W, D), BRANCH_W ** -0.5),
        "w_out": nrm((DEPTH, D, D), D ** -0.5),
        "w_gate": nrm((DEPTH, D, D_FF), D ** -0.5),
        "w_up": nrm((DEPTH, D, D_FF), D ** -0.5),
        "w_down": nrm((DEPTH, D_FF, D), D_FF ** -0.5),
    }


def reference(x_prompt, x_sample, norm_mix_pre, norm_mix_post, norm_ffn_pre, norm_ffn_post, w_in,
              hy_conv_w, hy_conv_b, hy_ffn_w1, hy_ffn_b1, hy_sin_freq, hy_ffn_w2, hy_ffn_b2,
              hy_ffn_w3, hy_skip, mla_q_norm, mla_wq_b, mla_kv_norm, mla_wkv_b,
              gdn_conv_w, gdn_a_log, gdn_dt_bias, gdn_out_norm,
              w_branch, w_out, w_gate, w_up, w_down):
    weights = (norm_mix_pre, norm_mix_post, norm_ffn_pre, norm_ffn_post, w_in,
               hy_conv_w, hy_conv_b, hy_ffn_w1, hy_ffn_b1, hy_sin_freq, hy_ffn_w2, hy_ffn_b2,
               hy_ffn_w3, hy_skip, mla_q_norm, mla_wq_b, mla_kv_norm, mla_wkv_b,
               gdn_conv_w, gdn_a_log, gdn_dt_bias, gdn_out_norm,
               w_branch, w_out, w_gate, w_up, w_down)

    def run_trunk(x):
        for layer in range(DEPTH):
            x = trunk_layer(x, *[w[layer] for w in weights])
        return x

    y_prompt = run_trunk(x_prompt)
    y_sample = run_trunk(x_sample)
    return (y_prompt, y_sample)
```

```python
import functools
import math

import jax
import jax.numpy as jnp
from jax import lax
from jax.experimental import pallas as pl
from jax.experimental.pallas import tpu as pltpu
import numpy as np

D_MODEL = 1024
DEPTH = 2
BRANCH_W = 512
N_BRANCH = 3
HY_W = BRANCH_W
HY_ORDER = 2
HY_EMB = 33
HY_FAST_PCT = 0.3
HY_SLOW_PCT = 1.5
HY_TARGET = 1e-2
MLA_HEADS = 4
MLA_NOPE = 128
MLA_ROPE = 64
MLA_V = 128
MLA_Q_LORA = 256
MLA_KV_LORA = 128
ROPE_THETA = 10000.0
Q_BLOCK = 128
GDN_HEADS = 4
GDN_DK = 128
GDN_DV = 128
GDN_CHUNK = 64
NORM_EPS = 1e-6

IN_SIZES = (
    (HY_ORDER + 1) * HY_W,
    MLA_Q_LORA,
    MLA_KV_LORA + MLA_ROPE,
    GDN_HEADS * (2 * GDN_DK + GDN_DV),
    GDN_HEADS * GDN_DV,
    2 * GDN_HEADS,
    2 * GDN_HEADS,
    N_BRANCH * D_MODEL,
)

VMEM_LIMIT_BYTES = 56 * 1024 * 1024


def _mm_kernel(a_ref, b_ref, o_ref):
    a = a_ref[...].astype(jnp.bfloat16)
    o_ref[...] = jnp.dot(a, b_ref[...], preferred_element_type=jnp.float32)


def _mm(a, b, tm=512, tn=512):
    M, K = a.shape
    N = b.shape[1]
    b = b.astype(jnp.bfloat16)
    tn = max(t for t in range(128, min(tn, N) + 1, 128) if N % t == 0)
    tm = min(tm, M)
    assert M % tm == 0 and N % tn == 0, (M, N, tm, tn)
    return pl.pallas_call(
        _mm_kernel,
        grid=(M // tm, N // tn),
        in_specs=[pl.BlockSpec((tm, K), lambda i, j: (i, 0)),
                  pl.BlockSpec((K, tn), lambda i, j: (0, j))],
        out_specs=pl.BlockSpec((tm, tn), lambda i, j: (i, j)),
        out_shape=jax.ShapeDtypeStruct((M, N), jnp.float32),
        compiler_params=pltpu.CompilerParams(
            dimension_semantics=("parallel", "arbitrary"),
            vmem_limit_bytes=VMEM_LIMIT_BYTES),
    )(a, b)


def _mm3(x, w, **kw):
    B, L, K = x.shape
    return _mm(x.reshape(B * L, K), w, **kw).reshape(B, L, w.shape[1])


def rms_norm(x, g):
    xf = x.astype(jnp.float32)
    y = xf * lax.rsqrt(jnp.mean(xf * xf, axis=-1, keepdims=True) + NORM_EPS)
    return (y * g.astype(jnp.float32)).astype(x.dtype)


def centred_dwconv(x, w):
    k = w.shape[0]
    p = k // 2
    L = x.shape[1]
    xp = jnp.pad(x, ((0, 0), (p, p), (0, 0)))
    out = xp[:, 0:L] * w[0]
    for i in range(1, k):
        out = out + xp[:, i:i + L] * w[i]
    return out


def hyena_filter_bank(L, w1, b1, freq, w2, b2, w3):
    f32 = jnp.float32
    t = jnp.linspace(0.0, 1.0, L, dtype=f32)[:, None]
    bands = (HY_EMB - 1) // 2
    wpos = (2.0 * math.pi / L) * jnp.arange(L, dtype=f32)[:, None]
    fr = jnp.linspace(1e-4, bands - 1, bands, dtype=f32)[None, :]
    feats = jnp.concatenate([t, jnp.cos(fr * wpos), -jnp.sin(fr * wpos)], axis=-1)
    freq = freq.astype(f32)
    h = jnp.sin(freq * (feats @ w1.astype(f32) + b1.astype(f32)))
    h = jnp.sin(freq * (h @ w2.astype(f32) + b2.astype(f32)))
    h = (h @ w3.astype(f32)).reshape(L, HY_ORDER, 2, HY_W)
    deltas = jnp.abs(jnp.linspace(math.log(HY_TARGET) / HY_SLOW_PCT,
                                  math.log(HY_TARGET) / HY_FAST_PCT, HY_W, dtype=f32))
    h = h * jnp.exp(-t * deltas)[:, None, None, :]
    fwd = h[:, :, 0]
    bwd = h[1:, :, 1][::-1]
    kern = jnp.concatenate([fwd, jnp.zeros((1, HY_ORDER, HY_W), f32), bwd], axis=0)
    kern = kern / jnp.sum(jnp.abs(kern), axis=0, keepdims=True)
    return jnp.fft.rfft(kern, axis=0)


def hyena_mixer(u, conv_w, conv_b, w1, b1, freq, w2, b2, w3, skip):
    L = u.shape[1]
    u = (centred_dwconv(u, conv_w) + conv_b).astype(jnp.float32)
    x1, x2, v = jnp.split(u, 3, axis=-1)
    kf = hyena_filter_bank(L, w1, b1, freq, w2, b2, w3)
    z = v
    for o, gate in enumerate((x1, x2)):
        zf = jnp.fft.rfft(z, n=2 * L, axis=1)
        conv = jnp.fft.irfft(zf * kf[None, :, o], n=2 * L, axis=1)[:, :L]
        z = gate * (conv + skip[o].astype(jnp.float32) * z)
    return z


def rope_tables(L):
    half = MLA_ROPE // 2
    inv = ROPE_THETA ** (-jnp.arange(half, dtype=jnp.float32) / half)
    ang = jnp.arange(L, dtype=jnp.float32)[:, None] * inv[None, :]
    return jnp.cos(ang), jnp.sin(ang)


def apply_rope(x, cos, sin):
    x1, x2 = jnp.split(x, 2, axis=-1)
    return jnp.concatenate([x1 * cos - x2 * sin, x2 * cos + x1 * sin], axis=-1).astype(x.dtype)


def mla_mixer(q_lat, kv_lat, q_norm, wq_b, kv_norm, wkv_b):
    B, L, _ = q_lat.shape
    q = _mm3(rms_norm(q_lat, q_norm), wq_b).reshape(B, L, MLA_HEADS, MLA_NOPE + MLA_ROPE)
    q_nope, q_pe = q[..., :MLA_NOPE], q[..., MLA_NOPE:]
    c_kv, k_pe = kv_lat[..., :MLA_KV_LORA], kv_lat[..., MLA_KV_LORA:]
    kv = _mm3(rms_norm(c_kv, kv_norm), wkv_b).reshape(B, L, MLA_HEADS, MLA_NOPE + MLA_V)
    k_nope, v = kv[..., :MLA_NOPE], kv[..., MLA_NOPE:]
    cos, sin = rope_tables(L)
    q_pe = apply_rope(q_pe, cos[:, None, :], sin[:, None, :])
    k_pe = apply_rope(k_pe, cos, sin)
    scale = (MLA_NOPE + MLA_ROPE) ** -0.5
    nblk = L // Q_BLOCK
    qn_b = jnp.moveaxis(q_nope.reshape(B, nblk, Q_BLOCK, MLA_HEADS, MLA_NOPE), 1, 0)
    qp_b = jnp.moveaxis(q_pe.reshape(B, nblk, Q_BLOCK, MLA_HEADS, MLA_ROPE), 1, 0)

    def attend(args):
        qn, qp = args
        s = (jnp.einsum('bqhd,bkhd->bhqk', qn, k_nope)
             + jnp.einsum('bqhr,bkr->bhqk', qp, k_pe))
        p = jax.nn.softmax(s.astype(jnp.float32) * scale, axis=-1).astype(v.dtype)
        return jnp.einsum('bhqk,bkhd->bqhd', p, v)

    o = lax.map(attend, (qn_b, qp_b))
    return jnp.moveaxis(o, 0, 1).reshape(B, L, MLA_HEADS * MLA_V)


def l2_normalise(x):
    return x * lax.rsqrt(jnp.sum(x * x, axis=-1, keepdims=True) + NORM_EPS)


def chunk_gated_delta(q, k, v, g, beta):
    B, L, H, DK = q.shape
    DV = v.shape[-1]
    C = GDN_CHUNK
    N = L // C

    def chunks(t):
        return jnp.swapaxes(t.reshape(B, N, C, H, t.shape[-1]), 2, 3)

    q, k, v = chunks(q), chunks(k), chunks(v)
    g = jnp.swapaxes(g.reshape(B, N, C, H), 2, 3)
    beta = jnp.swapaxes(beta.reshape(B, N, C, H), 2, 3)
    gc = jnp.cumsum(g, axis=-1)
    idx = jnp.arange(C)
    causal = idx[:, None] >= idx[None, :]
    strict = idx[:, None] > idx[None, :]
    decay_mask = jnp.exp(jnp.where(causal, gc[..., :, None] - gc[..., None, :], -jnp.inf))
    kb = k * beta[..., None]
    vb = v * beta[..., None]
    a_strict = jnp.where(strict, jnp.einsum('bnhid,bnhjd->bnhij', kb, k) * decay_mask, 0.0)
    m = a_strict + jnp.eye(C, dtype=q.dtype)
    rhs = jnp.concatenate([vb, kb * jnp.exp(gc)[..., None]], axis=-1)
    sol = lax.linalg.triangular_solve(m, rhs, left_side=True, lower=True, unit_diagonal=True)
    u, w = sol[..., :DV], sol[..., DV:]
    attn_qk = jnp.einsum('bnhid,bnhjd->bnhij', q, k) * decay_mask
    q_dec = q * jnp.exp(gc)[..., None]
    k_dec = k * jnp.exp(gc[..., -1:] - gc)[..., None]
    chunk_decay = jnp.exp(gc[..., -1])

    def step(S, xs):
        w_c, u_c, qd, kd, aqk, dec = xs
        v_new = u_c - jnp.einsum('bhcd,bhde->bhce', w_c, S)
        o = jnp.einsum('bhcd,bhde->bhce', qd, S) + jnp.einsum('bhij,bhje->bhie', aqk, v_new)
        S = S * dec[..., None, None] + jnp.einsum('bhcd,bhce->bhde', kd, v_new)
        return S, o

    xs = tuple(jnp.moveaxis(t, 1, 0) for t in (w, u, q_dec, k_dec, attn_qk, chunk_decay))
    S0 = jnp.zeros((B, H, DK, DV), q.dtype)
    _, o = lax.scan(step, S0, xs)
    return jnp.transpose(o, (1, 0, 3, 2, 4)).reshape(B, L, H, DV)


def gdn_mixer(qkv, z, b_raw, a_raw, conv_w, a_log, dt_bias, out_norm):
    B, L, _ = qkv.shape
    f32 = jnp.float32
    qkv = jax.nn.silu(centred_dwconv(qkv, conv_w).astype(f32))
    nqk = GDN_HEADS * GDN_DK
    q = l2_normalise(qkv[..., :nqk].reshape(B, L, GDN_HEADS, GDN_DK)) * (GDN_DK ** -0.5)
    k = l2_normalise(qkv[..., nqk:2 * nqk].reshape(B, L, GDN_HEADS, GDN_DK))
    v = qkv[..., 2 * nqk:].reshape(B, L, GDN_HEADS, GDN_DV)
    beta = jax.nn.sigmoid(b_raw.astype(f32)).reshape(B, L, 2, GDN_HEADS)
    g = (-jnp.exp(a_log.astype(f32))
         * jax.nn.softplus(a_raw.astype(f32).reshape(B, L, 2, GDN_HEADS) + dt_bias.astype(f32)))
    o_f = chunk_gated_delta(q, k, v, g[:, :, 0], beta[:, :, 0])
    flip = lambda t: jnp.flip(t, axis=1)
    o_b = flip(chunk_gated_delta(flip(q), flip(k), flip(v), flip(g[:, :, 1]), flip(beta[:, :, 1])))
    o = rms_norm(o_f + o_b, out_norm) * jax.nn.silu(z.astype(f32).reshape(B, L, GDN_HEADS, GDN_DV))
    return o.reshape(B, L, GDN_HEADS * GDN_DV)


def trunk_layer(x, norm_mix_pre, norm_mix_post, norm_ffn_pre, norm_ffn_post, w_in,
                hy_conv_w, hy_conv_b, hy_ffn_w1, hy_ffn_b1, hy_sin_freq, hy_ffn_w2, hy_ffn_b2,
                hy_ffn_w3, hy_skip, mla_q_norm, mla_wq_b, mla_kv_norm, mla_wkv_b,
                gdn_conv_w, gdn_a_log, gdn_dt_bias, gdn_out_norm,
                w_branch, w_out, w_gate, w_up, w_down):
    B, L, _ = x.shape
    dt = x.dtype
    h = rms_norm(x, norm_mix_pre)
    d_in = w_in.shape[1]
    d_pad = -(-d_in // 512) * 512
    proj = _mm3(h, jnp.pad(w_in, ((0, 0), (0, d_pad - d_in))))
    points = [int(p) for p in np.cumsum(IN_SIZES)]
    hy_in, mla_q, mla_kv, gdn_qkv, gdn_z, gdn_b, gdn_a, gate_logits, _ = jnp.split(proj, points, axis=-1)
    o_hy = hyena_mixer(hy_in, hy_conv_w, hy_conv_b, hy_ffn_w1, hy_ffn_b1, hy_sin_freq,
                       hy_ffn_w2, hy_ffn_b2, hy_ffn_w3, hy_skip)
    o_mla = mla_mixer(mla_q, mla_kv, mla_q_norm, mla_wq_b, mla_kv_norm, mla_wkv_b)
    o_gdn = gdn_mixer(gdn_qkv, gdn_z, gdn_b, gdn_a, gdn_conv_w, gdn_a_log, gdn_dt_bias, gdn_out_norm)
    gates = jax.nn.sigmoid(gate_logits.astype(jnp.float32)).astype(dt).reshape(B, L, N_BRANCH, D_MODEL)
    branches = (o_hy, o_mla, o_gdn)
    merged = gates[:, :, 0] * _mm3(branches[0].astype(dt), w_branch[0])
    for i in range(1, N_BRANCH):
        merged = merged + gates[:, :, i] * _mm3(branches[i].astype(dt), w_branch[i])
    x = x + rms_norm(_mm3(merged, w_out), norm_mix_post)
    h = rms_norm(x, norm_ffn_pre)
    f = _mm3(jax.nn.silu(_mm3(h, w_gate)) * _mm3(h, w_up), w_down)
    return x + rms_norm(f, norm_ffn_post)


def kernel(x_prompt, x_sample, norm_mix_pre, norm_mix_post, norm_ffn_pre, norm_ffn_post, w_in,
           hy_conv_w, hy_conv_b, hy_ffn_w1, hy_ffn_b1, hy_sin_freq, hy_ffn_w2, hy_ffn_b2,
           hy_ffn_w3, hy_skip, mla_q_norm, mla_wq_b, mla_kv_norm, mla_wkv_b,
           gdn_conv_w, gdn_a_log, gdn_dt_bias, gdn_out_norm,
           w_branch, w_out, w_gate, w_up, w_down):
    weights = (norm_mix_pre, norm_mix_post, norm_ffn_pre, norm_ffn_post, w_in,
               hy_conv_w, hy_conv_b, hy_ffn_w1, hy_ffn_b1, hy_sin_freq, hy_ffn_w2, hy_ffn_b2,
               hy_ffn_w3, hy_skip, mla_q_norm, mla_wq_b, mla_kv_norm, mla_wkv_b,
               gdn_conv_w, gdn_a_log, gdn_dt_bias, gdn_out_norm,
               w_branch, w_out, w_gate, w_up, w_down)

    def run_trunk(x):
        for layer in range(DEPTH):
            x = trunk_layer(x, *[w[layer] for w in weights])
        return x

    return (run_trunk(x_prompt), run_trunk(x_sample))
```

```python
import functools
import math

import jax
import jax.numpy as jnp
from jax import lax
from jax.experimental import pallas as pl
from jax.experimental.pallas import tpu as pltpu
import numpy as np

F32 = jnp.float32
BF16 = jnp.bfloat16

D_MODEL = 1024
DEPTH = 2
BRANCH_W = 512
N_BRANCH = 3
HY_W = BRANCH_W
HY_ORDER = 2
HY_EMB = 33
HY_FAST_PCT = 0.3
HY_SLOW_PCT = 1.5
HY_TARGET = 1e-2
MLA_HEADS = 4
MLA_NOPE = 128
MLA_ROPE = 64
MLA_V = 128
MLA_Q_LORA = 256
MLA_KV_LORA = 128
ROPE_THETA = 10000.0
GDN_HEADS = 4
GDN_DK = 128
GDN_DV = 128
GDN_CHUNK = 64
NORM_EPS = 1e-6

_OFF_HY = 0
_OFF_MQ = _OFF_HY + (HY_ORDER + 1) * HY_W
_OFF_MKV = _OFF_MQ + MLA_Q_LORA
_OFF_GQKV = _OFF_MKV + MLA_KV_LORA + MLA_ROPE
_OFF_GZ = _OFF_GQKV + GDN_HEADS * (2 * GDN_DK + GDN_DV)
_OFF_GB = _OFF_GZ + GDN_HEADS * GDN_DV
_OFF_GA = _OFF_GB + 2 * GDN_HEADS
_OFF_GATE = _OFF_GA + 2 * GDN_HEADS
_D_IN = _OFF_GATE + N_BRANCH * D_MODEL

MAIN_W = 3072 + 1536 + 1536 + 512
MISC_W = 640

LANES = 128
VMEM_LIMIT_BYTES = 56 * 1024 * 1024


def _cparams(*sem):
    return pltpu.CompilerParams(dimension_semantics=sem, vmem_limit_bytes=VMEM_LIMIT_BYTES)


def _full(shape):
    nd = len(shape)
    return pl.BlockSpec(shape, lambda *_: (0,) * nd)


def _norm_mm_kernel(x_ref, g_ref, w_ref, o_ref, h_ref):
    @pl.when(pl.program_id(1) == 0)
    def _():
        x = x_ref[...]
        r = lax.rsqrt(jnp.mean(x * x, axis=-1, keepdims=True) + NORM_EPS)
        h_ref[...] = (x * r * g_ref[...]).astype(BF16)

    o_ref[...] = jnp.dot(h_ref[...], w_ref[...], preferred_element_type=F32)


def norm_mm(x, g, w, tm, tn):
    T, D = x.shape
    N = w.shape[1]
    tm = min(tm, T)
    return pl.pallas_call(
        _norm_mm_kernel,
        grid=(T // tm, N // tn),
        in_specs=[pl.BlockSpec((tm, D), lambda i, j: (i, 0)),
                  pl.BlockSpec((1, D), lambda i, j: (0, 0)),
                  pl.BlockSpec((D, tn), lambda i, j: (0, j))],
        out_specs=pl.BlockSpec((tm, tn), lambda i, j: (i, j)),
        out_shape=jax.ShapeDtypeStruct((T, N), F32),
        scratch_shapes=[pltpu.VMEM((tm, D), BF16)],
        compiler_params=_cparams("parallel", "arbitrary"),
        name="norm_mm",
    )(x, g, w)


def _dwconv_kernel(x_ref, p_ref, n_ref, w_ref, b_ref, o_ref, *, silu):
    i = pl.program_id(1)
    last = pl.num_programs(1) - 1
    x = x_ref[0]
    tl = x.shape[0]
    prev_row = jnp.where(i > 0, p_ref[0, 7:8, :], 0.0)
    next_row = jnp.where(i < last, n_ref[0, 0:1, :], 0.0)
    rows = lax.broadcasted_iota(jnp.int32, x.shape, 0)
    x_dn = jnp.where(rows == 0, prev_row, pltpu.roll(x, 1, axis=0))
    x_up = jnp.where(rows == tl - 1, next_row, pltpu.roll(x, tl - 1, axis=0))
    out = x_dn * w_ref[0:1, :] + x * w_ref[1:2, :] + x_up * w_ref[2:3, :] + b_ref[...]
    if silu:
        out = out * jax.nn.sigmoid(out)
    o_ref[0] = out


def dwconv3(x, col_blk, cw, w, b, silu, tl=512):
    B, L, _ = x.shape
    tl = min(tl, L)
    nsub = tl // 8
    lastblk = L // 8 - 1
    return pl.pallas_call(
        functools.partial(_dwconv_kernel, silu=silu),
        grid=(B, L // tl),
        in_specs=[pl.BlockSpec((1, tl, cw), lambda b_, i: (b_, i, col_blk)),
                  pl.BlockSpec((1, 8, cw), lambda b_, i: (b_, jnp.maximum(i * nsub - 1, 0), col_blk)),
                  pl.BlockSpec((1, 8, cw), lambda b_, i: (b_, jnp.minimum((i + 1) * nsub, lastblk), col_blk)),
                  _full((3, cw)), _full((1, cw))],
        out_specs=pl.BlockSpec((1, tl, cw), lambda b_, i: (b_, i, 0)),
        out_shape=jax.ShapeDtypeStruct((B, L, cw), F32),
        compiler_params=_cparams("parallel", "parallel"),
        name="dwconv3",
    )(x, x, x, w, b)


def _tile_lanes(t, c):
    return t if c == LANES else jnp.concatenate([t] * (c // LANES), axis=-1)


def _fft_s1_kernel(f_ref, x_ref, twr_ref, twi_ref, o_ref, *, nb, n1):
    c = x_ref.shape[-1]
    for t in range(nb):
        a = jnp.dot(f_ref[...], x_ref[0, t], preferred_element_type=F32)
        ar, ai = a[:n1], a[n1:]
        twr = _tile_lanes(twr_ref[t], c)
        twi = _tile_lanes(twi_ref[t], c)
        o_ref[0, t, :n1, :] = (ar * twr + ai * twi).astype(BF16)
        o_ref[0, t, n1:, :] = (ai * twr - ar * twi).astype(BF16)


def _fft_s2_kernel(f_ref, finv_ref, b_ref, kr_ref, ki_ref, twr_ref, twi_ref, o_ref, *, nb, n2):
    c = b_ref.shape[-1]
    for t in range(nb):
        x = jnp.dot(f_ref[...], b_ref[0, t], preferred_element_type=F32)
        xr, xi = x[:n2], x[n2:]
        kr, ki = kr_ref[t], ki_ref[t]
        y = jnp.concatenate([xr * kr - xi * ki, xr * ki + xi * kr], axis=0).astype(BF16)
        cm = jnp.dot(finv_ref[...], y, preferred_element_type=F32)
        cr, ci = cm[:n2], cm[n2:]
        twr = _tile_lanes(twr_ref[t], c)
        twi = _tile_lanes(twi_ref[t], c)
        o_ref[0, t, :n2, :] = (cr * twr - ci * twi).astype(BF16)
        o_ref[0, t, n2:, :] = (cr * twi + ci * twr).astype(BF16)


def _fft_s3_kernel(g_ref, d_ref, o_ref, *, nb):
    for t in range(nb):
        o_ref[0, t] = jnp.dot(g_ref[...], d_ref[0, t], preferred_element_type=F32)


def _dft_tables(L):
    N = 2 * L
    n2 = 128 if N >= 4096 else 16
    n1 = N // n2
    n1h = n1 // 2
    k1 = np.arange(n1)[:, None]
    m1 = np.arange(n1h)[None, :]
    th = 2.0 * np.pi * ((k1 * m1) % n1) / n1
    c, s = np.cos(th), np.sin(th)
    f1 = np.block([[c, s], [-s, c]])
    g1 = np.block([[c.T, -s.T], [s.T, c.T]]) / N
    a2 = np.arange(n2)
    ph = 2.0 * np.pi * ((a2[:, None] * a2[None, :]) % n2) / n2
    c2, s2 = np.cos(ph), np.sin(ph)
    f2 = np.block([[c2, s2], [-s2, c2]])
    f2inv = f2.T
    as_bf16 = lambda m: jnp.asarray(m, dtype=F32).astype(BF16)
    return n1, n2, as_bf16(f1), as_bf16(g1), as_bf16(f2), as_bf16(f2inv)


def _twiddles(n1, n2):
    N = n1 * n2
    prod = (jnp.arange(n2, dtype=jnp.int32)[:, None] * jnp.arange(n1, dtype=jnp.int32)[None, :]) % N
    ang = prod.astype(F32) * (2.0 * math.pi / N)
    shape = (n2, n1, LANES)
    return (jnp.broadcast_to(jnp.cos(ang)[:, :, None], shape),
            jnp.broadcast_to(jnp.sin(ang)[:, :, None], shape))


def fft_long_conv(z, kr, ki, tabs):
    B, L, C = z.shape
    n1, n2, f1, g1, f2, f2inv, tw1r, tw1i, tw2r, tw2i = tabs
    n1h = n1 // 2
    P = B // 2
    nb1 = min(8, n2)
    nb2 = min(8, n1)
    zt = z.astype(BF16).reshape(P, 2, n1h, n2, C).transpose(0, 3, 1, 2, 4).reshape(P, n2, 2 * n1h, C)
    a = pl.pallas_call(
        functools.partial(_fft_s1_kernel, nb=nb1, n1=n1),
        grid=(n2 // nb1, P),
        in_specs=[_full((2 * n1, 2 * n1h)),
                  pl.BlockSpec((1, nb1, 2 * n1h, C), lambda j, p: (p, j, 0, 0)),
                  pl.BlockSpec((nb1, n1, LANES), lambda j, p: (j, 0, 0)),
                  pl.BlockSpec((nb1, n1, LANES), lambda j, p: (j, 0, 0))],
        out_specs=pl.BlockSpec((1, nb1, 2 * n1, C), lambda j, p: (p, j, 0, 0)),
        out_shape=jax.ShapeDtypeStruct((P, n2, 2 * n1, C), BF16),
        compiler_params=_cparams("parallel", "parallel"),
        name="fft_s1",
    )(f1, zt, tw1r, tw1i)
    bt = a.reshape(P, n2, 2, n1, C).transpose(0, 3, 2, 1, 4).reshape(P, n1, 2 * n2, C)
    d = pl.pallas_call(
        functools.partial(_fft_s2_kernel, nb=nb2, n2=n2),
        grid=(n1 // nb2, P),
        in_specs=[_full((2 * n2, 2 * n2)), _full((2 * n2, 2 * n2)),
                  pl.BlockSpec((1, nb2, 2 * n2, C), lambda j, p: (p, j, 0, 0)),
                  pl.BlockSpec((nb2, n2, C), lambda j, p: (j, 0, 0)),
                  pl.BlockSpec((nb2, n2, C), lambda j, p: (j, 0, 0)),
                  pl.BlockSpec((nb2, n2, LANES), lambda j, p: (j, 0, 0)),
                  pl.BlockSpec((nb2, n2, LANES), lambda j, p: (j, 0, 0))],
        out_specs=pl.BlockSpec((1, nb2, 2 * n2, C), lambda j, p: (p, j, 0, 0)),
        out_shape=jax.ShapeDtypeStruct((P, n1, 2 * n2, C), BF16),
        compiler_params=_cparams("parallel", "parallel"),
        name="fft_s2",
    )(f2, f2inv, bt, kr, ki, tw2r, tw2i)
    dt = d.reshape(P, n1, 2, n2, C).transpose(0, 3, 2, 1, 4).reshape(P, n2, 2 * n1, C)
    y = pl.pallas_call(
        functools.partial(_fft_s3_kernel, nb=nb1),
        grid=(n2 // nb1, P),
        in_specs=[_full((2 * n1h, 2 * n1)),
                  pl.BlockSpec((1, nb1, 2 * n1, C), lambda j, p: (p, j, 0, 0))],
        out_specs=pl.BlockSpec((1, nb1, 2 * n1h, C), lambda j, p: (p, j, 0, 0)),
        out_shape=jax.ShapeDtypeStruct((P, n2, 2 * n1h, C), F32),
        compiler_params=_cparams("parallel", "parallel"),
        name="fft_s3",
    )(g1, dt)
    return y.reshape(P, n2, 2, n1h, C).transpose(0, 2, 3, 1, 4).reshape(B, L, C)


def _hy_gate_kernel(g_ref, c_ref, z_ref, s_ref, o_ref):
    o_ref[...] = g_ref[...] * (c_ref[...] + s_ref[...] * z_ref[...])


def hy_gate(u, gate_blk, conv, zsrc, z_blk, skip, tl=1024):
    T, C = conv.shape
    tl = min(tl, T)
    return pl.pallas_call(
        _hy_gate_kernel,
        grid=(T // tl,),
        in_specs=[pl.BlockSpec((tl, C), lambda i: (i, gate_blk)),
                  pl.BlockSpec((tl, C), lambda i: (i, 0)),
                  pl.BlockSpec((tl, C), lambda i: (i, z_blk)),
                  _full((1, C))],
        out_specs=pl.BlockSpec((tl, C), lambda i: (i, 0)),
        out_shape=jax.ShapeDtypeStruct((T, C), F32),
        compiler_params=_cparams("parallel"),
        name="hy_gate",
    )(u, conv, zsrc, skip)


def hyena_filter_spectrum(L, n1, n2, w1, b1, freq, w2, b2, w3):
    t = jnp.linspace(0.0, 1.0, L, dtype=F32)[:, None]
    bands = (HY_EMB - 1) // 2
    wpos = (2.0 * math.pi / L) * jnp.arange(L, dtype=F32)[:, None]
    fr = jnp.linspace(1e-4, bands - 1, bands, dtype=F32)[None, :]
    feats = jnp.concatenate([t, jnp.cos(fr * wpos), -jnp.sin(fr * wpos)], axis=-1)
    hp = lax.Precision.HIGHEST
    h = jnp.sin(freq * (jnp.dot(feats, w1, precision=hp) + b1))
    h = jnp.sin(freq * (jnp.dot(h, w2, precision=hp) + b2))
    h = jnp.dot(h, w3, precision=hp).reshape(L, HY_ORDER, 2, HY_W)
    deltas = jnp.abs(jnp.linspace(math.log(HY_TARGET) / HY_SLOW_PCT,
                                  math.log(HY_TARGET) / HY_FAST_PCT, HY_W, dtype=F32))
    h = h * jnp.exp(-t * deltas)[:, None, None, :]
    fwd = h[:, :, 0]
    bwd = h[1:, :, 1][::-1]
    kern = jnp.concatenate([fwd, jnp.zeros((1, HY_ORDER, HY_W), F32), bwd], axis=0)
    kern = kern / jnp.sum(jnp.abs(kern), axis=0, keepdims=True)
    kf = jnp.fft.fft(kern, axis=0).reshape(n2, n1, HY_ORDER, HY_W).transpose(2, 1, 0, 3)
    return jnp.real(kf).astype(F32), jnp.imag(kf).astype(F32)


def hyena_mixer(proj3, conv_w, conv_b, w1, b1, freq, w2, b2, w3, skip):
    B, L, _ = proj3.shape
    T = B * L
    u = dwconv3(proj3, 2, 1536, conv_w, conv_b[None, :], silu=False).reshape(T, 1536)
    n1, n2, f1, g1, f2, f2inv = _dft_tables(L)
    tw1r, tw1i = _twiddles(n1, n2)
    tw2r, tw2i = tw1r.transpose(1, 0, 2), tw1i.transpose(1, 0, 2)
    tabs = (n1, n2, f1, g1, f2, f2inv, tw1r, tw1i, tw2r, tw2i)
    kr, ki = hyena_filter_spectrum(L, n1, n2, w1, b1[None, :], freq[None, :], w2, b2[None, :], w3)
    conv = fft_long_conv(u[:, 1024:1536].reshape(B, L, HY_W), kr[0], ki[0], tabs).reshape(T, HY_W)
    z1 = hy_gate(u, 0, conv, u, 2, skip[0:1])
    conv = fft_long_conv(z1.reshape(B, L, HY_W), kr[1], ki[1], tabs).reshape(T, HY_W)
    return hy_gate(u, 1, conv, z1, 0, skip[1:2])


def _mla_proj_kernel(m_ref, qg_ref, kg_ref, wq_ref, wkv_ref, cq_ref, sq_ref, ck_ref, sk_ref,
                     qn_ref, y1_ref, y2_ref, kn_ref, v_ref, kp_ref):
    m = m_ref[0]
    ql = m[:, 0:256]
    r = lax.rsqrt(jnp.mean(ql * ql, axis=-1, keepdims=True) + NORM_EPS)
    q = jnp.dot((ql * r * qg_ref[...]).astype(BF16), wq_ref[...], preferred_element_type=F32)
    ckv = m[:, 256:384]
    r = lax.rsqrt(jnp.mean(ckv * ckv, axis=-1, keepdims=True) + NORM_EPS)
    kv = jnp.dot((ckv * r * kg_ref[...]).astype(BF16), wkv_ref[...], preferred_element_type=F32)
    x1, x2 = q[:, 512:640], q[:, 640:768]
    cq, sq = cq_ref[...], sq_ref[...]
    qn_ref[0] = q[:, 0:512].astype(BF16)
    y1_ref[0] = (x1 * cq - x2 * sq).astype(BF16)
    y2_ref[0] = (x2 * cq + x1 * sq).astype(BF16)
    kn_ref[0] = kv[:, 0:512].astype(BF16)
    v_ref[0] = kv[:, 512:1024].astype(BF16)
    kp_ref[0] = (m[:, 384:512] * ck_ref[...] + m[:, 512:640] * sk_ref[...]).astype(BF16)


def _attn_kernel(q_ref, k_ref, v_ref, o_ref, m_ref, l_ref, acc_ref, *, scale):
    ki = pl.program_id(3)

    @pl.when(ki == 0)
    def _():
        m_ref[...] = jnp.full(m_ref.shape, -jnp.inf, F32)
        l_ref[...] = jnp.zeros(l_ref.shape, F32)
        acc_ref[...] = jnp.zeros(acc_ref.shape, F32)

    s = lax.dot_general(q_ref[0, 0], k_ref[0, 0], (((1,), (1,)), ((), ())),
                        preferred_element_type=F32) * scale
    m_prev = m_ref[...]
    m_new = jnp.maximum(m_prev, jnp.max(s, axis=-1, keepdims=True))
    alpha = jnp.exp(m_prev - m_new)
    p = jnp.exp(s - m_new)
    l_ref[...] = alpha * l_ref[...] + jnp.sum(p, axis=-1, keepdims=True)
    acc_ref[...] = alpha * acc_ref[...] + jnp.dot(p.astype(BF16), v_ref[0], preferred_element_type=F32)
    m_ref[...] = m_new

    @pl.when(ki == pl.num_programs(3) - 1)
    def _():
        o_ref[0] = acc_ref[...] / l_ref[...]


def _rope_lane_tables(L):
    half = MLA_ROPE // 2
    inv = ROPE_THETA ** (-jnp.arange(half, dtype=F32) / half)
    ang = jnp.arange(L, dtype=F32)[:, None] * inv[None, :]
    cos, sin = jnp.cos(ang), jnp.sin(ang)
    zeros = jnp.zeros((L, LANES - MLA_ROPE), F32)
    cq = jnp.tile(cos, (1, MLA_HEADS))
    sq = jnp.tile(sin, (1, MLA_HEADS))
    ck = jnp.concatenate([cos, cos, zeros], axis=-1)
    sk = jnp.concatenate([-sin, sin, zeros], axis=-1)
    return cq, sq, ck, sk


def mla_mixer(misc3, q_norm, wq_b, kv_norm, wkv_b, tl=512, tq=1024, tk=1024):
    B, L, _ = misc3.shape
    H = MLA_HEADS
    tl, tq, tk = min(tl, L), min(tq, L), min(tk, L)
    wq = wq_b.reshape(MLA_Q_LORA, H, MLA_NOPE + MLA_ROPE)
    half = MLA_ROPE // 2
    wq = jnp.concatenate([wq[:, :, :MLA_NOPE].reshape(MLA_Q_LORA, -1),
                          wq[:, :, MLA_NOPE:MLA_NOPE + half].reshape(MLA_Q_LORA, -1),
                          wq[:, :, MLA_NOPE + half:].reshape(MLA_Q_LORA, -1)], axis=-1).astype(BF16)
    wkv = wkv_b.reshape(MLA_KV_LORA, H, MLA_NOPE + MLA_V)
    wkv = jnp.concatenate([wkv[:, :, :MLA_NOPE].reshape(MLA_KV_LORA, -1),
                           wkv[:, :, MLA_NOPE:].reshape(MLA_KV_LORA, -1)], axis=-1).astype(BF16)
    cq, sq, ck, sk = _rope_lane_tables(L)
    tok = lambda w: pl.BlockSpec((1, tl, w), lambda b, i: (b, i, 0))
    pos = pl.BlockSpec((tl, LANES), lambda b, i: (i, 0))
    sds = lambda w: jax.ShapeDtypeStruct((B, L, w), BF16)
    qn, y1, y2, kn, v, kp = pl.pallas_call(
        _mla_proj_kernel,
        grid=(B, L // tl),
        in_specs=[tok(MISC_W), _full((1, MLA_Q_LORA)), _full((1, MLA_KV_LORA)),
                  _full(wq.shape), _full(wkv.shape), pos, pos, pos, pos],
        out_specs=[tok(512), tok(LANES), tok(LANES), tok(512), tok(512), tok(LANES)],
        out_shape=[sds(512), sds(LANES), sds(LANES), sds(512), sds(512), sds(LANES)],
        compiler_params=_cparams("parallel", "parallel"),
        name="mla_proj",
    )(misc3, q_norm[None, :], kv_norm[None, :], wq, wkv, cq, sq, ck, sk)
    q = jnp.concatenate([qn.reshape(B, L, H, MLA_NOPE), y1.reshape(B, L, H, half),
                         y2.reshape(B, L, H, half)], axis=-1).transpose(0, 2, 1, 3)
    k = jnp.concatenate([kn.reshape(B, L, H, MLA_NOPE),
                         jnp.broadcast_to(kp[:, :, None, :MLA_ROPE], (B, L, H, MLA_ROPE))],
                        axis=-1).transpose(0, 2, 1, 3)
    dqk = MLA_NOPE + MLA_ROPE
    return pl.pallas_call(
        functools.partial(_attn_kernel, scale=dqk ** -0.5),
        grid=(B, H, L // tq, L // tk),
        in_specs=[pl.BlockSpec((1, 1, tq, dqk), lambda b, h, i, j: (b, h, i, 0)),
                  pl.BlockSpec((1, 1, tk, dqk), lambda b, h, i, j: (b, h, j, 0)),
                  pl.BlockSpec((1, tk, MLA_V), lambda b, h, i, j: (b, j, h))],
        out_specs=pl.BlockSpec((1, tq, MLA_V), lambda b, h, i, j: (b, i, h)),
        out_shape=jax.ShapeDtypeStruct((B, L, H * MLA_V), F32),
        scratch_shapes=[pltpu.VMEM((tq, 1), F32), pltpu.VMEM((tq, 1), F32), pltpu.VMEM((tq, MLA_V), F32)],
        compiler_params=_cparams("parallel", "parallel", "parallel", "arbitrary"),
        name="mla_attn",
    )(q, k, v)


def rms_norm(x, g):
    xf = x.astype(F32)
    y = xf * lax.rsqrt(jnp.mean(xf * xf, axis=-1, keepdims=True) + NORM_EPS)
    return (y * g.astype(F32)).astype(x.dtype)


def l2_normalise(x):
    return x * lax.rsqrt(jnp.sum(x * x, axis=-1, keepdims=True) + NORM_EPS)


def chunk_gated_delta(q, k, v, g, beta):
    B, L, H, DK = q.shape
    DV = v.shape[-1]
    C = GDN_CHUNK
    N = L // C

    def chunks(t):
        return jnp.swapaxes(t.reshape(B, N, C, H, t.shape[-1]), 2, 3)

    q, k, v = chunks(q), chunks(k), chunks(v)
    g = jnp.swapaxes(g.reshape(B, N, C, H), 2, 3)
    beta = jnp.swapaxes(beta.reshape(B, N, C, H), 2, 3)
    gc = jnp.cumsum(g, axis=-1)
    idx = jnp.arange(C)
    causal = idx[:, None] >= idx[None, :]
    strict = idx[:, None] > idx[None, :]
    decay_mask = jnp.exp(jnp.where(causal, gc[..., :, None] - gc[..., None, :], -jnp.inf))
    kb = k * beta[..., None]
    vb = v * beta[..., None]
    a_strict = jnp.where(strict, jnp.einsum('bnhid,bnhjd->bnhij', kb, k) * decay_mask, 0.0)
    m = a_strict + jnp.eye(C, dtype=q.dtype)
    rhs = jnp.concatenate([vb, kb * jnp.exp(gc)[..., None]], axis=-1)
    sol = lax.linalg.triangular_solve(m, rhs, left_side=True, lower=True, unit_diagonal=True)
    u, w = sol[..., :DV], sol[..., DV:]
    attn_qk = jnp.einsum('bnhid,bnhjd->bnhij', q, k) * decay_mask
    q_dec = q * jnp.exp(gc)[..., None]
    k_dec = k * jnp.exp(gc[..., -1:] - gc)[..., None]
    chunk_decay = jnp.exp(gc[..., -1])

    def step(S, xs):
        w_c, u_c, qd, kd, aqk, dec = xs
        v_new = u_c - jnp.einsum('bhcd,bhde->bhce', w_c, S)
        o = jnp.einsum('bhcd,bhde->bhce', qd, S) + jnp.einsum('bhij,bhje->bhie', aqk, v_new)
        S = S * dec[..., None, None] + jnp.einsum('bhcd,bhce->bhde', kd, v_new)
        return S, o

    xs = tuple(jnp.moveaxis(t, 1, 0) for t in (w, u, q_dec, k_dec, attn_qk, chunk_decay))
    S0 = jnp.zeros((B, H, DK, DV), q.dtype)
    _, o = lax.scan(step, S0, xs)
    return jnp.transpose(o, (1, 0, 3, 2, 4)).reshape(B, L, H, DV)


def gdn_mixer(proj3, misc3, conv_w, a_log, dt_bias, out_norm):
    B, L, _ = proj3.shape
    qkv = dwconv3(proj3, 3, 1536, conv_w, jnp.zeros((1, 1536), F32), silu=True)
    z = proj3[..., 6144:6656]
    b_raw = misc3[..., 448:456]
    a_raw = misc3[..., 456:464]
    nqk = GDN_HEADS * GDN_DK
    q = l2_normalise(qkv[..., :nqk].reshape(B, L, GDN_HEADS, GDN_DK)) * (GDN_DK ** -0.5)
    k = l2_normalise(qkv[..., nqk:2 * nqk].reshape(B, L, GDN_HEADS, GDN_DK))
    v = qkv[..., 2 * nqk:].reshape(B, L, GDN_HEADS, GDN_DV)
    beta = jax.nn.sigmoid(b_raw).reshape(B, L, 2, GDN_HEADS)
    g = -jnp.exp(a_log) * jax.nn.softplus(a_raw.reshape(B, L, 2, GDN_HEADS) + dt_bias)
    o_f = chunk_gated_delta(q, k, v, g[:, :, 0], beta[:, :, 0])
    flip = lambda t: jnp.flip(t, axis=1)
    o_b = flip(chunk_gated_delta(flip(q), flip(k), flip(v), flip(g[:, :, 1]), flip(beta[:, :, 1])))
    o = rms_norm(o_f + o_b, out_norm) * jax.nn.silu(z.reshape(B, L, GDN_HEADS, GDN_DV))
    return o.reshape(B, L, GDN_HEADS * GDN_DV)


def _merge_kernel(x_ref, g_ref, oh_ref, om_ref, og_ref, wb_ref, wo_ref, n_ref, o_ref):
    merged = None
    for i, b_ref in enumerate((oh_ref, om_ref, og_ref)):
        gate = jax.nn.sigmoid(g_ref[:, i * D_MODEL:(i + 1) * D_MODEL])
        term = gate * jnp.dot(b_ref[...].astype(BF16), wb_ref[i], preferred_element_type=F32)
        merged = term if merged is None else merged + term
    y = jnp.dot(merged.astype(BF16), wo_ref[...], preferred_element_type=F32)
    r = lax.rsqrt(jnp.mean(y * y, axis=-1, keepdims=True) + NORM_EPS)
    o_ref[...] = x_ref[...] + y * r * n_ref[...]


def merge_out(x, proj, o_hy, o_mla, o_gdn, w_branch, w_out, norm_post, tm=512):
    T = x.shape[0]
    tm = min(tm, T)
    row = lambda w: pl.BlockSpec((tm, w), lambda i: (i, 0))
    return pl.pallas_call(
        _merge_kernel,
        grid=(T // tm,),
        in_specs=[row(D_MODEL), row(N_BRANCH * D_MODEL), row(BRANCH_W), row(BRANCH_W), row(BRANCH_W),
                  _full((N_BRANCH, BRANCH_W, D_MODEL)), _full((D_MODEL, D_MODEL)), _full((1, D_MODEL))],
        out_specs=row(D_MODEL),
        out_shape=jax.ShapeDtypeStruct((T, D_MODEL), F32),
        compiler_params=_cparams("parallel"),
        name="merge_out",
    )(x, proj, o_hy, o_mla, o_gdn, w_branch, w_out, norm_post)


def _ffn_kernel(x_ref, gpre_ref, wg_ref, wu_ref, wd_ref, gpost_ref, o_ref, h_ref, acc_ref):
    j = pl.program_id(1)

    @pl.when(j == 0)
    def _():
        x = x_ref[...]
        r = lax.rsqrt(jnp.mean(x * x, axis=-1, keepdims=True) + NORM_EPS)
        h_ref[...] = (x * r * gpre_ref[...]).astype(BF16)

    h = h_ref[...]
    gate = jnp.dot(h, wg_ref[...], preferred_element_type=F32)
    up = jnp.dot(h, wu_ref[...], preferred_element_type=F32)
    part = jnp.dot((gate * jax.nn.sigmoid(gate) * up).astype(BF16), wd_ref[...], preferred_element_type=F32)

    @pl.when(j == 0)
    def _():
        acc_ref[...] = part

    @pl.when(j > 0)
    def _():
        acc_ref[...] += part

    @pl.when(j == pl.num_programs(1) - 1)
    def _():
        f = acc_ref[...]
        r = lax.rsqrt(jnp.mean(f * f, axis=-1, keepdims=True) + NORM_EPS)
        o_ref[...] = x_ref[...] + f * r * gpost_ref[...]


def ffn(x, g_pre, w_gate, w_up, w_down, g_post, tm=512):
    T = x.shape[0]
    dff = w_gate.shape[1]
    tf = dff // 2
    tm = min(tm, T)
    return pl.pallas_call(
        _ffn_kernel,
        grid=(T // tm, dff // tf),
        in_specs=[pl.BlockSpec((tm, D_MODEL), lambda i, j: (i, 0)), _full((1, D_MODEL)),
                  pl.BlockSpec((D_MODEL, tf), lambda i, j: (0, j)),
                  pl.BlockSpec((D_MODEL, tf), lambda i, j: (0, j)),
                  pl.BlockSpec((tf, D_MODEL), lambda i, j: (j, 0)), _full((1, D_MODEL))],
        out_specs=pl.BlockSpec((tm, D_MODEL), lambda i, j: (i, 0)),
        out_shape=jax.ShapeDtypeStruct((T, D_MODEL), F32),
        scratch_shapes=[pltpu.VMEM((tm, D_MODEL), BF16), pltpu.VMEM((tm, D_MODEL), F32)],
        compiler_params=_cparams("parallel", "arbitrary"),
        name="ffn",
    )(x, g_pre, w_gate, w_up, w_down, g_post)


def _split_w_in(w_in):
    cols = lambda off, n: w_in[:, off:off + n]
    main = jnp.concatenate([cols(_OFF_GATE, N_BRANCH * D_MODEL), cols(_OFF_HY, 1536),
                            cols(_OFF_GQKV, 1536), cols(_OFF_GZ, 512)], axis=-1).astype(BF16)
    half = MLA_ROPE // 2
    kpe = _OFF_MKV + MLA_KV_LORA
    zeros = lambda n: jnp.zeros((D_MODEL, n), w_in.dtype)
    misc = jnp.concatenate([cols(_OFF_MQ, MLA_Q_LORA), cols(_OFF_MKV, MLA_KV_LORA),
                            cols(kpe, MLA_ROPE), cols(_OFF_GB, 8), cols(_OFF_GA, 8), zeros(48),
                            cols(kpe + half, half), cols(kpe, half), zeros(64)], axis=-1).astype(BF16)
    return main, misc


def trunk_layer(x, norm_mix_pre, norm_mix_post, norm_ffn_pre, norm_ffn_post, w_in,
                hy_conv_w, hy_conv_b, hy_ffn_w1, hy_ffn_b1, hy_sin_freq, hy_ffn_w2, hy_ffn_b2,
                hy_ffn_w3, hy_skip, mla_q_norm, mla_wq_b, mla_kv_norm, mla_wkv_b,
                gdn_conv_w, gdn_a_log, gdn_dt_bias, gdn_out_norm,
                w_branch, w_out, w_gate, w_up, w_down):
    B, L, D = x.shape
    T = B * L
    xt = x.reshape(T, D)
    w_main, w_misc = _split_w_in(w_in)
    g_pre = norm_mix_pre[None, :]
    proj = norm_mm(xt, g_pre, w_main, tm=1024, tn=512)
    misc = norm_mm(xt, g_pre, w_misc, tm=1024, tn=MISC_W)
    proj3 = proj.reshape(B, L, MAIN_W)
    misc3 = misc.reshape(B, L, MISC_W)
    o_hy = hyena_mixer(proj3, hy_conv_w, hy_conv_b, hy_ffn_w1, hy_ffn_b1, hy_sin_freq,
                       hy_ffn_w2, hy_ffn_b2, hy_ffn_w3, hy_skip)
    o_mla = mla_mixer(misc3, mla_q_norm, mla_wq_b, mla_kv_norm, mla_wkv_b).reshape(T, BRANCH_W)
    o_gdn = gdn_mixer(proj3, misc3, gdn_conv_w, gdn_a_log, gdn_dt_bias, gdn_out_norm).reshape(T, BRANCH_W)
    xt = merge_out(xt, proj, o_hy, o_mla, o_gdn, w_branch.astype(BF16), w_out.astype(BF16),
                   norm_mix_post[None, :])
    xt = ffn(xt, norm_ffn_pre[None, :], w_gate.astype(BF16), w_up.astype(BF16), w_down.astype(BF16),
             norm_ffn_post[None, :])
    return xt.reshape(B, L, D)


def kernel(x_prompt, x_sample, norm_mix_pre, norm_mix_post, norm_ffn_pre, norm_ffn_post, w_in,
           hy_conv_w, hy_conv_b, hy_ffn_w1, hy_ffn_b1, hy_sin_freq, hy_ffn_w2, hy_ffn_b2,
           hy_ffn_w3, hy_skip, mla_q_norm, mla_wq_b, mla_kv_norm, mla_wkv_b,
           gdn_conv_w, gdn_a_log, gdn_dt_bias, gdn_out_norm,
           w_branch, w_out, w_gate, w_up, w_down):
    weights = (norm_mix_pre, norm_mix_post, norm_ffn_pre, norm_ffn_post, w_in,
               hy_conv_w, hy_conv_b, hy_ffn_w1, hy_ffn_b1, hy_sin_freq, hy_ffn_w2, hy_ffn_b2,
               hy_ffn_w3, hy_skip, mla_q_norm, mla_wq_b, mla_kv_norm, mla_wkv_b,
               gdn_conv_w, gdn_a_log, gdn_dt_bias, gdn_out_norm,
               w_branch, w_out, w_gate, w_up, w_down)

    def run_trunk(x):
        for layer in range(DEPTH):
            x = trunk_layer(x, *[w[layer] for w in weights])
        return x

    return (run_trunk(x_prompt), run_trunk(x_sample))
```

```python
import functools
import math

import jax
import jax.numpy as jnp
from jax import lax
from jax.experimental import pallas as pl
from jax.experimental.pallas import tpu as pltpu
import numpy as np

F32 = jnp.float32
BF16 = jnp.bfloat16

D_MODEL = 1024
DEPTH = 2
BRANCH_W = 512
N_BRANCH = 3
HY_W = BRANCH_W
HY_ORDER = 2
HY_EMB = 33
HY_FAST_PCT = 0.3
HY_SLOW_PCT = 1.5
HY_TARGET = 1e-2
MLA_HEADS = 4
MLA_NOPE = 128
MLA_ROPE = 64
MLA_V = 128
MLA_Q_LORA = 256
MLA_KV_LORA = 128
ROPE_THETA = 10000.0
GDN_HEADS = 4
GDN_DK = 128
GDN_DV = 128
GDN_CHUNK = 64
NORM_EPS = 1e-6

_OFF_HY = 0
_OFF_MQ = _OFF_HY + (HY_ORDER + 1) * HY_W
_OFF_MKV = _OFF_MQ + MLA_Q_LORA
_OFF_GQKV = _OFF_MKV + MLA_KV_LORA + MLA_ROPE
_OFF_GZ = _OFF_GQKV + GDN_HEADS * (2 * GDN_DK + GDN_DV)
_OFF_GB = _OFF_GZ + GDN_HEADS * GDN_DV
_OFF_GA = _OFF_GB + 2 * GDN_HEADS
_OFF_GATE = _OFF_GA + 2 * GDN_HEADS
_D_IN = _OFF_GATE + N_BRANCH * D_MODEL

MAIN_W = 3072 + 1536 + 1536 + 512
MISC_W = 640

LANES = 128
VMEM_LIMIT_BYTES = 56 * 1024 * 1024


def _cparams(*sem):
    return pltpu.CompilerParams(dimension_semantics=sem, vmem_limit_bytes=VMEM_LIMIT_BYTES)


def _full(shape):
    nd = len(shape)
    return pl.BlockSpec(shape, lambda *_: (0,) * nd)


def _norm_mm_kernel(x_ref, g_ref, w_ref, o_ref, h_ref):
    @pl.when(pl.program_id(1) == 0)
    def _():
        x = x_ref[...]
        r = lax.rsqrt(jnp.mean(x * x, axis=-1, keepdims=True) + NORM_EPS)
        h_ref[...] = (x * r * g_ref[...]).astype(BF16)

    o_ref[...] = jnp.dot(h_ref[...], w_ref[...], preferred_element_type=F32)


def norm_mm(x, g, w, tm, tn):
    T, D = x.shape
    N = w.shape[1]
    tm = min(tm, T)
    return pl.pallas_call(
        _norm_mm_kernel,
        grid=(T // tm, N // tn),
        in_specs=[pl.BlockSpec((tm, D), lambda i, j: (i, 0)),
                  pl.BlockSpec((1, D), lambda i, j: (0, 0)),
                  pl.BlockSpec((D, tn), lambda i, j: (0, j))],
        out_specs=pl.BlockSpec((tm, tn), lambda i, j: (i, j)),
        out_shape=jax.ShapeDtypeStruct((T, N), F32),
        scratch_shapes=[pltpu.VMEM((tm, D), BF16)],
        compiler_params=_cparams("parallel", "arbitrary"),
        name="norm_mm",
    )(x, g, w)


def _dwconv_kernel(x_ref, p_ref, n_ref, w_ref, b_ref, o_ref, *, silu):
    i = pl.program_id(1)
    last = pl.num_programs(1) - 1
    x = x_ref[0]
    tl = x.shape[0]
    prev_row = jnp.where(i > 0, p_ref[0, 7:8, :], 0.0)
    next_row = jnp.where(i < last, n_ref[0, 0:1, :], 0.0)
    rows = lax.broadcasted_iota(jnp.int32, x.shape, 0)
    x_dn = jnp.where(rows == 0, prev_row, pltpu.roll(x, 1, axis=0))
    x_up = jnp.where(rows == tl - 1, next_row, pltpu.roll(x, tl - 1, axis=0))
    out = x_dn * w_ref[0:1, :] + x * w_ref[1:2, :] + x_up * w_ref[2:3, :] + b_ref[...]
    if silu:
        out = out * jax.nn.sigmoid(out)
    o_ref[0] = out


def dwconv3(x, col_blk, cw, w, b, silu, tl=512):
    B, L, _ = x.shape
    tl = min(tl, L)
    nsub = tl // 8
    lastblk = L // 8 - 1
    return pl.pallas_call(
        functools.partial(_dwconv_kernel, silu=silu),
        grid=(B, L // tl),
        in_specs=[pl.BlockSpec((1, tl, cw), lambda b_, i: (b_, i, col_blk)),
                  pl.BlockSpec((1, 8, cw), lambda b_, i: (b_, jnp.maximum(i * nsub - 1, 0), col_blk)),
                  pl.BlockSpec((1, 8, cw), lambda b_, i: (b_, jnp.minimum((i + 1) * nsub, lastblk), col_blk)),
                  _full((3, cw)), _full((1, cw))],
        out_specs=pl.BlockSpec((1, tl, cw), lambda b_, i: (b_, i, 0)),
        out_shape=jax.ShapeDtypeStruct((B, L, cw), F32),
        compiler_params=_cparams("parallel", "parallel"),
        name="dwconv3",
    )(x, x, x, w, b)


def _tile_lanes(t, c):
    return t if c == LANES else jnp.concatenate([t] * (c // LANES), axis=-1)


def _fft_s1_kernel(f_ref, x_ref, twr_ref, twi_ref, o_ref, *, nb, n1):
    c = x_ref.shape[-1]
    for t in range(nb):
        a = jnp.dot(f_ref[...], x_ref[0, t], preferred_element_type=F32)
        ar, ai = a[:n1], a[n1:]
        twr = _tile_lanes(twr_ref[t], c)
        twi = _tile_lanes(twi_ref[t], c)
        o_ref[0, t, :n1, :] = (ar * twr + ai * twi).astype(BF16)
        o_ref[0, t, n1:, :] = (ai * twr - ar * twi).astype(BF16)


def _fft_s2_kernel(f_ref, finv_ref, b_ref, kr_ref, ki_ref, twr_ref, twi_ref, o_ref, *, nb, n2):
    c = b_ref.shape[-1]
    for t in range(nb):
        x = jnp.dot(f_ref[...], b_ref[0, t], preferred_element_type=F32)
        xr, xi = x[:n2], x[n2:]
        kr, ki = kr_ref[t], ki_ref[t]
        y = jnp.concatenate([xr * kr - xi * ki, xr * ki + xi * kr], axis=0).astype(BF16)
        cm = jnp.dot(finv_ref[...], y, preferred_element_type=F32)
        cr, ci = cm[:n2], cm[n2:]
        twr = _tile_lanes(twr_ref[t], c)
        twi = _tile_lanes(twi_ref[t], c)
        o_ref[0, t, :n2, :] = (cr * twr - ci * twi).astype(BF16)
        o_ref[0, t, n2:, :] = (cr * twi + ci * twr).astype(BF16)


def _fft_s3_kernel(g_ref, d_ref, o_ref, *, nb):
    for t in range(nb):
        o_ref[0, t] = jnp.dot(g_ref[...], d_ref[0, t], preferred_element_type=F32)


def _dft_tables(L):
    N = 2 * L
    n2 = 128 if N >= 4096 else 16
    n1 = N // n2
    n1h = n1 // 2
    k1 = np.arange(n1)[:, None]
    m1 = np.arange(n1h)[None, :]
    th = 2.0 * np.pi * ((k1 * m1) % n1) / n1
    c, s = np.cos(th), np.sin(th)
    f1 = np.block([[c, s], [-s, c]])
    g1 = np.block([[c.T, -s.T], [s.T, c.T]]) / N
    a2 = np.arange(n2)
    ph = 2.0 * np.pi * ((a2[:, None] * a2[None, :]) % n2) / n2
    c2, s2 = np.cos(ph), np.sin(ph)
    f2 = np.block([[c2, s2], [-s2, c2]])
    f2inv = f2.T
    as_bf16 = lambda m: jnp.asarray(m, dtype=F32).astype(BF16)
    return n1, n2, as_bf16(f1), as_bf16(g1), as_bf16(f2), as_bf16(f2inv)


def _twiddles(n1, n2):
    N = n1 * n2
    prod = (jnp.arange(n2, dtype=jnp.int32)[:, None] * jnp.arange(n1, dtype=jnp.int32)[None, :]) % N
    ang = prod.astype(F32) * (2.0 * math.pi / N)
    shape = (n2, n1, LANES)
    return (jnp.broadcast_to(jnp.cos(ang)[:, :, None], shape),
            jnp.broadcast_to(jnp.sin(ang)[:, :, None], shape))


def fft_long_conv(z, kspec, order, tabs):
    B, L, C = z.shape
    n1, n2, f1, g1, f2, f2inv, tw1r, tw1i, tw2r, tw2i = tabs
    n1h = n1 // 2
    P = B // 2
    nb1 = min(8, n2)
    nb2 = min(8, n1)
    zt = z.astype(BF16).reshape(P, 2, n1h, n2, C).transpose(0, 3, 1, 2, 4).reshape(P, n2, 2 * n1h, C)
    a = pl.pallas_call(
        functools.partial(_fft_s1_kernel, nb=nb1, n1=n1),
        grid=(n2 // nb1, P),
        in_specs=[_full((2 * n1, 2 * n1h)),
                  pl.BlockSpec((1, nb1, 2 * n1h, C), lambda j, p: (p, j, 0, 0)),
                  pl.BlockSpec((nb1, n1, LANES), lambda j, p: (j, 0, 0)),
                  pl.BlockSpec((nb1, n1, LANES), lambda j, p: (j, 0, 0))],
        out_specs=pl.BlockSpec((1, nb1, 2 * n1, C), lambda j, p: (p, j, 0, 0)),
        out_shape=jax.ShapeDtypeStruct((P, n2, 2 * n1, C), BF16),
        compiler_params=_cparams("parallel", "parallel"),
        name="fft_s1",
    )(f1, zt, tw1r, tw1i)
    bt = a.reshape(P, n2, 2, n1, C).transpose(0, 3, 2, 1, 4).reshape(P, n1, 2 * n2, C)
    d = pl.pallas_call(
        functools.partial(_fft_s2_kernel, nb=nb2, n2=n2),
        grid=(n1 // nb2, P),
        in_specs=[_full((2 * n2, 2 * n2)), _full((2 * n2, 2 * n2)),
                  pl.BlockSpec((1, nb2, 2 * n2, C), lambda j, p: (p, j, 0, 0)),
                  pl.BlockSpec((nb2, n2, C), lambda j, p: (j, 0, order)),
                  pl.BlockSpec((nb2, n2, C), lambda j, p: (j, 1, order)),
                  pl.BlockSpec((nb2, n2, LANES), lambda j, p: (j, 0, 0)),
                  pl.BlockSpec((nb2, n2, LANES), lambda j, p: (j, 0, 0))],
        out_specs=pl.BlockSpec((1, nb2, 2 * n2, C), lambda j, p: (p, j, 0, 0)),
        out_shape=jax.ShapeDtypeStruct((P, n1, 2 * n2, C), BF16),
        compiler_params=_cparams("parallel", "parallel"),
        name="fft_s2",
    )(f2, f2inv, bt, kspec, kspec, tw2r, tw2i)
    dt = d.reshape(P, n1, 2, n2, C).transpose(0, 3, 2, 1, 4).reshape(P, n2, 2 * n1, C)
    y = pl.pallas_call(
        functools.partial(_fft_s3_kernel, nb=nb1),
        grid=(n2 // nb1, P),
        in_specs=[_full((2 * n1h, 2 * n1)),
                  pl.BlockSpec((1, nb1, 2 * n1, C), lambda j, p: (p, j, 0, 0))],
        out_specs=pl.BlockSpec((1, nb1, 2 * n1h, C), lambda j, p: (p, j, 0, 0)),
        out_shape=jax.ShapeDtypeStruct((P, n2, 2 * n1h, C), F32),
        compiler_params=_cparams("parallel", "parallel"),
        name="fft_s3",
    )(g1, dt)
    return y.reshape(P, n2, 2, n1h, C).transpose(0, 2, 3, 1, 4).reshape(B, L, C)


def _hy_gate_kernel(g_ref, c_ref, z_ref, s_ref, o_ref):
    o_ref[...] = g_ref[...] * (c_ref[...] + s_ref[...] * z_ref[...])


def hy_gate(u, gate_blk, conv, zsrc, z_blk, skip, tl=1024):
    T, C = conv.shape
    tl = min(tl, T)
    return pl.pallas_call(
        _hy_gate_kernel,
        grid=(T // tl,),
        in_specs=[pl.BlockSpec((tl, C), lambda i: (i, gate_blk)),
                  pl.BlockSpec((tl, C), lambda i: (i, 0)),
                  pl.BlockSpec((tl, C), lambda i: (i, z_blk)),
                  _full((1, C))],
        out_specs=pl.BlockSpec((tl, C), lambda i: (i, 0)),
        out_shape=jax.ShapeDtypeStruct((T, C), F32),
        compiler_params=_cparams("parallel"),
        name="hy_gate",
    )(u, conv, zsrc, skip)


def hyena_filter_spectrum(L, n1, n2, tw1r, tw1i, w1, b1, freq, w2, b2, w3):
    t = jnp.linspace(0.0, 1.0, L, dtype=F32)[:, None]
    bands = (HY_EMB - 1) // 2
    wpos = (2.0 * math.pi / L) * jnp.arange(L, dtype=F32)[:, None]
    fr = jnp.linspace(1e-4, bands - 1, bands, dtype=F32)[None, :]
    feats = jnp.concatenate([t, jnp.cos(fr * wpos), -jnp.sin(fr * wpos)], axis=-1)
    hp = lax.Precision.HIGHEST
    h = jnp.sin(freq * (jnp.dot(feats, w1, precision=hp) + b1))
    h = jnp.sin(freq * (jnp.dot(h, w2, precision=hp) + b2))
    h = jnp.dot(h, w3, precision=hp).reshape(L, HY_ORDER, 2, HY_W)
    deltas = jnp.abs(jnp.linspace(math.log(HY_TARGET) / HY_SLOW_PCT,
                                  math.log(HY_TARGET) / HY_FAST_PCT, HY_W, dtype=F32))
    h = h * jnp.exp(-t * deltas)[:, None, None, :]
    fwd = h[:, :, 0]
    bwd = h[1:, :, 1][::-1]
    kern = jnp.concatenate([fwd, jnp.zeros((1, HY_ORDER, HY_W), F32), bwd], axis=0)
    kern = kern / jnp.sum(jnp.abs(kern), axis=0, keepdims=True)
    return filter_dft(kern.reshape(2 * L, HY_ORDER * HY_W), n1, n2, tw1r, tw1i)


def _split2(x):
    hi = x.astype(BF16)
    return hi, (x - hi.astype(F32)).astype(BF16)


def _dot3(fh, fl, x):
    xh, xl = _split2(x)
    return (jnp.dot(fh, xh, preferred_element_type=F32) + jnp.dot(fh, xl, preferred_element_type=F32)
            + jnp.dot(fl, xh, preferred_element_type=F32))


def _spec_s1_kernel(fh_ref, fl_ref, x_ref, twr_ref, twi_ref, o_ref, *, nb, n1):
    c = x_ref.shape[-1]
    for t in range(nb):
        a = _dot3(fh_ref[...], fl_ref[...], x_ref[t])
        ar, ai = a[:n1], a[n1:]
        twr = _tile_lanes(twr_ref[t], c)
        twi = _tile_lanes(twi_ref[t], c)
        o_ref[t, :n1, :] = ar * twr + ai * twi
        o_ref[t, n1:, :] = ai * twr - ar * twi


def _spec_s2_kernel(fh_ref, fl_ref, b_ref, o_ref, *, nb):
    for t in range(nb):
        o_ref[t] = _dot3(fh_ref[...], fl_ref[...], b_ref[t])


def filter_dft(kern, n1, n2, tw1r, tw1i):
    N, C = kern.shape
    ct = 512
    k1 = np.arange(n1)
    th = 2.0 * np.pi * ((k1[:, None] * k1[None, :]) % n1) / n1
    f1 = np.concatenate([np.cos(th), -np.sin(th)], axis=0)
    a2 = np.arange(n2)
    ph = 2.0 * np.pi * ((a2[:, None] * a2[None, :]) % n2) / n2
    f2 = np.block([[np.cos(ph), np.sin(ph)], [-np.sin(ph), np.cos(ph)]])

    def hi_lo(m):
        hi = m.astype(BF16)
        lo = (m - hi.astype(np.float64)).astype(BF16)
        return jnp.asarray(hi), jnp.asarray(lo)

    f1h, f1l = hi_lo(f1)
    f2h, f2l = hi_lo(f2)
    nb1 = min(4, n2)
    nb2 = min(4, n1)
    xt = kern.reshape(n1, n2, C).transpose(1, 0, 2)
    a = pl.pallas_call(
        functools.partial(_spec_s1_kernel, nb=nb1, n1=n1),
        grid=(n2 // nb1, C // ct),
        in_specs=[_full((2 * n1, n1)), _full((2 * n1, n1)),
                  pl.BlockSpec((nb1, n1, ct), lambda j, c: (j, 0, c)),
                  pl.BlockSpec((nb1, n1, LANES), lambda j, c: (j, 0, 0)),
                  pl.BlockSpec((nb1, n1, LANES), lambda j, c: (j, 0, 0))],
        out_specs=pl.BlockSpec((nb1, 2 * n1, ct), lambda j, c: (j, 0, c)),
        out_shape=jax.ShapeDtypeStruct((n2, 2 * n1, C), F32),
        compiler_params=_cparams("parallel", "parallel"),
        name="spec_s1",
    )(f1h, f1l, xt, tw1r, tw1i)
    bt = a.reshape(n2, 2, n1, C).transpose(2, 1, 0, 3).reshape(n1, 2 * n2, C)
    return pl.pallas_call(
        functools.partial(_spec_s2_kernel, nb=nb2),
        grid=(n1 // nb2, C // ct),
        in_specs=[_full((2 * n2, 2 * n2)), _full((2 * n2, 2 * n2)),
                  pl.BlockSpec((nb2, 2 * n2, ct), lambda j, c: (j, 0, c))],
        out_specs=pl.BlockSpec((nb2, 2 * n2, ct), lambda j, c: (j, 0, c)),
        out_shape=jax.ShapeDtypeStruct((n1, 2 * n2, C), F32),
        compiler_params=_cparams("parallel", "parallel"),
        name="spec_s2",
    )(f2h, f2l, bt)


def hyena_mixer(proj3, conv_w, conv_b, w1, b1, freq, w2, b2, w3, skip):
    B, L, _ = proj3.shape
    T = B * L
    u = dwconv3(proj3, 2, 1536, conv_w, conv_b[None, :], silu=False).reshape(T, 1536)
    n1, n2, f1, g1, f2, f2inv = _dft_tables(L)
    tw1r, tw1i = _twiddles(n1, n2)
    tw2r, tw2i = tw1r.transpose(1, 0, 2), tw1i.transpose(1, 0, 2)
    tabs = (n1, n2, f1, g1, f2, f2inv, tw1r, tw1i, tw2r, tw2i)
    kspec = hyena_filter_spectrum(L, n1, n2, tw1r, tw1i, w1, b1[None, :], freq[None, :], w2, b2[None, :], w3)
    conv = fft_long_conv(u[:, 1024:1536].reshape(B, L, HY_W), kspec, 0, tabs).reshape(T, HY_W)
    z1 = hy_gate(u, 0, conv, u, 2, skip[0:1])
    conv = fft_long_conv(z1.reshape(B, L, HY_W), kspec, 1, tabs).reshape(T, HY_W)
    return hy_gate(u, 1, conv, z1, 0, skip[1:2])


def _mla_proj_kernel(m_ref, qg_ref, kg_ref, wq_ref, wkv_ref, cq_ref, sq_ref, ck_ref, sk_ref,
                     qn_ref, y1_ref, y2_ref, kn_ref, v_ref, kp_ref):
    m = m_ref[0]
    ql = m[:, 0:256]
    r = lax.rsqrt(jnp.mean(ql * ql, axis=-1, keepdims=True) + NORM_EPS)
    q = jnp.dot((ql * r * qg_ref[...]).astype(BF16), wq_ref[...], preferred_element_type=F32)
    ckv = m[:, 256:384]
    r = lax.rsqrt(jnp.mean(ckv * ckv, axis=-1, keepdims=True) + NORM_EPS)
    kv = jnp.dot((ckv * r * kg_ref[...]).astype(BF16), wkv_ref[...], preferred_element_type=F32)
    x1, x2 = q[:, 512:640], q[:, 640:768]
    cq, sq = cq_ref[...], sq_ref[...]
    qn_ref[0] = q[:, 0:512].astype(BF16)
    y1_ref[0] = (x1 * cq - x2 * sq).astype(BF16)
    y2_ref[0] = (x2 * cq + x1 * sq).astype(BF16)
    kn_ref[0] = kv[:, 0:512].astype(BF16)
    v_ref[0] = kv[:, 512:1024].astype(BF16)
    kp_ref[0] = (m[:, 384:512] * ck_ref[...] + m[:, 512:640] * sk_ref[...]).astype(BF16)


def _attn_kernel(q_ref, k_ref, v_ref, o_ref, m_ref, l_ref, acc_ref, *, scale):
    ki = pl.program_id(3)

    @pl.when(ki == 0)
    def _():
        m_ref[...] = jnp.full(m_ref.shape, -jnp.inf, F32)
        l_ref[...] = jnp.zeros(l_ref.shape, F32)
        acc_ref[...] = jnp.zeros(acc_ref.shape, F32)

    s = lax.dot_general(q_ref[0, 0], k_ref[0, 0], (((1,), (1,)), ((), ())),
                        preferred_element_type=F32) * scale
    m_prev = m_ref[...]
    m_new = jnp.maximum(m_prev, jnp.max(s, axis=-1, keepdims=True))
    alpha = jnp.exp(m_prev - m_new)
    p = jnp.exp(s - m_new)
    l_ref[...] = alpha * l_ref[...] + jnp.sum(p, axis=-1, keepdims=True)
    acc_ref[...] = alpha * acc_ref[...] + jnp.dot(p.astype(BF16), v_ref[0], preferred_element_type=F32)
    m_ref[...] = m_new

    @pl.when(ki == pl.num_programs(3) - 1)
    def _():
        o_ref[0] = acc_ref[...] / l_ref[...]


def _rope_lane_tables(L):
    half = MLA_ROPE // 2
    inv = ROPE_THETA ** (-jnp.arange(half, dtype=F32) / half)
    ang = jnp.arange(L, dtype=F32)[:, None] * inv[None, :]
    cos, sin = jnp.cos(ang), jnp.sin(ang)
    zeros = jnp.zeros((L, LANES - MLA_ROPE), F32)
    cq = jnp.tile(cos, (1, MLA_HEADS))
    sq = jnp.tile(sin, (1, MLA_HEADS))
    ck = jnp.concatenate([cos, cos, zeros], axis=-1)
    sk = jnp.concatenate([-sin, sin, zeros], axis=-1)
    return cq, sq, ck, sk


def mla_mixer(misc3, q_norm, wq_b, kv_norm, wkv_b, tl=512, tq=1024, tk=1024):
    B, L, _ = misc3.shape
    H = MLA_HEADS
    tl, tq, tk = min(tl, L), min(tq, L), min(tk, L)
    wq = wq_b.reshape(MLA_Q_LORA, H, MLA_NOPE + MLA_ROPE)
    half = MLA_ROPE // 2
    wq = jnp.concatenate([wq[:, :, :MLA_NOPE].reshape(MLA_Q_LORA, -1),
                          wq[:, :, MLA_NOPE:MLA_NOPE + half].reshape(MLA_Q_LORA, -1),
                          wq[:, :, MLA_NOPE + half:].reshape(MLA_Q_LORA, -1)], axis=-1).astype(BF16)
    wkv = wkv_b.reshape(MLA_KV_LORA, H, MLA_NOPE + MLA_V)
    wkv = jnp.concatenate([wkv[:, :, :MLA_NOPE].reshape(MLA_KV_LORA, -1),
                           wkv[:, :, MLA_NOPE:].reshape(MLA_KV_LORA, -1)], axis=-1).astype(BF16)
    cq, sq, ck, sk = _rope_lane_tables(L)
    tok = lambda w: pl.BlockSpec((1, tl, w), lambda b, i: (b, i, 0))
    pos = pl.BlockSpec((tl, LANES), lambda b, i: (i, 0))
    sds = lambda w: jax.ShapeDtypeStruct((B, L, w), BF16)
    qn, y1, y2, kn, v, kp = pl.pallas_call(
        _mla_proj_kernel,
        grid=(B, L // tl),
        in_specs=[tok(MISC_W), _full((1, MLA_Q_LORA)), _full((1, MLA_KV_LORA)),
                  _full(wq.shape), _full(wkv.shape), pos, pos, pos, pos],
        out_specs=[tok(512), tok(LANES), tok(LANES), tok(512), tok(512), tok(LANES)],
        out_shape=[sds(512), sds(LANES), sds(LANES), sds(512), sds(512), sds(LANES)],
        compiler_params=_cparams("parallel", "parallel"),
        name="mla_proj",
    )(misc3, q_norm[None, :], kv_norm[None, :], wq, wkv, cq, sq, ck, sk)
    q = jnp.concatenate([qn.reshape(B, L, H, MLA_NOPE), y1.reshape(B, L, H, half),
                         y2.reshape(B, L, H, half)], axis=-1).transpose(0, 2, 1, 3)
    k = jnp.concatenate([kn.reshape(B, L, H, MLA_NOPE),
                         jnp.broadcast_to(kp[:, :, None, :MLA_ROPE], (B, L, H, MLA_ROPE))],
                        axis=-1).transpose(0, 2, 1, 3)
    dqk = MLA_NOPE + MLA_ROPE
    return pl.pallas_call(
        functools.partial(_attn_kernel, scale=dqk ** -0.5),
        grid=(B, H, L // tq, L // tk),
        in_specs=[pl.BlockSpec((1, 1, tq, dqk), lambda b, h, i, j: (b, h, i, 0)),
                  pl.BlockSpec((1, 1, tk, dqk), lambda b, h, i, j: (b, h, j, 0)),
                  pl.BlockSpec((1, tk, MLA_V), lambda b, h, i, j: (b, j, h))],
        out_specs=pl.BlockSpec((1, tq, MLA_V), lambda b, h, i, j: (b, i, h)),
        out_shape=jax.ShapeDtypeStruct((B, L, H * MLA_V), F32),
        scratch_shapes=[pltpu.VMEM((tq, 1), F32), pltpu.VMEM((tq, 1), F32), pltpu.VMEM((tq, MLA_V), F32)],
        compiler_params=_cparams("parallel", "parallel", "parallel", "arbitrary"),
        name="mla_attn",
    )(q, k, v)


_BETA_LANE = 64
_G_LANE = 72
CH = GDN_CHUNK
NDH = 2 * GDN_HEADS


def _gdn_prep_kernel(c_ref, m_ref, alog_ref, dtb_ref, qn_ref, kn_ref, bg_ref):
    for h in range(GDN_HEADS):
        sl = slice(h * GDN_DK, (h + 1) * GDN_DK)
        qh = c_ref[0, :, sl]
        qn_ref[0, :, sl] = (qh * lax.rsqrt(jnp.sum(qh * qh, axis=-1, keepdims=True) + NORM_EPS)
                            * (GDN_DK ** -0.5))
        kh = c_ref[0, :, GDN_HEADS * GDN_DK + h * GDN_DK:GDN_HEADS * GDN_DK + (h + 1) * GDN_DK]
        kn_ref[0, :, sl] = kh * lax.rsqrt(jnp.sum(kh * kh, axis=-1, keepdims=True) + NORM_EPS)
    m = m_ref[0]
    lane = lax.broadcasted_iota(jnp.int32, m.shape, 1)
    beta = jax.nn.sigmoid(m)
    x = m + dtb_ref[...]
    softplus = jnp.maximum(x, 0.0) + jnp.log1p(jnp.exp(-jnp.abs(x)))
    g = -jnp.exp(alog_ref[...]) * softplus
    is_beta = (lane >= _BETA_LANE) & (lane < _BETA_LANE + NDH)
    is_g = (lane >= _G_LANE) & (lane < _G_LANE + NDH)
    bg_ref[0] = jnp.where(is_beta, beta, jnp.where(is_g, g, 0.0))


def _split3(x):
    hi = x.astype(BF16)
    r = x - hi.astype(F32)
    mid = r.astype(BF16)
    lo = (r - mid.astype(F32)).astype(BF16)
    return hi, mid, lo


def _dot_nt(a, b):
    return lax.dot_general(a, b, (((1,), (1,)), ((), ())), preferred_element_type=F32)


def _gdn_chunk_kernel(q_ref, k_ref, bg_ref, a_ref, attn_ref, gc_ref):
    bg = bg_ref[0]
    lane = lax.broadcasted_iota(jnp.int32, bg.shape, 1)
    is_g = (lane >= _G_LANE) & (lane < _G_LANE + NDH)
    ri = lax.broadcasted_iota(jnp.int32, (CH, CH), 0)
    ci = lax.broadcasted_iota(jnp.int32, (CH, CH), 1)
    lower = ri >= ci
    upper = ri <= ci
    pieces = _split3(jnp.where(is_g, bg, 0.0))
    tril = lower.astype(BF16)
    triu = upper.astype(BF16)
    pre = sum(jnp.dot(tril, p, preferred_element_type=F32) for p in pieces)
    suf = sum(jnp.dot(triu, p, preferred_element_type=F32) for p in pieces)
    gc = jnp.where(lane >= _G_LANE + GDN_HEADS, suf, pre)
    gc_ref[0] = gc
    gct = gc.T
    for d in range(2):
        causal = lower if d == 0 else upper
        strict = (ri > ci) if d == 0 else (ri < ci)
        for h in range(GDN_HEADS):
            dh = d * GDN_HEADS + h
            sl = slice(h * GDN_DK, (h + 1) * GDN_DK)
            kh = k_ref[0, :, sl]
            kb = (kh * bg[:, _BETA_LANE + dh:_BETA_LANE + dh + 1]).astype(BF16)
            khb = kh.astype(BF16)
            diff = gc[:, _G_LANE + dh:_G_LANE + dh + 1] - gct[_G_LANE + dh:_G_LANE + dh + 1, :]
            dec = jnp.exp(jnp.where(causal, diff, -jnp.inf))
            a_ref[0, 0, dh] = jnp.where(strict, _dot_nt(kb, khb) * dec, 0.0)
            attn_ref[0, 0, dh] = _dot_nt(q_ref[0, :, sl].astype(BF16), khb) * dec


def _gdn_solve_kernel(a_ref, t_ref, *, nblk_fwd):
    bwd = pl.program_id(0) >= nblk_fwd
    t_ref[...] = jnp.zeros(t_ref.shape, F32)
    sub = lax.broadcasted_iota(jnp.int32, (CH, LANES), 0)

    def row(it, carry):
        i = jnp.where(bwd, CH - 1 - it, it)

        def blk(jb, acc):
            base = pl.multiple_of(jb * 8, 8)
            ablk = a_ref[i, pl.ds(base, 8), :]
            for jj in range(8):
                acc = acc - ablk[jj:jj + 1, :] * t_ref[base + jj]
            return acc

        lo = jnp.where(bwd, i // 8, 0)
        hi = jnp.where(bwd, CH // 8, i // 8 + 1)
        t_ref[i] = lax.fori_loop(lo, hi, blk, (sub == i).astype(F32))
        return carry

    lax.fori_loop(0, CH, row, 0)


def _gdn_scan_kernel(qf_ref, kf_ref, vf_ref, bgf_ref, gcf_ref, tf_ref, af_ref,
                     qb_ref, kb_ref, vb_ref, bgb_ref, gcb_ref, tb_ref, ab_ref,
                     of_ref, ob_ref, s_ref):
    @pl.when(pl.program_id(1) == 0)
    def _():
        s_ref[...] = jnp.zeros(s_ref.shape, F32)

    dirs = ((qf_ref, kf_ref, vf_ref, bgf_ref, gcf_ref, tf_ref, af_ref, of_ref, CH - 1),
            (qb_ref, kb_ref, vb_ref, bgb_ref, gcb_ref, tb_ref, ab_ref, ob_ref, 0))
    for d, (q_ref, k_ref, v_ref, bg_ref, gc_ref, t_ref, a_ref, o_ref, last) in enumerate(dirs):
        for h in range(GDN_HEADS):
            dh = d * GDN_HEADS + h
            sl = slice(h * GDN_DK, (h + 1) * GDN_DK)
            q, k, v = q_ref[0, :, sl], k_ref[0, :, sl], v_ref[0, :, sl]
            beta = bg_ref[0, :, _BETA_LANE + dh:_BETA_LANE + dh + 1]
            gc = gc_ref[0, :, _G_LANE + dh:_G_LANE + dh + 1]
            gl = gc_ref[0, last:last + 1, _G_LANE + dh:_G_LANE + dh + 1]
            egc = jnp.exp(gc)
            kbeta = k * beta
            rhs = jnp.concatenate([v * beta, kbeta * egc], axis=-1).astype(BF16)
            sol = jnp.dot(t_ref[0, 0, h], rhs, preferred_element_type=F32)
            u, w = sol[:, :GDN_DV], sol[:, GDN_DV:]
            s = s_ref[dh]
            sb = s.astype(BF16)
            v_new = u - jnp.dot(w.astype(BF16), sb, preferred_element_type=F32)
            vnb = v_new.astype(BF16)
            o_ref[0, :, sl] = (jnp.dot((q * egc).astype(BF16), sb, preferred_element_type=F32)
                               + jnp.dot(a_ref[0, 0, h].astype(BF16), vnb, preferred_element_type=F32))
            kd = (k * jnp.exp(gl - gc)).astype(BF16)
            s_ref[dh] = s * jnp.exp(gl) + lax.dot_general(kd, vnb, (((0,), (0,)), ((), ())),
                                                           preferred_element_type=F32)


def _gdn_out_kernel(of_ref, ob_ref, z_ref, n_ref, o_ref):
    z = z_ref[...]
    for h in range(GDN_HEADS):
        sl = slice(h * GDN_DV, (h + 1) * GDN_DV)
        o = of_ref[:, sl] + ob_ref[:, sl]
        y = o * lax.rsqrt(jnp.mean(o * o, axis=-1, keepdims=True) + NORM_EPS) * n_ref[...]
        zh = z[:, sl]
        o_ref[:, sl] = y * (zh * jax.nn.sigmoid(zh))


def gdn_mixer(proj3, misc3, conv_w, a_log, dt_bias, out_norm, tl=512):
    B, L, _ = proj3.shape
    T = B * L
    N = L // CH
    H = GDN_HEADS
    tl = min(tl, L)
    qkv = dwconv3(proj3, 3, 1536, conv_w, jnp.zeros((1, 1536), F32), silu=True)
    lane_vec = lambda p: jnp.zeros((1, LANES), F32).at[0, _G_LANE:_G_LANE + NDH].set(p.reshape(-1))
    tok = lambda w, c=0: pl.BlockSpec((1, tl, w), lambda b, i: (b, i, c))
    qn, kn, bg = pl.pallas_call(
        _gdn_prep_kernel,
        grid=(B, L // tl),
        in_specs=[tok(1536), tok(LANES, 3), _full((1, LANES)), _full((1, LANES))],
        out_specs=[tok(512), tok(512), tok(LANES)],
        out_shape=[jax.ShapeDtypeStruct((B, L, 512), F32), jax.ShapeDtypeStruct((B, L, 512), F32),
                   jax.ShapeDtypeStruct((B, L, LANES), F32)],
        compiler_params=_cparams("parallel", "parallel"),
        name="gdn_prep",
    )(qkv, misc3, lane_vec(a_log), lane_vec(dt_bias))

    chunk = lambda w, c=0: pl.BlockSpec((1, CH, w), lambda b, n: (b, n, c))
    mats = pl.BlockSpec((1, 1, NDH, CH, CH), lambda b, n: (b, n, 0, 0, 0))
    mat_shape = jax.ShapeDtypeStruct((B, N, NDH, CH, CH), F32)
    a, attn, gc = pl.pallas_call(
        _gdn_chunk_kernel,
        grid=(B, N),
        in_specs=[chunk(512), chunk(512), chunk(LANES)],
        out_specs=[mats, mats, chunk(LANES)],
        out_shape=[mat_shape, mat_shape, jax.ShapeDtypeStruct((B, L, LANES), F32)],
        compiler_params=_cparams("parallel", "parallel"),
        name="gdn_chunk",
    )(qn, kn, bg)

    P = NDH * B * N
    at = a.transpose(3, 4, 2, 0, 1).reshape(CH, CH, P)
    tt = pl.pallas_call(
        functools.partial(_gdn_solve_kernel, nblk_fwd=P // LANES // 2),
        grid=(P // LANES,),
        in_specs=[pl.BlockSpec((CH, CH, LANES), lambda p: (0, 0, p))],
        out_specs=pl.BlockSpec((CH, CH, LANES), lambda p: (0, 0, p)),
        out_shape=jax.ShapeDtypeStruct((CH, CH, P), F32),
        compiler_params=_cparams("parallel"),
        name="gdn_solve",
    )(at)
    tmat = tt.reshape(CH, CH, NDH, B, N).transpose(3, 4, 2, 0, 1).astype(BF16)

    fwd = lambda w, c=0: pl.BlockSpec((1, CH, w), lambda b, n: (b, n, c))
    bwd = lambda w, c=0: pl.BlockSpec((1, CH, w), lambda b, n: (b, N - 1 - n, c))
    mf = pl.BlockSpec((1, 1, H, CH, CH), lambda b, n: (b, n, 0, 0, 0))
    mb = pl.BlockSpec((1, 1, H, CH, CH), lambda b, n: (b, N - 1 - n, 1, 0, 0))
    o_f, o_b = pl.pallas_call(
        _gdn_scan_kernel,
        grid=(B, N),
        in_specs=[fwd(512), fwd(512), fwd(512, 2), fwd(LANES), fwd(LANES), mf, mf,
                  bwd(512), bwd(512), bwd(512, 2), bwd(LANES), bwd(LANES), mb, mb],
        out_specs=[fwd(512), bwd(512)],
        out_shape=[jax.ShapeDtypeStruct((B, L, 512), F32), jax.ShapeDtypeStruct((B, L, 512), F32)],
        scratch_shapes=[pltpu.VMEM((NDH, GDN_DK, GDN_DV), F32)],
        compiler_params=_cparams("parallel", "arbitrary"),
        name="gdn_scan",
    )(qn, kn, qkv, bg, gc, tmat, attn, qn, kn, qkv, bg, gc, tmat, attn)

    tm = min(1024, T)
    row = lambda c=0: pl.BlockSpec((tm, 512), lambda i: (i, c))
    return pl.pallas_call(
        _gdn_out_kernel,
        grid=(T // tm,),
        in_specs=[row(), row(), row(12), _full((1, GDN_DV))],
        out_specs=row(),
        out_shape=jax.ShapeDtypeStruct((T, 512), F32),
        compiler_params=_cparams("parallel"),
        name="gdn_out",
    )(o_f.reshape(T, 512), o_b.reshape(T, 512), proj3.reshape(T, MAIN_W), out_norm[None, :])


def _merge_kernel(x_ref, g_ref, oh_ref, om_ref, og_ref, wb_ref, wo_ref, n_ref, o_ref):
    merged = None
    for i, b_ref in enumerate((oh_ref, om_ref, og_ref)):
        gate = jax.nn.sigmoid(g_ref[:, i * D_MODEL:(i + 1) * D_MODEL])
        term = gate * jnp.dot(b_ref[...].astype(BF16), wb_ref[i], preferred_element_type=F32)
        merged = term if merged is None else merged + term
    y = jnp.dot(merged.astype(BF16), wo_ref[...], preferred_element_type=F32)
    r = lax.rsqrt(jnp.mean(y * y, axis=-1, keepdims=True) + NORM_EPS)
    o_ref[...] = x_ref[...] + y * r * n_ref[...]


def merge_out(x, proj, o_hy, o_mla, o_gdn, w_branch, w_out, norm_post, tm=512):
    T = x.shape[0]
    tm = min(tm, T)
    row = lambda w: pl.BlockSpec((tm, w), lambda i: (i, 0))
    return pl.pallas_call(
        _merge_kernel,
        grid=(T // tm,),
        in_specs=[row(D_MODEL), row(N_BRANCH * D_MODEL), row(BRANCH_W), row(BRANCH_W), row(BRANCH_W),
                  _full((N_BRANCH, BRANCH_W, D_MODEL)), _full((D_MODEL, D_MODEL)), _full((1, D_MODEL))],
        out_specs=row(D_MODEL),
        out_shape=jax.ShapeDtypeStruct((T, D_MODEL), F32),
        compiler_params=_cparams("parallel"),
        name="merge_out",
    )(x, proj, o_hy, o_mla, o_gdn, w_branch, w_out, norm_post)


def _ffn_kernel(x_ref, gpre_ref, wg_ref, wu_ref, wd_ref, gpost_ref, o_ref, h_ref, acc_ref):
    j = pl.program_id(1)

    @pl.when(j == 0)
    def _():
        x = x_ref[...]
        r = lax.rsqrt(jnp.mean(x * x, axis=-1, keepdims=True) + NORM_EPS)
        h_ref[...] = (x * r * gpre_ref[...]).astype(BF16)

    h = h_ref[...]
    gate = jnp.dot(h, wg_ref[...], preferred_element_type=F32)
    up = jnp.dot(h, wu_ref[...], preferred_element_type=F32)
    part = jnp.dot((gate * jax.nn.sigmoid(gate) * up).astype(BF16), wd_ref[...], preferred_element_type=F32)

    @pl.when(j == 0)
    def _():
        acc_ref[...] = part

    @pl.when(j > 0)
    def _():
        acc_ref[...] += part

    @pl.when(j == pl.num_programs(1) - 1)
    def _():
        f = acc_ref[...]
        r = lax.rsqrt(jnp.mean(f * f, axis=-1, keepdims=True) + NORM_EPS)
        o_ref[...] = x_ref[...] + f * r * gpost_ref[...]


def ffn(x, g_pre, w_gate, w_up, w_down, g_post, tm=512):
    T = x.shape[0]
    dff = w_gate.shape[1]
    tf = dff // 2
    tm = min(tm, T)
    return pl.pallas_call(
        _ffn_kernel,
        grid=(T // tm, dff // tf),
        in_specs=[pl.BlockSpec((tm, D_MODEL), lambda i, j: (i, 0)), _full((1, D_MODEL)),
                  pl.BlockSpec((D_MODEL, tf), lambda i, j: (0, j)),
                  pl.BlockSpec((D_MODEL, tf), lambda i, j: (0, j)),
                  pl.BlockSpec((tf, D_MODEL), lambda i, j: (j, 0)), _full((1, D_MODEL))],
        out_specs=pl.BlockSpec((tm, D_MODEL), lambda i, j: (i, 0)),
        out_shape=jax.ShapeDtypeStruct((T, D_MODEL), F32),
        scratch_shapes=[pltpu.VMEM((tm, D_MODEL), BF16), pltpu.VMEM((tm, D_MODEL), F32)],
        compiler_params=_cparams("parallel", "arbitrary"),
        name="ffn",
    )(x, g_pre, w_gate, w_up, w_down, g_post)


def _split_w_in(w_in):
    cols = lambda off, n: w_in[:, off:off + n]
    main = jnp.concatenate([cols(_OFF_GATE, N_BRANCH * D_MODEL), cols(_OFF_HY, 1536),
                            cols(_OFF_GQKV, 1536), cols(_OFF_GZ, 512)], axis=-1).astype(BF16)
    half = MLA_ROPE // 2
    kpe = _OFF_MKV + MLA_KV_LORA
    zeros = lambda n: jnp.zeros((D_MODEL, n), w_in.dtype)
    misc = jnp.concatenate([cols(_OFF_MQ, MLA_Q_LORA), cols(_OFF_MKV, MLA_KV_LORA),
                            cols(kpe, MLA_ROPE), cols(_OFF_GB, 8), cols(_OFF_GA, 8), zeros(48),
                            cols(kpe + half, half), cols(kpe, half), zeros(64)], axis=-1).astype(BF16)
    return main, misc


def trunk_layer(x, norm_mix_pre, norm_mix_post, norm_ffn_pre, norm_ffn_post, w_in,
                hy_conv_w, hy_conv_b, hy_ffn_w1, hy_ffn_b1, hy_sin_freq, hy_ffn_w2, hy_ffn_b2,
                hy_ffn_w3, hy_skip, mla_q_norm, mla_wq_b, mla_kv_norm, mla_wkv_b,
                gdn_conv_w, gdn_a_log, gdn_dt_bias, gdn_out_norm,
                w_branch, w_out, w_gate, w_up, w_down):
    B, L, D = x.shape
    T = B * L
    xt = x.reshape(T, D)
    w_main, w_misc = _split_w_in(w_in)
    g_pre = norm_mix_pre[None, :]
    proj = norm_mm(xt, g_pre, w_main, tm=1024, tn=512)
    misc = norm_mm(xt, g_pre, w_misc, tm=1024, tn=MISC_W)
    proj3 = proj.reshape(B, L, MAIN_W)
    misc3 = misc.reshape(B, L, MISC_W)
    o_hy = hyena_mixer(proj3, hy_conv_w, hy_conv_b, hy_ffn_w1, hy_ffn_b1, hy_sin_freq,
                       hy_ffn_w2, hy_ffn_b2, hy_ffn_w3, hy_skip)
    o_mla = mla_mixer(misc3, mla_q_norm, mla_wq_b, mla_kv_norm, mla_wkv_b).reshape(T, BRANCH_W)
    o_gdn = gdn_mixer(proj3, misc3, gdn_conv_w, gdn_a_log, gdn_dt_bias, gdn_out_norm).reshape(T, BRANCH_W)
    xt = merge_out(xt, proj, o_hy, o_mla, o_gdn, w_branch.astype(BF16), w_out.astype(BF16),
                   norm_mix_post[None, :])
    xt = ffn(xt, norm_ffn_pre[None, :], w_gate.astype(BF16), w_up.astype(BF16), w_down.astype(BF16),
             norm_ffn_post[None, :])
    return xt.reshape(B, L, D)


def kernel(x_prompt, x_sample, norm_mix_pre, norm_mix_post, norm_ffn_pre, norm_ffn_post, w_in,
           hy_conv_w, hy_conv_b, hy_ffn_w1, hy_ffn_b1, hy_sin_freq, hy_ffn_w2, hy_ffn_b2,
           hy_ffn_w3, hy_skip, mla_q_norm, mla_wq_b, mla_kv_norm, mla_wkv_b,
           gdn_conv_w, gdn_a_log, gdn_dt_bias, gdn_out_norm,
           w_branch, w_out, w_gate, w_up, w_down):
    weights = (norm_mix_pre, norm_mix_post, norm_ffn_pre, norm_ffn_post, w_in,
               hy_conv_w, hy_conv_b, hy_ffn_w1, hy_ffn_b1, hy_sin_freq, hy_ffn_w2, hy_ffn_b2,
               hy_ffn_w3, hy_skip, mla_q_norm, mla_wq_b, mla_kv_norm, mla_wkv_b,
               gdn_conv_w, gdn_a_log, gdn_dt_bias, gdn_out_norm,
               w_branch, w_out, w_gate, w_up, w_down)

    def run_trunk(x):
        for layer in range(DEPTH):
            x = trunk_layer(x, *[w[layer] for w in weights])
        return x

    return (run_trunk(x_prompt), run_trunk(x_sample))
```

```python
import functools
import math

import jax
import jax.numpy as jnp
from jax import lax
from jax.experimental import pallas as pl
from jax.experimental.pallas import tpu as pltpu
import numpy as np

F32 = jnp.float32
BF16 = jnp.bfloat16

D_MODEL = 1024
DEPTH = 2
BRANCH_W = 512
N_BRANCH = 3
HY_W = BRANCH_W
HY_ORDER = 2
HY_EMB = 33
HY_FAST_PCT = 0.3
HY_SLOW_PCT = 1.5
HY_TARGET = 1e-2
MLA_HEADS = 4
MLA_NOPE = 128
MLA_ROPE = 64
MLA_V = 128
MLA_Q_LORA = 256
MLA_KV_LORA = 128
ROPE_THETA = 10000.0
GDN_HEADS = 4
GDN_DK = 128
GDN_DV = 128
GDN_CHUNK = 64
NORM_EPS = 1e-6

_OFF_HY = 0
_OFF_MQ = _OFF_HY + (HY_ORDER + 1) * HY_W
_OFF_MKV = _OFF_MQ + MLA_Q_LORA
_OFF_GQKV = _OFF_MKV + MLA_KV_LORA + MLA_ROPE
_OFF_GZ = _OFF_GQKV + GDN_HEADS * (2 * GDN_DK + GDN_DV)
_OFF_GB = _OFF_GZ + GDN_HEADS * GDN_DV
_OFF_GA = _OFF_GB + 2 * GDN_HEADS
_OFF_GATE = _OFF_GA + 2 * GDN_HEADS
_D_IN = _OFF_GATE + N_BRANCH * D_MODEL

MAIN_W = 3072 + 1536 + 1536 + 512
MISC_W = 640

LANES = 128
VMEM_LIMIT_BYTES = 56 * 1024 * 1024


def _cparams(*sem):
    return pltpu.CompilerParams(dimension_semantics=sem, vmem_limit_bytes=VMEM_LIMIT_BYTES)


def _full(shape):
    nd = len(shape)
    return pl.BlockSpec(shape, lambda *_: (0,) * nd)


def _norm_mm_kernel(x_ref, g_ref, w_ref, o_ref, h_ref):
    @pl.when(pl.program_id(1) == 0)
    def _():
        x = x_ref[...]
        r = lax.rsqrt(jnp.mean(x * x, axis=-1, keepdims=True) + NORM_EPS)
        h_ref[...] = (x * r * g_ref[...]).astype(BF16)

    o_ref[...] = jnp.dot(h_ref[...], w_ref[...], preferred_element_type=F32)


def norm_mm(x, g, w, tm, tn):
    T, D = x.shape
    N = w.shape[1]
    tm = min(tm, T)
    return pl.pallas_call(
        _norm_mm_kernel,
        grid=(T // tm, N // tn),
        in_specs=[pl.BlockSpec((tm, D), lambda i, j: (i, 0)),
                  pl.BlockSpec((1, D), lambda i, j: (0, 0)),
                  pl.BlockSpec((D, tn), lambda i, j: (0, j))],
        out_specs=pl.BlockSpec((tm, tn), lambda i, j: (i, j)),
        out_shape=jax.ShapeDtypeStruct((T, N), F32),
        scratch_shapes=[pltpu.VMEM((tm, D), BF16)],
        compiler_params=_cparams("parallel", "arbitrary"),
        name="norm_mm",
    )(x, g, w)


def _dwconv_kernel(x_ref, p_ref, n_ref, w_ref, b_ref, o_ref, *, silu):
    i = pl.program_id(1)
    last = pl.num_programs(1) - 1
    x = x_ref[0]
    tl = x.shape[0]
    prev_row = jnp.where(i > 0, p_ref[0, 7:8, :], 0.0)
    next_row = jnp.where(i < last, n_ref[0, 0:1, :], 0.0)
    rows = lax.broadcasted_iota(jnp.int32, x.shape, 0)
    x_dn = jnp.where(rows == 0, prev_row, pltpu.roll(x, 1, axis=0))
    x_up = jnp.where(rows == tl - 1, next_row, pltpu.roll(x, tl - 1, axis=0))
    out = x_dn * w_ref[0:1, :] + x * w_ref[1:2, :] + x_up * w_ref[2:3, :] + b_ref[...]
    if silu:
        out = out * jax.nn.sigmoid(out)
    o_ref[0] = out


def dwconv3(x, col_blk, cw, w, b, silu, tl=512):
    B, L, _ = x.shape
    tl = min(tl, L)
    nsub = tl // 8
    lastblk = L // 8 - 1
    return pl.pallas_call(
        functools.partial(_dwconv_kernel, silu=silu),
        grid=(B, L // tl),
        in_specs=[pl.BlockSpec((1, tl, cw), lambda b_, i: (b_, i, col_blk)),
                  pl.BlockSpec((1, 8, cw), lambda b_, i: (b_, jnp.maximum(i * nsub - 1, 0), col_blk)),
                  pl.BlockSpec((1, 8, cw), lambda b_, i: (b_, jnp.minimum((i + 1) * nsub, lastblk), col_blk)),
                  _full((3, cw)), _full((1, cw))],
        out_specs=pl.BlockSpec((1, tl, cw), lambda b_, i: (b_, i, 0)),
        out_shape=jax.ShapeDtypeStruct((B, L, cw), F32),
        compiler_params=_cparams("parallel", "parallel"),
        name="dwconv3",
    )(x, x, x, w, b)


def _tile_lanes(t, c):
    return t if c == LANES else jnp.concatenate([t] * (c // LANES), axis=-1)


def _fft_s1_kernel(f_ref, x_ref, twr_ref, twi_ref, o_ref, *, nb, n1):
    c = x_ref.shape[-1]
    for t in range(nb):
        a = jnp.dot(f_ref[...], x_ref[0, t], preferred_element_type=F32)
        ar, ai = a[:n1], a[n1:]
        twr = _tile_lanes(twr_ref[t], c)
        twi = _tile_lanes(twi_ref[t], c)
        o_ref[0, t, :n1, :] = (ar * twr + ai * twi).astype(BF16)
        o_ref[0, t, n1:, :] = (ai * twr - ar * twi).astype(BF16)


def _fft_s2_kernel(f_ref, finv_ref, b_ref, kr_ref, ki_ref, twr_ref, twi_ref, o_ref, *, nb, n2):
    c = b_ref.shape[-1]
    for t in range(nb):
        x = jnp.dot(f_ref[...], b_ref[0, t], preferred_element_type=F32)
        xr, xi = x[:n2], x[n2:]
        kr, ki = kr_ref[t], ki_ref[t]
        y = jnp.concatenate([xr * kr - xi * ki, xr * ki + xi * kr], axis=0).astype(BF16)
        cm = jnp.dot(finv_ref[...], y, preferred_element_type=F32)
        cr, ci = cm[:n2], cm[n2:]
        twr = _tile_lanes(twr_ref[t], c)
        twi = _tile_lanes(twi_ref[t], c)
        o_ref[0, t, :n2, :] = (cr * twr - ci * twi).astype(BF16)
        o_ref[0, t, n2:, :] = (cr * twi + ci * twr).astype(BF16)


def _fft_s3_kernel(g_ref, d_ref, o_ref, *, nb):
    for t in range(nb):
        o_ref[0, t] = jnp.dot(g_ref[...], d_ref[0, t], preferred_element_type=F32)


def _dft_tables(L):
    N = 2 * L
    n2 = 128 if N >= 4096 else 16
    n1 = N // n2
    n1h = n1 // 2
    k1 = np.arange(n1)[:, None]
    m1 = np.arange(n1h)[None, :]
    th = 2.0 * np.pi * ((k1 * m1) % n1) / n1
    c, s = np.cos(th), np.sin(th)
    f1 = np.block([[c, s], [-s, c]])
    g1 = np.block([[c.T, -s.T], [s.T, c.T]]) / N
    a2 = np.arange(n2)
    ph = 2.0 * np.pi * ((a2[:, None] * a2[None, :]) % n2) / n2
    c2, s2 = np.cos(ph), np.sin(ph)
    f2 = np.block([[c2, s2], [-s2, c2]])
    f2inv = f2.T
    as_bf16 = lambda m: jnp.asarray(m, dtype=F32).astype(BF16)
    return n1, n2, as_bf16(f1), as_bf16(g1), as_bf16(f2), as_bf16(f2inv)


def _twiddles(n1, n2):
    N = n1 * n2
    prod = (jnp.arange(n2, dtype=jnp.int32)[:, None] * jnp.arange(n1, dtype=jnp.int32)[None, :]) % N
    ang = prod.astype(F32) * (2.0 * math.pi / N)
    shape = (n2, n1, LANES)
    return (jnp.broadcast_to(jnp.cos(ang)[:, :, None], shape),
            jnp.broadcast_to(jnp.sin(ang)[:, :, None], shape))


def fft_long_conv(z, kspec, order, tabs):
    B, L, C = z.shape
    n1, n2, f1, g1, f2, f2inv, tw1r, tw1i, tw2r, tw2i = tabs
    n1h = n1 // 2
    P = B // 2
    nb1 = min(8, n2)
    nb2 = min(8, n1)
    zt = z.astype(BF16).reshape(P, 2, n1h, n2, C).transpose(0, 3, 1, 2, 4).reshape(P, n2, 2 * n1h, C)
    a = pl.pallas_call(
        functools.partial(_fft_s1_kernel, nb=nb1, n1=n1),
        grid=(n2 // nb1, P),
        in_specs=[_full((2 * n1, 2 * n1h)),
                  pl.BlockSpec((1, nb1, 2 * n1h, C), lambda j, p: (p, j, 0, 0)),
                  pl.BlockSpec((nb1, n1, LANES), lambda j, p: (j, 0, 0)),
                  pl.BlockSpec((nb1, n1, LANES), lambda j, p: (j, 0, 0))],
        out_specs=pl.BlockSpec((1, nb1, 2 * n1, C), lambda j, p: (p, j, 0, 0)),
        out_shape=jax.ShapeDtypeStruct((P, n2, 2 * n1, C), BF16),
        compiler_params=_cparams("parallel", "parallel"),
        name="fft_s1",
    )(f1, zt, tw1r, tw1i)
    bt = a.reshape(P, n2, 2, n1, C).transpose(0, 3, 2, 1, 4).reshape(P, n1, 2 * n2, C)
    d = pl.pallas_call(
        functools.partial(_fft_s2_kernel, nb=nb2, n2=n2),
        grid=(n1 // nb2, P),
        in_specs=[_full((2 * n2, 2 * n2)), _full((2 * n2, 2 * n2)),
                  pl.BlockSpec((1, nb2, 2 * n2, C), lambda j, p: (p, j, 0, 0)),
                  pl.BlockSpec((nb2, n2, C), lambda j, p: (j, 0, order)),
                  pl.BlockSpec((nb2, n2, C), lambda j, p: (j, 1, order)),
                  pl.BlockSpec((nb2, n2, LANES), lambda j, p: (j, 0, 0)),
                  pl.BlockSpec((nb2, n2, LANES), lambda j, p: (j, 0, 0))],
        out_specs=pl.BlockSpec((1, nb2, 2 * n2, C), lambda j, p: (p, j, 0, 0)),
        out_shape=jax.ShapeDtypeStruct((P, n1, 2 * n2, C), BF16),
        compiler_params=_cparams("parallel", "parallel"),
        name="fft_s2",
    )(f2, f2inv, bt, kspec, kspec, tw2r, tw2i)
    dt = d.reshape(P, n1, 2, n2, C).transpose(0, 3, 2, 1, 4).reshape(P, n2, 2 * n1, C)
    y = pl.pallas_call(
        functools.partial(_fft_s3_kernel, nb=nb1),
        grid=(n2 // nb1, P),
        in_specs=[_full((2 * n1h, 2 * n1)),
                  pl.BlockSpec((1, nb1, 2 * n1, C), lambda j, p: (p, j, 0, 0))],
        out_specs=pl.BlockSpec((1, nb1, 2 * n1h, C), lambda j, p: (p, j, 0, 0)),
        out_shape=jax.ShapeDtypeStruct((P, n2, 2 * n1h, C), F32),
        compiler_params=_cparams("parallel", "parallel"),
        name="fft_s3",
    )(g1, dt)
    return y.reshape(P, n2, 2, n1h, C).transpose(0, 2, 3, 1, 4).reshape(B, L, C)


def _hy_gate_kernel(g_ref, c_ref, z_ref, s_ref, o_ref):
    o_ref[...] = g_ref[...] * (c_ref[...] + s_ref[...] * z_ref[...])


def hy_gate(u, gate_blk, conv, zsrc, z_blk, skip, tl=1024):
    T, C = conv.shape
    tl = min(tl, T)
    return pl.pallas_call(
        _hy_gate_kernel,
        grid=(T // tl,),
        in_specs=[pl.BlockSpec((tl, C), lambda i: (i, gate_blk)),
                  pl.BlockSpec((tl, C), lambda i: (i, 0)),
                  pl.BlockSpec((tl, C), lambda i: (i, z_blk)),
                  _full((1, C))],
        out_specs=pl.BlockSpec((tl, C), lambda i: (i, 0)),
        out_shape=jax.ShapeDtypeStruct((T, C), F32),
        compiler_params=_cparams("parallel"),
        name="hy_gate",
    )(u, conv, zsrc, skip)


def hyena_filter_spectrum(L, n1, n2, tw1r, tw1i, w1, b1, freq, w2, b2, w3):
    t = jnp.linspace(0.0, 1.0, L, dtype=F32)[:, None]
    bands = (HY_EMB - 1) // 2
    wpos = (2.0 * math.pi / L) * jnp.arange(L, dtype=F32)[:, None]
    fr = jnp.linspace(1e-4, bands - 1, bands, dtype=F32)[None, :]
    feats = jnp.concatenate([t, jnp.cos(fr * wpos), -jnp.sin(fr * wpos)], axis=-1)
    hp = lax.Precision.HIGHEST
    deltas = jnp.abs(jnp.linspace(math.log(HY_TARGET) / HY_SLOW_PCT,
                                  math.log(HY_TARGET) / HY_FAST_PCT, HY_W, dtype=F32))
    w3 = w3.reshape(w3.shape[0], HY_ORDER, 2, HY_W)

    def taps(ft, tt, direction):
        h = jnp.sin(freq * (jnp.dot(ft, w1, precision=hp) + b1))
        h = jnp.sin(freq * (jnp.dot(h, w2, precision=hp) + b2))
        h = jnp.dot(h, w3[:, :, direction].reshape(w3.shape[0], -1), precision=hp)
        return h.reshape(-1, HY_ORDER, HY_W) * jnp.exp(-tt * deltas)[:, None, :]

    fwd = taps(feats, t, 0)
    bwd = taps(feats[:0:-1], t[:0:-1], 1)
    kern = jnp.concatenate([fwd, jnp.zeros((1, HY_ORDER, HY_W), F32), bwd], axis=0)
    kern = kern / jnp.sum(jnp.abs(kern), axis=0, keepdims=True)
    return filter_dft(kern.reshape(2 * L, HY_ORDER * HY_W), n1, n2, tw1r, tw1i)


def _split2(x):
    hi = x.astype(BF16)
    return hi, (x - hi.astype(F32)).astype(BF16)


def _dot3(fh, fl, x):
    xh, xl = _split2(x)
    return (jnp.dot(fh, xh, preferred_element_type=F32) + jnp.dot(fh, xl, preferred_element_type=F32)
            + jnp.dot(fl, xh, preferred_element_type=F32))


def _spec_s1_kernel(fh_ref, fl_ref, x_ref, twr_ref, twi_ref, o_ref, *, nb, n1):
    c = x_ref.shape[-1]
    for t in range(nb):
        a = _dot3(fh_ref[...], fl_ref[...], x_ref[t])
        ar, ai = a[:n1], a[n1:]
        twr = _tile_lanes(twr_ref[t], c)
        twi = _tile_lanes(twi_ref[t], c)
        o_ref[t, :n1, :] = ar * twr + ai * twi
        o_ref[t, n1:, :] = ai * twr - ar * twi


def _spec_s2_kernel(fh_ref, fl_ref, b_ref, o_ref, *, nb):
    for t in range(nb):
        o_ref[t] = _dot3(fh_ref[...], fl_ref[...], b_ref[t])


def filter_dft(kern, n1, n2, tw1r, tw1i):
    N, C = kern.shape
    ct = 512
    k1 = np.arange(n1)
    th = 2.0 * np.pi * ((k1[:, None] * k1[None, :]) % n1) / n1
    f1 = np.concatenate([np.cos(th), -np.sin(th)], axis=0)
    a2 = np.arange(n2)
    ph = 2.0 * np.pi * ((a2[:, None] * a2[None, :]) % n2) / n2
    f2 = np.block([[np.cos(ph), np.sin(ph)], [-np.sin(ph), np.cos(ph)]])

    def hi_lo(m):
        hi = m.astype(BF16)
        lo = (m - hi.astype(np.float64)).astype(BF16)
        return jnp.asarray(hi), jnp.asarray(lo)

    f1h, f1l = hi_lo(f1)
    f2h, f2l = hi_lo(f2)
    nb1 = min(4, n2)
    nb2 = min(4, n1)
    xt = kern.reshape(n1, n2, C).transpose(1, 0, 2)
    a = pl.pallas_call(
        functools.partial(_spec_s1_kernel, nb=nb1, n1=n1),
        grid=(n2 // nb1, C // ct),
        in_specs=[_full((2 * n1, n1)), _full((2 * n1, n1)),
                  pl.BlockSpec((nb1, n1, ct), lambda j, c: (j, 0, c)),
                  pl.BlockSpec((nb1, n1, LANES), lambda j, c: (j, 0, 0)),
                  pl.BlockSpec((nb1, n1, LANES), lambda j, c: (j, 0, 0))],
        out_specs=pl.BlockSpec((nb1, 2 * n1, ct), lambda j, c: (j, 0, c)),
        out_shape=jax.ShapeDtypeStruct((n2, 2 * n1, C), F32),
        compiler_params=_cparams("parallel", "parallel"),
        name="spec_s1",
    )(f1h, f1l, xt, tw1r, tw1i)
    bt = a.reshape(n2, 2, n1, C).transpose(2, 1, 0, 3).reshape(n1, 2 * n2, C)
    return pl.pallas_call(
        functools.partial(_spec_s2_kernel, nb=nb2),
        grid=(n1 // nb2, C // ct),
        in_specs=[_full((2 * n2, 2 * n2)), _full((2 * n2, 2 * n2)),
                  pl.BlockSpec((nb2, 2 * n2, ct), lambda j, c: (j, 0, c))],
        out_specs=pl.BlockSpec((nb2, 2 * n2, ct), lambda j, c: (j, 0, c)),
        out_shape=jax.ShapeDtypeStruct((n1, 2 * n2, C), F32),
        compiler_params=_cparams("parallel", "parallel"),
        name="spec_s2",
    )(f2h, f2l, bt)


def hyena_mixer(proj3, conv_w, conv_b, w1, b1, freq, w2, b2, w3, skip):
    B, L, _ = proj3.shape
    T = B * L
    u = dwconv3(proj3, 2, 1536, conv_w, conv_b[None, :], silu=False).reshape(T, 1536)
    n1, n2, f1, g1, f2, f2inv = _dft_tables(L)
    tw1r, tw1i = _twiddles(n1, n2)
    tw2r, tw2i = tw1r.transpose(1, 0, 2), tw1i.transpose(1, 0, 2)
    tabs = (n1, n2, f1, g1, f2, f2inv, tw1r, tw1i, tw2r, tw2i)
    kspec = hyena_filter_spectrum(L, n1, n2, tw1r, tw1i, w1, b1[None, :], freq[None, :], w2, b2[None, :], w3)
    conv = fft_long_conv(u[:, 1024:1536].reshape(B, L, HY_W), kspec, 0, tabs).reshape(T, HY_W)
    z1 = hy_gate(u, 0, conv, u, 2, skip[0:1])
    conv = fft_long_conv(z1.reshape(B, L, HY_W), kspec, 1, tabs).reshape(T, HY_W)
    return hy_gate(u, 1, conv, z1, 0, skip[1:2])


def _mla_proj_kernel(m_ref, qg_ref, kg_ref, wq_ref, wkv_ref, cq_ref, sq_ref, ck_ref, sk_ref,
                     q_ref, k_ref, v_ref):
    m = m_ref[0]
    ql = m[:, 0:256]
    r = lax.rsqrt(jnp.mean(ql * ql, axis=-1, keepdims=True) + NORM_EPS)
    q = jnp.dot((ql * r * qg_ref[...]).astype(BF16), wq_ref[...], preferred_element_type=F32)
    ckv = m[:, 256:384]
    r = lax.rsqrt(jnp.mean(ckv * ckv, axis=-1, keepdims=True) + NORM_EPS)
    kv = jnp.dot((ckv * r * kg_ref[...]).astype(BF16), wkv_ref[...], preferred_element_type=F32)
    q_pe = (q[:, 512:1024] * cq_ref[...] + q[:, 1024:1536] * sq_ref[...]).astype(BF16)
    k_pe = (m[:, 384:512] * ck_ref[...] + m[:, 512:640] * sk_ref[...]).astype(BF16)
    for h in range(MLA_HEADS):
        sl = slice(h * LANES, (h + 1) * LANES)
        q_ref[0, :, 2 * h * LANES:(2 * h + 1) * LANES] = q[:, sl].astype(BF16)
        q_ref[0, :, (2 * h + 1) * LANES:(2 * h + 2) * LANES] = q_pe[:, sl]
        k_ref[0, :, 2 * h * LANES:(2 * h + 1) * LANES] = kv[:, sl].astype(BF16)
        k_ref[0, :, (2 * h + 1) * LANES:(2 * h + 2) * LANES] = k_pe
        v_ref[0, :, 2 * h * LANES:(2 * h + 1) * LANES] = kv[:, 512 + h * MLA_V:512 + (h + 1) * MLA_V].astype(BF16)
        v_ref[0, :, (2 * h + 1) * LANES:(2 * h + 2) * LANES] = jnp.ones((m.shape[0], LANES), BF16)


def _attn_kernel(q_ref, k_ref, v_ref, o_ref, *scratch, scale):
    ki = pl.program_id(2)
    c = scale * math.log2(math.e)
    dqk = 2 * LANES
    m_refs, acc_refs = scratch[:MLA_HEADS], scratch[MLA_HEADS:]

    @pl.when(ki == 0)
    def _():
        for m_ref, acc_ref in zip(m_refs, acc_refs):
            m_ref[...] = jnp.full(m_ref.shape, -jnp.inf, F32)
            acc_ref[...] = jnp.zeros(acc_ref.shape, F32)

    for h, (m_ref, acc_ref) in enumerate(zip(m_refs, acc_refs)):
        sl = slice(h * dqk, (h + 1) * dqk)
        s = lax.dot_general(q_ref[0, :, sl], k_ref[0, :, sl], (((1,), (1,)), ((), ())),
                            preferred_element_type=F32)
        m_prev = m_ref[...]
        m_new = jnp.maximum(m_prev, jnp.max(s, axis=-1, keepdims=True) * c)
        p = jnp.exp2(s * c - m_new).astype(BF16)
        acc_ref[...] = (jnp.exp2(m_prev - m_new) * acc_ref[...]
                        + jnp.dot(p, v_ref[0, :, sl], preferred_element_type=F32))
        m_ref[...] = m_new

    @pl.when(ki == pl.num_programs(2) - 1)
    def _():
        for h, acc_ref in enumerate(acc_refs):
            o_ref[0, :, h * MLA_V:(h + 1) * MLA_V] = acc_ref[:, :MLA_V] / acc_ref[:, MLA_V:]


def _rope_lane_tables(L):
    half = MLA_ROPE // 2
    inv = ROPE_THETA ** (-jnp.arange(half, dtype=F32) / half)
    ang = jnp.arange(L, dtype=F32)[:, None] * inv[None, :]
    cos, sin = jnp.cos(ang), jnp.sin(ang)
    zeros = jnp.zeros((L, LANES - MLA_ROPE), F32)
    ck = jnp.concatenate([cos, cos, zeros], axis=-1)
    sk = jnp.concatenate([-sin, sin, zeros], axis=-1)
    return jnp.tile(ck, (1, MLA_HEADS)), jnp.tile(sk, (1, MLA_HEADS)), ck, sk


def mla_mixer(misc3, q_norm, wq_b, kv_norm, wkv_b, tl=512, tq=1024, tk=1024):
    B, L, _ = misc3.shape
    H = MLA_HEADS
    tl, tq, tk = min(tl, L), min(tq, L), min(tk, L)
    half = MLA_ROPE // 2
    wq = wq_b.reshape(MLA_Q_LORA, H, MLA_NOPE + MLA_ROPE)
    x1 = wq[:, :, MLA_NOPE:MLA_NOPE + half]
    x2 = wq[:, :, MLA_NOPE + half:]
    zpad = jnp.zeros((MLA_Q_LORA, H, LANES - MLA_ROPE), wq.dtype)
    wq = jnp.concatenate([wq[:, :, :MLA_NOPE].reshape(MLA_Q_LORA, -1),
                          jnp.concatenate([x1, x2, zpad], axis=-1).reshape(MLA_Q_LORA, -1),
                          jnp.concatenate([x2, x1, zpad], axis=-1).reshape(MLA_Q_LORA, -1)],
                         axis=-1).astype(BF16)
    wkv = wkv_b.reshape(MLA_KV_LORA, H, MLA_NOPE + MLA_V)
    wkv = jnp.concatenate([wkv[:, :, :MLA_NOPE].reshape(MLA_KV_LORA, -1),
                           wkv[:, :, MLA_NOPE:].reshape(MLA_KV_LORA, -1)], axis=-1).astype(BF16)
    cq, sq, ck, sk = _rope_lane_tables(L)
    tok = lambda w: pl.BlockSpec((1, tl, w), lambda b, i: (b, i, 0))
    pos = lambda w: pl.BlockSpec((tl, w), lambda b, i: (i, 0))
    sds = lambda w: jax.ShapeDtypeStruct((B, L, w), BF16)
    dqk = 2 * LANES
    q, k, v = pl.pallas_call(
        _mla_proj_kernel,
        grid=(B, L // tl),
        in_specs=[tok(MISC_W), _full((1, MLA_Q_LORA)), _full((1, MLA_KV_LORA)),
                  _full(wq.shape), _full(wkv.shape), pos(H * LANES), pos(H * LANES), pos(LANES), pos(LANES)],
        out_specs=[tok(H * dqk), tok(H * dqk), tok(H * dqk)],
        out_shape=[sds(H * dqk), sds(H * dqk), sds(H * dqk)],
        compiler_params=_cparams("parallel", "parallel"),
        name="mla_proj",
    )(misc3, q_norm[None, :], kv_norm[None, :], wq, wkv, cq, sq, ck, sk)
    return pl.pallas_call(
        functools.partial(_attn_kernel, scale=(MLA_NOPE + MLA_ROPE) ** -0.5),
        grid=(B, L // tq, L // tk),
        in_specs=[pl.BlockSpec((1, tq, H * dqk), lambda b, i, j: (b, i, 0)),
                  pl.BlockSpec((1, tk, H * dqk), lambda b, i, j: (b, j, 0)),
                  pl.BlockSpec((1, tk, H * dqk), lambda b, i, j: (b, j, 0))],
        out_specs=pl.BlockSpec((1, tq, H * MLA_V), lambda b, i, j: (b, i, 0)),
        out_shape=jax.ShapeDtypeStruct((B, L, H * MLA_V), F32),
        scratch_shapes=([pltpu.VMEM((tq, 1), F32)] * H + [pltpu.VMEM((tq, 2 * MLA_V), F32)] * H),
        compiler_params=_cparams("parallel", "parallel", "arbitrary"),
        name="mla_attn",
    )(q, k, v)


_BETA_LANE = 64
_G_LANE = 72
CH = GDN_CHUNK
NDH = 2 * GDN_HEADS


def _gdn_prep_kernel(c_ref, m_ref, alog_ref, dtb_ref, qn_ref, kn_ref, bg_ref):
    for h in range(GDN_HEADS):
        sl = slice(h * GDN_DK, (h + 1) * GDN_DK)
        qh = c_ref[0, :, sl]
        qn_ref[0, :, sl] = (qh * lax.rsqrt(jnp.sum(qh * qh, axis=-1, keepdims=True) + NORM_EPS)
                            * (GDN_DK ** -0.5))
        kh = c_ref[0, :, GDN_HEADS * GDN_DK + h * GDN_DK:GDN_HEADS * GDN_DK + (h + 1) * GDN_DK]
        kn_ref[0, :, sl] = kh * lax.rsqrt(jnp.sum(kh * kh, axis=-1, keepdims=True) + NORM_EPS)
    m = m_ref[0]
    lane = lax.broadcasted_iota(jnp.int32, m.shape, 1)
    beta = jax.nn.sigmoid(m)
    x = m + dtb_ref[...]
    softplus = jnp.maximum(x, 0.0) + jnp.log1p(jnp.exp(-jnp.abs(x)))
    g = -jnp.exp(alog_ref[...]) * softplus
    is_beta = (lane >= _BETA_LANE) & (lane < _BETA_LANE + NDH)
    is_g = (lane >= _G_LANE) & (lane < _G_LANE + NDH)
    bg_ref[0] = jnp.where(is_beta, beta, jnp.where(is_g, g, 0.0))


def _split3(x):
    hi = x.astype(BF16)
    r = x - hi.astype(F32)
    mid = r.astype(BF16)
    lo = (r - mid.astype(F32)).astype(BF16)
    return hi, mid, lo


def _dot_nt(a, b):
    return lax.dot_general(a, b, (((1,), (1,)), ((), ())), preferred_element_type=F32)


def _gdn_chunk_kernel(q_ref, k_ref, bg_ref, a_ref, attn_ref, gc_ref):
    bg = bg_ref[0]
    lane = lax.broadcasted_iota(jnp.int32, bg.shape, 1)
    is_g = (lane >= _G_LANE) & (lane < _G_LANE + NDH)
    ri = lax.broadcasted_iota(jnp.int32, (CH, CH), 0)
    ci = lax.broadcasted_iota(jnp.int32, (CH, CH), 1)
    lower = ri >= ci
    upper = ri <= ci
    pieces = _split3(jnp.where(is_g, bg, 0.0))
    tril = lower.astype(BF16)
    triu = upper.astype(BF16)
    pre = sum(jnp.dot(tril, p, preferred_element_type=F32) for p in pieces)
    suf = sum(jnp.dot(triu, p, preferred_element_type=F32) for p in pieces)
    gc = jnp.where(lane >= _G_LANE + GDN_HEADS, suf, pre)
    gc_ref[0] = gc
    gct = gc.T
    for d in range(2):
        causal = lower if d == 0 else upper
        strict = (ri > ci) if d == 0 else (ri < ci)
        for h in range(GDN_HEADS):
            dh = d * GDN_HEADS + h
            sl = slice(h * GDN_DK, (h + 1) * GDN_DK)
            kh = k_ref[0, :, sl]
            kb = (kh * bg[:, _BETA_LANE + dh:_BETA_LANE + dh + 1]).astype(BF16)
            khb = kh.astype(BF16)
            diff = gc[:, _G_LANE + dh:_G_LANE + dh + 1] - gct[_G_LANE + dh:_G_LANE + dh + 1, :]
            dec = jnp.exp(jnp.where(causal, diff, -jnp.inf))
            a_ref[0, 0, dh] = jnp.where(strict, _dot_nt(kb, khb) * dec, 0.0)
            attn_ref[0, 0, dh] = _dot_nt(q_ref[0, :, sl].astype(BF16), khb) * dec


def _gdn_solve_kernel(a_ref, t_ref, *, nblk_fwd):
    bwd = pl.program_id(0) >= nblk_fwd
    t_ref[...] = jnp.zeros(t_ref.shape, F32)
    sub = lax.broadcasted_iota(jnp.int32, (CH, LANES), 0)

    def row(it, carry):
        i = jnp.where(bwd, CH - 1 - it, it)

        def blk(jb, acc):
            base = pl.multiple_of(jb * 8, 8)
            ablk = a_ref[i, pl.ds(base, 8), :]
            for jj in range(8):
                acc = acc - ablk[jj:jj + 1, :] * t_ref[base + jj]
            return acc

        lo = jnp.where(bwd, i // 8, 0)
        hi = jnp.where(bwd, CH // 8, i // 8 + 1)
        t_ref[i] = lax.fori_loop(lo, hi, blk, (sub == i).astype(F32))
        return carry

    lax.fori_loop(0, CH, row, 0)


def _gdn_scan_kernel(qf_ref, kf_ref, vf_ref, bgf_ref, gcf_ref, tf_ref, af_ref,
                     qb_ref, kb_ref, vb_ref, bgb_ref, gcb_ref, tb_ref, ab_ref,
                     of_ref, ob_ref, *s_refs):
    @pl.when(pl.program_id(1) == 0)
    def _():
        for s_ref in s_refs:
            s_ref[...] = jnp.zeros(s_ref.shape, F32)

    dirs = ((qf_ref, kf_ref, vf_ref, bgf_ref, gcf_ref, tf_ref, af_ref, of_ref, CH - 1),
            (qb_ref, kb_ref, vb_ref, bgb_ref, gcb_ref, tb_ref, ab_ref, ob_ref, 0))
    probs = []
    for d, (q_ref, k_ref, v_ref, bg_ref, gc_ref, t_ref, a_ref, o_ref, last) in enumerate(dirs):
        for h in range(GDN_HEADS):
            dh = d * GDN_HEADS + h
            sl = slice(h * GDN_DK, (h + 1) * GDN_DK)
            k, v = k_ref[0, :, sl], v_ref[0, :, sl]
            beta = bg_ref[0, :, _BETA_LANE + dh:_BETA_LANE + dh + 1]
            gc = gc_ref[0, :, _G_LANE + dh:_G_LANE + dh + 1]
            gl = gc_ref[0, last:last + 1, _G_LANE + dh:_G_LANE + dh + 1]
            egc = jnp.exp(gc)
            rhs = jnp.concatenate([v * beta, k * beta * egc], axis=-1).astype(BF16)
            sol = jnp.dot(t_ref[0, 0, h], rhs, preferred_element_type=F32)
            s = s_refs[dh][...]
            probs.append(dict(sl=sl, o_ref=o_ref, s_ref=s_refs[dh], s=s, sb=s.astype(BF16),
                              u=sol[:, :GDN_DV], w=sol[:, GDN_DV:].astype(BF16),
                              qd=(q_ref[0, :, sl] * egc).astype(BF16), a=a_ref[0, 0, h].astype(BF16),
                              kd=(k * jnp.exp(gl - gc)).astype(BF16), dec=jnp.exp(gl)))
    for p in probs:
        p["vn"] = (p["u"] - jnp.dot(p["w"], p["sb"], preferred_element_type=F32)).astype(BF16)
    for p in probs:
        p["o_ref"][0, :, p["sl"]] = (jnp.dot(p["qd"], p["sb"], preferred_element_type=F32)
                                     + jnp.dot(p["a"], p["vn"], preferred_element_type=F32))
    for p in probs:
        p["s_ref"][...] = p["s"] * p["dec"] + lax.dot_general(
            p["kd"], p["vn"], (((0,), (0,)), ((), ())), preferred_element_type=F32)


def _gdn_out_kernel(of_ref, ob_ref, z_ref, n_ref, o_ref):
    z = z_ref[...]
    for h in range(GDN_HEADS):
        sl = slice(h * GDN_DV, (h + 1) * GDN_DV)
        o = of_ref[:, sl] + ob_ref[:, sl]
        y = o * lax.rsqrt(jnp.mean(o * o, axis=-1, keepdims=True) + NORM_EPS) * n_ref[...]
        zh = z[:, sl]
        o_ref[:, sl] = y * (zh * jax.nn.sigmoid(zh))


def gdn_mixer(proj3, misc3, conv_w, a_log, dt_bias, out_norm, tl=512):
    B, L, _ = proj3.shape
    T = B * L
    N = L // CH
    H = GDN_HEADS
    tl = min(tl, L)
    qkv = dwconv3(proj3, 3, 1536, conv_w, jnp.zeros((1, 1536), F32), silu=True)
    lane_vec = lambda p: jnp.zeros((1, LANES), F32).at[0, _G_LANE:_G_LANE + NDH].set(p.reshape(-1))
    tok = lambda w, c=0: pl.BlockSpec((1, tl, w), lambda b, i: (b, i, c))
    qn, kn, bg = pl.pallas_call(
        _gdn_prep_kernel,
        grid=(B, L // tl),
        in_specs=[tok(1536), tok(LANES, 3), _full((1, LANES)), _full((1, LANES))],
        out_specs=[tok(512), tok(512), tok(LANES)],
        out_shape=[jax.ShapeDtypeStruct((B, L, 512), F32), jax.ShapeDtypeStruct((B, L, 512), F32),
                   jax.ShapeDtypeStruct((B, L, LANES), F32)],
        compiler_params=_cparams("parallel", "parallel"),
        name="gdn_prep",
    )(qkv, misc3, lane_vec(a_log), lane_vec(dt_bias))

    chunk = lambda w, c=0: pl.BlockSpec((1, CH, w), lambda b, n: (b, n, c))
    mats = pl.BlockSpec((1, 1, NDH, CH, CH), lambda b, n: (b, n, 0, 0, 0))
    mat_shape = jax.ShapeDtypeStruct((B, N, NDH, CH, CH), F32)
    a, attn, gc = pl.pallas_call(
        _gdn_chunk_kernel,
        grid=(B, N),
        in_specs=[chunk(512), chunk(512), chunk(LANES)],
        out_specs=[mats, mats, chunk(LANES)],
        out_shape=[mat_shape, mat_shape, jax.ShapeDtypeStruct((B, L, LANES), F32)],
        compiler_params=_cparams("parallel", "parallel"),
        name="gdn_chunk",
    )(qn, kn, bg)

    P = NDH * B * N
    at = a.transpose(3, 4, 2, 0, 1).reshape(CH, CH, P)
    tt = pl.pallas_call(
        functools.partial(_gdn_solve_kernel, nblk_fwd=P // LANES // 2),
        grid=(P // LANES,),
        in_specs=[pl.BlockSpec((CH, CH, LANES), lambda p: (0, 0, p))],
        out_specs=pl.BlockSpec((CH, CH, LANES), lambda p: (0, 0, p)),
        out_shape=jax.ShapeDtypeStruct((CH, CH, P), F32),
        compiler_params=_cparams("parallel"),
        name="gdn_solve",
    )(at)
    tmat = tt.reshape(CH, CH, NDH, B, N).transpose(3, 4, 2, 0, 1).astype(BF16)

    fwd = lambda w, c=0: pl.BlockSpec((1, CH, w), lambda b, n: (b, n, c))
    bwd = lambda w, c=0: pl.BlockSpec((1, CH, w), lambda b, n: (b, N - 1 - n, c))
    mf = pl.BlockSpec((1, 1, H, CH, CH), lambda b, n: (b, n, 0, 0, 0))
    mb = pl.BlockSpec((1, 1, H, CH, CH), lambda b, n: (b, N - 1 - n, 1, 0, 0))
    o_f, o_b = pl.pallas_call(
        _gdn_scan_kernel,
        grid=(B, N),
        in_specs=[fwd(512), fwd(512), fwd(512, 2), fwd(LANES), fwd(LANES), mf, mf,
                  bwd(512), bwd(512), bwd(512, 2), bwd(LANES), bwd(LANES), mb, mb],
        out_specs=[fwd(512), bwd(512)],
        out_shape=[jax.ShapeDtypeStruct((B, L, 512), F32), jax.ShapeDtypeStruct((B, L, 512), F32)],
        scratch_shapes=[pltpu.VMEM((GDN_DK, GDN_DV), F32)] * NDH,
        compiler_params=_cparams("parallel", "arbitrary"),
        name="gdn_scan",
    )(qn, kn, qkv, bg, gc, tmat, attn, qn, kn, qkv, bg, gc, tmat, attn)

    tm = min(1024, T)
    row = lambda c=0: pl.BlockSpec((tm, 512), lambda i: (i, c))
    return pl.pallas_call(
        _gdn_out_kernel,
        grid=(T // tm,),
        in_specs=[row(), row(), row(12), _full((1, GDN_DV))],
        out_specs=row(),
        out_shape=jax.ShapeDtypeStruct((T, 512), F32),
        compiler_params=_cparams("parallel"),
        name="gdn_out",
    )(o_f.reshape(T, 512), o_b.reshape(T, 512), proj3.reshape(T, MAIN_W), out_norm[None, :])


def _merge_kernel(x_ref, g_ref, oh_ref, om_ref, og_ref, wb_ref, wo_ref, n_ref, o_ref):
    merged = None
    for i, b_ref in enumerate((oh_ref, om_ref, og_ref)):
        gate = jax.nn.sigmoid(g_ref[:, i * D_MODEL:(i + 1) * D_MODEL])
        term = gate * jnp.dot(b_ref[...].astype(BF16), wb_ref[i], preferred_element_type=F32)
        merged = term if merged is None else merged + term
    y = jnp.dot(merged.astype(BF16), wo_ref[...], preferred_element_type=F32)
    r = lax.rsqrt(jnp.mean(y * y, axis=-1, keepdims=True) + NORM_EPS)
    o_ref[...] = x_ref[...] + y * r * n_ref[...]


def merge_out(x, proj, o_hy, o_mla, o_gdn, w_branch, w_out, norm_post, tm=512):
    T = x.shape[0]
    tm = min(tm, T)
    row = lambda w: pl.BlockSpec((tm, w), lambda i: (i, 0))
    return pl.pallas_call(
        _merge_kernel,
        grid=(T // tm,),
        in_specs=[row(D_MODEL), row(N_BRANCH * D_MODEL), row(BRANCH_W), row(BRANCH_W), row(BRANCH_W),
                  _full((N_BRANCH, BRANCH_W, D_MODEL)), _full((D_MODEL, D_MODEL)), _full((1, D_MODEL))],
        out_specs=row(D_MODEL),
        out_shape=jax.ShapeDtypeStruct((T, D_MODEL), F32),
        compiler_params=_cparams("parallel"),
        name="merge_out",
    )(x, proj, o_hy, o_mla, o_gdn, w_branch, w_out, norm_post)


def _ffn_kernel(x_ref, gpre_ref, wg_ref, wu_ref, wd_ref, gpost_ref, o_ref, h_ref, acc_ref):
    j = pl.program_id(1)

    @pl.when(j == 0)
    def _():
        x = x_ref[...]
        r = lax.rsqrt(jnp.mean(x * x, axis=-1, keepdims=True) + NORM_EPS)
        h_ref[...] = (x * r * gpre_ref[...]).astype(BF16)

    h = h_ref[...]
    gate = jnp.dot(h, wg_ref[...], preferred_element_type=F32)
    up = jnp.dot(h, wu_ref[...], preferred_element_type=F32)
    part = jnp.dot((gate * jax.nn.sigmoid(gate) * up).astype(BF16), wd_ref[...], preferred_element_type=F32)

    @pl.when(j == 0)
    def _():
        acc_ref[...] = part

    @pl.when(j > 0)
    def _():
        acc_ref[...] += part

    @pl.when(j == pl.num_programs(1) - 1)
    def _():
        f = acc_ref[...]
        r = lax.rsqrt(jnp.mean(f * f, axis=-1, keepdims=True) + NORM_EPS)
        o_ref[...] = x_ref[...] + f * r * gpost_ref[...]


def ffn(x, g_pre, w_gate, w_up, w_down, g_post, tm=512):
    T = x.shape[0]
    dff = w_gate.shape[1]
    tf = dff // 2
    tm = min(tm, T)
    return pl.pallas_call(
        _ffn_kernel,
        grid=(T // tm, dff // tf),
        in_specs=[pl.BlockSpec((tm, D_MODEL), lambda i, j: (i, 0)), _full((1, D_MODEL)),
                  pl.BlockSpec((D_MODEL, tf), lambda i, j: (0, j)),
                  pl.BlockSpec((D_MODEL, tf), lambda i, j: (0, j)),
                  pl.BlockSpec((tf, D_MODEL), lambda i, j: (j, 0)), _full((1, D_MODEL))],
        out_specs=pl.BlockSpec((tm, D_MODEL), lambda i, j: (i, 0)),
        out_shape=jax.ShapeDtypeStruct((T, D_MODEL), F32),
        scratch_shapes=[pltpu.VMEM((tm, D_MODEL), BF16), pltpu.VMEM((tm, D_MODEL), F32)],
        compiler_params=_cparams("parallel", "arbitrary"),
        name="ffn",
    )(x, g_pre, w_gate, w_up, w_down, g_post)


def _split_w_in(w_in):
    cols = lambda off, n: w_in[:, off:off + n]
    main = jnp.concatenate([cols(_OFF_GATE, N_BRANCH * D_MODEL), cols(_OFF_HY, 1536),
                            cols(_OFF_GQKV, 1536), cols(_OFF_GZ, 512)], axis=-1).astype(BF16)
    half = MLA_ROPE // 2
    kpe = _OFF_MKV + MLA_KV_LORA
    zeros = lambda n: jnp.zeros((D_MODEL, n), w_in.dtype)
    misc = jnp.concatenate([cols(_OFF_MQ, MLA_Q_LORA), cols(_OFF_MKV, MLA_KV_LORA),
                            cols(kpe, MLA_ROPE), cols(_OFF_GB, 8), cols(_OFF_GA, 8), zeros(48),
                            cols(kpe + half, half), cols(kpe, half), zeros(64)], axis=-1).astype(BF16)
    return main, misc


def trunk_layer(x, norm_mix_pre, norm_mix_post, norm_ffn_pre, norm_ffn_post, w_in,
                hy_conv_w, hy_conv_b, hy_ffn_w1, hy_ffn_b1, hy_sin_freq, hy_ffn_w2, hy_ffn_b2,
                hy_ffn_w3, hy_skip, mla_q_norm, mla_wq_b, mla_kv_norm, mla_wkv_b,
                gdn_conv_w, gdn_a_log, gdn_dt_bias, gdn_out_norm,
                w_branch, w_out, w_gate, w_up, w_down):
    B, L, D = x.shape
    T = B * L
    xt = x.reshape(T, D)
    w_main, w_misc = _split_w_in(w_in)
    g_pre = norm_mix_pre[None, :]
    proj = norm_mm(xt, g_pre, w_main, tm=1024, tn=512)
    misc = norm_mm(xt, g_pre, w_misc, tm=1024, tn=MISC_W)
    proj3 = proj.reshape(B, L, MAIN_W)
    misc3 = misc.reshape(B, L, MISC_W)
    o_hy = hyena_mixer(proj3, hy_conv_w, hy_conv_b, hy_ffn_w1, hy_ffn_b1, hy_sin_freq,
                       hy_ffn_w2, hy_ffn_b2, hy_ffn_w3, hy_skip)
    o_mla = mla_mixer(misc3, mla_q_norm, mla_wq_b, mla_kv_norm, mla_wkv_b).reshape(T, BRANCH_W)
    o_gdn = gdn_mixer(proj3, misc3, gdn_conv_w, gdn_a_log, gdn_dt_bias, gdn_out_norm).reshape(T, BRANCH_W)
    xt = merge_out(xt, proj, o_hy, o_mla, o_gdn, w_branch.astype(BF16), w_out.astype(BF16),
                   norm_mix_post[None, :])
    xt = ffn(xt, norm_ffn_pre[None, :], w_gate.astype(BF16), w_up.astype(BF16), w_down.astype(BF16),
             norm_ffn_post[None, :])
    return xt.reshape(B, L, D)


def kernel(x_prompt, x_sample, norm_mix_pre, norm_mix_post, norm_ffn_pre, norm_ffn_post, w_in,
           hy_conv_w, hy_conv_b, hy_ffn_w1, hy_ffn_b1, hy_sin_freq, hy_ffn_w2, hy_ffn_b2,
           hy_ffn_w3, hy_skip, mla_q_norm, mla_wq_b, mla_kv_norm, mla_wkv_b,
           gdn_conv_w, gdn_a_log, gdn_dt_bias, gdn_out_norm,
           w_branch, w_out, w_gate, w_up, w_down):
    weights = (norm_mix_pre, norm_mix_post, norm_ffn_pre, norm_ffn_post, w_in,
               hy_conv_w, hy_conv_b, hy_ffn_w1, hy_ffn_b1, hy_sin_freq, hy_ffn_w2, hy_ffn_b2,
               hy_ffn_w3, hy_skip, mla_q_norm, mla_wq_b, mla_kv_norm, mla_wkv_b,
               gdn_conv_w, gdn_a_log, gdn_dt_bias, gdn_out_norm,
               w_branch, w_out, w_gate, w_up, w_down)

    def run_trunk(x):
        for layer in range(DEPTH):
            x = trunk_layer(x, *[w[layer] for w in weights])
        return x

    return (run_trunk(x_prompt), run_trunk(x_sample))
```

```python
import functools
import math

import jax
import jax.numpy as jnp
from jax import lax
from jax.experimental import pallas as pl
from jax.experimental.pallas import tpu as pltpu
import numpy as np

F32 = jnp.float32
BF16 = jnp.bfloat16

D_MODEL = 1024
DEPTH = 2
BRANCH_W = 512
N_BRANCH = 3
HY_W = BRANCH_W
HY_ORDER = 2
HY_EMB = 33
HY_FAST_PCT = 0.3
HY_SLOW_PCT = 1.5
HY_TARGET = 1e-2
MLA_HEADS = 4
MLA_NOPE = 128
MLA_ROPE = 64
MLA_V = 128
MLA_Q_LORA = 256
MLA_KV_LORA = 128
ROPE_THETA = 10000.0
GDN_HEADS = 4
GDN_DK = 128
GDN_DV = 128
GDN_CHUNK = 64
NORM_EPS = 1e-6

_OFF_HY = 0
_OFF_MQ = _OFF_HY + (HY_ORDER + 1) * HY_W
_OFF_MKV = _OFF_MQ + MLA_Q_LORA
_OFF_GQKV = _OFF_MKV + MLA_KV_LORA + MLA_ROPE
_OFF_GZ = _OFF_GQKV + GDN_HEADS * (2 * GDN_DK + GDN_DV)
_OFF_GB = _OFF_GZ + GDN_HEADS * GDN_DV
_OFF_GA = _OFF_GB + 2 * GDN_HEADS
_OFF_GATE = _OFF_GA + 2 * GDN_HEADS
_D_IN = _OFF_GATE + N_BRANCH * D_MODEL

MAIN_W = 3072 + 1536 + 1536 + 512
MISC_W = 640

LANES = 128
VMEM_LIMIT_BYTES = 56 * 1024 * 1024


def _cparams(*sem):
    return pltpu.CompilerParams(dimension_semantics=sem, vmem_limit_bytes=VMEM_LIMIT_BYTES)


def _full(shape):
    nd = len(shape)
    return pl.BlockSpec(shape, lambda *_: (0,) * nd)


def _norm_mm_kernel(x_ref, g_ref, w_ref, o_ref, h_ref):
    @pl.when(pl.program_id(1) == 0)
    def _():
        x = x_ref[...]
        r = lax.rsqrt(jnp.mean(x * x, axis=-1, keepdims=True) + NORM_EPS)
        h_ref[...] = (x * r * g_ref[...]).astype(BF16)

    o_ref[...] = jnp.dot(h_ref[...], w_ref[...], preferred_element_type=F32).astype(o_ref.dtype)


def norm_mm(x, g, w, tm, tn, out_dtype):
    T, D = x.shape
    N = w.shape[1]
    tm = min(tm, T)
    return pl.pallas_call(
        _norm_mm_kernel,
        grid=(T // tm, N // tn),
        in_specs=[pl.BlockSpec((tm, D), lambda i, j: (i, 0)),
                  pl.BlockSpec((1, D), lambda i, j: (0, 0)),
                  pl.BlockSpec((D, tn), lambda i, j: (0, j))],
        out_specs=pl.BlockSpec((tm, tn), lambda i, j: (i, j)),
        out_shape=jax.ShapeDtypeStruct((T, N), out_dtype),
        scratch_shapes=[pltpu.VMEM((tm, D), BF16)],
        compiler_params=_cparams("parallel", "arbitrary"),
        name="norm_mm",
    )(x, g, w)


def _dwconv_kernel(x_ref, p_ref, n_ref, w_ref, b_ref, o_ref, *, silu):
    i = pl.program_id(1)
    last = pl.num_programs(1) - 1
    x = x_ref[0].astype(F32)
    tl = x.shape[0]
    halo = p_ref.shape[1]
    prev_row = jnp.where(i > 0, p_ref[0, halo - 1:halo, :].astype(F32), 0.0)
    next_row = jnp.where(i < last, n_ref[0, 0:1, :].astype(F32), 0.0)
    rows = lax.broadcasted_iota(jnp.int32, x.shape, 0)
    x_dn = jnp.where(rows == 0, prev_row, pltpu.roll(x, 1, axis=0))
    x_up = jnp.where(rows == tl - 1, next_row, pltpu.roll(x, tl - 1, axis=0))
    out = x_dn * w_ref[0:1, :] + x * w_ref[1:2, :] + x_up * w_ref[2:3, :] + b_ref[...]
    if silu:
        out = out * jax.nn.sigmoid(out)
    o_ref[0] = out


def dwconv3(x, col_blk, cw, w, b, silu, tl=512):
    B, L, _ = x.shape
    tl = min(tl, L)
    halo = 8 * (4 // x.dtype.itemsize)
    nsub = tl // halo
    lastblk = L // halo - 1
    return pl.pallas_call(
        functools.partial(_dwconv_kernel, silu=silu),
        grid=(B, L // tl),
        in_specs=[pl.BlockSpec((1, tl, cw), lambda b_, i: (b_, i, col_blk)),
                  pl.BlockSpec((1, halo, cw), lambda b_, i: (b_, jnp.maximum(i * nsub - 1, 0), col_blk)),
                  pl.BlockSpec((1, halo, cw), lambda b_, i: (b_, jnp.minimum((i + 1) * nsub, lastblk), col_blk)),
                  _full((3, cw)), _full((1, cw))],
        out_specs=pl.BlockSpec((1, tl, cw), lambda b_, i: (b_, i, 0)),
        out_shape=jax.ShapeDtypeStruct((B, L, cw), F32),
        compiler_params=_cparams("parallel", "parallel"),
        name="dwconv3",
    )(x, x, x, w, b)


def _tile_lanes(t, c):
    return t if c == LANES else jnp.concatenate([t] * (c // LANES), axis=-1)


def _fft_s1_kernel(f_ref, x_ref, twr_ref, twi_ref, o_ref, *, nb, n1):
    c = x_ref.shape[-1]
    for t in range(nb):
        a = jnp.dot(f_ref[...], x_ref[0, t], preferred_element_type=F32)
        ar, ai = a[:n1], a[n1:]
        twr = _tile_lanes(twr_ref[t], c)
        twi = _tile_lanes(twi_ref[t], c)
        o_ref[0, t, :n1, :] = (ar * twr + ai * twi).astype(BF16)
        o_ref[0, t, n1:, :] = (ai * twr - ar * twi).astype(BF16)


def _fft_s2_kernel(f_ref, finv_ref, b_ref, kr_ref, ki_ref, twr_ref, twi_ref, o_ref, *, nb, n2):
    c = b_ref.shape[-1]
    for t in range(nb):
        x = jnp.dot(f_ref[...], b_ref[0, t], preferred_element_type=F32)
        xr, xi = x[:n2], x[n2:]
        kr, ki = kr_ref[t], ki_ref[t]
        y = jnp.concatenate([xr * kr - xi * ki, xr * ki + xi * kr], axis=0).astype(BF16)
        cm = jnp.dot(finv_ref[...], y, preferred_element_type=F32)
        cr, ci = cm[:n2], cm[n2:]
        twr = _tile_lanes(twr_ref[t], c)
        twi = _tile_lanes(twi_ref[t], c)
        o_ref[0, t, :n2, :] = (cr * twr - ci * twi).astype(BF16)
        o_ref[0, t, n2:, :] = (cr * twi + ci * twr).astype(BF16)


def _fft_s3_kernel(g_ref, d_ref, o_ref, *, nb):
    for t in range(nb):
        o_ref[0, t] = jnp.dot(g_ref[...], d_ref[0, t], preferred_element_type=F32)


def _dft_tables(L):
    N = 2 * L
    n2 = 128 if N >= 4096 else 16
    n1 = N // n2
    n1h = n1 // 2
    k1 = np.arange(n1)[:, None]
    m1 = np.arange(n1h)[None, :]
    th = 2.0 * np.pi * ((k1 * m1) % n1) / n1
    c, s = np.cos(th), np.sin(th)
    f1 = np.block([[c, s], [-s, c]])
    g1 = np.block([[c.T, -s.T], [s.T, c.T]]) / N
    a2 = np.arange(n2)
    ph = 2.0 * np.pi * ((a2[:, None] * a2[None, :]) % n2) / n2
    c2, s2 = np.cos(ph), np.sin(ph)
    f2 = np.block([[c2, s2], [-s2, c2]])
    f2inv = f2.T
    as_bf16 = lambda m: jnp.asarray(m, dtype=F32).astype(BF16)
    return n1, n2, as_bf16(f1), as_bf16(g1), as_bf16(f2), as_bf16(f2inv)


def _twiddles(n1, n2):
    N = n1 * n2
    prod = (jnp.arange(n2, dtype=jnp.int32)[:, None] * jnp.arange(n1, dtype=jnp.int32)[None, :]) % N
    ang = prod.astype(F32) * (2.0 * math.pi / N)
    shape = (n2, n1, LANES)
    return (jnp.broadcast_to(jnp.cos(ang)[:, :, None], shape),
            jnp.broadcast_to(jnp.sin(ang)[:, :, None], shape))


def fft_long_conv(z, kspec, order, tabs):
    B, L, C = z.shape
    n1, n2, f1, g1, f2, f2inv, tw1r, tw1i, tw2r, tw2i = tabs
    n1h = n1 // 2
    P = B // 2
    nb1 = min(8, n2)
    nb2 = min(8, n1)
    zt = z.astype(BF16).reshape(P, 2, n1h, n2, C).transpose(0, 3, 1, 2, 4).reshape(P, n2, 2 * n1h, C)
    a = pl.pallas_call(
        functools.partial(_fft_s1_kernel, nb=nb1, n1=n1),
        grid=(n2 // nb1, P),
        in_specs=[_full((2 * n1, 2 * n1h)),
                  pl.BlockSpec((1, nb1, 2 * n1h, C), lambda j, p: (p, j, 0, 0)),
                  pl.BlockSpec((nb1, n1, LANES), lambda j, p: (j, 0, 0)),
                  pl.BlockSpec((nb1, n1, LANES), lambda j, p: (j, 0, 0))],
        out_specs=pl.BlockSpec((1, nb1, 2 * n1, C), lambda j, p: (p, j, 0, 0)),
        out_shape=jax.ShapeDtypeStruct((P, n2, 2 * n1, C), BF16),
        compiler_params=_cparams("parallel", "parallel"),
        name="fft_s1",
    )(f1, zt, tw1r, tw1i)
    bt = a.reshape(P, n2, 2, n1, C).transpose(0, 3, 2, 1, 4).reshape(P, n1, 2 * n2, C)
    d = pl.pallas_call(
        functools.partial(_fft_s2_kernel, nb=nb2, n2=n2),
        grid=(n1 // nb2, P),
        in_specs=[_full((2 * n2, 2 * n2)), _full((2 * n2, 2 * n2)),
                  pl.BlockSpec((1, nb2, 2 * n2, C), lambda j, p: (p, j, 0, 0)),
                  pl.BlockSpec((nb2, n2, C), lambda j, p: (j, 0, order)),
                  pl.BlockSpec((nb2, n2, C), lambda j, p: (j, 1, order)),
                  pl.BlockSpec((nb2, n2, LANES), lambda j, p: (j, 0, 0)),
                  pl.BlockSpec((nb2, n2, LANES), lambda j, p: (j, 0, 0))],
        out_specs=pl.BlockSpec((1, nb2, 2 * n2, C), lambda j, p: (p, j, 0, 0)),
        out_shape=jax.ShapeDtypeStruct((P, n1, 2 * n2, C), BF16),
        compiler_params=_cparams("parallel", "parallel"),
        name="fft_s2",
    )(f2, f2inv, bt, kspec, kspec, tw2r, tw2i)
    dt = d.reshape(P, n1, 2, n2, C).transpose(0, 3, 2, 1, 4).reshape(P, n2, 2 * n1, C)
    y = pl.pallas_call(
        functools.partial(_fft_s3_kernel, nb=nb1),
        grid=(n2 // nb1, P),
        in_specs=[_full((2 * n1h, 2 * n1)),
                  pl.BlockSpec((1, nb1, 2 * n1, C), lambda j, p: (p, j, 0, 0))],
        out_specs=pl.BlockSpec((1, nb1, 2 * n1h, C), lambda j, p: (p, j, 0, 0)),
        out_shape=jax.ShapeDtypeStruct((P, n2, 2 * n1h, C), F32),
        compiler_params=_cparams("parallel", "parallel"),
        name="fft_s3",
    )(g1, dt)
    return y.reshape(P, n2, 2, n1h, C).transpose(0, 2, 3, 1, 4).reshape(B, L, C)


def _hy_gate_kernel(g_ref, c_ref, z_ref, s_ref, o_ref):
    o_ref[...] = g_ref[...] * (c_ref[...] + s_ref[...] * z_ref[...])


def hy_gate(u, gate_blk, conv, zsrc, z_blk, skip, tl=1024):
    T, C = conv.shape
    tl = min(tl, T)
    return pl.pallas_call(
        _hy_gate_kernel,
        grid=(T // tl,),
        in_specs=[pl.BlockSpec((tl, C), lambda i: (i, gate_blk)),
                  pl.BlockSpec((tl, C), lambda i: (i, 0)),
                  pl.BlockSpec((tl, C), lambda i: (i, z_blk)),
                  _full((1, C))],
        out_specs=pl.BlockSpec((tl, C), lambda i: (i, 0)),
        out_shape=jax.ShapeDtypeStruct((T, C), F32),
        compiler_params=_cparams("parallel"),
        name="hy_gate",
    )(u, conv, zsrc, skip)


def hyena_filter_spectrum(L, n1, n2, tw1r, tw1i, w1, b1, freq, w2, b2, w3):
    t = jnp.linspace(0.0, 1.0, L, dtype=F32)[:, None]
    bands = (HY_EMB - 1) // 2
    wpos = (2.0 * math.pi / L) * jnp.arange(L, dtype=F32)[:, None]
    fr = jnp.linspace(1e-4, bands - 1, bands, dtype=F32)[None, :]
    feats = jnp.concatenate([t, jnp.cos(fr * wpos), -jnp.sin(fr * wpos)], axis=-1)
    hp = lax.Precision.HIGHEST
    deltas = jnp.abs(jnp.linspace(math.log(HY_TARGET) / HY_SLOW_PCT,
                                  math.log(HY_TARGET) / HY_FAST_PCT, HY_W, dtype=F32))
    w3 = w3.reshape(w3.shape[0], HY_ORDER, 2, HY_W)

    def taps(ft, tt, direction):
        h = jnp.sin(freq * (jnp.dot(ft, w1, precision=hp) + b1))
        h = jnp.sin(freq * (jnp.dot(h, w2, precision=hp) + b2))
        h = jnp.dot(h, w3[:, :, direction].reshape(w3.shape[0], -1), precision=hp)
        return h.reshape(-1, HY_ORDER, HY_W) * jnp.exp(-tt * deltas)[:, None, :]

    fwd = taps(feats, t, 0)
    bwd = taps(feats[:0:-1], t[:0:-1], 1)
    kern = jnp.concatenate([fwd, jnp.zeros((1, HY_ORDER, HY_W), F32), bwd], axis=0)
    kern = kern / jnp.sum(jnp.abs(kern), axis=0, keepdims=True)
    return filter_dft(kern.reshape(2 * L, HY_ORDER * HY_W), n1, n2, tw1r, tw1i)


def _split2(x):
    hi = x.astype(BF16)
    return hi, (x - hi.astype(F32)).astype(BF16)


def _dot3(fh, fl, x):
    xh, xl = _split2(x)
    return (jnp.dot(fh, xh, preferred_element_type=F32) + jnp.dot(fh, xl, preferred_element_type=F32)
            + jnp.dot(fl, xh, preferred_element_type=F32))


def _spec_s1_kernel(fh_ref, fl_ref, x_ref, twr_ref, twi_ref, o_ref, *, nb, n1):
    c = x_ref.shape[-1]
    for t in range(nb):
        a = _dot3(fh_ref[...], fl_ref[...], x_ref[t])
        ar, ai = a[:n1], a[n1:]
        twr = _tile_lanes(twr_ref[t], c)
        twi = _tile_lanes(twi_ref[t], c)
        o_ref[t, :n1, :] = ar * twr + ai * twi
        o_ref[t, n1:, :] = ai * twr - ar * twi


def _spec_s2_kernel(fh_ref, fl_ref, b_ref, o_ref, *, nb):
    for t in range(nb):
        o_ref[t] = _dot3(fh_ref[...], fl_ref[...], b_ref[t])


def filter_dft(kern, n1, n2, tw1r, tw1i):
    N, C = kern.shape
    ct = 512
    k1 = np.arange(n1)
    th = 2.0 * np.pi * ((k1[:, None] * k1[None, :]) % n1) / n1
    f1 = np.concatenate([np.cos(th), -np.sin(th)], axis=0)
    a2 = np.arange(n2)
    ph = 2.0 * np.pi * ((a2[:, None] * a2[None, :]) % n2) / n2
    f2 = np.block([[np.cos(ph), np.sin(ph)], [-np.sin(ph), np.cos(ph)]])

    def hi_lo(m):
        hi = m.astype(BF16)
        lo = (m - hi.astype(np.float64)).astype(BF16)
        return jnp.asarray(hi), jnp.asarray(lo)

    f1h, f1l = hi_lo(f1)
    f2h, f2l = hi_lo(f2)
    nb1 = min(4, n2)
    nb2 = min(4, n1)
    xt = kern.reshape(n1, n2, C).transpose(1, 0, 2)
    a = pl.pallas_call(
        functools.partial(_spec_s1_kernel, nb=nb1, n1=n1),
        grid=(n2 // nb1, C // ct),
        in_specs=[_full((2 * n1, n1)), _full((2 * n1, n1)),
                  pl.BlockSpec((nb1, n1, ct), lambda j, c: (j, 0, c)),
                  pl.BlockSpec((nb1, n1, LANES), lambda j, c: (j, 0, 0)),
                  pl.BlockSpec((nb1, n1, LANES), lambda j, c: (j, 0, 0))],
        out_specs=pl.BlockSpec((nb1, 2 * n1, ct), lambda j, c: (j, 0, c)),
        out_shape=jax.ShapeDtypeStruct((n2, 2 * n1, C), F32),
        compiler_params=_cparams("parallel", "parallel"),
        name="spec_s1",
    )(f1h, f1l, xt, tw1r, tw1i)
    bt = a.reshape(n2, 2, n1, C).transpose(2, 1, 0, 3).reshape(n1, 2 * n2, C)
    return pl.pallas_call(
        functools.partial(_spec_s2_kernel, nb=nb2),
        grid=(n1 // nb2, C // ct),
        in_specs=[_full((2 * n2, 2 * n2)), _full((2 * n2, 2 * n2)),
                  pl.BlockSpec((nb2, 2 * n2, ct), lambda j, c: (j, 0, c))],
        out_specs=pl.BlockSpec((nb2, 2 * n2, ct), lambda j, c: (j, 0, c)),
        out_shape=jax.ShapeDtypeStruct((n1, 2 * n2, C), F32),
        compiler_params=_cparams("parallel", "parallel"),
        name="spec_s2",
    )(f2h, f2l, bt)


def hyena_mixer(proj3, conv_w, conv_b, w1, b1, freq, w2, b2, w3, skip):
    B, L, _ = proj3.shape
    T = B * L
    u = dwconv3(proj3, 2, 1536, conv_w, conv_b[None, :], silu=False).reshape(T, 1536)
    n1, n2, f1, g1, f2, f2inv = _dft_tables(L)
    tw1r, tw1i = _twiddles(n1, n2)
    tw2r, tw2i = tw1r.transpose(1, 0, 2), tw1i.transpose(1, 0, 2)
    tabs = (n1, n2, f1, g1, f2, f2inv, tw1r, tw1i, tw2r, tw2i)
    kspec = hyena_filter_spectrum(L, n1, n2, tw1r, tw1i, w1, b1[None, :], freq[None, :], w2, b2[None, :], w3)
    conv = fft_long_conv(u[:, 1024:1536].reshape(B, L, HY_W), kspec, 0, tabs).reshape(T, HY_W)
    z1 = hy_gate(u, 0, conv, u, 2, skip[0:1])
    conv = fft_long_conv(z1.reshape(B, L, HY_W), kspec, 1, tabs).reshape(T, HY_W)
    return hy_gate(u, 1, conv, z1, 0, skip[1:2])


def _mla_proj_kernel(m_ref, qg_ref, kg_ref, wq_ref, wkv_ref, cq_ref, sq_ref, ck_ref, sk_ref,
                     q_ref, k_ref, v_ref):
    m = m_ref[0]
    ql = m[:, 0:256]
    r = lax.rsqrt(jnp.mean(ql * ql, axis=-1, keepdims=True) + NORM_EPS)
    q = jnp.dot((ql * r * qg_ref[...]).astype(BF16), wq_ref[...], preferred_element_type=F32)
    ckv = m[:, 256:384]
    r = lax.rsqrt(jnp.mean(ckv * ckv, axis=-1, keepdims=True) + NORM_EPS)
    kv = jnp.dot((ckv * r * kg_ref[...]).astype(BF16), wkv_ref[...], preferred_element_type=F32)
    q_pe = (q[:, 512:1024] * cq_ref[...] + q[:, 1024:1536] * sq_ref[...]).astype(BF16)
    k_pe = (m[:, 384:512] * ck_ref[...] + m[:, 512:640] * sk_ref[...]).astype(BF16)
    for h in range(MLA_HEADS):
        sl = slice(h * LANES, (h + 1) * LANES)
        q_ref[0, :, 2 * h * LANES:(2 * h + 1) * LANES] = q[:, sl].astype(BF16)
        q_ref[0, :, (2 * h + 1) * LANES:(2 * h + 2) * LANES] = q_pe[:, sl]
        k_ref[0, :, 2 * h * LANES:(2 * h + 1) * LANES] = kv[:, sl].astype(BF16)
        k_ref[0, :, (2 * h + 1) * LANES:(2 * h + 2) * LANES] = k_pe
        v_ref[0, :, 2 * h * LANES:(2 * h + 1) * LANES] = kv[:, 512 + h * MLA_V:512 + (h + 1) * MLA_V].astype(BF16)
        v_ref[0, :, (2 * h + 1) * LANES:(2 * h + 2) * LANES] = jnp.ones((m.shape[0], LANES), BF16)


def _attn_kernel(q_ref, k_ref, v_ref, o_ref, *scratch, scale):
    ki = pl.program_id(2)
    c = scale * math.log2(math.e)
    dqk = 2 * LANES
    m_refs, acc_refs = scratch[:MLA_HEADS], scratch[MLA_HEADS:]

    @pl.when(ki == 0)
    def _():
        for m_ref, acc_ref in zip(m_refs, acc_refs):
            m_ref[...] = jnp.full(m_ref.shape, -jnp.inf, F32)
            acc_ref[...] = jnp.zeros(acc_ref.shape, F32)

    for h, (m_ref, acc_ref) in enumerate(zip(m_refs, acc_refs)):
        sl = slice(h * dqk, (h + 1) * dqk)
        s = lax.dot_general(q_ref[0, :, sl], k_ref[0, :, sl], (((1,), (1,)), ((), ())),
                            preferred_element_type=F32)
        m_prev = m_ref[...]
        m_new = jnp.maximum(m_prev, jnp.max(s, axis=-1, keepdims=True) * c)
        p = jnp.exp2(s * c - m_new).astype(BF16)
        acc_ref[...] = (jnp.exp2(m_prev - m_new) * acc_ref[...]
                        + jnp.dot(p, v_ref[0, :, sl], preferred_element_type=F32))
        m_ref[...] = m_new

    @pl.when(ki == pl.num_programs(2) - 1)
    def _():
        for h, acc_ref in enumerate(acc_refs):
            o_ref[0, :, h * MLA_V:(h + 1) * MLA_V] = acc_ref[:, :MLA_V] / acc_ref[:, MLA_V:]


def _rope_lane_tables(L):
    half = MLA_ROPE // 2
    inv = ROPE_THETA ** (-jnp.arange(half, dtype=F32) / half)
    ang = jnp.arange(L, dtype=F32)[:, None] * inv[None, :]
    cos, sin = jnp.cos(ang), jnp.sin(ang)
    zeros = jnp.zeros((L, LANES - MLA_ROPE), F32)
    ck = jnp.concatenate([cos, cos, zeros], axis=-1)
    sk = jnp.concatenate([-sin, sin, zeros], axis=-1)
    return jnp.tile(ck, (1, MLA_HEADS)), jnp.tile(sk, (1, MLA_HEADS)), ck, sk


def mla_mixer(misc3, q_norm, wq_b, kv_norm, wkv_b, tl=512, tq=1024, tk=1024):
    B, L, _ = misc3.shape
    H = MLA_HEADS
    tl, tq, tk = min(tl, L), min(tq, L), min(tk, L)
    half = MLA_ROPE // 2
    wq = wq_b.reshape(MLA_Q_LORA, H, MLA_NOPE + MLA_ROPE)
    x1 = wq[:, :, MLA_NOPE:MLA_NOPE + half]
    x2 = wq[:, :, MLA_NOPE + half:]
    zpad = jnp.zeros((MLA_Q_LORA, H, LANES - MLA_ROPE), wq.dtype)
    wq = jnp.concatenate([wq[:, :, :MLA_NOPE].reshape(MLA_Q_LORA, -1),
                          jnp.concatenate([x1, x2, zpad], axis=-1).reshape(MLA_Q_LORA, -1),
                          jnp.concatenate([x2, x1, zpad], axis=-1).reshape(MLA_Q_LORA, -1)],
                         axis=-1).astype(BF16)
    wkv = wkv_b.reshape(MLA_KV_LORA, H, MLA_NOPE + MLA_V)
    wkv = jnp.concatenate([wkv[:, :, :MLA_NOPE].reshape(MLA_KV_LORA, -1),
                           wkv[:, :, MLA_NOPE:].reshape(MLA_KV_LORA, -1)], axis=-1).astype(BF16)
    cq, sq, ck, sk = _rope_lane_tables(L)
    tok = lambda w: pl.BlockSpec((1, tl, w), lambda b, i: (b, i, 0))
    pos = lambda w: pl.BlockSpec((tl, w), lambda b, i: (i, 0))
    sds = lambda w: jax.ShapeDtypeStruct((B, L, w), BF16)
    dqk = 2 * LANES
    q, k, v = pl.pallas_call(
        _mla_proj_kernel,
        grid=(B, L // tl),
        in_specs=[tok(MISC_W), _full((1, MLA_Q_LORA)), _full((1, MLA_KV_LORA)),
                  _full(wq.shape), _full(wkv.shape), pos(H * LANES), pos(H * LANES), pos(LANES), pos(LANES)],
        out_specs=[tok(H * dqk), tok(H * dqk), tok(H * dqk)],
        out_shape=[sds(H * dqk), sds(H * dqk), sds(H * dqk)],
        compiler_params=_cparams("parallel", "parallel"),
        name="mla_proj",
    )(misc3, q_norm[None, :], kv_norm[None, :], wq, wkv, cq, sq, ck, sk)
    return pl.pallas_call(
        functools.partial(_attn_kernel, scale=(MLA_NOPE + MLA_ROPE) ** -0.5),
        grid=(B, L // tq, L // tk),
        in_specs=[pl.BlockSpec((1, tq, H * dqk), lambda b, i, j: (b, i, 0)),
                  pl.BlockSpec((1, tk, H * dqk), lambda b, i, j: (b, j, 0)),
                  pl.BlockSpec((1, tk, H * dqk), lambda b, i, j: (b, j, 0))],
        out_specs=pl.BlockSpec((1, tq, H * MLA_V), lambda b, i, j: (b, i, 0)),
        out_shape=jax.ShapeDtypeStruct((B, L, H * MLA_V), F32),
        scratch_shapes=([pltpu.VMEM((tq, 1), F32)] * H + [pltpu.VMEM((tq, 2 * MLA_V), F32)] * H),
        compiler_params=_cparams("parallel", "parallel", "arbitrary"),
        name="mla_attn",
    )(q, k, v)


_BETA_LANE = 64
_G_LANE = 72
CH = GDN_CHUNK
NDH = 2 * GDN_HEADS
GDN_STEP_CHUNKS = 4


def _gdn_prep_kernel(c_ref, m_ref, alog_ref, dtb_ref, qn_ref, kn_ref, bg_ref):
    for h in range(GDN_HEADS):
        sl = slice(h * GDN_DK, (h + 1) * GDN_DK)
        qh = c_ref[0, :, sl]
        qn_ref[0, :, sl] = (qh * lax.rsqrt(jnp.sum(qh * qh, axis=-1, keepdims=True) + NORM_EPS)
                            * (GDN_DK ** -0.5))
        kh = c_ref[0, :, GDN_HEADS * GDN_DK + h * GDN_DK:GDN_HEADS * GDN_DK + (h + 1) * GDN_DK]
        kn_ref[0, :, sl] = kh * lax.rsqrt(jnp.sum(kh * kh, axis=-1, keepdims=True) + NORM_EPS)
    m = m_ref[0]
    lane = lax.broadcasted_iota(jnp.int32, m.shape, 1)
    beta = jax.nn.sigmoid(m)
    x = m + dtb_ref[...]
    softplus = jnp.maximum(x, 0.0) + jnp.log1p(jnp.exp(-jnp.abs(x)))
    g = -jnp.exp(alog_ref[...]) * softplus
    is_beta = (lane >= _BETA_LANE) & (lane < _BETA_LANE + NDH)
    is_g = (lane >= _G_LANE) & (lane < _G_LANE + NDH)
    bg_ref[0] = jnp.where(is_beta, beta, jnp.where(is_g, g, 0.0))


def _split3(x):
    hi = x.astype(BF16)
    r = x - hi.astype(F32)
    mid = r.astype(BF16)
    lo = (r - mid.astype(F32)).astype(BF16)
    return hi, mid, lo


def _dot_nt(a, b):
    return lax.dot_general(a, b, (((1,), (1,)), ((), ())), preferred_element_type=F32)


def _gdn_chunk_kernel(q_ref, k_ref, bg_ref, a_ref, attn_ref, gc_ref):
    lane = lax.broadcasted_iota(jnp.int32, (CH, LANES), 1)
    is_g = (lane >= _G_LANE) & (lane < _G_LANE + NDH)
    ri = lax.broadcasted_iota(jnp.int32, (CH, CH), 0)
    ci = lax.broadcasted_iota(jnp.int32, (CH, CH), 1)
    lower = ri >= ci
    upper = ri <= ci
    tril = lower.astype(BF16)
    triu = upper.astype(BF16)
    for c in range(GDN_STEP_CHUNKS):
        rows = slice(c * CH, (c + 1) * CH)
        bg = bg_ref[0, rows, :]
        pieces = _split3(jnp.where(is_g, bg, 0.0))
        pre = sum(jnp.dot(tril, p, preferred_element_type=F32) for p in pieces)
        suf = sum(jnp.dot(triu, p, preferred_element_type=F32) for p in pieces)
        gc = jnp.where(lane >= _G_LANE + GDN_HEADS, suf, pre)
        gc_ref[0, rows, :] = gc
        gct = gc.T
        for d in range(2):
            causal = lower if d == 0 else upper
            strict = (ri > ci) if d == 0 else (ri < ci)
            for h in range(GDN_HEADS):
                dh = d * GDN_HEADS + h
                sl = slice(h * GDN_DK, (h + 1) * GDN_DK)
                kh = k_ref[0, rows, sl]
                kb = (kh * bg[:, _BETA_LANE + dh:_BETA_LANE + dh + 1]).astype(BF16)
                khb = kh.astype(BF16)
                diff = gc[:, _G_LANE + dh:_G_LANE + dh + 1] - gct[_G_LANE + dh:_G_LANE + dh + 1, :]
                dec = jnp.exp(jnp.where(causal, diff, -jnp.inf))
                a_ref[0, c, dh] = jnp.where(strict, _dot_nt(kb, khb) * dec, 0.0)
                attn_ref[0, c, dh] = _dot_nt(q_ref[0, rows, sl].astype(BF16), khb) * dec


def _gdn_solve_kernel(a_ref, t_ref, *, nblk_fwd):
    bwd = pl.program_id(0) >= nblk_fwd
    t_ref[...] = jnp.zeros(t_ref.shape, F32)
    sub = lax.broadcasted_iota(jnp.int32, (CH, LANES), 0)

    def row(it, carry):
        i = jnp.where(bwd, CH - 1 - it, it)

        def blk(jb, acc):
            base = pl.multiple_of(jb * 8, 8)
            ablk = a_ref[i, pl.ds(base, 8), :]
            for jj in range(8):
                acc = acc - ablk[jj:jj + 1, :] * t_ref[base + jj]
            return acc

        lo = jnp.where(bwd, i // 8, 0)
        hi = jnp.where(bwd, CH // 8, i // 8 + 1)
        t_ref[i] = lax.fori_loop(lo, hi, blk, (sub == i).astype(F32))
        return carry

    lax.fori_loop(0, CH, row, 0)


def _gdn_scan_kernel(qf_ref, kf_ref, vf_ref, bgf_ref, gcf_ref, tf_ref, af_ref,
                     qb_ref, kb_ref, vb_ref, bgb_ref, gcb_ref, tb_ref, ab_ref,
                     of_ref, ob_ref, *s_refs):
    @pl.when(pl.program_id(1) == 0)
    def _():
        for s_ref in s_refs:
            s_ref[...] = jnp.zeros(s_ref.shape, F32)

    dirs = ((qf_ref, kf_ref, vf_ref, bgf_ref, gcf_ref, tf_ref, af_ref, of_ref, CH - 1),
            (qb_ref, kb_ref, vb_ref, bgb_ref, gcb_ref, tb_ref, ab_ref, ob_ref, 0))
    for step in range(GDN_STEP_CHUNKS):
        probs = []
        for d, (q_ref, k_ref, v_ref, bg_ref, gc_ref, t_ref, a_ref, o_ref, last) in enumerate(dirs):
            c = step if d == 0 else GDN_STEP_CHUNKS - 1 - step
            rows = slice(c * CH, (c + 1) * CH)
            for h in range(GDN_HEADS):
                dh = d * GDN_HEADS + h
                sl = slice(h * GDN_DK, (h + 1) * GDN_DK)
                k, v = k_ref[0, rows, sl], v_ref[0, rows, sl]
                beta = bg_ref[0, rows, _BETA_LANE + dh:_BETA_LANE + dh + 1]
                gc = gc_ref[0, rows, _G_LANE + dh:_G_LANE + dh + 1]
                gl = gc_ref[0, c * CH + last:c * CH + last + 1, _G_LANE + dh:_G_LANE + dh + 1]
                egc = jnp.exp(gc)
                rhs = jnp.concatenate([v * beta, k * beta * egc], axis=-1).astype(BF16)
                sol = jnp.dot(t_ref[0, c, h], rhs, preferred_element_type=F32)
                s = s_refs[dh][...]
                probs.append(dict(sl=sl, rows=rows, o_ref=o_ref, s_ref=s_refs[dh], s=s, sb=s.astype(BF16),
                                  u=sol[:, :GDN_DV], w=sol[:, GDN_DV:].astype(BF16),
                                  qd=(q_ref[0, rows, sl] * egc).astype(BF16),
                                  a=a_ref[0, c, h].astype(BF16),
                                  kd=(k * jnp.exp(gl - gc)).astype(BF16), dec=jnp.exp(gl)))
        for p in probs:
            p["vn"] = (p["u"] - jnp.dot(p["w"], p["sb"], preferred_element_type=F32)).astype(BF16)
        for p in probs:
            p["o_ref"][0, p["rows"], p["sl"]] = (jnp.dot(p["qd"], p["sb"], preferred_element_type=F32)
                                                 + jnp.dot(p["a"], p["vn"], preferred_element_type=F32))
        for p in probs:
            p["s_ref"][...] = p["s"] * p["dec"] + lax.dot_general(
                p["kd"], p["vn"], (((0,), (0,)), ((), ())), preferred_element_type=F32)


def _gdn_out_kernel(of_ref, ob_ref, z_ref, n_ref, o_ref):
    z = z_ref[...].astype(F32)
    for h in range(GDN_HEADS):
        sl = slice(h * GDN_DV, (h + 1) * GDN_DV)
        o = of_ref[:, sl] + ob_ref[:, sl]
        y = o * lax.rsqrt(jnp.mean(o * o, axis=-1, keepdims=True) + NORM_EPS) * n_ref[...]
        zh = z[:, sl]
        o_ref[:, sl] = y * (zh * jax.nn.sigmoid(zh))


def gdn_mixer(proj3, misc3, conv_w, a_log, dt_bias, out_norm, tl=512):
    B, L, _ = proj3.shape
    T = B * L
    N = L // CH
    H = GDN_HEADS
    tl = min(tl, L)
    qkv = dwconv3(proj3, 3, 1536, conv_w, jnp.zeros((1, 1536), F32), silu=True)
    lane_vec = lambda p: jnp.zeros((1, LANES), F32).at[0, _G_LANE:_G_LANE + NDH].set(p.reshape(-1))
    tok = lambda w, c=0: pl.BlockSpec((1, tl, w), lambda b, i: (b, i, c))
    qn, kn, bg = pl.pallas_call(
        _gdn_prep_kernel,
        grid=(B, L // tl),
        in_specs=[tok(1536), tok(LANES, 3), _full((1, LANES)), _full((1, LANES))],
        out_specs=[tok(512), tok(512), tok(LANES)],
        out_shape=[jax.ShapeDtypeStruct((B, L, 512), F32), jax.ShapeDtypeStruct((B, L, 512), F32),
                   jax.ShapeDtypeStruct((B, L, LANES), F32)],
        compiler_params=_cparams("parallel", "parallel"),
        name="gdn_prep",
    )(qkv, misc3, lane_vec(a_log), lane_vec(dt_bias))

    CB = GDN_STEP_CHUNKS
    NB = N // CB
    chunk = lambda w, c=0: pl.BlockSpec((1, CB * CH, w), lambda b, n: (b, n, c))
    mats = pl.BlockSpec((1, CB, NDH, CH, CH), lambda b, n: (b, n, 0, 0, 0))
    mat_shape = jax.ShapeDtypeStruct((B, N, NDH, CH, CH), F32)
    a, attn, gc = pl.pallas_call(
        _gdn_chunk_kernel,
        grid=(B, NB),
        in_specs=[chunk(512), chunk(512), chunk(LANES)],
        out_specs=[mats, mats, chunk(LANES)],
        out_shape=[mat_shape, mat_shape, jax.ShapeDtypeStruct((B, L, LANES), F32)],
        compiler_params=_cparams("parallel", "parallel"),
        name="gdn_chunk",
    )(qn, kn, bg)

    P = NDH * B * N
    at = a.transpose(3, 4, 2, 0, 1).reshape(CH, CH, P)
    tt = pl.pallas_call(
        functools.partial(_gdn_solve_kernel, nblk_fwd=P // LANES // 2),
        grid=(P // LANES,),
        in_specs=[pl.BlockSpec((CH, CH, LANES), lambda p: (0, 0, p))],
        out_specs=pl.BlockSpec((CH, CH, LANES), lambda p: (0, 0, p)),
        out_shape=jax.ShapeDtypeStruct((CH, CH, P), F32),
        compiler_params=_cparams("parallel"),
        name="gdn_solve",
    )(at)
    tmat = tt.reshape(CH, CH, NDH, B, N).transpose(3, 4, 2, 0, 1).astype(BF16)

    fwd = lambda w, c=0: pl.BlockSpec((1, CB * CH, w), lambda b, n: (b, n, c))
    bwd = lambda w, c=0: pl.BlockSpec((1, CB * CH, w), lambda b, n: (b, NB - 1 - n, c))
    mf = pl.BlockSpec((1, CB, H, CH, CH), lambda b, n: (b, n, 0, 0, 0))
    mb = pl.BlockSpec((1, CB, H, CH, CH), lambda b, n: (b, NB - 1 - n, 1, 0, 0))
    o_f, o_b = pl.pallas_call(
        _gdn_scan_kernel,
        grid=(B, NB),
        in_specs=[fwd(512), fwd(512), fwd(512, 2), fwd(LANES), fwd(LANES), mf, mf,
                  bwd(512), bwd(512), bwd(512, 2), bwd(LANES), bwd(LANES), mb, mb],
        out_specs=[fwd(512), bwd(512)],
        out_shape=[jax.ShapeDtypeStruct((B, L, 512), F32), jax.ShapeDtypeStruct((B, L, 512), F32)],
        scratch_shapes=[pltpu.VMEM((GDN_DK, GDN_DV), F32)] * NDH,
        compiler_params=_cparams("parallel", "arbitrary"),
        name="gdn_scan",
    )(qn, kn, qkv, bg, gc, tmat, attn, qn, kn, qkv, bg, gc, tmat, attn)

    tm = min(1024, T)
    row = lambda c=0: pl.BlockSpec((tm, 512), lambda i: (i, c))
    return pl.pallas_call(
        _gdn_out_kernel,
        grid=(T // tm,),
        in_specs=[row(), row(), row(12), _full((1, GDN_DV))],
        out_specs=row(),
        out_shape=jax.ShapeDtypeStruct((T, 512), F32),
        compiler_params=_cparams("parallel"),
        name="gdn_out",
    )(o_f.reshape(T, 512), o_b.reshape(T, 512), proj3.reshape(T, MAIN_W), out_norm[None, :])


def _merge_kernel(x_ref, g_ref, oh_ref, om_ref, og_ref, wb_ref, wo_ref, n_ref, o_ref):
    merged = None
    for i, b_ref in enumerate((oh_ref, om_ref, og_ref)):
        gate = jax.nn.sigmoid(g_ref[:, i * D_MODEL:(i + 1) * D_MODEL].astype(F32))
        term = gate * jnp.dot(b_ref[...].astype(BF16), wb_ref[i], preferred_element_type=F32)
        merged = term if merged is None else merged + term
    y = jnp.dot(merged.astype(BF16), wo_ref[...], preferred_element_type=F32)
    r = lax.rsqrt(jnp.mean(y * y, axis=-1, keepdims=True) + NORM_EPS)
    o_ref[...] = x_ref[...] + y * r * n_ref[...]


def merge_out(x, proj, o_hy, o_mla, o_gdn, w_branch, w_out, norm_post, tm=512):
    T = x.shape[0]
    tm = min(tm, T)
    row = lambda w: pl.BlockSpec((tm, w), lambda i: (i, 0))
    return pl.pallas_call(
        _merge_kernel,
        grid=(T // tm,),
        in_specs=[row(D_MODEL), row(N_BRANCH * D_MODEL), row(BRANCH_W), row(BRANCH_W), row(BRANCH_W),
                  _full((N_BRANCH, BRANCH_W, D_MODEL)), _full((D_MODEL, D_MODEL)), _full((1, D_MODEL))],
        out_specs=row(D_MODEL),
        out_shape=jax.ShapeDtypeStruct((T, D_MODEL), F32),
        compiler_params=_cparams("parallel"),
        name="merge_out",
    )(x, proj, o_hy, o_mla, o_gdn, w_branch, w_out, norm_post)


def _ffn_kernel(x_ref, gpre_ref, wg_ref, wu_ref, wd_ref, gpost_ref, o_ref, h_ref, acc_ref):
    j = pl.program_id(1)

    @pl.when(j == 0)
    def _():
        x = x_ref[...]
        r = lax.rsqrt(jnp.mean(x * x, axis=-1, keepdims=True) + NORM_EPS)
        h_ref[...] = (x * r * gpre_ref[...]).astype(BF16)

    h = h_ref[...]
    gate = jnp.dot(h, wg_ref[...], preferred_element_type=F32)
    up = jnp.dot(h, wu_ref[...], preferred_element_type=F32)
    part = jnp.dot((gate * jax.nn.sigmoid(gate) * up).astype(BF16), wd_ref[...], preferred_element_type=F32)

    @pl.when(j == 0)
    def _():
        acc_ref[...] = part

    @pl.when(j > 0)
    def _():
        acc_ref[...] += part

    @pl.when(j == pl.num_programs(1) - 1)
    def _():
        f = acc_ref[...]
        r = lax.rsqrt(jnp.mean(f * f, axis=-1, keepdims=True) + NORM_EPS)
        o_ref[...] = x_ref[...] + f * r * gpost_ref[...]


def ffn(x, g_pre, w_gate, w_up, w_down, g_post, tm=512):
    T = x.shape[0]
    dff = w_gate.shape[1]
    tf = dff // 2
    tm = min(tm, T)
    return pl.pallas_call(
        _ffn_kernel,
        grid=(T // tm, dff // tf),
        in_specs=[pl.BlockSpec((tm, D_MODEL), lambda i, j: (i, 0)), _full((1, D_MODEL)),
                  pl.BlockSpec((D_MODEL, tf), lambda i, j: (0, j)),
                  pl.BlockSpec((D_MODEL, tf), lambda i, j: (0, j)),
                  pl.BlockSpec((tf, D_MODEL), lambda i, j: (j, 0)), _full((1, D_MODEL))],
        out_specs=pl.BlockSpec((tm, D_MODEL), lambda i, j: (i, 0)),
        out_shape=jax.ShapeDtypeStruct((T, D_MODEL), F32),
        scratch_shapes=[pltpu.VMEM((tm, D_MODEL), BF16), pltpu.VMEM((tm, D_MODEL), F32)],
        compiler_params=_cparams("parallel", "arbitrary"),
        name="ffn",
    )(x, g_pre, w_gate, w_up, w_down, g_post)


def _split_w_in(w_in):
    cols = lambda off, n: w_in[:, off:off + n]
    main = jnp.concatenate([cols(_OFF_GATE, N_BRANCH * D_MODEL), cols(_OFF_HY, 1536),
                            cols(_OFF_GQKV, 1536), cols(_OFF_GZ, 512)], axis=-1).astype(BF16)
    half = MLA_ROPE // 2
    kpe = _OFF_MKV + MLA_KV_LORA
    zeros = lambda n: jnp.zeros((D_MODEL, n), w_in.dtype)
    misc = jnp.concatenate([cols(_OFF_MQ, MLA_Q_LORA), cols(_OFF_MKV, MLA_KV_LORA),
                            cols(kpe, MLA_ROPE), cols(_OFF_GB, 8), cols(_OFF_GA, 8), zeros(48),
                            cols(kpe + half, half), cols(kpe, half), zeros(64)], axis=-1).astype(BF16)
    return main, misc


def trunk_layer(x, norm_mix_pre, norm_mix_post, norm_ffn_pre, norm_ffn_post, w_in,
                hy_conv_w, hy_conv_b, hy_ffn_w1, hy_ffn_b1, hy_sin_freq, hy_ffn_w2, hy_ffn_b2,
                hy_ffn_w3, hy_skip, mla_q_norm, mla_wq_b, mla_kv_norm, mla_wkv_b,
                gdn_conv_w, gdn_a_log, gdn_dt_bias, gdn_out_norm,
                w_branch, w_out, w_gate, w_up, w_down):
    B, L, D = x.shape
    T = B * L
    xt = x.reshape(T, D)
    w_main, w_misc = _split_w_in(w_in)
    g_pre = norm_mix_pre[None, :]
    proj = norm_mm(xt, g_pre, w_main, tm=2048, tn=512, out_dtype=BF16)
    misc = norm_mm(xt, g_pre, w_misc, tm=1024, tn=MISC_W, out_dtype=F32)
    proj3 = proj.reshape(B, L, MAIN_W)
    misc3 = misc.reshape(B, L, MISC_W)
    o_hy = hyena_mixer(proj3, hy_conv_w, hy_conv_b, hy_ffn_w1, hy_ffn_b1, hy_sin_freq,
                       hy_ffn_w2, hy_ffn_b2, hy_ffn_w3, hy_skip)
    o_mla = mla_mixer(misc3, mla_q_norm, mla_wq_b, mla_kv_norm, mla_wkv_b).reshape(T, BRANCH_W)
    o_gdn = gdn_mixer(proj3, misc3, gdn_conv_w, gdn_a_log, gdn_dt_bias, gdn_out_norm).reshape(T, BRANCH_W)
    xt = merge_out(xt, proj, o_hy, o_mla, o_gdn, w_branch.astype(BF16), w_out.astype(BF16),
                   norm_mix_post[None, :])
    xt = ffn(xt, norm_ffn_pre[None, :], w_gate.astype(BF16), w_up.astype(BF16), w_down.astype(BF16),
             norm_ffn_post[None, :])
    return xt.reshape(B, L, D)


def kernel(x_prompt, x_sample, norm_mix_pre, norm_mix_post, norm_ffn_pre, norm_ffn_post, w_in,
           hy_conv_w, hy_conv_b, hy_ffn_w1, hy_ffn_b1, hy_sin_freq, hy_ffn_w2, hy_ffn_b2,
           hy_ffn_w3, hy_skip, mla_q_norm, mla_wq_b, mla_kv_norm, mla_wkv_b,
           gdn_conv_w, gdn_a_log, gdn_dt_bias, gdn_out_norm,
           w_branch, w_out, w_gate, w_up, w_down):
    weights = (norm_mix_pre, norm_mix_post, norm_ffn_pre, norm_ffn_post, w_in,
               hy_conv_w, hy_conv_b, hy_ffn_w1, hy_ffn_b1, hy_sin_freq, hy_ffn_w2, hy_ffn_b2,
               hy_ffn_w3, hy_skip, mla_q_norm, mla_wq_b, mla_kv_norm, mla_wkv_b,
               gdn_conv_w, gdn_a_log, gdn_dt_bias, gdn_out_norm,
               w_branch, w_out, w_gate, w_up, w_down)

    def run_trunk(x):
        for layer in range(DEPTH):
            x = trunk_layer(x, *[w[layer] for w in weights])
        return x

    return (run_trunk(x_prompt), run_trunk(x_sample))
```

```python
import functools
import math

import jax
import jax.numpy as jnp
from jax import lax
from jax.experimental import pallas as pl
from jax.experimental.pallas import tpu as pltpu
import numpy as np

F32 = jnp.float32
BF16 = jnp.bfloat16

D_MODEL = 1024
DEPTH = 2
BRANCH_W = 512
N_BRANCH = 3
HY_W = BRANCH_W
HY_ORDER = 2
HY_EMB = 33
HY_FAST_PCT = 0.3
HY_SLOW_PCT = 1.5
HY_TARGET = 1e-2
MLA_HEADS = 4
MLA_NOPE = 128
MLA_ROPE = 64
MLA_V = 128
MLA_Q_LORA = 256
MLA_KV_LORA = 128
ROPE_THETA = 10000.0
GDN_HEADS = 4
GDN_DK = 128
GDN_DV = 128
GDN_CHUNK = 64
NORM_EPS = 1e-6

_OFF_HY = 0
_OFF_MQ = _OFF_HY + (HY_ORDER + 1) * HY_W
_OFF_MKV = _OFF_MQ + MLA_Q_LORA
_OFF_GQKV = _OFF_MKV + MLA_KV_LORA + MLA_ROPE
_OFF_GZ = _OFF_GQKV + GDN_HEADS * (2 * GDN_DK + GDN_DV)
_OFF_GB = _OFF_GZ + GDN_HEADS * GDN_DV
_OFF_GA = _OFF_GB + 2 * GDN_HEADS
_OFF_GATE = _OFF_GA + 2 * GDN_HEADS
_D_IN = _OFF_GATE + N_BRANCH * D_MODEL

MAIN_W = 3072 + 1536 + 1536 + 512
MISC_W = 640

LANES = 128
VMEM_LIMIT_BYTES = 56 * 1024 * 1024


def _cparams(*sem):
    return pltpu.CompilerParams(dimension_semantics=sem, vmem_limit_bytes=VMEM_LIMIT_BYTES)


def _full(shape):
    nd = len(shape)
    return pl.BlockSpec(shape, lambda *_: (0,) * nd)


def _norm_mm_kernel(x_ref, g_ref, w_ref, o_ref, h_ref):
    @pl.when(pl.program_id(1) == 0)
    def _():
        x = x_ref[...]
        r = lax.rsqrt(jnp.mean(x * x, axis=-1, keepdims=True) + NORM_EPS)
        h_ref[...] = (x * r * g_ref[...]).astype(BF16)

    o_ref[...] = jnp.dot(h_ref[...], w_ref[...], preferred_element_type=F32).astype(o_ref.dtype)


def norm_mm(x, g, w, tm, tn, out_dtype):
    T, D = x.shape
    N = w.shape[1]
    tm = min(tm, T)
    return pl.pallas_call(
        _norm_mm_kernel,
        grid=(T // tm, N // tn),
        in_specs=[pl.BlockSpec((tm, D), lambda i, j: (i, 0)),
                  pl.BlockSpec((1, D), lambda i, j: (0, 0)),
                  pl.BlockSpec((D, tn), lambda i, j: (0, j))],
        out_specs=pl.BlockSpec((tm, tn), lambda i, j: (i, j)),
        out_shape=jax.ShapeDtypeStruct((T, N), out_dtype),
        scratch_shapes=[pltpu.VMEM((tm, D), BF16)],
        compiler_params=_cparams("parallel", "arbitrary"),
        name="norm_mm",
    )(x, g, w)


def _dwconv_kernel(x_ref, p_ref, n_ref, w_ref, b_ref, o_ref, *, silu):
    i = pl.program_id(1)
    last = pl.num_programs(1) - 1
    x = x_ref[0].astype(F32)
    tl = x.shape[0]
    halo = p_ref.shape[1]
    prev_row = jnp.where(i > 0, p_ref[0, halo - 1:halo, :].astype(F32), 0.0)
    next_row = jnp.where(i < last, n_ref[0, 0:1, :].astype(F32), 0.0)
    rows = lax.broadcasted_iota(jnp.int32, x.shape, 0)
    x_dn = jnp.where(rows == 0, prev_row, pltpu.roll(x, 1, axis=0))
    x_up = jnp.where(rows == tl - 1, next_row, pltpu.roll(x, tl - 1, axis=0))
    out = x_dn * w_ref[0:1, :] + x * w_ref[1:2, :] + x_up * w_ref[2:3, :] + b_ref[...]
    if silu:
        out = out * jax.nn.sigmoid(out)
    o_ref[0] = out


def dwconv3(x, col_blk, cw, w, b, silu, tl=512):
    B, L, _ = x.shape
    tl = min(tl, L)
    halo = 8 * (4 // x.dtype.itemsize)
    nsub = tl // halo
    lastblk = L // halo - 1
    return pl.pallas_call(
        functools.partial(_dwconv_kernel, silu=silu),
        grid=(B, L // tl),
        in_specs=[pl.BlockSpec((1, tl, cw), lambda b_, i: (b_, i, col_blk)),
                  pl.BlockSpec((1, halo, cw), lambda b_, i: (b_, jnp.maximum(i * nsub - 1, 0), col_blk)),
                  pl.BlockSpec((1, halo, cw), lambda b_, i: (b_, jnp.minimum((i + 1) * nsub, lastblk), col_blk)),
                  _full((3, cw)), _full((1, cw))],
        out_specs=pl.BlockSpec((1, tl, cw), lambda b_, i: (b_, i, 0)),
        out_shape=jax.ShapeDtypeStruct((B, L, cw), F32),
        compiler_params=_cparams("parallel", "parallel"),
        name="dwconv3",
    )(x, x, x, w, b)


def _tile_lanes(t, c):
    return t if c == LANES else jnp.concatenate([t] * (c // LANES), axis=-1)


def _fft_s1_kernel(f_ref, x_ref, twr_ref, twi_ref, o_ref, *, nb, n1):
    c = x_ref.shape[-1]
    for t in range(nb):
        a = jnp.dot(f_ref[...], x_ref[0, t], preferred_element_type=F32)
        ar, ai = a[:n1], a[n1:]
        twr = _tile_lanes(twr_ref[t], c)
        twi = _tile_lanes(twi_ref[t], c)
        o_ref[0, t, :n1, :] = (ar * twr + ai * twi).astype(BF16)
        o_ref[0, t, n1:, :] = (ai * twr - ar * twi).astype(BF16)


def _fft_s2_kernel(f_ref, finv_ref, b_ref, kr_ref, ki_ref, twr_ref, twi_ref, o_ref, *, nb, n2):
    c = b_ref.shape[-1]
    for t in range(nb):
        x = jnp.dot(f_ref[...], b_ref[0, t], preferred_element_type=F32)
        xr, xi = x[:n2], x[n2:]
        kr, ki = kr_ref[t], ki_ref[t]
        y = jnp.concatenate([xr * kr - xi * ki, xr * ki + xi * kr], axis=0).astype(BF16)
        cm = jnp.dot(finv_ref[...], y, preferred_element_type=F32)
        cr, ci = cm[:n2], cm[n2:]
        twr = _tile_lanes(twr_ref[t], c)
        twi = _tile_lanes(twi_ref[t], c)
        o_ref[0, t, :n2, :] = (cr * twr - ci * twi).astype(BF16)
        o_ref[0, t, n2:, :] = (cr * twi + ci * twr).astype(BF16)


def _fft_s3_kernel(g_ref, d_ref, o_ref, *, nb):
    for t in range(nb):
        o_ref[0, t] = jnp.dot(g_ref[...], d_ref[0, t], preferred_element_type=F32)


def _dft_tables(L):
    N = 2 * L
    n2 = 128 if N >= 4096 else 16
    n1 = N // n2
    n1h = n1 // 2
    k1 = np.arange(n1)[:, None]
    m1 = np.arange(n1h)[None, :]
    th = 2.0 * np.pi * ((k1 * m1) % n1) / n1
    c, s = np.cos(th), np.sin(th)
    f1 = np.block([[c, s], [-s, c]])
    g1 = np.block([[c.T, -s.T], [s.T, c.T]]) / N
    a2 = np.arange(n2)
    ph = 2.0 * np.pi * ((a2[:, None] * a2[None, :]) % n2) / n2
    c2, s2 = np.cos(ph), np.sin(ph)
    f2 = np.block([[c2, s2], [-s2, c2]])
    f2inv = f2.T
    as_bf16 = lambda m: jnp.asarray(m, dtype=F32).astype(BF16)
    return n1, n2, as_bf16(f1), as_bf16(g1), as_bf16(f2), as_bf16(f2inv)


def _twiddles(n1, n2):
    N = n1 * n2
    prod = (jnp.arange(n2, dtype=jnp.int32)[:, None] * jnp.arange(n1, dtype=jnp.int32)[None, :]) % N
    ang = prod.astype(F32) * (2.0 * math.pi / N)
    shape = (n2, n1, LANES)
    return (jnp.broadcast_to(jnp.cos(ang)[:, :, None], shape),
            jnp.broadcast_to(jnp.sin(ang)[:, :, None], shape))


def fft_long_conv(z, kspec, order, tabs):
    B, L, C = z.shape
    n1, n2, f1, g1, f2, f2inv, tw1r, tw1i, tw2r, tw2i = tabs
    n1h = n1 // 2
    P = B // 2
    nb1 = min(8, n2)
    nb2 = min(8, n1)
    zt = z.astype(BF16).reshape(P, 2, n1h, n2, C).transpose(0, 3, 1, 2, 4).reshape(P, n2, 2 * n1h, C)
    a = pl.pallas_call(
        functools.partial(_fft_s1_kernel, nb=nb1, n1=n1),
        grid=(n2 // nb1, P),
        in_specs=[_full((2 * n1, 2 * n1h)),
                  pl.BlockSpec((1, nb1, 2 * n1h, C), lambda j, p: (p, j, 0, 0)),
                  pl.BlockSpec((nb1, n1, LANES), lambda j, p: (j, 0, 0)),
                  pl.BlockSpec((nb1, n1, LANES), lambda j, p: (j, 0, 0))],
        out_specs=pl.BlockSpec((1, nb1, 2 * n1, C), lambda j, p: (p, j, 0, 0)),
        out_shape=jax.ShapeDtypeStruct((P, n2, 2 * n1, C), BF16),
        compiler_params=_cparams("parallel", "parallel"),
        name="fft_s1",
    )(f1, zt, tw1r, tw1i)
    bt = a.reshape(P, n2, 2, n1, C).transpose(0, 3, 2, 1, 4).reshape(P, n1, 2 * n2, C)
    d = pl.pallas_call(
        functools.partial(_fft_s2_kernel, nb=nb2, n2=n2),
        grid=(n1 // nb2, P),
        in_specs=[_full((2 * n2, 2 * n2)), _full((2 * n2, 2 * n2)),
                  pl.BlockSpec((1, nb2, 2 * n2, C), lambda j, p: (p, j, 0, 0)),
                  pl.BlockSpec((nb2, n2, C), lambda j, p: (j, 0, order)),
                  pl.BlockSpec((nb2, n2, C), lambda j, p: (j, 1, order)),
                  pl.BlockSpec((nb2, n2, LANES), lambda j, p: (j, 0, 0)),
                  pl.BlockSpec((nb2, n2, LANES), lambda j, p: (j, 0, 0))],
        out_specs=pl.BlockSpec((1, nb2, 2 * n2, C), lambda j, p: (p, j, 0, 0)),
        out_shape=jax.ShapeDtypeStruct((P, n1, 2 * n2, C), BF16),
        compiler_params=_cparams("parallel", "parallel"),
        name="fft_s2",
    )(f2, f2inv, bt, kspec, kspec, tw2r, tw2i)
    dt = d.reshape(P, n1, 2, n2, C).transpose(0, 3, 2, 1, 4).reshape(P, n2, 2 * n1, C)
    y = pl.pallas_call(
        functools.partial(_fft_s3_kernel, nb=nb1),
        grid=(n2 // nb1, P),
        in_specs=[_full((2 * n1h, 2 * n1)),
                  pl.BlockSpec((1, nb1, 2 * n1, C), lambda j, p: (p, j, 0, 0))],
        out_specs=pl.BlockSpec((1, nb1, 2 * n1h, C), lambda j, p: (p, j, 0, 0)),
        out_shape=jax.ShapeDtypeStruct((P, n2, 2 * n1h, C), F32),
        compiler_params=_cparams("parallel", "parallel"),
        name="fft_s3",
    )(g1, dt)
    return y.reshape(P, n2, 2, n1h, C).transpose(0, 2, 3, 1, 4).reshape(B, L, C)


def _hy_gate_kernel(g_ref, c_ref, z_ref, s_ref, o_ref):
    o_ref[...] = g_ref[...] * (c_ref[...] + s_ref[...] * z_ref[...])


def hy_gate(u, gate_blk, conv, zsrc, z_blk, skip, tl=1024):
    T, C = conv.shape
    tl = min(tl, T)
    return pl.pallas_call(
        _hy_gate_kernel,
        grid=(T // tl,),
        in_specs=[pl.BlockSpec((tl, C), lambda i: (i, gate_blk)),
                  pl.BlockSpec((tl, C), lambda i: (i, 0)),
                  pl.BlockSpec((tl, C), lambda i: (i, z_blk)),
                  _full((1, C))],
        out_specs=pl.BlockSpec((tl, C), lambda i: (i, 0)),
        out_shape=jax.ShapeDtypeStruct((T, C), F32),
        compiler_params=_cparams("parallel"),
        name="hy_gate",
    )(u, conv, zsrc, skip)


def hyena_filter_spectrum(L, n1, n2, tw1r, tw1i, w1, b1, freq, w2, b2, w3):
    t = jnp.linspace(0.0, 1.0, L, dtype=F32)[:, None]
    bands = (HY_EMB - 1) // 2
    wpos = (2.0 * math.pi / L) * jnp.arange(L, dtype=F32)[:, None]
    fr = jnp.linspace(1e-4, bands - 1, bands, dtype=F32)[None, :]
    feats = jnp.concatenate([t, jnp.cos(fr * wpos), -jnp.sin(fr * wpos)], axis=-1)
    deltas = jnp.abs(jnp.linspace(math.log(HY_TARGET) / HY_SLOW_PCT,
                                  math.log(HY_TARGET) / HY_FAST_PCT, HY_W, dtype=F32))
    feats2 = jnp.concatenate([feats, feats[0:1], feats[:0:-1]], axis=0)
    feats2 = jnp.pad(feats2, ((0, 0), (0, LANES - HY_EMB)))
    w1p = jnp.pad(w1, ((0, LANES - HY_EMB), (0, 0)))
    hid = w3.shape[0]
    oc = HY_ORDER * HY_W
    w3d = w3.reshape(hid, HY_ORDER, 2, HY_W).transpose(2, 0, 1, 3).reshape(2, hid, oc)
    tl = min(512, L)
    nh = L // tl
    kern, asum = pl.pallas_call(
        functools.partial(_filter_taps_kernel, zero_tile=nh),
        grid=(2 * nh,),
        in_specs=[pl.BlockSpec((tl, LANES), lambda i: (i, 0)), _full((LANES, hid)), _full((1, hid)),
                  _full((1, hid)), _full((hid, hid)), _full((1, hid)),
                  pl.BlockSpec((1, hid, oc), lambda i: (i // nh, 0, 0)), _full((1, oc))],
        out_specs=[pl.BlockSpec((tl, oc), lambda i: (i, 0)), _full((1, oc))],
        out_shape=[jax.ShapeDtypeStruct((2 * L, oc), F32), jax.ShapeDtypeStruct((1, oc), F32)],
        compiler_params=_cparams("arbitrary"),
        name="filter_taps",
    )(feats2, w1p, b1, freq, w2, b2, w3d, jnp.tile(deltas, HY_ORDER)[None, :])
    return filter_dft(kern, asum, n1, n2, tw1r, tw1i)


def _filter_taps_kernel(f_ref, w1_ref, b1_ref, fq_ref, w2_ref, b2_ref, w3_ref, dl_ref, k_ref, s_ref, *,
                        zero_tile):
    i = pl.program_id(0)
    hp = lax.Precision.HIGHEST
    f = f_ref[...]
    h = jnp.sin(fq_ref[...] * (jnp.dot(f, w1_ref[...], precision=hp, preferred_element_type=F32)
                               + b1_ref[...]))
    h = jnp.sin(fq_ref[...] * (jnp.dot(h, w2_ref[...], precision=hp, preferred_element_type=F32)
                               + b2_ref[...]))
    taps = jnp.dot(h, w3_ref[0], precision=hp, preferred_element_type=F32)
    taps = taps * jnp.exp(-f[:, 0:1] * dl_ref[...])
    row = lax.broadcasted_iota(jnp.int32, taps.shape, 0)
    taps = jnp.where((i == zero_tile) & (row == 0), 0.0, taps)
    k_ref[...] = taps

    @pl.when(i == 0)
    def _():
        s_ref[...] = jnp.zeros(s_ref.shape, F32)

    s_ref[...] += jnp.sum(jnp.abs(taps), axis=0, keepdims=True)


def _spec_s2_kernel(f_ref, b_ref, s_ref, o_ref, *, nb):
    for t in range(nb):
        o_ref[t] = jnp.dot(f_ref[...], b_ref[0, t], preferred_element_type=F32) / s_ref[...]


def filter_dft(kern, asum, n1, n2, tw1r, tw1i):
    N, C = kern.shape
    ct = 512
    k1 = np.arange(n1)
    th = 2.0 * np.pi * ((k1[:, None] * k1[None, :]) % n1) / n1
    f1 = np.concatenate([np.cos(th), -np.sin(th)], axis=0)
    a2 = np.arange(n2)
    ph = 2.0 * np.pi * ((a2[:, None] * a2[None, :]) % n2) / n2
    f2 = np.block([[np.cos(ph), np.sin(ph)], [-np.sin(ph), np.cos(ph)]])

    as_bf16 = lambda m: jnp.asarray(m, dtype=F32).astype(BF16)
    nb1 = min(8, n2)
    nb2 = min(8, n1)
    xt = kern.astype(BF16).reshape(n1, n2, C).transpose(1, 0, 2)[None]
    a = pl.pallas_call(
        functools.partial(_fft_s1_kernel, nb=nb1, n1=n1),
        grid=(n2 // nb1, C // ct),
        in_specs=[_full((2 * n1, n1)),
                  pl.BlockSpec((1, nb1, n1, ct), lambda j, c: (0, j, 0, c)),
                  pl.BlockSpec((nb1, n1, LANES), lambda j, c: (j, 0, 0)),
                  pl.BlockSpec((nb1, n1, LANES), lambda j, c: (j, 0, 0))],
        out_specs=pl.BlockSpec((1, nb1, 2 * n1, ct), lambda j, c: (0, j, 0, c)),
        out_shape=jax.ShapeDtypeStruct((1, n2, 2 * n1, C), BF16),
        compiler_params=_cparams("parallel", "parallel"),
        name="spec_s1",
    )(as_bf16(f1), xt, tw1r, tw1i)
    bt = a.reshape(n2, 2, n1, C).transpose(2, 1, 0, 3).reshape(1, n1, 2 * n2, C)
    return pl.pallas_call(
        functools.partial(_spec_s2_kernel, nb=nb2),
        grid=(n1 // nb2, C // ct),
        in_specs=[_full((2 * n2, 2 * n2)),
                  pl.BlockSpec((1, nb2, 2 * n2, ct), lambda j, c: (0, j, 0, c)),
                  pl.BlockSpec((1, ct), lambda j, c: (0, c))],
        out_specs=pl.BlockSpec((nb2, 2 * n2, ct), lambda j, c: (j, 0, c)),
        out_shape=jax.ShapeDtypeStruct((n1, 2 * n2, C), F32),
        compiler_params=_cparams("parallel", "parallel"),
        name="spec_s2",
    )(as_bf16(f2), bt, asum)


def hyena_mixer(proj3, conv_w, conv_b, w1, b1, freq, w2, b2, w3, skip):
    B, L, _ = proj3.shape
    T = B * L
    u = dwconv3(proj3, 2, 1536, conv_w, conv_b[None, :], silu=False).reshape(T, 1536)
    n1, n2, f1, g1, f2, f2inv = _dft_tables(L)
    tw1r, tw1i = _twiddles(n1, n2)
    tw2r, tw2i = tw1r.transpose(1, 0, 2), tw1i.transpose(1, 0, 2)
    tabs = (n1, n2, f1, g1, f2, f2inv, tw1r, tw1i, tw2r, tw2i)
    kspec = hyena_filter_spectrum(L, n1, n2, tw1r, tw1i, w1, b1[None, :], freq[None, :], w2, b2[None, :], w3)
    conv = fft_long_conv(u[:, 1024:1536].reshape(B, L, HY_W), kspec, 0, tabs).reshape(T, HY_W)
    z1 = hy_gate(u, 0, conv, u, 2, skip[0:1])
    conv = fft_long_conv(z1.reshape(B, L, HY_W), kspec, 1, tabs).reshape(T, HY_W)
    return hy_gate(u, 1, conv, z1, 0, skip[1:2])


def _mla_proj_kernel(m_ref, qg_ref, kg_ref, wq_ref, wkv_ref, cq_ref, sq_ref, ck_ref, sk_ref,
                     q_ref, k_ref, v_ref):
    m = m_ref[0]
    ql = m[:, 0:256]
    r = lax.rsqrt(jnp.mean(ql * ql, axis=-1, keepdims=True) + NORM_EPS)
    q = jnp.dot((ql * r * qg_ref[...]).astype(BF16), wq_ref[...], preferred_element_type=F32)
    ckv = m[:, 256:384]
    r = lax.rsqrt(jnp.mean(ckv * ckv, axis=-1, keepdims=True) + NORM_EPS)
    kv = jnp.dot((ckv * r * kg_ref[...]).astype(BF16), wkv_ref[...], preferred_element_type=F32)
    q_pe = (q[:, 512:1024] * cq_ref[...] + q[:, 1024:1536] * sq_ref[...]).astype(BF16)
    k_pe = (m[:, 384:512] * ck_ref[...] + m[:, 512:640] * sk_ref[...]).astype(BF16)
    for h in range(MLA_HEADS):
        sl = slice(h * LANES, (h + 1) * LANES)
        q_ref[0, :, 2 * h * LANES:(2 * h + 1) * LANES] = q[:, sl].astype(BF16)
        q_ref[0, :, (2 * h + 1) * LANES:(2 * h + 2) * LANES] = q_pe[:, sl]
        k_ref[0, :, 2 * h * LANES:(2 * h + 1) * LANES] = kv[:, sl].astype(BF16)
        k_ref[0, :, (2 * h + 1) * LANES:(2 * h + 2) * LANES] = k_pe
        v_ref[0, :, 2 * h * LANES:(2 * h + 1) * LANES] = kv[:, 512 + h * MLA_V:512 + (h + 1) * MLA_V].astype(BF16)
        v_ref[0, :, (2 * h + 1) * LANES:(2 * h + 2) * LANES] = jnp.ones((m.shape[0], LANES), BF16)


ATTN_ROW_PARTS = 2


def _attn_kernel(q_ref, k_ref, v_ref, o_ref, *scratch, scale):
    ki = pl.program_id(2)
    c = scale * math.log2(math.e)
    dqk = 2 * LANES
    m_refs, acc_refs = scratch[:MLA_HEADS], scratch[MLA_HEADS:]

    @pl.when(ki == 0)
    def _():
        for m_ref, acc_ref in zip(m_refs, acc_refs):
            m_ref[...] = jnp.full(m_ref.shape, -jnp.inf, F32)
            acc_ref[...] = jnp.zeros(acc_ref.shape, F32)

    part = q_ref.shape[1] // ATTN_ROW_PARTS
    for h, (m_ref, acc_ref) in enumerate(zip(m_refs, acc_refs)):
        sl = slice(h * dqk, (h + 1) * dqk)
        for r in range(ATTN_ROW_PARTS):
            rows = slice(r * part, (r + 1) * part)
            s = lax.dot_general(q_ref[0, rows, sl], k_ref[0, :, sl], (((1,), (1,)), ((), ())),
                                preferred_element_type=F32)
            m_prev = m_ref[rows, :]
            m_new = jnp.maximum(m_prev, jnp.max(s, axis=-1, keepdims=True) * c)
            p = jnp.exp2(s * c - m_new).astype(BF16)
            acc_ref[rows, :] = (jnp.exp2(m_prev - m_new) * acc_ref[rows, :]
                                + jnp.dot(p, v_ref[0, :, sl], preferred_element_type=F32))
            m_ref[rows, :] = m_new

    @pl.when(ki == pl.num_programs(2) - 1)
    def _():
        for h, acc_ref in enumerate(acc_refs):
            o_ref[0, :, h * MLA_V:(h + 1) * MLA_V] = acc_ref[:, :MLA_V] / acc_ref[:, MLA_V:]


def _rope_lane_tables(L):
    half = MLA_ROPE // 2
    inv = ROPE_THETA ** (-jnp.arange(half, dtype=F32) / half)
    ang = jnp.arange(L, dtype=F32)[:, None] * inv[None, :]
    cos, sin = jnp.cos(ang), jnp.sin(ang)
    zeros = jnp.zeros((L, LANES - MLA_ROPE), F32)
    ck = jnp.concatenate([cos, cos, zeros], axis=-1)
    sk = jnp.concatenate([-sin, sin, zeros], axis=-1)
    return jnp.tile(ck, (1, MLA_HEADS)), jnp.tile(sk, (1, MLA_HEADS)), ck, sk


def mla_mixer(misc3, q_norm, wq_b, kv_norm, wkv_b, tl=512, tq=1024, tk=1024):
    B, L, _ = misc3.shape
    H = MLA_HEADS
    tl, tq, tk = min(tl, L), min(tq, L), min(tk, L)
    half = MLA_ROPE // 2
    wq = wq_b.reshape(MLA_Q_LORA, H, MLA_NOPE + MLA_ROPE)
    x1 = wq[:, :, MLA_NOPE:MLA_NOPE + half]
    x2 = wq[:, :, MLA_NOPE + half:]
    zpad = jnp.zeros((MLA_Q_LORA, H, LANES - MLA_ROPE), wq.dtype)
    wq = jnp.concatenate([wq[:, :, :MLA_NOPE].reshape(MLA_Q_LORA, -1),
                          jnp.concatenate([x1, x2, zpad], axis=-1).reshape(MLA_Q_LORA, -1),
                          jnp.concatenate([x2, x1, zpad], axis=-1).reshape(MLA_Q_LORA, -1)],
                         axis=-1).astype(BF16)
    wkv = wkv_b.reshape(MLA_KV_LORA, H, MLA_NOPE + MLA_V)
    wkv = jnp.concatenate([wkv[:, :, :MLA_NOPE].reshape(MLA_KV_LORA, -1),
                           wkv[:, :, MLA_NOPE:].reshape(MLA_KV_LORA, -1)], axis=-1).astype(BF16)
    cq, sq, ck, sk = _rope_lane_tables(L)
    tok = lambda w: pl.BlockSpec((1, tl, w), lambda b, i: (b, i, 0))
    pos = lambda w: pl.BlockSpec((tl, w), lambda b, i: (i, 0))
    sds = lambda w: jax.ShapeDtypeStruct((B, L, w), BF16)
    dqk = 2 * LANES
    nk = L // tk
    q, k, v = pl.pallas_call(
        _mla_proj_kernel,
        grid=(B, L // tl),
        in_specs=[tok(MISC_W), _full((1, MLA_Q_LORA)), _full((1, MLA_KV_LORA)),
                  _full(wq.shape), _full(wkv.shape), pos(H * LANES), pos(H * LANES), pos(LANES), pos(LANES)],
        out_specs=[tok(H * dqk), tok(H * dqk), tok(H * dqk)],
        out_shape=[sds(H * dqk), sds(H * dqk), sds(H * dqk)],
        compiler_params=_cparams("parallel", "parallel"),
        name="mla_proj",
    )(misc3, q_norm[None, :], kv_norm[None, :], wq, wkv, cq, sq, ck, sk)
    return pl.pallas_call(
        functools.partial(_attn_kernel, scale=(MLA_NOPE + MLA_ROPE) ** -0.5),
        grid=(B, L // tq, nk),
        in_specs=[pl.BlockSpec((1, tq, H * dqk), lambda b, i, j: (b, i, 0)),
                  pl.BlockSpec((1, tk, H * dqk), lambda b, i, j: (b, j, 0)),
                  pl.BlockSpec((1, tk, H * dqk), lambda b, i, j: (b, j, 0))],
        out_specs=pl.BlockSpec((1, tq, H * MLA_V), lambda b, i, j: (b, i, 0)),
        out_shape=jax.ShapeDtypeStruct((B, L, H * MLA_V), F32),
        scratch_shapes=([pltpu.VMEM((tq, 1), F32)] * H + [pltpu.VMEM((tq, 2 * MLA_V), F32)] * H),
        compiler_params=_cparams("parallel", "parallel", "arbitrary"),
        name="mla_attn",
    )(q, k, v)


_BETA_LANE = 64
_G_LANE = 72
CH = GDN_CHUNK
NDH = 2 * GDN_HEADS
GDN_STEP_CHUNKS = 4


def _gdn_prep_kernel(c_ref, m_ref, alog_ref, dtb_ref, qn_ref, kn_ref, bg_ref):
    for h in range(GDN_HEADS):
        sl = slice(h * GDN_DK, (h + 1) * GDN_DK)
        qh = c_ref[0, :, sl]
        qn_ref[0, :, sl] = (qh * lax.rsqrt(jnp.sum(qh * qh, axis=-1, keepdims=True) + NORM_EPS)
                            * (GDN_DK ** -0.5))
        kh = c_ref[0, :, GDN_HEADS * GDN_DK + h * GDN_DK:GDN_HEADS * GDN_DK + (h + 1) * GDN_DK]
        kn_ref[0, :, sl] = kh * lax.rsqrt(jnp.sum(kh * kh, axis=-1, keepdims=True) + NORM_EPS)
    m = m_ref[0]
    lane = lax.broadcasted_iota(jnp.int32, m.shape, 1)
    beta = jax.nn.sigmoid(m)
    x = m + dtb_ref[...]
    softplus = jnp.maximum(x, 0.0) + jnp.log1p(jnp.exp(-jnp.abs(x)))
    g = -jnp.exp(alog_ref[...]) * softplus
    is_beta = (lane >= _BETA_LANE) & (lane < _BETA_LANE + NDH)
    is_g = (lane >= _G_LANE) & (lane < _G_LANE + NDH)
    bg_ref[0] = jnp.where(is_beta, beta, jnp.where(is_g, g, 0.0))


def _split3(x):
    hi = x.astype(BF16)
    r = x - hi.astype(F32)
    mid = r.astype(BF16)
    lo = (r - mid.astype(F32)).astype(BF16)
    return hi, mid, lo


def _dot_nt(a, b):
    return lax.dot_general(a, b, (((1,), (1,)), ((), ())), preferred_element_type=F32)


def _gdn_chunk_kernel(q_ref, k_ref, bg_ref, a_ref, attn_ref, gc_ref):
    lane = lax.broadcasted_iota(jnp.int32, (CH, LANES), 1)
    is_g = (lane >= _G_LANE) & (lane < _G_LANE + NDH)
    ri = lax.broadcasted_iota(jnp.int32, (CH, CH), 0)
    ci = lax.broadcasted_iota(jnp.int32, (CH, CH), 1)
    lower = ri >= ci
    upper = ri <= ci
    tril = lower.astype(BF16)
    triu = upper.astype(BF16)
    for c in range(GDN_STEP_CHUNKS):
        rows = slice(c * CH, (c + 1) * CH)
        bg = bg_ref[0, rows, :]
        pieces = _split3(jnp.where(is_g, bg, 0.0))
        pre = sum(jnp.dot(tril, p, preferred_element_type=F32) for p in pieces)
        suf = sum(jnp.dot(triu, p, preferred_element_type=F32) for p in pieces)
        gc = jnp.where(lane >= _G_LANE + GDN_HEADS, suf, pre)
        gc_ref[0, rows, :] = gc
        gct = gc.T
        for d in range(2):
            causal = lower if d == 0 else upper
            strict = (ri > ci) if d == 0 else (ri < ci)
            for h in range(GDN_HEADS):
                dh = d * GDN_HEADS + h
                sl = slice(h * GDN_DK, (h + 1) * GDN_DK)
                kh = k_ref[0, rows, sl]
                kb = (kh * bg[:, _BETA_LANE + dh:_BETA_LANE + dh + 1]).astype(BF16)
                khb = kh.astype(BF16)
                diff = gc[:, _G_LANE + dh:_G_LANE + dh + 1] - gct[_G_LANE + dh:_G_LANE + dh + 1, :]
                dec = jnp.exp(jnp.where(causal, diff, -jnp.inf))
                a_ref[0, c, dh] = jnp.where(strict, _dot_nt(kb, khb) * dec, 0.0)
                attn_ref[0, c, dh] = _dot_nt(q_ref[0, rows, sl].astype(BF16), khb) * dec


def _gdn_solve_kernel(a_ref, t_ref, *, nblk_fwd):
    bwd = pl.program_id(0) >= nblk_fwd
    t_ref[...] = jnp.zeros(t_ref.shape, F32)
    sub = lax.broadcasted_iota(jnp.int32, (CH, LANES), 0)

    def row(it, carry):
        i = jnp.where(bwd, CH - 1 - it, it)

        def blk(jb, acc):
            base = pl.multiple_of(jb * 8, 8)
            ablk = a_ref[i, pl.ds(base, 8), :]
            for jj in range(8):
                acc = acc - ablk[jj:jj + 1, :] * t_ref[base + jj]
            return acc

        lo = jnp.where(bwd, i // 8, 0)
        hi = jnp.where(bwd, CH // 8, i // 8 + 1)
        t_ref[i] = lax.fori_loop(lo, hi, blk, (sub == i).astype(F32))
        return carry

    lax.fori_loop(0, CH, row, 0)


def _gdn_scan_kernel(qf_ref, kf_ref, vf_ref, bgf_ref, gcf_ref, tf_ref, af_ref,
                     qb_ref, kb_ref, vb_ref, bgb_ref, gcb_ref, tb_ref, ab_ref,
                     of_ref, ob_ref, *s_refs):
    @pl.when(pl.program_id(1) == 0)
    def _():
        for s_ref in s_refs:
            s_ref[...] = jnp.zeros(s_ref.shape, F32)

    dirs = ((qf_ref, kf_ref, vf_ref, bgf_ref, gcf_ref, tf_ref, af_ref, of_ref, CH - 1),
            (qb_ref, kb_ref, vb_ref, bgb_ref, gcb_ref, tb_ref, ab_ref, ob_ref, 0))
    for step in range(GDN_STEP_CHUNKS):
        probs = []
        for d, (q_ref, k_ref, v_ref, bg_ref, gc_ref, t_ref, a_ref, o_ref, last) in enumerate(dirs):
            c = step if d == 0 else GDN_STEP_CHUNKS - 1 - step
            rows = slice(c * CH, (c + 1) * CH)
            for h in range(GDN_HEADS):
                dh = d * GDN_HEADS + h
                sl = slice(h * GDN_DK, (h + 1) * GDN_DK)
                k, v = k_ref[0, rows, sl], v_ref[0, rows, sl]
                beta = bg_ref[0, rows, _BETA_LANE + dh:_BETA_LANE + dh + 1]
                gc = gc_ref[0, rows, _G_LANE + dh:_G_LANE + dh + 1]
                gl = gc_ref[0, c * CH + last:c * CH + last + 1, _G_LANE + dh:_G_LANE + dh + 1]
                egc = jnp.exp(gc)
                rhs = jnp.concatenate([v * beta, k * beta * egc], axis=-1).astype(BF16)
                sol = jnp.dot(t_ref[0, c, h], rhs, preferred_element_type=F32)
                s = s_refs[dh][...]
                probs.append(dict(sl=sl, rows=rows, o_ref=o_ref, s_ref=s_refs[dh], s=s, sb=s.astype(BF16),
                                  u=sol[:, :GDN_DV], w=sol[:, GDN_DV:].astype(BF16),
                                  qd=(q_ref[0, rows, sl] * egc).astype(BF16),
                                  a=a_ref[0, c, h].astype(BF16),
                                  kd=(k * jnp.exp(gl - gc)).astype(BF16), dec=jnp.exp(gl)))
        for p in probs:
            p["vn"] = (p["u"] - jnp.dot(p["w"], p["sb"], preferred_element_type=F32)).astype(BF16)
        for p in probs:
            p["o_ref"][0, p["rows"], p["sl"]] = (jnp.dot(p["qd"], p["sb"], preferred_element_type=F32)
                                                 + jnp.dot(p["a"], p["vn"], preferred_element_type=F32))
        for p in probs:
            p["s_ref"][...] = p["s"] * p["dec"] + lax.dot_general(
                p["kd"], p["vn"], (((0,), (0,)), ((), ())), preferred_element_type=F32)


def _gdn_out_kernel(of_ref, ob_ref, z_ref, n_ref, o_ref):
    z = z_ref[...].astype(F32)
    for h in range(GDN_HEADS):
        sl = slice(h * GDN_DV, (h + 1) * GDN_DV)
        o = of_ref[:, sl] + ob_ref[:, sl]
        y = o * lax.rsqrt(jnp.mean(o * o, axis=-1, keepdims=True) + NORM_EPS) * n_ref[...]
        zh = z[:, sl]
        o_ref[:, sl] = y * (zh * jax.nn.sigmoid(zh))


def gdn_mixer(proj3, misc3, conv_w, a_log, dt_bias, out_norm, tl=512):
    B, L, _ = proj3.shape
    T = B * L
    N = L // CH
    H = GDN_HEADS
    tl = min(tl, L)
    qkv = dwconv3(proj3, 3, 1536, conv_w, jnp.zeros((1, 1536), F32), silu=True)
    lane_vec = lambda p: jnp.zeros((1, LANES), F32).at[0, _G_LANE:_G_LANE + NDH].set(p.reshape(-1))
    tok = lambda w, c=0: pl.BlockSpec((1, tl, w), lambda b, i: (b, i, c))
    qn, kn, bg = pl.pallas_call(
        _gdn_prep_kernel,
        grid=(B, L // tl),
        in_specs=[tok(1536), tok(LANES, 3), _full((1, LANES)), _full((1, LANES))],
        out_specs=[tok(512), tok(512), tok(LANES)],
        out_shape=[jax.ShapeDtypeStruct((B, L, 512), F32), jax.ShapeDtypeStruct((B, L, 512), F32),
                   jax.ShapeDtypeStruct((B, L, LANES), F32)],
        compiler_params=_cparams("parallel", "parallel"),
        name="gdn_prep",
    )(qkv, misc3, lane_vec(a_log), lane_vec(dt_bias))

    CB = GDN_STEP_CHUNKS
    NB = N // CB
    chunk = lambda w, c=0: pl.BlockSpec((1, CB * CH, w), lambda b, n: (b, n, c))
    mats = pl.BlockSpec((1, CB, NDH, CH, CH), lambda b, n: (b, n, 0, 0, 0))
    mat_shape = jax.ShapeDtypeStruct((B, N, NDH, CH, CH), F32)
    a, attn, gc = pl.pallas_call(
        _gdn_chunk_kernel,
        grid=(B, NB),
        in_specs=[chunk(512), chunk(512), chunk(LANES)],
        out_specs=[mats, mats, chunk(LANES)],
        out_shape=[mat_shape, mat_shape, jax.ShapeDtypeStruct((B, L, LANES), F32)],
        compiler_params=_cparams("parallel", "parallel"),
        name="gdn_chunk",
    )(qn, kn, bg)

    P = NDH * B * N
    at = a.transpose(3, 4, 2, 0, 1).reshape(CH, CH, P)
    tt = pl.pallas_call(
        functools.partial(_gdn_solve_kernel, nblk_fwd=P // LANES // 2),
        grid=(P // LANES,),
        in_specs=[pl.BlockSpec((CH, CH, LANES), lambda p: (0, 0, p))],
        out_specs=pl.BlockSpec((CH, CH, LANES), lambda p: (0, 0, p)),
        out_shape=jax.ShapeDtypeStruct((CH, CH, P), F32),
        compiler_params=_cparams("parallel"),
        name="gdn_solve",
    )(at)
    tmat = tt.reshape(CH, CH, NDH, B, N).transpose(3, 4, 2, 0, 1).astype(BF16)

    fwd = lambda w, c=0: pl.BlockSpec((1, CB * CH, w), lambda b, n: (b, n, c))
    bwd = lambda w, c=0: pl.BlockSpec((1, CB * CH, w), lambda b, n: (b, NB - 1 - n, c))
    mf = pl.BlockSpec((1, CB, H, CH, CH), lambda b, n: (b, n, 0, 0, 0))
    mb = pl.BlockSpec((1, CB, H, CH, CH), lambda b, n: (b, NB - 1 - n, 1, 0, 0))
    o_f, o_b = pl.pallas_call(
        _gdn_scan_kernel,
        grid=(B, NB),
        in_specs=[fwd(512), fwd(512), fwd(512, 2), fwd(LANES), fwd(LANES), mf, mf,
                  bwd(512), bwd(512), bwd(512, 2), bwd(LANES), bwd(LANES), mb, mb],
        out_specs=[fwd(512), bwd(512)],
        out_shape=[jax.ShapeDtypeStruct((B, L, 512), F32), jax.ShapeDtypeStruct((B, L, 512), F32)],
        scratch_shapes=[pltpu.VMEM((GDN_DK, GDN_DV), F32)] * NDH,
        compiler_params=_cparams("parallel", "arbitrary"),
        name="gdn_scan",
    )(qn, kn, qkv, bg, gc, tmat, attn, qn, kn, qkv, bg, gc, tmat, attn)

    tm = min(1024, T)
    row = lambda c=0: pl.BlockSpec((tm, 512), lambda i: (i, c))
    return pl.pallas_call(
        _gdn_out_kernel,
        grid=(T // tm,),
        in_specs=[row(), row(), row(12), _full((1, GDN_DV))],
        out_specs=row(),
        out_shape=jax.ShapeDtypeStruct((T, 512), F32),
        compiler_params=_cparams("parallel"),
        name="gdn_out",
    )(o_f.reshape(T, 512), o_b.reshape(T, 512), proj3.reshape(T, MAIN_W), out_norm[None, :])


def _merge_kernel(x_ref, g_ref, oh_ref, om_ref, og_ref, wb_ref, wo_ref, n_ref, o_ref):
    merged = None
    for i, b_ref in enumerate((oh_ref, om_ref, og_ref)):
        gate = jax.nn.sigmoid(g_ref[:, i * D_MODEL:(i + 1) * D_MODEL].astype(F32))
        term = gate * jnp.dot(b_ref[...].astype(BF16), wb_ref[i], preferred_element_type=F32)
        merged = term if merged is None else merged + term
    y = jnp.dot(merged.astype(BF16), wo_ref[...], preferred_element_type=F32)
    r = lax.rsqrt(jnp.mean(y * y, axis=-1, keepdims=True) + NORM_EPS)
    o_ref[...] = x_ref[...] + y * r * n_ref[...]


def merge_out(x, proj, o_hy, o_mla, o_gdn, w_branch, w_out, norm_post, tm=512):
    T = x.shape[0]
    tm = min(tm, T)
    row = lambda w: pl.BlockSpec((tm, w), lambda i: (i, 0))
    return pl.pallas_call(
        _merge_kernel,
        grid=(T // tm,),
        in_specs=[row(D_MODEL), row(N_BRANCH * D_MODEL), row(BRANCH_W), row(BRANCH_W), row(BRANCH_W),
                  _full((N_BRANCH, BRANCH_W, D_MODEL)), _full((D_MODEL, D_MODEL)), _full((1, D_MODEL))],
        out_specs=row(D_MODEL),
        out_shape=jax.ShapeDtypeStruct((T, D_MODEL), F32),
        compiler_params=_cparams("parallel"),
        name="merge_out",
    )(x, proj, o_hy, o_mla, o_gdn, w_branch, w_out, norm_post)


def _ffn_kernel(x_ref, gpre_ref, wg_ref, wu_ref, wd_ref, gpost_ref, o_ref, h_ref, acc_ref):
    j = pl.program_id(1)

    @pl.when(j == 0)
    def _():
        x = x_ref[...]
        r = lax.rsqrt(jnp.mean(x * x, axis=-1, keepdims=True) + NORM_EPS)
        h_ref[...] = (x * r * gpre_ref[...]).astype(BF16)

    h = h_ref[...]
    gate = jnp.dot(h, wg_ref[...], preferred_element_type=F32)
    up = jnp.dot(h, wu_ref[...], preferred_element_type=F32)
    part = jnp.dot((gate * jax.nn.sigmoid(gate) * up).astype(BF16), wd_ref[...], preferred_element_type=F32)

    @pl.when(j == 0)
    def _():
        acc_ref[...] = part

    @pl.when(j > 0)
    def _():
        acc_ref[...] += part

    @pl.when(j == pl.num_programs(1) - 1)
    def _():
        f = acc_ref[...]
        r = lax.rsqrt(jnp.mean(f * f, axis=-1, keepdims=True) + NORM_EPS)
        o_ref[...] = x_ref[...] + f * r * gpost_ref[...]


def ffn(x, g_pre, w_gate, w_up, w_down, g_post, tm=512):
    T = x.shape[0]
    dff = w_gate.shape[1]
    tf = dff // 2
    tm = min(tm, T)
    return pl.pallas_call(
        _ffn_kernel,
        grid=(T // tm, dff // tf),
        in_specs=[pl.BlockSpec((tm, D_MODEL), lambda i, j: (i, 0)), _full((1, D_MODEL)),
                  pl.BlockSpec((D_MODEL, tf), lambda i, j: (0, j)),
                  pl.BlockSpec((D_MODEL, tf), lambda i, j: (0, j)),
                  pl.BlockSpec((tf, D_MODEL), lambda i, j: (j, 0)), _full((1, D_MODEL))],
        out_specs=pl.BlockSpec((tm, D_MODEL), lambda i, j: (i, 0)),
        out_shape=jax.ShapeDtypeStruct((T, D_MODEL), F32),
        scratch_shapes=[pltpu.VMEM((tm, D_MODEL), BF16), pltpu.VMEM((tm, D_MODEL), F32)],
        compiler_params=_cparams("parallel", "arbitrary"),
        name="ffn",
    )(x, g_pre, w_gate, w_up, w_down, g_post)


def _split_w_in(w_in):
    cols = lambda off, n: w_in[:, off:off + n]
    main = jnp.concatenate([cols(_OFF_GATE, N_BRANCH * D_MODEL), cols(_OFF_HY, 1536),
                            cols(_OFF_GQKV, 1536), cols(_OFF_GZ, 512)], axis=-1).astype(BF16)
    half = MLA_ROPE // 2
    kpe = _OFF_MKV + MLA_KV_LORA
    zeros = lambda n: jnp.zeros((D_MODEL, n), w_in.dtype)
    misc = jnp.concatenate([cols(_OFF_MQ, MLA_Q_LORA), cols(_OFF_MKV, MLA_KV_LORA),
                            cols(kpe, MLA_ROPE), cols(_OFF_GB, 8), cols(_OFF_GA, 8), zeros(48),
                            cols(kpe + half, half), cols(kpe, half), zeros(64)], axis=-1).astype(BF16)
    return main, misc


def trunk_layer(x, norm_mix_pre, norm_mix_post, norm_ffn_pre, norm_ffn_post, w_in,
                hy_conv_w, hy_conv_b, hy_ffn_w1, hy_ffn_b1, hy_sin_freq, hy_ffn_w2, hy_ffn_b2,
                hy_ffn_w3, hy_skip, mla_q_norm, mla_wq_b, mla_kv_norm, mla_wkv_b,
                gdn_conv_w, gdn_a_log, gdn_dt_bias, gdn_out_norm,
                w_branch, w_out, w_gate, w_up, w_down):
    B, L, D = x.shape
    T = B * L
    xt = x.reshape(T, D)
    w_main, w_misc = _split_w_in(w_in)
    g_pre = norm_mix_pre[None, :]
    proj = norm_mm(xt, g_pre, w_main, tm=2048, tn=512, out_dtype=BF16)
    misc = norm_mm(xt, g_pre, w_misc, tm=1024, tn=MISC_W, out_dtype=F32)
    proj3 = proj.reshape(B, L, MAIN_W)
    misc3 = misc.reshape(B, L, MISC_W)
    o_hy = hyena_mixer(proj3, hy_conv_w, hy_conv_b, hy_ffn_w1, hy_ffn_b1, hy_sin_freq,
                       hy_ffn_w2, hy_ffn_b2, hy_ffn_w3, hy_skip)
    o_mla = mla_mixer(misc3, mla_q_norm, mla_wq_b, mla_kv_norm, mla_wkv_b).reshape(T, BRANCH_W)
    o_gdn = gdn_mixer(proj3, misc3, gdn_conv_w, gdn_a_log, gdn_dt_bias, gdn_out_norm).reshape(T, BRANCH_W)
    xt = merge_out(xt, proj, o_hy, o_mla, o_gdn, w_branch.astype(BF16), w_out.astype(BF16),
                   norm_mix_post[None, :])
    xt = ffn(xt, norm_ffn_pre[None, :], w_gate.astype(BF16), w_up.astype(BF16), w_down.astype(BF16),
             norm_ffn_post[None, :])
    return xt.reshape(B, L, D)


def kernel(x_prompt, x_sample, norm_mix_pre, norm_mix_post, norm_ffn_pre, norm_ffn_post, w_in,
           hy_conv_w, hy_conv_b, hy_ffn_w1, hy_ffn_b1, hy_sin_freq, hy_ffn_w2, hy_ffn_b2,
           hy_ffn_w3, hy_skip, mla_q_norm, mla_wq_b, mla_kv_norm, mla_wkv_b,
           gdn_conv_w, gdn_a_log, gdn_dt_bias, gdn_out_norm,
           w_branch, w_out, w_gate, w_up, w_down):
    weights = (norm_mix_pre, norm_mix_post, norm_ffn_pre, norm_ffn_post, w_in,
               hy_conv_w, hy_conv_b, hy_ffn_w1, hy_ffn_b1, hy_sin_freq, hy_ffn_w2, hy_ffn_b2,
               hy_ffn_w3, hy_skip, mla_q_norm, mla_wq_b, mla_kv_norm, mla_wkv_b,
               gdn_conv_w, gdn_a_log, gdn_dt_bias, gdn_out_norm,
               w_branch, w_out, w_gate, w_up, w_down)

    def run_trunk(x):
        for layer in range(DEPTH):
            x = trunk_layer(x, *[w[layer] for w in weights])
        return x

    return (run_trunk(x_prompt), run_trunk(x_sample))
```

```python
import functools
import math

import jax
import jax.numpy as jnp
from jax import lax
from jax.experimental import pallas as pl
from jax.experimental.pallas import tpu as pltpu
import numpy as np

F32 = jnp.float32
BF16 = jnp.bfloat16

D_MODEL = 1024
DEPTH = 2
BRANCH_W = 512
N_BRANCH = 3
HY_W = BRANCH_W
HY_ORDER = 2
HY_EMB = 33
HY_FAST_PCT = 0.3
HY_SLOW_PCT = 1.5
HY_TARGET = 1e-2
MLA_HEADS = 4
MLA_NOPE = 128
MLA_ROPE = 64
MLA_V = 128
MLA_Q_LORA = 256
MLA_KV_LORA = 128
ROPE_THETA = 10000.0
GDN_HEADS = 4
GDN_DK = 128
GDN_DV = 128
GDN_CHUNK = 64
NORM_EPS = 1e-6

_OFF_HY = 0
_OFF_MQ = _OFF_HY + (HY_ORDER + 1) * HY_W
_OFF_MKV = _OFF_MQ + MLA_Q_LORA
_OFF_GQKV = _OFF_MKV + MLA_KV_LORA + MLA_ROPE
_OFF_GZ = _OFF_GQKV + GDN_HEADS * (2 * GDN_DK + GDN_DV)
_OFF_GB = _OFF_GZ + GDN_HEADS * GDN_DV
_OFF_GA = _OFF_GB + 2 * GDN_HEADS
_OFF_GATE = _OFF_GA + 2 * GDN_HEADS
_D_IN = _OFF_GATE + N_BRANCH * D_MODEL

MAIN_W = 3072 + 1536 + 1536 + 512
MISC_W = 640

LANES = 128
VMEM_LIMIT_BYTES = 56 * 1024 * 1024


def _cparams(*sem):
    return pltpu.CompilerParams(dimension_semantics=sem, vmem_limit_bytes=VMEM_LIMIT_BYTES)


def _full(shape):
    nd = len(shape)
    return pl.BlockSpec(shape, lambda *_: (0,) * nd)


def _norm_mm_kernel(x_ref, g_ref, w_ref, o_ref, h_ref):
    @pl.when(pl.program_id(1) == 0)
    def _():
        x = x_ref[...]
        r = lax.rsqrt(jnp.mean(x * x, axis=-1, keepdims=True) + NORM_EPS)
        h_ref[...] = (x * r * g_ref[...]).astype(BF16)

    o_ref[...] = jnp.dot(h_ref[...], w_ref[...], preferred_element_type=F32).astype(o_ref.dtype)


def norm_mm(x, g, w, tm, tn, out_dtype):
    T, D = x.shape
    N = w.shape[1]
    tm = min(tm, T)
    return pl.pallas_call(
        _norm_mm_kernel,
        grid=(T // tm, N // tn),
        in_specs=[pl.BlockSpec((tm, D), lambda i, j: (i, 0)),
                  pl.BlockSpec((1, D), lambda i, j: (0, 0)),
                  pl.BlockSpec((D, tn), lambda i, j: (0, j))],
        out_specs=pl.BlockSpec((tm, tn), lambda i, j: (i, j)),
        out_shape=jax.ShapeDtypeStruct((T, N), out_dtype),
        scratch_shapes=[pltpu.VMEM((tm, D), BF16)],
        compiler_params=_cparams("parallel", "arbitrary"),
        name="norm_mm",
    )(x, g, w)


def _dwconv_kernel(x_ref, p_ref, n_ref, w_ref, b_ref, o_ref, *, silu):
    i = pl.program_id(1)
    last = pl.num_programs(1) - 1
    x = x_ref[0].astype(F32)
    tl = x.shape[0]
    halo = p_ref.shape[1]
    prev_row = jnp.where(i > 0, p_ref[0, halo - 1:halo, :].astype(F32), 0.0)
    next_row = jnp.where(i < last, n_ref[0, 0:1, :].astype(F32), 0.0)
    rows = lax.broadcasted_iota(jnp.int32, x.shape, 0)
    x_dn = jnp.where(rows == 0, prev_row, pltpu.roll(x, 1, axis=0))
    x_up = jnp.where(rows == tl - 1, next_row, pltpu.roll(x, tl - 1, axis=0))
    out = x_dn * w_ref[0:1, :] + x * w_ref[1:2, :] + x_up * w_ref[2:3, :] + b_ref[...]
    if silu:
        out = out * jax.nn.sigmoid(out)
    o_ref[0] = out


def dwconv3(x, col_blk, cw, w, b, silu, tl=512):
    B, L, _ = x.shape
    tl = min(tl, L)
    halo = 8 * (4 // x.dtype.itemsize)
    nsub = tl // halo
    lastblk = L // halo - 1
    return pl.pallas_call(
        functools.partial(_dwconv_kernel, silu=silu),
        grid=(B, L // tl),
        in_specs=[pl.BlockSpec((1, tl, cw), lambda b_, i: (b_, i, col_blk)),
                  pl.BlockSpec((1, halo, cw), lambda b_, i: (b_, jnp.maximum(i * nsub - 1, 0), col_blk)),
                  pl.BlockSpec((1, halo, cw), lambda b_, i: (b_, jnp.minimum((i + 1) * nsub, lastblk), col_blk)),
                  _full((3, cw)), _full((1, cw))],
        out_specs=pl.BlockSpec((1, tl, cw), lambda b_, i: (b_, i, 0)),
        out_shape=jax.ShapeDtypeStruct((B, L, cw), F32),
        compiler_params=_cparams("parallel", "parallel"),
        name="dwconv3",
    )(x, x, x, w, b)


def _tile_lanes(t, c):
    return t if c == LANES else jnp.concatenate([t] * (c // LANES), axis=-1)


def _fft_s1_kernel(f_ref, x_ref, twr_ref, twi_ref, o_ref, *, nb, n1):
    c = x_ref.shape[-1]
    for t in range(nb):
        a = jnp.dot(f_ref[...], x_ref[0, t], preferred_element_type=F32)
        ar, ai = a[:n1], a[n1:]
        twr = _tile_lanes(twr_ref[t], c)
        twi = _tile_lanes(twi_ref[t], c)
        o_ref[0, t, :n1, :] = (ar * twr + ai * twi).astype(BF16)
        o_ref[0, t, n1:, :] = (ai * twr - ar * twi).astype(BF16)


def _fft_s2_kernel(f_ref, finv_ref, b_ref, kr_ref, ki_ref, twr_ref, twi_ref, o_ref, *, nb, n2):
    c = b_ref.shape[-1]
    for t in range(nb):
        x = jnp.dot(f_ref[...], b_ref[0, t], preferred_element_type=F32)
        xr, xi = x[:n2], x[n2:]
        kr, ki = kr_ref[t], ki_ref[t]
        y = jnp.concatenate([xr * kr - xi * ki, xr * ki + xi * kr], axis=0).astype(BF16)
        cm = jnp.dot(finv_ref[...], y, preferred_element_type=F32)
        cr, ci = cm[:n2], cm[n2:]
        twr = _tile_lanes(twr_ref[t], c)
        twi = _tile_lanes(twi_ref[t], c)
        o_ref[0, t, :n2, :] = (cr * twr - ci * twi).astype(BF16)
        o_ref[0, t, n2:, :] = (cr * twi + ci * twr).astype(BF16)


def _fft_s3_kernel(g_ref, d_ref, o_ref, *, nb):
    for t in range(nb):
        o_ref[0, t] = jnp.dot(g_ref[...], d_ref[0, t], preferred_element_type=F32)


def _dft_tables(L):
    N = 2 * L
    n2 = 128 if N >= 4096 else 16
    n1 = N // n2
    n1h = n1 // 2
    k1 = np.arange(n1)[:, None]
    m1 = np.arange(n1h)[None, :]
    th = 2.0 * np.pi * ((k1 * m1) % n1) / n1
    c, s = np.cos(th), np.sin(th)
    f1 = np.block([[c, s], [-s, c]])
    g1 = np.block([[c.T, -s.T], [s.T, c.T]]) / N
    a2 = np.arange(n2)
    ph = 2.0 * np.pi * ((a2[:, None] * a2[None, :]) % n2) / n2
    c2, s2 = np.cos(ph), np.sin(ph)
    f2 = np.block([[c2, s2], [-s2, c2]])
    f2inv = f2.T
    as_bf16 = lambda m: jnp.asarray(m, dtype=F32).astype(BF16)
    return n1, n2, as_bf16(f1), as_bf16(g1), as_bf16(f2), as_bf16(f2inv)


def _twiddles(n1, n2):
    N = n1 * n2
    prod = (jnp.arange(n2, dtype=jnp.int32)[:, None] * jnp.arange(n1, dtype=jnp.int32)[None, :]) % N
    ang = prod.astype(F32) * (2.0 * math.pi / N)
    shape = (n2, n1, LANES)
    return (jnp.broadcast_to(jnp.cos(ang)[:, :, None], shape),
            jnp.broadcast_to(jnp.sin(ang)[:, :, None], shape))


def fft_long_conv(z, kspec, order, tabs):
    B, L, C = z.shape
    n1, n2, f1, g1, f2, f2inv, tw1r, tw1i, tw2r, tw2i = tabs
    n1h = n1 // 2
    P = B // 2
    nb1 = min(8, n2)
    nb2 = min(8, n1)
    zt = z.astype(BF16).reshape(P, 2, n1h, n2, C).transpose(0, 3, 1, 2, 4).reshape(P, n2, 2 * n1h, C)
    a = pl.pallas_call(
        functools.partial(_fft_s1_kernel, nb=nb1, n1=n1),
        grid=(n2 // nb1, P),
        in_specs=[_full((2 * n1, 2 * n1h)),
                  pl.BlockSpec((1, nb1, 2 * n1h, C), lambda j, p: (p, j, 0, 0)),
                  pl.BlockSpec((nb1, n1, LANES), lambda j, p: (j, 0, 0)),
                  pl.BlockSpec((nb1, n1, LANES), lambda j, p: (j, 0, 0))],
        out_specs=pl.BlockSpec((1, nb1, 2 * n1, C), lambda j, p: (p, j, 0, 0)),
        out_shape=jax.ShapeDtypeStruct((P, n2, 2 * n1, C), BF16),
        compiler_params=_cparams("parallel", "parallel"),
        name="fft_s1",
    )(f1, zt, tw1r, tw1i)
    bt = a.reshape(P, n2, 2, n1, C).transpose(0, 3, 2, 1, 4).reshape(P, n1, 2 * n2, C)
    d = pl.pallas_call(
        functools.partial(_fft_s2_kernel, nb=nb2, n2=n2),
        grid=(n1 // nb2, P),
        in_specs=[_full((2 * n2, 2 * n2)), _full((2 * n2, 2 * n2)),
                  pl.BlockSpec((1, nb2, 2 * n2, C), lambda j, p: (p, j, 0, 0)),
                  pl.BlockSpec((nb2, n2, C), lambda j, p: (j, 0, order)),
                  pl.BlockSpec((nb2, n2, C), lambda j, p: (j, 1, order)),
                  pl.BlockSpec((nb2, n2, LANES), lambda j, p: (j, 0, 0)),
                  pl.BlockSpec((nb2, n2, LANES), lambda j, p: (j, 0, 0))],
        out_specs=pl.BlockSpec((1, nb2, 2 * n2, C), lambda j, p: (p, j, 0, 0)),
        out_shape=jax.ShapeDtypeStruct((P, n1, 2 * n2, C), BF16),
        compiler_params=_cparams("parallel", "parallel"),
        name="fft_s2",
    )(f2, f2inv, bt, kspec, kspec, tw2r, tw2i)
    dt = d.reshape(P, n1, 2, n2, C).transpose(0, 3, 2, 1, 4).reshape(P, n2, 2 * n1, C)
    y = pl.pallas_call(
        functools.partial(_fft_s3_kernel, nb=nb1),
        grid=(n2 // nb1, P),
        in_specs=[_full((2 * n1h, 2 * n1)),
                  pl.BlockSpec((1, nb1, 2 * n1, C), lambda j, p: (p, j, 0, 0))],
        out_specs=pl.BlockSpec((1, nb1, 2 * n1h, C), lambda j, p: (p, j, 0, 0)),
        out_shape=jax.ShapeDtypeStruct((P, n2, 2 * n1h, C), F32),
        compiler_params=_cparams("parallel", "parallel"),
        name="fft_s3",
    )(g1, dt)
    return y.reshape(P, n2, 2, n1h, C).transpose(0, 2, 3, 1, 4).reshape(B, L, C)


def _hy_gate_kernel(g_ref, c_ref, z_ref, s_ref, o_ref):
    o_ref[...] = (g_ref[...] * (c_ref[...] + s_ref[...] * z_ref[...])).astype(o_ref.dtype)


def hy_gate(u, gate_blk, conv, zsrc, z_blk, skip, out_dtype, tl=1024):
    T, C = conv.shape
    tl = min(tl, T)
    return pl.pallas_call(
        _hy_gate_kernel,
        grid=(T // tl,),
        in_specs=[pl.BlockSpec((tl, C), lambda i: (i, gate_blk)),
                  pl.BlockSpec((tl, C), lambda i: (i, 0)),
                  pl.BlockSpec((tl, C), lambda i: (i, z_blk)),
                  _full((1, C))],
        out_specs=pl.BlockSpec((tl, C), lambda i: (i, 0)),
        out_shape=jax.ShapeDtypeStruct((T, C), out_dtype),
        compiler_params=_cparams("parallel"),
        name="hy_gate",
    )(u, conv, zsrc, skip)


def hyena_filter_spectrum(L, n1, n2, tw1r, tw1i, w1, b1, freq, w2, b2, w3):
    t = jnp.linspace(0.0, 1.0, L, dtype=F32)[:, None]
    bands = (HY_EMB - 1) // 2
    wpos = (2.0 * math.pi / L) * jnp.arange(L, dtype=F32)[:, None]
    fr = jnp.linspace(1e-4, bands - 1, bands, dtype=F32)[None, :]
    feats = jnp.concatenate([t, jnp.cos(fr * wpos), -jnp.sin(fr * wpos)], axis=-1)
    deltas = jnp.abs(jnp.linspace(math.log(HY_TARGET) / HY_SLOW_PCT,
                                  math.log(HY_TARGET) / HY_FAST_PCT, HY_W, dtype=F32))
    feats2 = jnp.concatenate([feats, feats[0:1], feats[:0:-1]], axis=0)
    feats2 = jnp.pad(feats2, ((0, 0), (0, LANES - HY_EMB)))
    w1p = jnp.pad(w1, ((0, LANES - HY_EMB), (0, 0)))
    hid = w3.shape[0]
    oc = HY_ORDER * HY_W
    w3d = w3.reshape(hid, HY_ORDER, 2, HY_W).transpose(2, 0, 1, 3).reshape(2, hid, oc)
    tl = min(512, L)
    nh = L // tl
    kern, asum = pl.pallas_call(
        functools.partial(_filter_taps_kernel, zero_tile=nh),
        grid=(2 * nh,),
        in_specs=[pl.BlockSpec((tl, LANES), lambda i: (i, 0)), _full((LANES, hid)), _full((1, hid)),
                  _full((1, hid)), _full((hid, hid)), _full((1, hid)),
                  pl.BlockSpec((1, hid, oc), lambda i: (i // nh, 0, 0)), _full((1, oc))],
        out_specs=[pl.BlockSpec((tl, oc), lambda i: (i, 0)), _full((1, oc))],
        out_shape=[jax.ShapeDtypeStruct((2 * L, oc), F32), jax.ShapeDtypeStruct((1, oc), F32)],
        compiler_params=_cparams("arbitrary"),
        name="filter_taps",
    )(feats2, w1p, b1, freq, w2, b2, w3d, jnp.tile(deltas, HY_ORDER)[None, :])
    return filter_dft(kern, asum, n1, n2, tw1r, tw1i)


def _filter_taps_kernel(f_ref, w1_ref, b1_ref, fq_ref, w2_ref, b2_ref, w3_ref, dl_ref, k_ref, s_ref, *,
                        zero_tile):
    i = pl.program_id(0)
    hp = lax.Precision.HIGHEST
    f = f_ref[...]
    h = jnp.sin(fq_ref[...] * (jnp.dot(f, w1_ref[...], precision=hp, preferred_element_type=F32)
                               + b1_ref[...]))
    h = jnp.sin(fq_ref[...] * (jnp.dot(h, w2_ref[...], precision=hp, preferred_element_type=F32)
                               + b2_ref[...]))
    taps = jnp.dot(h, w3_ref[0], precision=hp, preferred_element_type=F32)
    taps = taps * jnp.exp(-f[:, 0:1] * dl_ref[...])
    row = lax.broadcasted_iota(jnp.int32, taps.shape, 0)
    taps = jnp.where((i == zero_tile) & (row == 0), 0.0, taps)
    k_ref[...] = taps

    @pl.when(i == 0)
    def _():
        s_ref[...] = jnp.zeros(s_ref.shape, F32)

    s_ref[...] += jnp.sum(jnp.abs(taps), axis=0, keepdims=True)


def _spec_s2_kernel(f_ref, b_ref, s_ref, o_ref, *, nb):
    for t in range(nb):
        o_ref[t] = jnp.dot(f_ref[...], b_ref[0, t], preferred_element_type=F32) / s_ref[...]


def filter_dft(kern, asum, n1, n2, tw1r, tw1i):
    N, C = kern.shape
    ct = 512
    k1 = np.arange(n1)
    th = 2.0 * np.pi * ((k1[:, None] * k1[None, :]) % n1) / n1
    f1 = np.concatenate([np.cos(th), -np.sin(th)], axis=0)
    a2 = np.arange(n2)
    ph = 2.0 * np.pi * ((a2[:, None] * a2[None, :]) % n2) / n2
    f2 = np.block([[np.cos(ph), np.sin(ph)], [-np.sin(ph), np.cos(ph)]])

    as_bf16 = lambda m: jnp.asarray(m, dtype=F32).astype(BF16)
    nb1 = min(8, n2)
    nb2 = min(8, n1)
    xt = kern.astype(BF16).reshape(n1, n2, C).transpose(1, 0, 2)[None]
    a = pl.pallas_call(
        functools.partial(_fft_s1_kernel, nb=nb1, n1=n1),
        grid=(n2 // nb1, C // ct),
        in_specs=[_full((2 * n1, n1)),
                  pl.BlockSpec((1, nb1, n1, ct), lambda j, c: (0, j, 0, c)),
                  pl.BlockSpec((nb1, n1, LANES), lambda j, c: (j, 0, 0)),
                  pl.BlockSpec((nb1, n1, LANES), lambda j, c: (j, 0, 0))],
        out_specs=pl.BlockSpec((1, nb1, 2 * n1, ct), lambda j, c: (0, j, 0, c)),
        out_shape=jax.ShapeDtypeStruct((1, n2, 2 * n1, C), BF16),
        compiler_params=_cparams("parallel", "parallel"),
        name="spec_s1",
    )(as_bf16(f1), xt, tw1r, tw1i)
    bt = a.reshape(n2, 2, n1, C).transpose(2, 1, 0, 3).reshape(1, n1, 2 * n2, C)
    return pl.pallas_call(
        functools.partial(_spec_s2_kernel, nb=nb2),
        grid=(n1 // nb2, C // ct),
        in_specs=[_full((2 * n2, 2 * n2)),
                  pl.BlockSpec((1, nb2, 2 * n2, ct), lambda j, c: (0, j, 0, c)),
                  pl.BlockSpec((1, ct), lambda j, c: (0, c))],
        out_specs=pl.BlockSpec((nb2, 2 * n2, ct), lambda j, c: (j, 0, c)),
        out_shape=jax.ShapeDtypeStruct((n1, 2 * n2, C), F32),
        compiler_params=_cparams("parallel", "parallel"),
        name="spec_s2",
    )(as_bf16(f2), bt, asum)


def hyena_mixer(proj3, conv_w, conv_b, w1, b1, freq, w2, b2, w3, skip):
    B, L, _ = proj3.shape
    T = B * L
    u = dwconv3(proj3, 2, 1536, conv_w, conv_b[None, :], silu=False).reshape(T, 1536)
    n1, n2, f1, g1, f2, f2inv = _dft_tables(L)
    tw1r, tw1i = _twiddles(n1, n2)
    tw2r, tw2i = tw1r.transpose(1, 0, 2), tw1i.transpose(1, 0, 2)
    tabs = (n1, n2, f1, g1, f2, f2inv, tw1r, tw1i, tw2r, tw2i)
    kspec = hyena_filter_spectrum(L, n1, n2, tw1r, tw1i, w1, b1[None, :], freq[None, :], w2, b2[None, :], w3)
    conv = fft_long_conv(u[:, 1024:1536].reshape(B, L, HY_W), kspec, 0, tabs).reshape(T, HY_W)
    z1 = hy_gate(u, 0, conv, u, 2, skip[0:1], F32)
    conv = fft_long_conv(z1.reshape(B, L, HY_W), kspec, 1, tabs).reshape(T, HY_W)
    return hy_gate(u, 1, conv, z1, 0, skip[1:2], BF16)


def _mla_proj_kernel(m_ref, qg_ref, kg_ref, wq_ref, wkv_ref, cq_ref, sq_ref, ck_ref, sk_ref,
                     q_ref, k_ref, v_ref):
    m = m_ref[0]
    ql = m[:, 0:256]
    r = lax.rsqrt(jnp.mean(ql * ql, axis=-1, keepdims=True) + NORM_EPS)
    q = jnp.dot((ql * r * qg_ref[...]).astype(BF16), wq_ref[...], preferred_element_type=F32)
    ckv = m[:, 256:384]
    r = lax.rsqrt(jnp.mean(ckv * ckv, axis=-1, keepdims=True) + NORM_EPS)
    kv = jnp.dot((ckv * r * kg_ref[...]).astype(BF16), wkv_ref[...], preferred_element_type=F32)
    q_pe = (q[:, 512:1024] * cq_ref[...] + q[:, 1024:1536] * sq_ref[...]).astype(BF16)
    k_pe = (m[:, 384:512] * ck_ref[...] + m[:, 512:640] * sk_ref[...]).astype(BF16)
    for h in range(MLA_HEADS):
        sl = slice(h * LANES, (h + 1) * LANES)
        q_ref[0, :, 2 * h * LANES:(2 * h + 1) * LANES] = q[:, sl].astype(BF16)
        q_ref[0, :, (2 * h + 1) * LANES:(2 * h + 2) * LANES] = q_pe[:, sl]
        k_ref[0, :, 2 * h * LANES:(2 * h + 1) * LANES] = kv[:, sl].astype(BF16)
        k_ref[0, :, (2 * h + 1) * LANES:(2 * h + 2) * LANES] = k_pe
        v_ref[0, :, 2 * h * LANES:(2 * h + 1) * LANES] = kv[:, 512 + h * MLA_V:512 + (h + 1) * MLA_V].astype(BF16)
        v_ref[0, :, (2 * h + 1) * LANES:(2 * h + 2) * LANES] = jnp.ones((m.shape[0], LANES), BF16)


ATTN_ROW_PARTS = 2


def _attn_kernel(q_ref, k_ref, v_ref, o_ref, *scratch, scale):
    ki = pl.program_id(2)
    c = scale * math.log2(math.e)
    dqk = 2 * LANES
    m_refs, acc_refs = scratch[:MLA_HEADS], scratch[MLA_HEADS:]

    @pl.when(ki == 0)
    def _():
        for m_ref, acc_ref in zip(m_refs, acc_refs):
            m_ref[...] = jnp.full(m_ref.shape, -jnp.inf, F32)
            acc_ref[...] = jnp.zeros(acc_ref.shape, F32)

    part = q_ref.shape[1] // ATTN_ROW_PARTS
    for h, (m_ref, acc_ref) in enumerate(zip(m_refs, acc_refs)):
        sl = slice(h * dqk, (h + 1) * dqk)
        for r in range(ATTN_ROW_PARTS):
            rows = slice(r * part, (r + 1) * part)
            s = lax.dot_general(q_ref[0, rows, sl], k_ref[0, :, sl], (((1,), (1,)), ((), ())),
                                preferred_element_type=F32)
            m_prev = m_ref[rows, :]
            m_new = jnp.maximum(m_prev, jnp.max(s, axis=-1, keepdims=True) * c)
            p = jnp.exp2(s * c - m_new).astype(BF16)
            acc_ref[rows, :] = (jnp.exp2(m_prev - m_new) * acc_ref[rows, :]
                                + jnp.dot(p, v_ref[0, :, sl], preferred_element_type=F32))
            m_ref[rows, :] = m_new

    @pl.when(ki == pl.num_programs(2) - 1)
    def _():
        for h, acc_ref in enumerate(acc_refs):
            o_ref[0, :, h * MLA_V:(h + 1) * MLA_V] = (acc_ref[:, :MLA_V] / acc_ref[:, MLA_V:]).astype(BF16)


def _rope_lane_tables(L):
    half = MLA_ROPE // 2
    inv = ROPE_THETA ** (-jnp.arange(half, dtype=F32) / half)
    ang = jnp.arange(L, dtype=F32)[:, None] * inv[None, :]
    cos, sin = jnp.cos(ang), jnp.sin(ang)
    zeros = jnp.zeros((L, LANES - MLA_ROPE), F32)
    ck = jnp.concatenate([cos, cos, zeros], axis=-1)
    sk = jnp.concatenate([-sin, sin, zeros], axis=-1)
    return jnp.tile(ck, (1, MLA_HEADS)), jnp.tile(sk, (1, MLA_HEADS)), ck, sk


def mla_mixer(misc3, q_norm, wq_b, kv_norm, wkv_b, tl=512, tq=1024, tk=1024):
    B, L, _ = misc3.shape
    H = MLA_HEADS
    tl, tq, tk = min(tl, L), min(tq, L), min(tk, L)
    half = MLA_ROPE // 2
    wq = wq_b.reshape(MLA_Q_LORA, H, MLA_NOPE + MLA_ROPE)
    x1 = wq[:, :, MLA_NOPE:MLA_NOPE + half]
    x2 = wq[:, :, MLA_NOPE + half:]
    zpad = jnp.zeros((MLA_Q_LORA, H, LANES - MLA_ROPE), wq.dtype)
    wq = jnp.concatenate([wq[:, :, :MLA_NOPE].reshape(MLA_Q_LORA, -1),
                          jnp.concatenate([x1, x2, zpad], axis=-1).reshape(MLA_Q_LORA, -1),
                          jnp.concatenate([x2, x1, zpad], axis=-1).reshape(MLA_Q_LORA, -1)],
                         axis=-1).astype(BF16)
    wkv = wkv_b.reshape(MLA_KV_LORA, H, MLA_NOPE + MLA_V)
    wkv = jnp.concatenate([wkv[:, :, :MLA_NOPE].reshape(MLA_KV_LORA, -1),
                           wkv[:, :, MLA_NOPE:].reshape(MLA_KV_LORA, -1)], axis=-1).astype(BF16)
    cq, sq, ck, sk = _rope_lane_tables(L)
    tok = lambda w: pl.BlockSpec((1, tl, w), lambda b, i: (b, i, 0))
    pos = lambda w: pl.BlockSpec((tl, w), lambda b, i: (i, 0))
    sds = lambda w: jax.ShapeDtypeStruct((B, L, w), BF16)
    dqk = 2 * LANES
    nk = L // tk
    q, k, v = pl.pallas_call(
        _mla_proj_kernel,
        grid=(B, L // tl),
        in_specs=[tok(MISC_W), _full((1, MLA_Q_LORA)), _full((1, MLA_KV_LORA)),
                  _full(wq.shape), _full(wkv.shape), pos(H * LANES), pos(H * LANES), pos(LANES), pos(LANES)],
        out_specs=[tok(H * dqk), tok(H * dqk), tok(H * dqk)],
        out_shape=[sds(H * dqk), sds(H * dqk), sds(H * dqk)],
        compiler_params=_cparams("parallel", "parallel"),
        name="mla_proj",
    )(misc3, q_norm[None, :], kv_norm[None, :], wq, wkv, cq, sq, ck, sk)
    return pl.pallas_call(
        functools.partial(_attn_kernel, scale=(MLA_NOPE + MLA_ROPE) ** -0.5),
        grid=(B, L // tq, nk),
        in_specs=[pl.BlockSpec((1, tq, H * dqk), lambda b, i, j: (b, i, 0)),
                  pl.BlockSpec((1, tk, H * dqk), lambda b, i, j: (b, j, 0)),
                  pl.BlockSpec((1, tk, H * dqk), lambda b, i, j: (b, j, 0))],
        out_specs=pl.BlockSpec((1, tq, H * MLA_V), lambda b, i, j: (b, i, 0)),
        out_shape=jax.ShapeDtypeStruct((B, L, H * MLA_V), BF16),
        scratch_shapes=([pltpu.VMEM((tq, 1), F32)] * H + [pltpu.VMEM((tq, 2 * MLA_V), F32)] * H),
        compiler_params=_cparams("parallel", "parallel", "arbitrary"),
        name="mla_attn",
    )(q, k, v)


_BETA_LANE = 64
_G_LANE = 72
CH = GDN_CHUNK
NDH = 2 * GDN_HEADS
GDN_STEP_CHUNKS = 8


def _gdn_prep_kernel(c_ref, m_ref, alog_ref, dtb_ref, qn_ref, kn_ref, bg_ref):
    for h in range(GDN_HEADS):
        sl = slice(h * GDN_DK, (h + 1) * GDN_DK)
        qh = c_ref[0, :, sl]
        qn_ref[0, :, sl] = (qh * lax.rsqrt(jnp.sum(qh * qh, axis=-1, keepdims=True) + NORM_EPS)
                            * (GDN_DK ** -0.5))
        kh = c_ref[0, :, GDN_HEADS * GDN_DK + h * GDN_DK:GDN_HEADS * GDN_DK + (h + 1) * GDN_DK]
        kn_ref[0, :, sl] = kh * lax.rsqrt(jnp.sum(kh * kh, axis=-1, keepdims=True) + NORM_EPS)
    m = m_ref[0]
    lane = lax.broadcasted_iota(jnp.int32, m.shape, 1)
    beta = jax.nn.sigmoid(m)
    x = m + dtb_ref[...]
    softplus = jnp.maximum(x, 0.0) + jnp.log1p(jnp.exp(-jnp.abs(x)))
    g = -jnp.exp(alog_ref[...]) * softplus
    is_beta = (lane >= _BETA_LANE) & (lane < _BETA_LANE + NDH)
    is_g = (lane >= _G_LANE) & (lane < _G_LANE + NDH)
    bg_ref[0] = jnp.where(is_beta, beta, jnp.where(is_g, g, 0.0))


def _split3(x):
    hi = x.astype(BF16)
    r = x - hi.astype(F32)
    mid = r.astype(BF16)
    lo = (r - mid.astype(F32)).astype(BF16)
    return hi, mid, lo


def _dot_nt(a, b):
    return lax.dot_general(a, b, (((1,), (1,)), ((), ())), preferred_element_type=F32)


def _gdn_chunk_kernel(q_ref, k_ref, bg_ref, a_ref, attn_ref, gc_ref):
    lane = lax.broadcasted_iota(jnp.int32, (CH, LANES), 1)
    is_g = (lane >= _G_LANE) & (lane < _G_LANE + NDH)
    ri = lax.broadcasted_iota(jnp.int32, (CH, CH), 0)
    ci = lax.broadcasted_iota(jnp.int32, (CH, CH), 1)
    lower = ri >= ci
    upper = ri <= ci
    tril = lower.astype(BF16)
    triu = upper.astype(BF16)
    for c in range(GDN_STEP_CHUNKS):
        rows = slice(c * CH, (c + 1) * CH)
        bg = bg_ref[0, rows, :]
        pieces = _split3(jnp.where(is_g, bg, 0.0))
        pre = sum(jnp.dot(tril, p, preferred_element_type=F32) for p in pieces)
        suf = sum(jnp.dot(triu, p, preferred_element_type=F32) for p in pieces)
        gc = jnp.where(lane >= _G_LANE + GDN_HEADS, suf, pre)
        gc_ref[0, rows, :] = gc
        gct = gc.T
        for d in range(2):
            causal = lower if d == 0 else upper
            strict = (ri > ci) if d == 0 else (ri < ci)
            for h in range(GDN_HEADS):
                dh = d * GDN_HEADS + h
                sl = slice(h * GDN_DK, (h + 1) * GDN_DK)
                kh = k_ref[0, rows, sl]
                kb = (kh * bg[:, _BETA_LANE + dh:_BETA_LANE + dh + 1]).astype(BF16)
                khb = kh.astype(BF16)
                diff = gc[:, _G_LANE + dh:_G_LANE + dh + 1] - gct[_G_LANE + dh:_G_LANE + dh + 1, :]
                dec = jnp.exp(jnp.where(causal, diff, -jnp.inf))
                a_ref[0, c, dh] = jnp.where(strict, _dot_nt(kb, khb) * dec, 0.0)
                attn_ref[0, c, dh] = (_dot_nt(q_ref[0, rows, sl].astype(BF16), khb) * dec).astype(BF16)


def _gdn_solve_kernel(a_ref, t_ref, *, nblk_fwd):
    bwd = pl.program_id(0) >= nblk_fwd
    t_ref[...] = jnp.zeros(t_ref.shape, F32)

    @pl.when(jnp.logical_not(bwd))
    def _():
        _solve_triangular(a_ref, t_ref, reverse=False)

    @pl.when(bwd)
    def _():
        _solve_triangular(a_ref, t_ref, reverse=True)


def _solve_triangular(a_ref, t_ref, *, reverse):
    nblk = CH // 8
    sub8 = lax.broadcasted_iota(jnp.int32, (8, LANES), 0)
    for phase in range(nblk):
        rb = nblk - 1 - phase if reverse else phase
        groups = range(rb, nblk) if reverse else range(0, rb + 1)

        def row(it, carry, rb=rb, groups=groups):
            i = 8 * rb + (7 - it if reverse else it)
            acc = {cg: ((sub8 + 8 * cg == i).astype(F32) if cg == rb else jnp.zeros((8, LANES), F32))
                   for cg in groups}
            for jb in groups:
                ablk = a_ref[i, 8 * jb:8 * jb + 8, :]
                for jj in range(8):
                    arow = ablk[jj:jj + 1, :]
                    for cg in (range(jb, nblk) if reverse else range(0, jb + 1)):
                        acc[cg] = acc[cg] - arow * t_ref[8 * jb + jj, 8 * cg:8 * cg + 8, :]
            for cg in groups:
                t_ref[i, 8 * cg:8 * cg + 8, :] = acc[cg]
            return carry

        lax.fori_loop(0, 8, row, 0)


def _gdn_scan_kernel(qf_ref, kf_ref, vf_ref, bgf_ref, gcf_ref, tf_ref, af_ref,
                     qb_ref, kb_ref, vb_ref, bgb_ref, gcb_ref, tb_ref, ab_ref,
                     of_ref, ob_ref, *s_refs):
    @pl.when(pl.program_id(1) == 0)
    def _():
        for s_ref in s_refs:
            s_ref[...] = jnp.zeros(s_ref.shape, F32)

    dirs = ((qf_ref, kf_ref, vf_ref, bgf_ref, gcf_ref, tf_ref, af_ref, of_ref, CH - 1),
            (qb_ref, kb_ref, vb_ref, bgb_ref, gcb_ref, tb_ref, ab_ref, ob_ref, 0))
    for step in range(GDN_STEP_CHUNKS):
        probs = []
        for d, (q_ref, k_ref, v_ref, bg_ref, gc_ref, t_ref, a_ref, o_ref, last) in enumerate(dirs):
            c = step if d == 0 else GDN_STEP_CHUNKS - 1 - step
            rows = slice(c * CH, (c + 1) * CH)
            for h in range(GDN_HEADS):
                dh = d * GDN_HEADS + h
                sl = slice(h * GDN_DK, (h + 1) * GDN_DK)
                k, v = k_ref[0, rows, sl], v_ref[0, rows, sl]
                beta = bg_ref[0, rows, _BETA_LANE + dh:_BETA_LANE + dh + 1]
                gc = gc_ref[0, rows, _G_LANE + dh:_G_LANE + dh + 1]
                gl = gc_ref[0, c * CH + last:c * CH + last + 1, _G_LANE + dh:_G_LANE + dh + 1]
                egc = jnp.exp(gc)
                rhs = jnp.concatenate([v * beta, k * beta * egc], axis=-1).astype(BF16)
                sol = jnp.dot(t_ref[0, c, h], rhs, preferred_element_type=F32)
                s = s_refs[dh][...]
                probs.append(dict(sl=sl, rows=rows, o_ref=o_ref, s_ref=s_refs[dh], s=s, sb=s.astype(BF16),
                                  u=sol[:, :GDN_DV], w=sol[:, GDN_DV:].astype(BF16),
                                  qd=(q_ref[0, rows, sl] * egc).astype(BF16),
                                  a=a_ref[0, c, h].astype(BF16),
                                  kd=(k * jnp.exp(gl - gc)).astype(BF16), dec=jnp.exp(gl)))
        for p in probs:
            p["vn"] = (p["u"] - jnp.dot(p["w"], p["sb"], preferred_element_type=F32)).astype(BF16)
        for p in probs:
            p["o_ref"][0, p["rows"], p["sl"]] = (jnp.dot(p["qd"], p["sb"], preferred_element_type=F32)
                                                 + jnp.dot(p["a"], p["vn"], preferred_element_type=F32))
        for p in probs:
            p["s_ref"][...] = p["s"] * p["dec"] + lax.dot_general(
                p["kd"], p["vn"], (((0,), (0,)), ((), ())), preferred_element_type=F32)


def _gdn_out_kernel(of_ref, ob_ref, z_ref, n_ref, o_ref):
    z = z_ref[...].astype(F32)
    for h in range(GDN_HEADS):
        sl = slice(h * GDN_DV, (h + 1) * GDN_DV)
        o = of_ref[:, sl] + ob_ref[:, sl]
        y = o * lax.rsqrt(jnp.mean(o * o, axis=-1, keepdims=True) + NORM_EPS) * n_ref[...]
        zh = z[:, sl]
        o_ref[:, sl] = (y * (zh * jax.nn.sigmoid(zh))).astype(BF16)


def gdn_mixer(proj3, misc3, conv_w, a_log, dt_bias, out_norm, tl=512):
    B, L, _ = proj3.shape
    T = B * L
    N = L // CH
    H = GDN_HEADS
    tl = min(tl, L)
    qkv = dwconv3(proj3, 3, 1536, conv_w, jnp.zeros((1, 1536), F32), silu=True)
    lane_vec = lambda p: jnp.zeros((1, LANES), F32).at[0, _G_LANE:_G_LANE + NDH].set(p.reshape(-1))
    tok = lambda w, c=0: pl.BlockSpec((1, tl, w), lambda b, i: (b, i, c))
    qn, kn, bg = pl.pallas_call(
        _gdn_prep_kernel,
        grid=(B, L // tl),
        in_specs=[tok(1536), tok(LANES, 3), _full((1, LANES)), _full((1, LANES))],
        out_specs=[tok(512), tok(512), tok(LANES)],
        out_shape=[jax.ShapeDtypeStruct((B, L, 512), F32), jax.ShapeDtypeStruct((B, L, 512), F32),
                   jax.ShapeDtypeStruct((B, L, LANES), F32)],
        compiler_params=_cparams("parallel", "parallel"),
        name="gdn_prep",
    )(qkv, misc3, lane_vec(a_log), lane_vec(dt_bias))

    CB = GDN_STEP_CHUNKS
    NB = N // CB
    chunk = lambda w, c=0: pl.BlockSpec((1, CB * CH, w), lambda b, n: (b, n, c))
    mats = pl.BlockSpec((1, CB, NDH, CH, CH), lambda b, n: (b, n, 0, 0, 0))
    mat_shape = jax.ShapeDtypeStruct((B, N, NDH, CH, CH), F32)
    a, attn, gc = pl.pallas_call(
        _gdn_chunk_kernel,
        grid=(B, NB),
        in_specs=[chunk(512), chunk(512), chunk(LANES)],
        out_specs=[mats, mats, chunk(LANES)],
        out_shape=[mat_shape, jax.ShapeDtypeStruct(mat_shape.shape, BF16),
                   jax.ShapeDtypeStruct((B, L, LANES), F32)],
        compiler_params=_cparams("parallel", "parallel"),
        name="gdn_chunk",
    )(qn, kn, bg)

    P = NDH * B * N
    at = a.transpose(3, 4, 2, 0, 1).reshape(CH, CH, P)
    tt = pl.pallas_call(
        functools.partial(_gdn_solve_kernel, nblk_fwd=P // LANES // 2),
        grid=(P // LANES,),
        in_specs=[pl.BlockSpec((CH, CH, LANES), lambda p: (0, 0, p))],
        out_specs=pl.BlockSpec((CH, CH, LANES), lambda p: (0, 0, p)),
        out_shape=jax.ShapeDtypeStruct((CH, CH, P), F32),
        compiler_params=_cparams("parallel"),
        name="gdn_solve",
    )(at)
    tmat = tt.reshape(CH, CH, NDH, B, N).transpose(3, 4, 2, 0, 1).astype(BF16)

    fwd = lambda w, c=0: pl.BlockSpec((1, CB * CH, w), lambda b, n: (b, n, c))
    bwd = lambda w, c=0: pl.BlockSpec((1, CB * CH, w), lambda b, n: (b, NB - 1 - n, c))
    mf = pl.BlockSpec((1, CB, H, CH, CH), lambda b, n: (b, n, 0, 0, 0))
    mb = pl.BlockSpec((1, CB, H, CH, CH), lambda b, n: (b, NB - 1 - n, 1, 0, 0))
    o_f, o_b = pl.pallas_call(
        _gdn_scan_kernel,
        grid=(B, NB),
        in_specs=[fwd(512), fwd(512), fwd(512, 2), fwd(LANES), fwd(LANES), mf, mf,
                  bwd(512), bwd(512), bwd(512, 2), bwd(LANES), bwd(LANES), mb, mb],
        out_specs=[fwd(512), bwd(512)],
        out_shape=[jax.ShapeDtypeStruct((B, L, 512), F32), jax.ShapeDtypeStruct((B, L, 512), F32)],
        scratch_shapes=[pltpu.VMEM((GDN_DK, GDN_DV), F32)] * NDH,
        compiler_params=_cparams("parallel", "arbitrary"),
        name="gdn_scan",
    )(qn, kn, qkv, bg, gc, tmat, attn, qn, kn, qkv, bg, gc, tmat, attn)

    tm = min(1024, T)
    row = lambda c=0: pl.BlockSpec((tm, 512), lambda i: (i, c))
    return pl.pallas_call(
        _gdn_out_kernel,
        grid=(T // tm,),
        in_specs=[row(), row(), row(12), _full((1, GDN_DV))],
        out_specs=row(),
        out_shape=jax.ShapeDtypeStruct((T, 512), BF16),
        compiler_params=_cparams("parallel"),
        name="gdn_out",
    )(o_f.reshape(T, 512), o_b.reshape(T, 512), proj3.reshape(T, MAIN_W), out_norm[None, :])


def _merge_kernel(x_ref, g_ref, oh_ref, om_ref, og_ref, wb_ref, wo_ref, n_ref, o_ref):
    merged = None
    for i, b_ref in enumerate((oh_ref, om_ref, og_ref)):
        gate = jax.nn.sigmoid(g_ref[:, i * D_MODEL:(i + 1) * D_MODEL].astype(F32))
        term = gate * jnp.dot(b_ref[...].astype(BF16), wb_ref[i], preferred_element_type=F32)
        merged = term if merged is None else merged + term
    y = jnp.dot(merged.astype(BF16), wo_ref[...], preferred_element_type=F32)
    r = lax.rsqrt(jnp.mean(y * y, axis=-1, keepdims=True) + NORM_EPS)
    o_ref[...] = x_ref[...] + y * r * n_ref[...]


def merge_out(x, proj, o_hy, o_mla, o_gdn, w_branch, w_out, norm_post, tm=512):
    T = x.shape[0]
    tm = min(tm, T)
    row = lambda w: pl.BlockSpec((tm, w), lambda i: (i, 0))
    return pl.pallas_call(
        _merge_kernel,
        grid=(T // tm,),
        in_specs=[row(D_MODEL), row(N_BRANCH * D_MODEL), row(BRANCH_W), row(BRANCH_W), row(BRANCH_W),
                  _full((N_BRANCH, BRANCH_W, D_MODEL)), _full((D_MODEL, D_MODEL)), _full((1, D_MODEL))],
        out_specs=row(D_MODEL),
        out_shape=jax.ShapeDtypeStruct((T, D_MODEL), F32),
        compiler_params=_cparams("parallel"),
        name="merge_out",
    )(x, proj, o_hy, o_mla, o_gdn, w_branch, w_out, norm_post)


def _ffn_kernel(x_ref, gpre_ref, wg_ref, wu_ref, wd_ref, gpost_ref, o_ref, h_ref, acc_ref):
    j = pl.program_id(1)

    @pl.when(j == 0)
    def _():
        x = x_ref[...]
        r = lax.rsqrt(jnp.mean(x * x, axis=-1, keepdims=True) + NORM_EPS)
        h_ref[...] = (x * r * gpre_ref[...]).astype(BF16)

    h = h_ref[...]
    gate = jnp.dot(h, wg_ref[...], preferred_element_type=F32)
    up = jnp.dot(h, wu_ref[...], preferred_element_type=F32)
    part = jnp.dot((gate * jax.nn.sigmoid(gate) * up).astype(BF16), wd_ref[...], preferred_element_type=F32)

    @pl.when(j == 0)
    def _():
        acc_ref[...] = part

    @pl.when(j > 0)
    def _():
        acc_ref[...] += part

    @pl.when(j == pl.num_programs(1) - 1)
    def _():
        f = acc_ref[...]
        r = lax.rsqrt(jnp.mean(f * f, axis=-1, keepdims=True) + NORM_EPS)
        o_ref[...] = x_ref[...] + f * r * gpost_ref[...]


def ffn(x, g_pre, w_gate, w_up, w_down, g_post, tm=512):
    T = x.shape[0]
    dff = w_gate.shape[1]
    tf = dff // 2
    tm = min(tm, T)
    return pl.pallas_call(
        _ffn_kernel,
        grid=(T // tm, dff // tf),
        in_specs=[pl.BlockSpec((tm, D_MODEL), lambda i, j: (i, 0)), _full((1, D_MODEL)),
                  pl.BlockSpec((D_MODEL, tf), lambda i, j: (0, j)),
                  pl.BlockSpec((D_MODEL, tf), lambda i, j: (0, j)),
                  pl.BlockSpec((tf, D_MODEL), lambda i, j: (j, 0)), _full((1, D_MODEL))],
        out_specs=pl.BlockSpec((tm, D_MODEL), lambda i, j: (i, 0)),
        out_shape=jax.ShapeDtypeStruct((T, D_MODEL), F32),
        scratch_shapes=[pltpu.VMEM((tm, D_MODEL), BF16), pltpu.VMEM((tm, D_MODEL), F32)],
        compiler_params=_cparams("parallel", "arbitrary"),
        name="ffn",
    )(x, g_pre, w_gate, w_up, w_down, g_post)


def _split_w_in(w_in):
    cols = lambda off, n: w_in[:, off:off + n]
    main = jnp.concatenate([cols(_OFF_GATE, N_BRANCH * D_MODEL), cols(_OFF_HY, 1536),
                            cols(_OFF_GQKV, 1536), cols(_OFF_GZ, 512)], axis=-1).astype(BF16)
    half = MLA_ROPE // 2
    kpe = _OFF_MKV + MLA_KV_LORA
    zeros = lambda n: jnp.zeros((D_MODEL, n), w_in.dtype)
    misc = jnp.concatenate([cols(_OFF_MQ, MLA_Q_LORA), cols(_OFF_MKV, MLA_KV_LORA),
                            cols(kpe, MLA_ROPE), cols(_OFF_GB, 8), cols(_OFF_GA, 8), zeros(48),
                            cols(kpe + half, half), cols(kpe, half), zeros(64)], axis=-1).astype(BF16)
    return main, misc


def trunk_layer(x, norm_mix_pre, norm_mix_post, norm_ffn_pre, norm_ffn_post, w_in,
                hy_conv_w, hy_conv_b, hy_ffn_w1, hy_ffn_b1, hy_sin_freq, hy_ffn_w2, hy_ffn_b2,
                hy_ffn_w3, hy_skip, mla_q_norm, mla_wq_b, mla_kv_norm, mla_wkv_b,
                gdn_conv_w, gdn_a_log, gdn_dt_bias, gdn_out_norm,
                w_branch, w_out, w_gate, w_up, w_down):
    B, L, D = x.shape
    T = B * L
    xt = x.reshape(T, D)
    w_main, w_misc = _split_w_in(w_in)
    g_pre = norm_mix_pre[None, :]
    proj = norm_mm(xt, g_pre, w_main, tm=2048, tn=512, out_dtype=BF16)
    misc = norm_mm(xt, g_pre, w_misc, tm=1024, tn=MISC_W, out_dtype=F32)
    proj3 = proj.reshape(B, L, MAIN_W)
    misc3 = misc.reshape(B, L, MISC_W)
    o_hy = hyena_mixer(proj3, hy_conv_w, hy_conv_b, hy_ffn_w1, hy_ffn_b1, hy_sin_freq,
                       hy_ffn_w2, hy_ffn_b2, hy_ffn_w3, hy_skip)
    o_mla = mla_mixer(misc3, mla_q_norm, mla_wq_b, mla_kv_norm, mla_wkv_b).reshape(T, BRANCH_W)
    o_gdn = gdn_mixer(proj3, misc3, gdn_conv_w, gdn_a_log, gdn_dt_bias, gdn_out_norm).reshape(T, BRANCH_W)
    xt = merge_out(xt, proj, o_hy, o_mla, o_gdn, w_branch.astype(BF16), w_out.astype(BF16),
                   norm_mix_post[None, :])
    xt = ffn(xt, norm_ffn_pre[None, :], w_gate.astype(BF16), w_up.astype(BF16), w_down.astype(BF16),
             norm_ffn_post[None, :])
    return xt.reshape(B, L, D)


def kernel(x_prompt, x_sample, norm_mix_pre, norm_mix_post, norm_ffn_pre, norm_ffn_post, w_in,
           hy_conv_w, hy_conv_b, hy_ffn_w1, hy_ffn_b1, hy_sin_freq, hy_ffn_w2, hy_ffn_b2,
           hy_ffn_w3, hy_skip, mla_q_norm, mla_wq_b, mla_kv_norm, mla_wkv_b,
           gdn_conv_w, gdn_a_log, gdn_dt_bias, gdn_out_norm,
           w_branch, w_out, w_gate, w_up, w_down):
    weights = (norm_mix_pre, norm_mix_post, norm_ffn_pre, norm_ffn_post, w_in,
               hy_conv_w, hy_conv_b, hy_ffn_w1, hy_ffn_b1, hy_sin_freq, hy_ffn_w2, hy_ffn_b2,
               hy_ffn_w3, hy_skip, mla_q_norm, mla_wq_b, mla_kv_norm, mla_wkv_b,
               gdn_conv_w, gdn_a_log, gdn_dt_bias, gdn_out_norm,
               w_branch, w_out, w_gate, w_up, w_down)

    def run_trunk(x):
        for layer in range(DEPTH):
            x = trunk_layer(x, *[w[layer] for w in weights])
        return x

    return (run_trunk(x_prompt), run_trunk(x_sample))
```

```python
import functools
import math

import jax
import jax.numpy as jnp
from jax import lax
from jax.experimental import pallas as pl
from jax.experimental.pallas import tpu as pltpu
import numpy as np

F32 = jnp.float32
BF16 = jnp.bfloat16

D_MODEL = 1024
DEPTH = 2
BRANCH_W = 512
N_BRANCH = 3
HY_W = BRANCH_W
HY_ORDER = 2
HY_EMB = 33
HY_FAST_PCT = 0.3
HY_SLOW_PCT = 1.5
HY_TARGET = 1e-2
MLA_HEADS = 4
MLA_NOPE = 128
MLA_ROPE = 64
MLA_V = 128
MLA_Q_LORA = 256
MLA_KV_LORA = 128
ROPE_THETA = 10000.0
GDN_HEADS = 4
GDN_DK = 128
GDN_DV = 128
GDN_CHUNK = 64
NORM_EPS = 1e-6

_OFF_HY = 0
_OFF_MQ = _OFF_HY + (HY_ORDER + 1) * HY_W
_OFF_MKV = _OFF_MQ + MLA_Q_LORA
_OFF_GQKV = _OFF_MKV + MLA_KV_LORA + MLA_ROPE
_OFF_GZ = _OFF_GQKV + GDN_HEADS * (2 * GDN_DK + GDN_DV)
_OFF_GB = _OFF_GZ + GDN_HEADS * GDN_DV
_OFF_GA = _OFF_GB + 2 * GDN_HEADS
_OFF_GATE = _OFF_GA + 2 * GDN_HEADS
_D_IN = _OFF_GATE + N_BRANCH * D_MODEL

MAIN_W = 3072 + 1536 + 1536 + 512
MISC_W = 640

LANES = 128
VMEM_LIMIT_BYTES = 56 * 1024 * 1024


def _cparams(*sem):
    return pltpu.CompilerParams(dimension_semantics=sem, vmem_limit_bytes=VMEM_LIMIT_BYTES)


def _full(shape):
    nd = len(shape)
    return pl.BlockSpec(shape, lambda *_: (0,) * nd)


def _norm_mm_kernel(x_ref, g_ref, w_ref, o_ref, h_ref):
    @pl.when(pl.program_id(1) == 0)
    def _():
        x = x_ref[...]
        r = lax.rsqrt(jnp.mean(x * x, axis=-1, keepdims=True) + NORM_EPS)
        h_ref[...] = (x * r * g_ref[...]).astype(BF16)

    o_ref[...] = jnp.dot(h_ref[...], w_ref[...], preferred_element_type=F32).astype(o_ref.dtype)


def norm_mm(x, g, w, tm, tn, out_dtype):
    T, D = x.shape
    N = w.shape[1]
    tm = min(tm, T)
    return pl.pallas_call(
        _norm_mm_kernel,
        grid=(T // tm, N // tn),
        in_specs=[pl.BlockSpec((tm, D), lambda i, j: (i, 0)),
                  pl.BlockSpec((1, D), lambda i, j: (0, 0)),
                  pl.BlockSpec((D, tn), lambda i, j: (0, j))],
        out_specs=pl.BlockSpec((tm, tn), lambda i, j: (i, j)),
        out_shape=jax.ShapeDtypeStruct((T, N), out_dtype),
        scratch_shapes=[pltpu.VMEM((tm, D), BF16)],
        compiler_params=_cparams("parallel", "arbitrary"),
        name="norm_mm",
    )(x, g, w)


def _dwconv_kernel(x_ref, p_ref, n_ref, w_ref, b_ref, o_ref, *, silu):
    i = pl.program_id(1)
    last = pl.num_programs(1) - 1
    x = x_ref[0].astype(F32)
    tl = x.shape[0]
    halo = p_ref.shape[1]
    prev_row = jnp.where(i > 0, p_ref[0, halo - 1:halo, :].astype(F32), 0.0)
    next_row = jnp.where(i < last, n_ref[0, 0:1, :].astype(F32), 0.0)
    rows = lax.broadcasted_iota(jnp.int32, x.shape, 0)
    x_dn = jnp.where(rows == 0, prev_row, pltpu.roll(x, 1, axis=0))
    x_up = jnp.where(rows == tl - 1, next_row, pltpu.roll(x, tl - 1, axis=0))
    out = x_dn * w_ref[0:1, :] + x * w_ref[1:2, :] + x_up * w_ref[2:3, :] + b_ref[...]
    if silu:
        out = out * jax.nn.sigmoid(out)
    o_ref[0] = out


def dwconv3(x, col_blk, cw, w, b, silu, tl=512):
    B, L, _ = x.shape
    tl = min(tl, L)
    halo = 8 * (4 // x.dtype.itemsize)
    nsub = tl // halo
    lastblk = L // halo - 1
    return pl.pallas_call(
        functools.partial(_dwconv_kernel, silu=silu),
        grid=(B, L // tl),
        in_specs=[pl.BlockSpec((1, tl, cw), lambda b_, i: (b_, i, col_blk)),
                  pl.BlockSpec((1, halo, cw), lambda b_, i: (b_, jnp.maximum(i * nsub - 1, 0), col_blk)),
                  pl.BlockSpec((1, halo, cw), lambda b_, i: (b_, jnp.minimum((i + 1) * nsub, lastblk), col_blk)),
                  _full((3, cw)), _full((1, cw))],
        out_specs=pl.BlockSpec((1, tl, cw), lambda b_, i: (b_, i, 0)),
        out_shape=jax.ShapeDtypeStruct((B, L, cw), F32),
        compiler_params=_cparams("parallel", "parallel"),
        name="dwconv3",
    )(x, x, x, w, b)


def _tile_lanes(t, c):
    return t if c == LANES else jnp.concatenate([t] * (c // LANES), axis=-1)


def _fft_s1_kernel(f_ref, x_ref, twr_ref, twi_ref, o_ref, *, nb, n1):
    c = x_ref.shape[-1]
    for t in range(nb):
        a = jnp.dot(f_ref[...], x_ref[0, t], preferred_element_type=F32)
        ar, ai = a[:n1], a[n1:]
        twr = _tile_lanes(twr_ref[t], c)
        twi = _tile_lanes(twi_ref[t], c)
        o_ref[0, t, :n1, :] = (ar * twr + ai * twi).astype(BF16)
        o_ref[0, t, n1:, :] = (ai * twr - ar * twi).astype(BF16)


def _fft_s2_kernel(f_ref, finv_ref, b_ref, kr_ref, ki_ref, o_ref, *, nb, n2):
    for t in range(nb):
        x = jnp.dot(f_ref[t], b_ref[0, t], preferred_element_type=F32)
        xr, xi = x[:n2], x[n2:]
        kr, ki = kr_ref[t], ki_ref[t]
        y = jnp.concatenate([xr * kr - xi * ki, xr * ki + xi * kr], axis=0).astype(BF16)
        o_ref[0, t] = jnp.dot(finv_ref[t], y, preferred_element_type=F32).astype(BF16)


def _fft_mm_kernel(g_ref, d_ref, o_ref, *, nb):
    for t in range(nb):
        o_ref[0, t] = jnp.dot(g_ref[...], d_ref[0, t], preferred_element_type=F32).astype(o_ref.dtype)


def _dft_tables(L):
    N = 2 * L
    n2 = 128 if N >= 4096 else 16
    n1 = N // n2
    n1h = n1 // 2
    k1 = np.arange(n1)[:, None]
    m1 = np.arange(n1h)[None, :]
    th = 2.0 * np.pi * ((k1 * m1) % n1) / n1
    c, s = np.cos(th), np.sin(th)
    f1 = np.block([[c, s], [-s, c]])
    g1 = np.block([[c.T, -s.T], [s.T, c.T]]) / N
    as_bf16 = lambda m: jnp.asarray(m, dtype=F32).astype(BF16)
    k = (jnp.arange(n1, dtype=jnp.int32)[:, None, None]
         + n1 * jnp.arange(n2, dtype=jnp.int32)[None, :, None])
    ang = ((k * jnp.arange(n2, dtype=jnp.int32)[None, None, :]) % N).astype(F32) * (2.0 * math.pi / N)
    c2, s2 = jnp.cos(ang), jnp.sin(ang)
    f2 = jnp.concatenate([jnp.concatenate([c2, s2], axis=2),
                          jnp.concatenate([-s2, c2], axis=2)], axis=1).astype(BF16)
    return n1, n2, as_bf16(f1), as_bf16(g1), f2, f2.transpose(0, 2, 1)


def _twiddles(n1, n2):
    N = n1 * n2
    prod = (jnp.arange(n2, dtype=jnp.int32)[:, None] * jnp.arange(n1, dtype=jnp.int32)[None, :]) % N
    ang = prod.astype(F32) * (2.0 * math.pi / N)
    shape = (n2, n1, LANES)
    return (jnp.broadcast_to(jnp.cos(ang)[:, :, None], shape),
            jnp.broadcast_to(jnp.sin(ang)[:, :, None], shape))


def fft_long_conv(z, kspec, order, tabs):
    B, L, C = z.shape
    n1, n2, f1, g1, f2, f2inv = tabs
    n1h = n1 // 2
    P = B // 2
    nb1 = min(8, n2)
    nb2 = min(8, n1)
    zt = z.astype(BF16).reshape(P, 2, n1h, n2, C).transpose(0, 3, 1, 2, 4).reshape(P, n2, 2 * n1h, C)
    a = pl.pallas_call(
        functools.partial(_fft_mm_kernel, nb=nb1),
        grid=(n2 // nb1, P),
        in_specs=[_full((2 * n1, 2 * n1h)),
                  pl.BlockSpec((1, nb1, 2 * n1h, C), lambda j, p: (p, j, 0, 0))],
        out_specs=pl.BlockSpec((1, nb1, 2 * n1, C), lambda j, p: (p, j, 0, 0)),
        out_shape=jax.ShapeDtypeStruct((P, n2, 2 * n1, C), BF16),
        compiler_params=_cparams("parallel", "parallel"),
        name="fft_s1",
    )(f1, zt)
    bt = a.reshape(P, n2, 2, n1, C).transpose(0, 3, 2, 1, 4).reshape(P, n1, 2 * n2, C)
    d = pl.pallas_call(
        functools.partial(_fft_s2_kernel, nb=nb2, n2=n2),
        grid=(n1 // nb2, P),
        in_specs=[pl.BlockSpec((nb2, 2 * n2, 2 * n2), lambda j, p: (j, 0, 0)),
                  pl.BlockSpec((nb2, 2 * n2, 2 * n2), lambda j, p: (j, 0, 0)),
                  pl.BlockSpec((1, nb2, 2 * n2, C), lambda j, p: (p, j, 0, 0)),
                  pl.BlockSpec((nb2, n2, C), lambda j, p: (j, 0, order)),
                  pl.BlockSpec((nb2, n2, C), lambda j, p: (j, 1, order))],
        out_specs=pl.BlockSpec((1, nb2, 2 * n2, C), lambda j, p: (p, j, 0, 0)),
        out_shape=jax.ShapeDtypeStruct((P, n1, 2 * n2, C), BF16),
        compiler_params=_cparams("parallel", "parallel"),
        name="fft_s2",
    )(f2, f2inv, bt, kspec, kspec)
    dt = d.reshape(P, n1, 2, n2, C).transpose(0, 3, 2, 1, 4).reshape(P, n2, 2 * n1, C)
    y = pl.pallas_call(
        functools.partial(_fft_mm_kernel, nb=nb1),
        grid=(n2 // nb1, P),
        in_specs=[_full((2 * n1h, 2 * n1)),
                  pl.BlockSpec((1, nb1, 2 * n1, C), lambda j, p: (p, j, 0, 0))],
        out_specs=pl.BlockSpec((1, nb1, 2 * n1h, C), lambda j, p: (p, j, 0, 0)),
        out_shape=jax.ShapeDtypeStruct((P, n2, 2 * n1h, C), F32),
        compiler_params=_cparams("parallel", "parallel"),
        name="fft_s3",
    )(g1, dt)
    return y.reshape(P, n2, 2, n1h, C).transpose(0, 2, 3, 1, 4).reshape(B, L, C)


def _hy_gate_kernel(g_ref, c_ref, z_ref, s_ref, o_ref):
    o_ref[...] = (g_ref[...] * (c_ref[...] + s_ref[...] * z_ref[...])).astype(o_ref.dtype)


def hy_gate(u, gate_blk, conv, zsrc, z_blk, skip, out_dtype, tl=1024):
    T, C = conv.shape
    tl = min(tl, T)
    return pl.pallas_call(
        _hy_gate_kernel,
        grid=(T // tl,),
        in_specs=[pl.BlockSpec((tl, C), lambda i: (i, gate_blk)),
                  pl.BlockSpec((tl, C), lambda i: (i, 0)),
                  pl.BlockSpec((tl, C), lambda i: (i, z_blk)),
                  _full((1, C))],
        out_specs=pl.BlockSpec((tl, C), lambda i: (i, 0)),
        out_shape=jax.ShapeDtypeStruct((T, C), out_dtype),
        compiler_params=_cparams("parallel"),
        name="hy_gate",
    )(u, conv, zsrc, skip)


def hyena_filter_spectrum(L, n1, n2, tw1r, tw1i, w1, b1, freq, w2, b2, w3):
    t = jnp.linspace(0.0, 1.0, L, dtype=F32)[:, None]
    bands = (HY_EMB - 1) // 2
    wpos = (2.0 * math.pi / L) * jnp.arange(L, dtype=F32)[:, None]
    fr = jnp.linspace(1e-4, bands - 1, bands, dtype=F32)[None, :]
    feats = jnp.concatenate([t, jnp.cos(fr * wpos), -jnp.sin(fr * wpos)], axis=-1)
    deltas = jnp.abs(jnp.linspace(math.log(HY_TARGET) / HY_SLOW_PCT,
                                  math.log(HY_TARGET) / HY_FAST_PCT, HY_W, dtype=F32))
    feats2 = jnp.concatenate([feats, feats[0:1], feats[:0:-1]], axis=0)
    feats2 = jnp.pad(feats2, ((0, 0), (0, LANES - HY_EMB)))
    w1p = jnp.pad(w1, ((0, LANES - HY_EMB), (0, 0)))
    hid = w3.shape[0]
    oc = HY_ORDER * HY_W
    w3d = w3.reshape(hid, HY_ORDER, 2, HY_W).transpose(2, 0, 1, 3).reshape(2, hid, oc)
    tl = min(512, L)
    nh = L // tl
    kern, asum = pl.pallas_call(
        functools.partial(_filter_taps_kernel, zero_tile=nh),
        grid=(2 * nh,),
        in_specs=[pl.BlockSpec((tl, LANES), lambda i: (i, 0)), _full((LANES, hid)), _full((1, hid)),
                  _full((1, hid)), _full((hid, hid)), _full((1, hid)),
                  pl.BlockSpec((1, hid, oc), lambda i: (i // nh, 0, 0)), _full((1, oc))],
        out_specs=[pl.BlockSpec((tl, oc), lambda i: (i, 0)), _full((1, oc))],
        out_shape=[jax.ShapeDtypeStruct((2 * L, oc), F32), jax.ShapeDtypeStruct((1, oc), F32)],
        compiler_params=_cparams("arbitrary"),
        name="filter_taps",
    )(feats2, w1p, b1, freq, w2, b2, w3d, jnp.tile(deltas, HY_ORDER)[None, :])
    return filter_dft(kern, asum, n1, n2, tw1r, tw1i)


def _filter_taps_kernel(f_ref, w1_ref, b1_ref, fq_ref, w2_ref, b2_ref, w3_ref, dl_ref, k_ref, s_ref, *,
                        zero_tile):
    i = pl.program_id(0)
    hp = lax.Precision.HIGHEST
    f = f_ref[...]
    h = jnp.sin(fq_ref[...] * (jnp.dot(f, w1_ref[...], precision=hp, preferred_element_type=F32)
                               + b1_ref[...]))
    h = jnp.sin(fq_ref[...] * (jnp.dot(h, w2_ref[...], precision=hp, preferred_element_type=F32)
                               + b2_ref[...]))
    taps = jnp.dot(h, w3_ref[0], precision=hp, preferred_element_type=F32)
    taps = taps * jnp.exp(-f[:, 0:1] * dl_ref[...])
    row = lax.broadcasted_iota(jnp.int32, taps.shape, 0)
    taps = jnp.where((i == zero_tile) & (row == 0), 0.0, taps)
    k_ref[...] = taps

    @pl.when(i == 0)
    def _():
        s_ref[...] = jnp.zeros(s_ref.shape, F32)

    s_ref[...] += jnp.sum(jnp.abs(taps), axis=0, keepdims=True)


def _spec_s2_kernel(f_ref, b_ref, s_ref, o_ref, *, nb):
    for t in range(nb):
        o_ref[t] = jnp.dot(f_ref[...], b_ref[0, t], preferred_element_type=F32) / s_ref[...]


def filter_dft(kern, asum, n1, n2, tw1r, tw1i):
    N, C = kern.shape
    ct = 512
    k1 = np.arange(n1)
    th = 2.0 * np.pi * ((k1[:, None] * k1[None, :]) % n1) / n1
    f1 = np.concatenate([np.cos(th), -np.sin(th)], axis=0)
    a2 = np.arange(n2)
    ph = 2.0 * np.pi * ((a2[:, None] * a2[None, :]) % n2) / n2
    f2 = np.block([[np.cos(ph), np.sin(ph)], [-np.sin(ph), np.cos(ph)]])

    as_bf16 = lambda m: jnp.asarray(m, dtype=F32).astype(BF16)
    nb1 = min(8, n2)
    nb2 = min(8, n1)
    xt = kern.astype(BF16).reshape(n1, n2, C).transpose(1, 0, 2)[None]
    a = pl.pallas_call(
        functools.partial(_fft_s1_kernel, nb=nb1, n1=n1),
        grid=(n2 // nb1, C // ct),
        in_specs=[_full((2 * n1, n1)),
                  pl.BlockSpec((1, nb1, n1, ct), lambda j, c: (0, j, 0, c)),
                  pl.BlockSpec((nb1, n1, LANES), lambda j, c: (j, 0, 0)),
                  pl.BlockSpec((nb1, n1, LANES), lambda j, c: (j, 0, 0))],
        out_specs=pl.BlockSpec((1, nb1, 2 * n1, ct), lambda j, c: (0, j, 0, c)),
        out_shape=jax.ShapeDtypeStruct((1, n2, 2 * n1, C), BF16),
        compiler_params=_cparams("parallel", "parallel"),
        name="spec_s1",
    )(as_bf16(f1), xt, tw1r, tw1i)
    bt = a.reshape(n2, 2, n1, C).transpose(2, 1, 0, 3).reshape(1, n1, 2 * n2, C)
    return pl.pallas_call(
        functools.partial(_spec_s2_kernel, nb=nb2),
        grid=(n1 // nb2, C // ct),
        in_specs=[_full((2 * n2, 2 * n2)),
                  pl.BlockSpec((1, nb2, 2 * n2, ct), lambda j, c: (0, j, 0, c)),
                  pl.BlockSpec((1, ct), lambda j, c: (0, c))],
        out_specs=pl.BlockSpec((nb2, 2 * n2, ct), lambda j, c: (j, 0, c)),
        out_shape=jax.ShapeDtypeStruct((n1, 2 * n2, C), F32),
        compiler_params=_cparams("parallel", "parallel"),
        name="spec_s2",
    )(as_bf16(f2), bt, asum)


def hyena_mixer(proj3, conv_w, conv_b, w1, b1, freq, w2, b2, w3, skip):
    B, L, _ = proj3.shape
    T = B * L
    u = dwconv3(proj3, 2, 1536, conv_w, conv_b[None, :], silu=False).reshape(T, 1536)
    n1, n2, f1, g1, f2, f2inv = _dft_tables(L)
    tw1r, tw1i = _twiddles(n1, n2)
    tabs = (n1, n2, f1, g1, f2, f2inv)
    kspec = hyena_filter_spectrum(L, n1, n2, tw1r, tw1i, w1, b1[None, :], freq[None, :], w2, b2[None, :], w3)
    conv = fft_long_conv(u[:, 1024:1536].reshape(B, L, HY_W), kspec, 0, tabs).reshape(T, HY_W)
    z1 = hy_gate(u, 0, conv, u, 2, skip[0:1], F32)
    conv = fft_long_conv(z1.reshape(B, L, HY_W), kspec, 1, tabs).reshape(T, HY_W)
    return hy_gate(u, 1, conv, z1, 0, skip[1:2], BF16)


def _mla_proj_kernel(m_ref, qg_ref, kg_ref, wq_ref, wkv_ref, cq_ref, sq_ref, ck_ref, sk_ref,
                     q_ref, k_ref, v_ref):
    m = m_ref[0]
    ql = m[:, 0:256]
    r = lax.rsqrt(jnp.mean(ql * ql, axis=-1, keepdims=True) + NORM_EPS)
    q = jnp.dot((ql * r * qg_ref[...]).astype(BF16), wq_ref[...], preferred_element_type=F32)
    ckv = m[:, 256:384]
    r = lax.rsqrt(jnp.mean(ckv * ckv, axis=-1, keepdims=True) + NORM_EPS)
    kv = jnp.dot((ckv * r * kg_ref[...]).astype(BF16), wkv_ref[...], preferred_element_type=F32)
    q_pe = (q[:, 512:1024] * cq_ref[...] + q[:, 1024:1536] * sq_ref[...]).astype(BF16)
    k_pe = (m[:, 384:512] * ck_ref[...] + m[:, 512:640] * sk_ref[...]).astype(BF16)
    for h in range(MLA_HEADS):
        sl = slice(h * LANES, (h + 1) * LANES)
        q_ref[0, :, 2 * h * LANES:(2 * h + 1) * LANES] = q[:, sl].astype(BF16)
        q_ref[0, :, (2 * h + 1) * LANES:(2 * h + 2) * LANES] = q_pe[:, sl]
        k_ref[0, :, 2 * h * LANES:(2 * h + 1) * LANES] = kv[:, sl].astype(BF16)
        k_ref[0, :, (2 * h + 1) * LANES:(2 * h + 2) * LANES] = k_pe
        v_ref[0, :, 2 * h * LANES:(2 * h + 1) * LANES] = kv[:, 512 + h * MLA_V:512 + (h + 1) * MLA_V].astype(BF16)
        v_ref[0, :, (2 * h + 1) * LANES:(2 * h + 2) * LANES] = jnp.ones((m.shape[0], LANES), BF16)


ATTN_ROW_PARTS = 2


def _attn_kernel(q_ref, k_ref, v_ref, o_ref, *scratch, scale):
    ki = pl.program_id(2)
    c = scale * math.log2(math.e)
    dqk = 2 * LANES
    m_refs, acc_refs = scratch[:MLA_HEADS], scratch[MLA_HEADS:]

    @pl.when(ki == 0)
    def _():
        for m_ref, acc_ref in zip(m_refs, acc_refs):
            m_ref[...] = jnp.full(m_ref.shape, -jnp.inf, F32)
            acc_ref[...] = jnp.zeros(acc_ref.shape, F32)

    part = q_ref.shape[1] // ATTN_ROW_PARTS
    for h, (m_ref, acc_ref) in enumerate(zip(m_refs, acc_refs)):
        sl = slice(h * dqk, (h + 1) * dqk)
        for r in range(ATTN_ROW_PARTS):
            rows = slice(r * part, (r + 1) * part)
            s = lax.dot_general(q_ref[0, rows, sl], k_ref[0, :, sl], (((1,), (1,)), ((), ())),
                                preferred_element_type=F32)
            m_prev = m_ref[rows, :]
            m_new = jnp.maximum(m_prev, jnp.max(s, axis=-1, keepdims=True) * c)
            p = jnp.exp2(s * c - m_new).astype(BF16)
            acc_ref[rows, :] = (jnp.exp2(m_prev - m_new) * acc_ref[rows, :]
                                + jnp.dot(p, v_ref[0, :, sl], preferred_element_type=F32))
            m_ref[rows, :] = m_new

    @pl.when(ki == pl.num_programs(2) - 1)
    def _():
        for h, acc_ref in enumerate(acc_refs):
            o_ref[0, :, h * MLA_V:(h + 1) * MLA_V] = (acc_ref[:, :MLA_V] / acc_ref[:, MLA_V:]).astype(BF16)


def _rope_lane_tables(L):
    half = MLA_ROPE // 2
    inv = ROPE_THETA ** (-jnp.arange(half, dtype=F32) / half)
    ang = jnp.arange(L, dtype=F32)[:, None] * inv[None, :]
    cos, sin = jnp.cos(ang), jnp.sin(ang)
    zeros = jnp.zeros((L, LANES - MLA_ROPE), F32)
    ck = jnp.concatenate([cos, cos, zeros], axis=-1)
    sk = jnp.concatenate([-sin, sin, zeros], axis=-1)
    return jnp.tile(ck, (1, MLA_HEADS)), jnp.tile(sk, (1, MLA_HEADS)), ck, sk


def mla_mixer(misc3, q_norm, wq_b, kv_norm, wkv_b, tl=512, tq=1024, tk=1024):
    B, L, _ = misc3.shape
    H = MLA_HEADS
    tl, tq, tk = min(tl, L), min(tq, L), min(tk, L)
    half = MLA_ROPE // 2
    wq = wq_b.reshape(MLA_Q_LORA, H, MLA_NOPE + MLA_ROPE)
    x1 = wq[:, :, MLA_NOPE:MLA_NOPE + half]
    x2 = wq[:, :, MLA_NOPE + half:]
    zpad = jnp.zeros((MLA_Q_LORA, H, LANES - MLA_ROPE), wq.dtype)
    wq = jnp.concatenate([wq[:, :, :MLA_NOPE].reshape(MLA_Q_LORA, -1),
                          jnp.concatenate([x1, x2, zpad], axis=-1).reshape(MLA_Q_LORA, -1),
                          jnp.concatenate([x2, x1, zpad], axis=-1).reshape(MLA_Q_LORA, -1)],
                         axis=-1).astype(BF16)
    wkv = wkv_b.reshape(MLA_KV_LORA, H, MLA_NOPE + MLA_V)
    wkv = jnp.concatenate([wkv[:, :, :MLA_NOPE].reshape(MLA_KV_LORA, -1),
                           wkv[:, :, MLA_NOPE:].reshape(MLA_KV_LORA, -1)], axis=-1).astype(BF16)
    cq, sq, ck, sk = _rope_lane_tables(L)
    tok = lambda w: pl.BlockSpec((1, tl, w), lambda b, i: (b, i, 0))
    pos = lambda w: pl.BlockSpec((tl, w), lambda b, i: (i, 0))
    sds = lambda w: jax.ShapeDtypeStruct((B, L, w), BF16)
    dqk = 2 * LANES
    nk = L // tk
    q, k, v = pl.pallas_call(
        _mla_proj_kernel,
        grid=(B, L // tl),
        in_specs=[tok(MISC_W), _full((1, MLA_Q_LORA)), _full((1, MLA_KV_LORA)),
                  _full(wq.shape), _full(wkv.shape), pos(H * LANES), pos(H * LANES), pos(LANES), pos(LANES)],
        out_specs=[tok(H * dqk), tok(H * dqk), tok(H * dqk)],
        out_shape=[sds(H * dqk), sds(H * dqk), sds(H * dqk)],
        compiler_params=_cparams("parallel", "parallel"),
        name="mla_proj",
    )(misc3, q_norm[None, :], kv_norm[None, :], wq, wkv, cq, sq, ck, sk)
    return pl.pallas_call(
        functools.partial(_attn_kernel, scale=(MLA_NOPE + MLA_ROPE) ** -0.5),
        grid=(B, L // tq, nk),
        in_specs=[pl.BlockSpec((1, tq, H * dqk), lambda b, i, j: (b, i, 0)),
                  pl.BlockSpec((1, tk, H * dqk), lambda b, i, j: (b, j, 0)),
                  pl.BlockSpec((1, tk, H * dqk), lambda b, i, j: (b, j, 0))],
        out_specs=pl.BlockSpec((1, tq, H * MLA_V), lambda b, i, j: (b, i, 0)),
        out_shape=jax.ShapeDtypeStruct((B, L, H * MLA_V), BF16),
        scratch_shapes=([pltpu.VMEM((tq, 1), F32)] * H + [pltpu.VMEM((tq, 2 * MLA_V), F32)] * H),
        compiler_params=_cparams("parallel", "parallel", "arbitrary"),
        name="mla_attn",
    )(q, k, v)


_BETA_LANE = 64
_G_LANE = 72
CH = GDN_CHUNK
NDH = 2 * GDN_HEADS
GDN_STEP_CHUNKS = 8


def _gdn_prep_kernel(c_ref, m_ref, alog_ref, dtb_ref, qn_ref, kn_ref, bg_ref):
    for h in range(GDN_HEADS):
        sl = slice(h * GDN_DK, (h + 1) * GDN_DK)
        qh = c_ref[0, :, sl]
        qn_ref[0, :, sl] = (qh * lax.rsqrt(jnp.sum(qh * qh, axis=-1, keepdims=True) + NORM_EPS)
                            * (GDN_DK ** -0.5))
        kh = c_ref[0, :, GDN_HEADS * GDN_DK + h * GDN_DK:GDN_HEADS * GDN_DK + (h + 1) * GDN_DK]
        kn_ref[0, :, sl] = kh * lax.rsqrt(jnp.sum(kh * kh, axis=-1, keepdims=True) + NORM_EPS)
    m = m_ref[0]
    lane = lax.broadcasted_iota(jnp.int32, m.shape, 1)
    beta = jax.nn.sigmoid(m)
    x = m + dtb_ref[...]
    softplus = jnp.maximum(x, 0.0) + jnp.log1p(jnp.exp(-jnp.abs(x)))
    g = -jnp.exp(alog_ref[...]) * softplus
    is_beta = (lane >= _BETA_LANE) & (lane < _BETA_LANE + NDH)
    is_g = (lane >= _G_LANE) & (lane < _G_LANE + NDH)
    bg_ref[0] = jnp.where(is_beta, beta, jnp.where(is_g, g, 0.0))


def _split3(x):
    hi = x.astype(BF16)
    r = x - hi.astype(F32)
    mid = r.astype(BF16)
    lo = (r - mid.astype(F32)).astype(BF16)
    return hi, mid, lo


def _dot_nt(a, b):
    return lax.dot_general(a, b, (((1,), (1,)), ((), ())), preferred_element_type=F32)


def _gdn_chunk_kernel(q_ref, k_ref, bg_ref, a_ref, attn_ref, gc_ref):
    lane = lax.broadcasted_iota(jnp.int32, (CH, LANES), 1)
    is_g = (lane >= _G_LANE) & (lane < _G_LANE + NDH)
    ri = lax.broadcasted_iota(jnp.int32, (CH, CH), 0)
    ci = lax.broadcasted_iota(jnp.int32, (CH, CH), 1)
    lower = ri >= ci
    upper = ri <= ci
    tril = lower.astype(BF16)
    triu = upper.astype(BF16)
    for c in range(GDN_STEP_CHUNKS):
        rows = slice(c * CH, (c + 1) * CH)
        bg = bg_ref[0, rows, :]
        pieces = _split3(jnp.where(is_g, bg, 0.0))
        pre = sum(jnp.dot(tril, p, preferred_element_type=F32) for p in pieces)
        suf = sum(jnp.dot(triu, p, preferred_element_type=F32) for p in pieces)
        gc = jnp.where(lane >= _G_LANE + GDN_HEADS, suf, pre)
        gc_ref[0, rows, :] = gc
        gct = gc.T
        for d in range(2):
            causal = lower if d == 0 else upper
            strict = (ri > ci) if d == 0 else (ri < ci)
            for h in range(GDN_HEADS):
                dh = d * GDN_HEADS + h
                sl = slice(h * GDN_DK, (h + 1) * GDN_DK)
                kh = k_ref[0, rows, sl]
                kb = (kh * bg[:, _BETA_LANE + dh:_BETA_LANE + dh + 1]).astype(BF16)
                khb = kh.astype(BF16)
                diff = gc[:, _G_LANE + dh:_G_LANE + dh + 1] - gct[_G_LANE + dh:_G_LANE + dh + 1, :]
                dec = jnp.exp(jnp.where(causal, diff, -jnp.inf))
                a_ref[0, c, dh] = jnp.where(strict, _dot_nt(kb, khb) * dec, 0.0)
                attn_ref[0, c, dh] = (_dot_nt(q_ref[0, rows, sl].astype(BF16), khb) * dec).astype(BF16)


def _gdn_solve_kernel(a_ref, t_ref, *, nblk_fwd):
    bwd = pl.program_id(0) >= nblk_fwd
    t_ref[...] = jnp.zeros(t_ref.shape, F32)

    @pl.when(jnp.logical_not(bwd))
    def _():
        _solve_triangular(a_ref, t_ref, reverse=False)

    @pl.when(bwd)
    def _():
        _solve_triangular(a_ref, t_ref, reverse=True)


def _solve_triangular(a_ref, t_ref, *, reverse):
    nblk = CH // 8
    sub8 = lax.broadcasted_iota(jnp.int32, (8, LANES), 0)
    for phase in range(nblk):
        rb = nblk - 1 - phase if reverse else phase
        groups = range(rb, nblk) if reverse else range(0, rb + 1)

        def row(it, carry, rb=rb, groups=groups):
            i = 8 * rb + (7 - it if reverse else it)
            acc = {cg: ((sub8 + 8 * cg == i).astype(F32) if cg == rb else jnp.zeros((8, LANES), F32))
                   for cg in groups}
            for jb in groups:
                ablk = a_ref[i, 8 * jb:8 * jb + 8, :]
                for jj in range(8):
                    arow = ablk[jj:jj + 1, :]
                    for cg in (range(jb, nblk) if reverse else range(0, jb + 1)):
                        acc[cg] = acc[cg] - arow * t_ref[8 * jb + jj, 8 * cg:8 * cg + 8, :]
            for cg in groups:
                t_ref[i, 8 * cg:8 * cg + 8, :] = acc[cg]
            return carry

        lax.fori_loop(0, 8, row, 0)


def _gdn_scan_kernel(qf_ref, kf_ref, vf_ref, bgf_ref, gcf_ref, tf_ref, af_ref,
                     qb_ref, kb_ref, vb_ref, bgb_ref, gcb_ref, tb_ref, ab_ref,
                     of_ref, ob_ref, *s_refs):
    @pl.when(pl.program_id(1) == 0)
    def _():
        for s_ref in s_refs:
            s_ref[...] = jnp.zeros(s_ref.shape, F32)

    dirs = ((qf_ref, kf_ref, vf_ref, bgf_ref, gcf_ref, tf_ref, af_ref, of_ref, CH - 1),
            (qb_ref, kb_ref, vb_ref, bgb_ref, gcb_ref, tb_ref, ab_ref, ob_ref, 0))
    for step in range(GDN_STEP_CHUNKS):
        probs = []
        for d, (q_ref, k_ref, v_ref, bg_ref, gc_ref, t_ref, a_ref, o_ref, last) in enumerate(dirs):
            c = step if d == 0 else GDN_STEP_CHUNKS - 1 - step
            rows = slice(c * CH, (c + 1) * CH)
            for h in range(GDN_HEADS):
                dh = d * GDN_HEADS + h
                sl = slice(h * GDN_DK, (h + 1) * GDN_DK)
                k, v = k_ref[0, rows, sl], v_ref[0, rows, sl]
                beta = bg_ref[0, rows, _BETA_LANE + dh:_BETA_LANE + dh + 1]
                gc = gc_ref[0, rows, _G_LANE + dh:_G_LANE + dh + 1]
                gl = gc_ref[0, c * CH + last:c * CH + last + 1, _G_LANE + dh:_G_LANE + dh + 1]
                egc = jnp.exp(gc)
                rhs = jnp.concatenate([v * beta, k * beta * egc], axis=-1).astype(BF16)
                sol = jnp.dot(t_ref[0, c, h], rhs, preferred_element_type=F32)
                s = s_refs[dh][...]
                probs.append(dict(sl=sl, rows=rows, o_ref=o_ref, s_ref=s_refs[dh], s=s, sb=s.astype(BF16),
                                  u=sol[:, :GDN_DV], w=sol[:, GDN_DV:].astype(BF16),
                                  qd=(q_ref[0, rows, sl] * egc).astype(BF16),
                                  a=a_ref[0, c, h].astype(BF16),
                                  kd=(k * jnp.exp(gl - gc)).astype(BF16), dec=jnp.exp(gl)))
        for p in probs:
            p["vn"] = (p["u"] - jnp.dot(p["w"], p["sb"], preferred_element_type=F32)).astype(BF16)
        for p in probs:
            p["o_ref"][0, p["rows"], p["sl"]] = (jnp.dot(p["qd"], p["sb"], preferred_element_type=F32)
                                                 + jnp.dot(p["a"], p["vn"], preferred_element_type=F32))
        for p in probs:
            p["s_ref"][...] = p["s"] * p["dec"] + lax.dot_general(
                p["kd"], p["vn"], (((0,), (0,)), ((), ())), preferred_element_type=F32)


def _gdn_out_kernel(of_ref, ob_ref, z_ref, n_ref, o_ref):
    z = z_ref[...].astype(F32)
    for h in range(GDN_HEADS):
        sl = slice(h * GDN_DV, (h + 1) * GDN_DV)
        o = of_ref[:, sl] + ob_ref[:, sl]
        y = o * lax.rsqrt(jnp.mean(o * o, axis=-1, keepdims=True) + NORM_EPS) * n_ref[...]
        zh = z[:, sl]
        o_ref[:, sl] = (y * (zh * jax.nn.sigmoid(zh))).astype(BF16)


def gdn_mixer(proj3, misc3, conv_w, a_log, dt_bias, out_norm, tl=512):
    B, L, _ = proj3.shape
    T = B * L
    N = L // CH
    H = GDN_HEADS
    tl = min(tl, L)
    qkv = dwconv3(proj3, 3, 1536, conv_w, jnp.zeros((1, 1536), F32), silu=True)
    lane_vec = lambda p: jnp.zeros((1, LANES), F32).at[0, _G_LANE:_G_LANE + NDH].set(p.reshape(-1))
    tok = lambda w, c=0: pl.BlockSpec((1, tl, w), lambda b, i: (b, i, c))
    qn, kn, bg = pl.pallas_call(
        _gdn_prep_kernel,
        grid=(B, L // tl),
        in_specs=[tok(1536), tok(LANES, 3), _full((1, LANES)), _full((1, LANES))],
        out_specs=[tok(512), tok(512), tok(LANES)],
        out_shape=[jax.ShapeDtypeStruct((B, L, 512), F32), jax.ShapeDtypeStruct((B, L, 512), F32),
                   jax.ShapeDtypeStruct((B, L, LANES), F32)],
        compiler_params=_cparams("parallel", "parallel"),
        name="gdn_prep",
    )(qkv, misc3, lane_vec(a_log), lane_vec(dt_bias))

    CB = GDN_STEP_CHUNKS
    NB = N // CB
    chunk = lambda w, c=0: pl.BlockSpec((1, CB * CH, w), lambda b, n: (b, n, c))
    mats = pl.BlockSpec((1, CB, NDH, CH, CH), lambda b, n: (b, n, 0, 0, 0))
    mat_shape = jax.ShapeDtypeStruct((B, N, NDH, CH, CH), F32)
    a, attn, gc = pl.pallas_call(
        _gdn_chunk_kernel,
        grid=(B, NB),
        in_specs=[chunk(512), chunk(512), chunk(LANES)],
        out_specs=[mats, mats, chunk(LANES)],
        out_shape=[mat_shape, jax.ShapeDtypeStruct(mat_shape.shape, BF16),
                   jax.ShapeDtypeStruct((B, L, LANES), F32)],
        compiler_params=_cparams("parallel", "parallel"),
        name="gdn_chunk",
    )(qn, kn, bg)

    P = NDH * B * N
    at = a.transpose(3, 4, 2, 0, 1).reshape(CH, CH, P)
    tt = pl.pallas_call(
        functools.partial(_gdn_solve_kernel, nblk_fwd=P // LANES // 2),
        grid=(P // LANES,),
        in_specs=[pl.BlockSpec((CH, CH, LANES), lambda p: (0, 0, p))],
        out_specs=pl.BlockSpec((CH, CH, LANES), lambda p: (0, 0, p)),
        out_shape=jax.ShapeDtypeStruct((CH, CH, P), F32),
        compiler_params=_cparams("parallel"),
        name="gdn_solve",
    )(at)
    tmat = tt.reshape(CH, CH, NDH, B, N).transpose(3, 4, 2, 0, 1).astype(BF16)

    fwd = lambda w, c=0: pl.BlockSpec((1, CB * CH, w), lambda b, n: (b, n, c))
    bwd = lambda w, c=0: pl.BlockSpec((1, CB * CH, w), lambda b, n: (b, NB - 1 - n, c))
    mf = pl.BlockSpec((1, CB, H, CH, CH), lambda b, n: (b, n, 0, 0, 0))
    mb = pl.BlockSpec((1, CB, H, CH, CH), lambda b, n: (b, NB - 1 - n, 1, 0, 0))
    o_f, o_b = pl.pallas_call(
        _gdn_scan_kernel,
        grid=(B, NB),
        in_specs=[fwd(512), fwd(512), fwd(512, 2), fwd(LANES), fwd(LANES), mf, mf,
                  bwd(512), bwd(512), bwd(512, 2), bwd(LANES), bwd(LANES), mb, mb],
        out_specs=[fwd(512), bwd(512)],
        out_shape=[jax.ShapeDtypeStruct((B, L, 512), F32), jax.ShapeDtypeStruct((B, L, 512), F32)],
        scratch_shapes=[pltpu.VMEM((GDN_DK, GDN_DV), F32)] * NDH,
        compiler_params=_cparams("parallel", "arbitrary"),
        name="gdn_scan",
    )(qn, kn, qkv, bg, gc, tmat, attn, qn, kn, qkv, bg, gc, tmat, attn)

    tm = min(1024, T)
    row = lambda c=0: pl.BlockSpec((tm, 512), lambda i: (i, c))
    return pl.pallas_call(
        _gdn_out_kernel,
        grid=(T // tm,),
        in_specs=[row(), row(), row(12), _full((1, GDN_DV))],
        out_specs=row(),
        out_shape=jax.ShapeDtypeStruct((T, 512), BF16),
        compiler_params=_cparams("parallel"),
        name="gdn_out",
    )(o_f.reshape(T, 512), o_b.reshape(T, 512), proj3.reshape(T, MAIN_W), out_norm[None, :])


def _merge_kernel(x_ref, g_ref, oh_ref, om_ref, og_ref, wb_ref, wo_ref, n_ref, o_ref):
    merged = None
    for i, b_ref in enumerate((oh_ref, om_ref, og_ref)):
        gate = jax.nn.sigmoid(g_ref[:, i * D_MODEL:(i + 1) * D_MODEL].astype(F32))
        term = gate * jnp.dot(b_ref[...].astype(BF16), wb_ref[i], preferred_element_type=F32)
        merged = term if merged is None else merged + term
    y = jnp.dot(merged.astype(BF16), wo_ref[...], preferred_element_type=F32)
    r = lax.rsqrt(jnp.mean(y * y, axis=-1, keepdims=True) + NORM_EPS)
    o_ref[...] = x_ref[...] + y * r * n_ref[...]


def merge_out(x, proj, o_hy, o_mla, o_gdn, w_branch, w_out, norm_post, tm=512):
    T = x.shape[0]
    tm = min(tm, T)
    row = lambda w: pl.BlockSpec((tm, w), lambda i: (i, 0))
    return pl.pallas_call(
        _merge_kernel,
        grid=(T // tm,),
        in_specs=[row(D_MODEL), row(N_BRANCH * D_MODEL), row(BRANCH_W), row(BRANCH_W), row(BRANCH_W),
                  _full((N_BRANCH, BRANCH_W, D_MODEL)), _full((D_MODEL, D_MODEL)), _full((1, D_MODEL))],
        out_specs=row(D_MODEL),
        out_shape=jax.ShapeDtypeStruct((T, D_MODEL), F32),
        compiler_params=_cparams("parallel"),
        name="merge_out",
    )(x, proj, o_hy, o_mla, o_gdn, w_branch, w_out, norm_post)


def _ffn_kernel(x_ref, gpre_ref, wg_ref, wu_ref, wd_ref, gpost_ref, o_ref, h_ref, acc_ref):
    j = pl.program_id(1)

    @pl.when(j == 0)
    def _():
        x = x_ref[...]
        r = lax.rsqrt(jnp.mean(x * x, axis=-1, keepdims=True) + NORM_EPS)
        h_ref[...] = (x * r * gpre_ref[...]).astype(BF16)

    h = h_ref[...]
    gate = jnp.dot(h, wg_ref[...], preferred_element_type=F32)
    up = jnp.dot(h, wu_ref[...], preferred_element_type=F32)
    part = jnp.dot((gate * jax.nn.sigmoid(gate) * up).astype(BF16), wd_ref[...], preferred_element_type=F32)

    @pl.when(j == 0)
    def _():
        acc_ref[...] = part

    @pl.when(j > 0)
    def _():
        acc_ref[...] += part

    @pl.when(j == pl.num_programs(1) - 1)
    def _():
        f = acc_ref[...]
        r = lax.rsqrt(jnp.mean(f * f, axis=-1, keepdims=True) + NORM_EPS)
        o_ref[...] = x_ref[...] + f * r * gpost_ref[...]


def ffn(x, g_pre, w_gate, w_up, w_down, g_post, tm=512):
    T = x.shape[0]
    dff = w_gate.shape[1]
    tf = dff // 2
    tm = min(tm, T)
    return pl.pallas_call(
        _ffn_kernel,
        grid=(T // tm, dff // tf),
        in_specs=[pl.BlockSpec((tm, D_MODEL), lambda i, j: (i, 0)), _full((1, D_MODEL)),
                  pl.BlockSpec((D_MODEL, tf), lambda i, j: (0, j)),
                  pl.BlockSpec((D_MODEL, tf), lambda i, j: (0, j)),
                  pl.BlockSpec((tf, D_MODEL), lambda i, j: (j, 0)), _full((1, D_MODEL))],
        out_specs=pl.BlockSpec((tm, D_MODEL), lambda i, j: (i, 0)),
        out_shape=jax.ShapeDtypeStruct((T, D_MODEL), F32),
        scratch_shapes=[pltpu.VMEM((tm, D_MODEL), BF16), pltpu.VMEM((tm, D_MODEL), F32)],
        compiler_params=_cparams("parallel", "arbitrary"),
        name="ffn",
    )(x, g_pre, w_gate, w_up, w_down, g_post)


def _split_w_in(w_in):
    cols = lambda off, n: w_in[:, off:off + n]
    main = jnp.concatenate([cols(_OFF_GATE, N_BRANCH * D_MODEL), cols(_OFF_HY, 1536),
                            cols(_OFF_GQKV, 1536), cols(_OFF_GZ, 512)], axis=-1).astype(BF16)
    half = MLA_ROPE // 2
    kpe = _OFF_MKV + MLA_KV_LORA
    zeros = lambda n: jnp.zeros((D_MODEL, n), w_in.dtype)
    misc = jnp.concatenate([cols(_OFF_MQ, MLA_Q_LORA), cols(_OFF_MKV, MLA_KV_LORA),
                            cols(kpe, MLA_ROPE), cols(_OFF_GB, 8), cols(_OFF_GA, 8), zeros(48),
                            cols(kpe + half, half), cols(kpe, half), zeros(64)], axis=-1).astype(BF16)
    return main, misc


def trunk_layer(x, norm_mix_pre, norm_mix_post, norm_ffn_pre, norm_ffn_post, w_in,
                hy_conv_w, hy_conv_b, hy_ffn_w1, hy_ffn_b1, hy_sin_freq, hy_ffn_w2, hy_ffn_b2,
                hy_ffn_w3, hy_skip, mla_q_norm, mla_wq_b, mla_kv_norm, mla_wkv_b,
                gdn_conv_w, gdn_a_log, gdn_dt_bias, gdn_out_norm,
                w_branch, w_out, w_gate, w_up, w_down):
    B, L, D = x.shape
    T = B * L
    xt = x.reshape(T, D)
    w_main, w_misc = _split_w_in(w_in)
    g_pre = norm_mix_pre[None, :]
    proj = norm_mm(xt, g_pre, w_main, tm=2048, tn=512, out_dtype=BF16)
    misc = norm_mm(xt, g_pre, w_misc, tm=1024, tn=MISC_W, out_dtype=F32)
    proj3 = proj.reshape(B, L, MAIN_W)
    misc3 = misc.reshape(B, L, MISC_W)
    o_hy = hyena_mixer(proj3, hy_conv_w, hy_conv_b, hy_ffn_w1, hy_ffn_b1, hy_sin_freq,
                       hy_ffn_w2, hy_ffn_b2, hy_ffn_w3, hy_skip)
    o_mla = mla_mixer(misc3, mla_q_norm, mla_wq_b, mla_kv_norm, mla_wkv_b).reshape(T, BRANCH_W)
    o_gdn = gdn_mixer(proj3, misc3, gdn_conv_w, gdn_a_log, gdn_dt_bias, gdn_out_norm).reshape(T, BRANCH_W)
    xt = merge_out(xt, proj, o_hy, o_mla, o_gdn, w_branch.astype(BF16), w_out.astype(BF16),
                   norm_mix_post[None, :])
    xt = ffn(xt, norm_ffn_pre[None, :], w_gate.astype(BF16), w_up.astype(BF16), w_down.astype(BF16),
             norm_ffn_post[None, :])
    return xt.reshape(B, L, D)


def kernel(x_prompt, x_sample, norm_mix_pre, norm_mix_post, norm_ffn_pre, norm_ffn_post, w_in,
           hy_conv_w, hy_conv_b, hy_ffn_w1, hy_ffn_b1, hy_sin_freq, hy_ffn_w2, hy_ffn_b2,
           hy_ffn_w3, hy_skip, mla_q_norm, mla_wq_b, mla_kv_norm, mla_wkv_b,
           gdn_conv_w, gdn_a_log, gdn_dt_bias, gdn_out_norm,
           w_branch, w_out, w_gate, w_up, w_down):
    weights = (norm_mix_pre, norm_mix_post, norm_ffn_pre, norm_ffn_post, w_in,
               hy_conv_w, hy_conv_b, hy_ffn_w1, hy_ffn_b1, hy_sin_freq, hy_ffn_w2, hy_ffn_b2,
               hy_ffn_w3, hy_skip, mla_q_norm, mla_wq_b, mla_kv_norm, mla_wkv_b,
               gdn_conv_w, gdn_a_log, gdn_dt_bias, gdn_out_norm,
               w_branch, w_out, w_gate, w_up, w_down)

    def run_trunk(x):
        for layer in range(DEPTH):
            x = trunk_layer(x, *[w[layer] for w in weights])
        return x

    return (run_trunk(x_prompt), run_trunk(x_sample))
```

```python
import functools
import math

import jax
import jax.numpy as jnp
from jax import lax
from jax.experimental import pallas as pl
from jax.experimental.pallas import tpu as pltpu
import numpy as np

F32 = jnp.float32
BF16 = jnp.bfloat16

D_MODEL = 1024
DEPTH = 2
BRANCH_W = 512
N_BRANCH = 3
HY_W = BRANCH_W
HY_ORDER = 2
HY_EMB = 33
HY_FAST_PCT = 0.3
HY_SLOW_PCT = 1.5
HY_TARGET = 1e-2
MLA_HEADS = 4
MLA_NOPE = 128
MLA_ROPE = 64
MLA_V = 128
MLA_Q_LORA = 256
MLA_KV_LORA = 128
ROPE_THETA = 10000.0
GDN_HEADS = 4
GDN_DK = 128
GDN_DV = 128
GDN_CHUNK = 64
NORM_EPS = 1e-6

_OFF_HY = 0
_OFF_MQ = _OFF_HY + (HY_ORDER + 1) * HY_W
_OFF_MKV = _OFF_MQ + MLA_Q_LORA
_OFF_GQKV = _OFF_MKV + MLA_KV_LORA + MLA_ROPE
_OFF_GZ = _OFF_GQKV + GDN_HEADS * (2 * GDN_DK + GDN_DV)
_OFF_GB = _OFF_GZ + GDN_HEADS * GDN_DV
_OFF_GA = _OFF_GB + 2 * GDN_HEADS
_OFF_GATE = _OFF_GA + 2 * GDN_HEADS
_D_IN = _OFF_GATE + N_BRANCH * D_MODEL

MAIN_W = 3072 + 1536 + 1536 + 512
MISC_W = 640

LANES = 128
VMEM_LIMIT_BYTES = 56 * 1024 * 1024


def _cparams(*sem):
    return pltpu.CompilerParams(dimension_semantics=sem, vmem_limit_bytes=VMEM_LIMIT_BYTES)


def _full(shape):
    nd = len(shape)
    return pl.BlockSpec(shape, lambda *_: (0,) * nd)


def _norm_mm_kernel(x_ref, g_ref, w_ref, o_ref, h_ref):
    @pl.when(pl.program_id(1) == 0)
    def _():
        x = x_ref[...]
        r = lax.rsqrt(jnp.mean(x * x, axis=-1, keepdims=True) + NORM_EPS)
        h_ref[...] = (x * r * g_ref[...]).astype(BF16)

    o_ref[...] = jnp.dot(h_ref[...], w_ref[...], preferred_element_type=F32).astype(o_ref.dtype)


def norm_mm(x, g, w, tm, tn, out_dtype):
    T, D = x.shape
    N = w.shape[1]
    tm = min(tm, T)
    return pl.pallas_call(
        _norm_mm_kernel,
        grid=(T // tm, N // tn),
        in_specs=[pl.BlockSpec((tm, D), lambda i, j: (i, 0)),
                  pl.BlockSpec((1, D), lambda i, j: (0, 0)),
                  pl.BlockSpec((D, tn), lambda i, j: (0, j))],
        out_specs=pl.BlockSpec((tm, tn), lambda i, j: (i, j)),
        out_shape=jax.ShapeDtypeStruct((T, N), out_dtype),
        scratch_shapes=[pltpu.VMEM((tm, D), BF16)],
        compiler_params=_cparams("parallel", "arbitrary"),
        name="norm_mm",
    )(x, g, w)


def _dwconv_kernel(x_ref, p_ref, n_ref, w_ref, b_ref, o_ref, *, silu):
    i = pl.program_id(1)
    last = pl.num_programs(1) - 1
    x = x_ref[0].astype(F32)
    tl = x.shape[0]
    halo = p_ref.shape[1]
    prev_row = jnp.where(i > 0, p_ref[0, halo - 1:halo, :].astype(F32), 0.0)
    next_row = jnp.where(i < last, n_ref[0, 0:1, :].astype(F32), 0.0)
    rows = lax.broadcasted_iota(jnp.int32, x.shape, 0)
    x_dn = jnp.where(rows == 0, prev_row, pltpu.roll(x, 1, axis=0))
    x_up = jnp.where(rows == tl - 1, next_row, pltpu.roll(x, tl - 1, axis=0))
    out = x_dn * w_ref[0:1, :] + x * w_ref[1:2, :] + x_up * w_ref[2:3, :] + b_ref[...]
    if silu:
        out = out * jax.nn.sigmoid(out)
    o_ref[0] = out.astype(o_ref.dtype)


def dwconv3(x, col_blk, cw, w, b, silu, out_dtype, tl=512):
    B, L, _ = x.shape
    tl = min(tl, L)
    halo = 8 * (4 // x.dtype.itemsize)
    nsub = tl // halo
    lastblk = L // halo - 1
    return pl.pallas_call(
        functools.partial(_dwconv_kernel, silu=silu),
        grid=(B, L // tl),
        in_specs=[pl.BlockSpec((1, tl, cw), lambda b_, i: (b_, i, col_blk)),
                  pl.BlockSpec((1, halo, cw), lambda b_, i: (b_, jnp.maximum(i * nsub - 1, 0), col_blk)),
                  pl.BlockSpec((1, halo, cw), lambda b_, i: (b_, jnp.minimum((i + 1) * nsub, lastblk), col_blk)),
                  _full((3, cw)), _full((1, cw))],
        out_specs=pl.BlockSpec((1, tl, cw), lambda b_, i: (b_, i, 0)),
        out_shape=jax.ShapeDtypeStruct((B, L, cw), out_dtype),
        compiler_params=_cparams("parallel", "parallel"),
        name="dwconv3",
    )(x, x, x, w, b)


def _tile_lanes(t, c):
    return t if c == LANES else jnp.concatenate([t] * (c // LANES), axis=-1)


def _fft_s1_kernel(f_ref, x_ref, twr_ref, twi_ref, o_ref, *, nb, n1):
    c = x_ref.shape[-1]
    for t in range(nb):
        a = jnp.dot(f_ref[...], x_ref[0, t], preferred_element_type=F32)
        ar, ai = a[:n1], a[n1:]
        twr = _tile_lanes(twr_ref[t], c)
        twi = _tile_lanes(twi_ref[t], c)
        o_ref[0, t, :n1, :] = (ar * twr + ai * twi).astype(BF16)
        o_ref[0, t, n1:, :] = (ai * twr - ar * twi).astype(BF16)


def _fft_s2_kernel(f_ref, finv_ref, b_ref, kr_ref, ki_ref, o_ref, *, nb, n2):
    for t in range(nb):
        x = jnp.dot(f_ref[t], b_ref[0, t], preferred_element_type=F32)
        xr, xi = x[:n2], x[n2:]
        kr, ki = kr_ref[t], ki_ref[t]
        y = jnp.concatenate([xr * kr - xi * ki, xr * ki + xi * kr], axis=0).astype(BF16)
        o_ref[0, t] = jnp.dot(finv_ref[t], y, preferred_element_type=F32).astype(BF16)


def _fft_mm_kernel(g_ref, d_ref, o_ref, *, nb):
    for t in range(nb):
        o_ref[0, t] = jnp.dot(g_ref[...], d_ref[0, t], preferred_element_type=F32).astype(o_ref.dtype)


def _dft_tables(L):
    N = 2 * L
    n2 = 128 if N >= 4096 else 16
    n1 = N // n2
    n1h = n1 // 2
    k1 = np.arange(n1)[:, None]
    m1 = np.arange(n1h)[None, :]
    th = 2.0 * np.pi * ((k1 * m1) % n1) / n1
    c, s = np.cos(th), np.sin(th)
    f1 = np.block([[c, s], [-s, c]])
    g1 = np.block([[c.T, -s.T], [s.T, c.T]]) / N
    as_bf16 = lambda m: jnp.asarray(m, dtype=F32).astype(BF16)
    k = (jnp.arange(n1, dtype=jnp.int32)[:, None, None]
         + n1 * jnp.arange(n2, dtype=jnp.int32)[None, :, None])
    ang = ((k * jnp.arange(n2, dtype=jnp.int32)[None, None, :]) % N).astype(F32) * (2.0 * math.pi / N)
    c2, s2 = jnp.cos(ang), jnp.sin(ang)
    f2 = jnp.concatenate([jnp.concatenate([c2, s2], axis=2),
                          jnp.concatenate([-s2, c2], axis=2)], axis=1).astype(BF16)
    return n1, n2, as_bf16(f1), as_bf16(g1), f2, f2.transpose(0, 2, 1)


def _twiddles(n1, n2):
    N = n1 * n2
    prod = (jnp.arange(n2, dtype=jnp.int32)[:, None] * jnp.arange(n1, dtype=jnp.int32)[None, :]) % N
    ang = prod.astype(F32) * (2.0 * math.pi / N)
    shape = (n2, n1, LANES)
    return (jnp.broadcast_to(jnp.cos(ang)[:, :, None], shape),
            jnp.broadcast_to(jnp.sin(ang)[:, :, None], shape))


def fft_long_conv(z, kspec, order, tabs):
    B, L, C = z.shape
    n1, n2, f1, g1, f2, f2inv = tabs
    n1h = n1 // 2
    P = B // 2
    nb1 = min(8, n2)
    nb2 = min(8, n1)
    zt = z.astype(BF16).reshape(P, 2, n1h, n2, C).transpose(0, 3, 1, 2, 4).reshape(P, n2, 2 * n1h, C)
    a = pl.pallas_call(
        functools.partial(_fft_mm_kernel, nb=nb1),
        grid=(n2 // nb1, P),
        in_specs=[_full((2 * n1, 2 * n1h)),
                  pl.BlockSpec((1, nb1, 2 * n1h, C), lambda j, p: (p, j, 0, 0))],
        out_specs=pl.BlockSpec((1, nb1, 2 * n1, C), lambda j, p: (p, j, 0, 0)),
        out_shape=jax.ShapeDtypeStruct((P, n2, 2 * n1, C), BF16),
        compiler_params=_cparams("parallel", "parallel"),
        name="fft_s1",
    )(f1, zt)
    bt = a.reshape(P, n2, 2, n1, C).transpose(0, 3, 2, 1, 4).reshape(P, n1, 2 * n2, C)
    d = pl.pallas_call(
        functools.partial(_fft_s2_kernel, nb=nb2, n2=n2),
        grid=(n1 // nb2, P),
        in_specs=[pl.BlockSpec((nb2, 2 * n2, 2 * n2), lambda j, p: (j, 0, 0)),
                  pl.BlockSpec((nb2, 2 * n2, 2 * n2), lambda j, p: (j, 0, 0)),
                  pl.BlockSpec((1, nb2, 2 * n2, C), lambda j, p: (p, j, 0, 0)),
                  pl.BlockSpec((nb2, n2, C), lambda j, p: (j, 0, order)),
                  pl.BlockSpec((nb2, n2, C), lambda j, p: (j, 1, order))],
        out_specs=pl.BlockSpec((1, nb2, 2 * n2, C), lambda j, p: (p, j, 0, 0)),
        out_shape=jax.ShapeDtypeStruct((P, n1, 2 * n2, C), BF16),
        compiler_params=_cparams("parallel", "parallel"),
        name="fft_s2",
    )(f2, f2inv, bt, kspec, kspec)
    dt = d.reshape(P, n1, 2, n2, C).transpose(0, 3, 2, 1, 4).reshape(P, n2, 2 * n1, C)
    y = pl.pallas_call(
        functools.partial(_fft_mm_kernel, nb=nb1),
        grid=(n2 // nb1, P),
        in_specs=[_full((2 * n1h, 2 * n1)),
                  pl.BlockSpec((1, nb1, 2 * n1, C), lambda j, p: (p, j, 0, 0))],
        out_specs=pl.BlockSpec((1, nb1, 2 * n1h, C), lambda j, p: (p, j, 0, 0)),
        out_shape=jax.ShapeDtypeStruct((P, n2, 2 * n1h, C), BF16),
        compiler_params=_cparams("parallel", "parallel"),
        name="fft_s3",
    )(g1, dt)
    return y.reshape(P, n2, 2, n1h, C).transpose(0, 2, 3, 1, 4).reshape(B, L, C)


def _hy_gate_kernel(g_ref, c_ref, z_ref, s_ref, o_ref):
    o_ref[...] = (g_ref[...].astype(F32) * (c_ref[...].astype(F32) + s_ref[...] * z_ref[...].astype(F32))
                  ).astype(o_ref.dtype)


def hy_gate(u, gate_blk, conv, zsrc, z_blk, skip, out_dtype, tl=1024):
    T, C = conv.shape
    tl = min(tl, T)
    return pl.pallas_call(
        _hy_gate_kernel,
        grid=(T // tl,),
        in_specs=[pl.BlockSpec((tl, C), lambda i: (i, gate_blk)),
                  pl.BlockSpec((tl, C), lambda i: (i, 0)),
                  pl.BlockSpec((tl, C), lambda i: (i, z_blk)),
                  _full((1, C))],
        out_specs=pl.BlockSpec((tl, C), lambda i: (i, 0)),
        out_shape=jax.ShapeDtypeStruct((T, C), out_dtype),
        compiler_params=_cparams("parallel"),
        name="hy_gate",
    )(u, conv, zsrc, skip)


def hyena_filter_spectrum(L, n1, n2, tw1r, tw1i, w1, b1, freq, w2, b2, w3):
    t = jnp.linspace(0.0, 1.0, L, dtype=F32)[:, None]
    bands = (HY_EMB - 1) // 2
    wpos = (2.0 * math.pi / L) * jnp.arange(L, dtype=F32)[:, None]
    fr = jnp.linspace(1e-4, bands - 1, bands, dtype=F32)[None, :]
    feats = jnp.concatenate([t, jnp.cos(fr * wpos), -jnp.sin(fr * wpos)], axis=-1)
    deltas = jnp.abs(jnp.linspace(math.log(HY_TARGET) / HY_SLOW_PCT,
                                  math.log(HY_TARGET) / HY_FAST_PCT, HY_W, dtype=F32))
    feats2 = jnp.concatenate([feats, feats[0:1], feats[:0:-1]], axis=0)
    feats2 = jnp.pad(feats2, ((0, 0), (0, LANES - HY_EMB)))
    w1p = jnp.pad(w1, ((0, LANES - HY_EMB), (0, 0)))
    hid = w3.shape[0]
    oc = HY_ORDER * HY_W
    w3d = w3.reshape(hid, HY_ORDER, 2, HY_W).transpose(2, 0, 1, 3).reshape(2, hid, oc)
    tl = min(512, L)
    nh = L // tl
    kern, asum = pl.pallas_call(
        functools.partial(_filter_taps_kernel, zero_tile=nh),
        grid=(2 * nh,),
        in_specs=[pl.BlockSpec((tl, LANES), lambda i: (i, 0)), _full((LANES, hid)), _full((1, hid)),
                  _full((1, hid)), _full((hid, hid)), _full((1, hid)),
                  pl.BlockSpec((1, hid, oc), lambda i: (i // nh, 0, 0)), _full((1, oc))],
        out_specs=[pl.BlockSpec((tl, oc), lambda i: (i, 0)), _full((1, oc))],
        out_shape=[jax.ShapeDtypeStruct((2 * L, oc), F32), jax.ShapeDtypeStruct((1, oc), F32)],
        compiler_params=_cparams("arbitrary"),
        name="filter_taps",
    )(feats2, w1p, b1, freq, w2, b2, w3d, jnp.tile(deltas, HY_ORDER)[None, :])
    return filter_dft(kern, asum, n1, n2, tw1r, tw1i)


def _filter_taps_kernel(f_ref, w1_ref, b1_ref, fq_ref, w2_ref, b2_ref, w3_ref, dl_ref, k_ref, s_ref, *,
                        zero_tile):
    i = pl.program_id(0)
    hp = lax.Precision.HIGHEST
    f = f_ref[...]
    h = jnp.sin(fq_ref[...] * (jnp.dot(f, w1_ref[...], precision=hp, preferred_element_type=F32)
                               + b1_ref[...]))
    h = jnp.sin(fq_ref[...] * (jnp.dot(h, w2_ref[...], precision=hp, preferred_element_type=F32)
                               + b2_ref[...]))
    taps = jnp.dot(h, w3_ref[0], precision=hp, preferred_element_type=F32)
    taps = taps * jnp.exp(-f[:, 0:1] * dl_ref[...])
    row = lax.broadcasted_iota(jnp.int32, taps.shape, 0)
    taps = jnp.where((i == zero_tile) & (row == 0), 0.0, taps)
    k_ref[...] = taps

    @pl.when(i == 0)
    def _():
        s_ref[...] = jnp.zeros(s_ref.shape, F32)

    s_ref[...] += jnp.sum(jnp.abs(taps), axis=0, keepdims=True)


def _spec_s2_kernel(f_ref, b_ref, s_ref, o_ref, *, nb):
    for t in range(nb):
        o_ref[t] = jnp.dot(f_ref[...], b_ref[0, t], preferred_element_type=F32) / s_ref[...]


def filter_dft(kern, asum, n1, n2, tw1r, tw1i):
    N, C = kern.shape
    ct = 512
    k1 = np.arange(n1)
    th = 2.0 * np.pi * ((k1[:, None] * k1[None, :]) % n1) / n1
    f1 = np.concatenate([np.cos(th), -np.sin(th)], axis=0)
    a2 = np.arange(n2)
    ph = 2.0 * np.pi * ((a2[:, None] * a2[None, :]) % n2) / n2
    f2 = np.block([[np.cos(ph), np.sin(ph)], [-np.sin(ph), np.cos(ph)]])

    as_bf16 = lambda m: jnp.asarray(m, dtype=F32).astype(BF16)
    nb1 = min(8, n2)
    nb2 = min(8, n1)
    xt = kern.astype(BF16).reshape(n1, n2, C).transpose(1, 0, 2)[None]
    a = pl.pallas_call(
        functools.partial(_fft_s1_kernel, nb=nb1, n1=n1),
        grid=(n2 // nb1, C // ct),
        in_specs=[_full((2 * n1, n1)),
                  pl.BlockSpec((1, nb1, n1, ct), lambda j, c: (0, j, 0, c)),
                  pl.BlockSpec((nb1, n1, LANES), lambda j, c: (j, 0, 0)),
                  pl.BlockSpec((nb1, n1, LANES), lambda j, c: (j, 0, 0))],
        out_specs=pl.BlockSpec((1, nb1, 2 * n1, ct), lambda j, c: (0, j, 0, c)),
        out_shape=jax.ShapeDtypeStruct((1, n2, 2 * n1, C), BF16),
        compiler_params=_cparams("parallel", "parallel"),
        name="spec_s1",
    )(as_bf16(f1), xt, tw1r, tw1i)
    bt = a.reshape(n2, 2, n1, C).transpose(2, 1, 0, 3).reshape(1, n1, 2 * n2, C)
    return pl.pallas_call(
        functools.partial(_spec_s2_kernel, nb=nb2),
        grid=(n1 // nb2, C // ct),
        in_specs=[_full((2 * n2, 2 * n2)),
                  pl.BlockSpec((1, nb2, 2 * n2, ct), lambda j, c: (0, j, 0, c)),
                  pl.BlockSpec((1, ct), lambda j, c: (0, c))],
        out_specs=pl.BlockSpec((nb2, 2 * n2, ct), lambda j, c: (j, 0, c)),
        out_shape=jax.ShapeDtypeStruct((n1, 2 * n2, C), F32),
        compiler_params=_cparams("parallel", "parallel"),
        name="spec_s2",
    )(as_bf16(f2), bt, asum)


def hyena_mixer(proj3, conv_w, conv_b, w1, b1, freq, w2, b2, w3, skip):
    B, L, _ = proj3.shape
    T = B * L
    u = dwconv3(proj3, 2, 1536, conv_w, conv_b[None, :], silu=False, out_dtype=BF16).reshape(T, 1536)
    n1, n2, f1, g1, f2, f2inv = _dft_tables(L)
    tw1r, tw1i = _twiddles(n1, n2)
    tabs = (n1, n2, f1, g1, f2, f2inv)
    kspec = hyena_filter_spectrum(L, n1, n2, tw1r, tw1i, w1, b1[None, :], freq[None, :], w2, b2[None, :], w3)
    conv = fft_long_conv(u[:, 1024:1536].reshape(B, L, HY_W), kspec, 0, tabs).reshape(T, HY_W)
    z1 = hy_gate(u, 0, conv, u, 2, skip[0:1], F32)
    conv = fft_long_conv(z1.reshape(B, L, HY_W), kspec, 1, tabs).reshape(T, HY_W)
    return hy_gate(u, 1, conv, z1, 0, skip[1:2], BF16)


def _mla_proj_kernel(m_ref, qg_ref, kg_ref, wq_ref, wkv_ref, cq_ref, sq_ref, ck_ref, sk_ref,
                     q_ref, k_ref, v_ref):
    m = m_ref[0]
    ql = m[:, 0:256]
    r = lax.rsqrt(jnp.mean(ql * ql, axis=-1, keepdims=True) + NORM_EPS)
    q = jnp.dot((ql * r * qg_ref[...]).astype(BF16), wq_ref[...], preferred_element_type=F32)
    ckv = m[:, 256:384]
    r = lax.rsqrt(jnp.mean(ckv * ckv, axis=-1, keepdims=True) + NORM_EPS)
    kv = jnp.dot((ckv * r * kg_ref[...]).astype(BF16), wkv_ref[...], preferred_element_type=F32)
    q_pe = ((q[:, 512:1024] * cq_ref[...] + q[:, 1024:1536] * sq_ref[...]) * ATTN_Q_SCALE).astype(BF16)
    k_pe = (m[:, 384:512] * ck_ref[...] + m[:, 512:640] * sk_ref[...]).astype(BF16)
    for h in range(MLA_HEADS):
        sl = slice(h * LANES, (h + 1) * LANES)
        q_ref[0, :, 2 * h * LANES:(2 * h + 1) * LANES] = (q[:, sl] * ATTN_Q_SCALE).astype(BF16)
        q_ref[0, :, (2 * h + 1) * LANES:(2 * h + 2) * LANES] = q_pe[:, sl]
        k_ref[0, :, 2 * h * LANES:(2 * h + 1) * LANES] = kv[:, sl].astype(BF16)
        k_ref[0, :, (2 * h + 1) * LANES:(2 * h + 2) * LANES] = k_pe
        v_ref[0, :, 2 * h * LANES:(2 * h + 1) * LANES] = kv[:, 512 + h * MLA_V:512 + (h + 1) * MLA_V].astype(BF16)
        v_ref[0, :, (2 * h + 1) * LANES:(2 * h + 2) * LANES] = jnp.ones((m.shape[0], LANES), BF16)


ATTN_Q_SCALE = (MLA_NOPE + MLA_ROPE) ** -0.5 * math.log2(math.e)
ATTN_ROW_PARTS = 2


def _attn_kernel(q_ref, k_ref, v_ref, o_ref, *scratch):
    ki = pl.program_id(2)
    dqk = 2 * LANES
    m_refs, acc_refs = scratch[:MLA_HEADS], scratch[MLA_HEADS:]

    @pl.when(ki == 0)
    def _():
        for m_ref, acc_ref in zip(m_refs, acc_refs):
            m_ref[...] = jnp.full(m_ref.shape, -jnp.inf, F32)
            acc_ref[...] = jnp.zeros(acc_ref.shape, F32)

    part = q_ref.shape[1] // ATTN_ROW_PARTS
    chains = [(h, slice(h * dqk, (h + 1) * dqk), slice(r * part, (r + 1) * part))
              for h in range(MLA_HEADS) for r in range(ATTN_ROW_PARTS)]
    scores = [lax.dot_general(q_ref[0, rows, sl], k_ref[0, :, sl], (((1,), (1,)), ((), ())),
                              preferred_element_type=F32) for _, sl, rows in chains]
    weights = []
    for (h, sl, rows), s in zip(chains, scores):
        m_prev = m_refs[h][rows, :]
        m_new = jnp.maximum(m_prev, jnp.max(s, axis=-1, keepdims=True))
        m_refs[h][rows, :] = m_new
        weights.append((jnp.exp2(m_prev - m_new), jnp.exp2(s - m_new).astype(BF16)))
    for (h, sl, rows), (alpha, p) in zip(chains, weights):
        acc_refs[h][rows, :] = (alpha * acc_refs[h][rows, :]
                                + jnp.dot(p, v_ref[0, :, sl], preferred_element_type=F32))

    @pl.when(ki == pl.num_programs(2) - 1)
    def _():
        for h, acc_ref in enumerate(acc_refs):
            o_ref[0, :, h * MLA_V:(h + 1) * MLA_V] = (acc_ref[:, :MLA_V] / acc_ref[:, MLA_V:]).astype(BF16)


def _rope_lane_tables(L):
    half = MLA_ROPE // 2
    inv = ROPE_THETA ** (-jnp.arange(half, dtype=F32) / half)
    ang = jnp.arange(L, dtype=F32)[:, None] * inv[None, :]
    cos, sin = jnp.cos(ang), jnp.sin(ang)
    zeros = jnp.zeros((L, LANES - MLA_ROPE), F32)
    ck = jnp.concatenate([cos, cos, zeros], axis=-1)
    sk = jnp.concatenate([-sin, sin, zeros], axis=-1)
    return jnp.tile(ck, (1, MLA_HEADS)), jnp.tile(sk, (1, MLA_HEADS)), ck, sk


def mla_mixer(misc3, q_norm, wq_b, kv_norm, wkv_b, tl=512, tq=1024, tk=1024):
    B, L, _ = misc3.shape
    H = MLA_HEADS
    tl, tq, tk = min(tl, L), min(tq, L), min(tk, L)
    half = MLA_ROPE // 2
    wq = wq_b.reshape(MLA_Q_LORA, H, MLA_NOPE + MLA_ROPE)
    x1 = wq[:, :, MLA_NOPE:MLA_NOPE + half]
    x2 = wq[:, :, MLA_NOPE + half:]
    zpad = jnp.zeros((MLA_Q_LORA, H, LANES - MLA_ROPE), wq.dtype)
    wq = jnp.concatenate([wq[:, :, :MLA_NOPE].reshape(MLA_Q_LORA, -1),
                          jnp.concatenate([x1, x2, zpad], axis=-1).reshape(MLA_Q_LORA, -1),
                          jnp.concatenate([x2, x1, zpad], axis=-1).reshape(MLA_Q_LORA, -1)],
                         axis=-1).astype(BF16)
    wkv = wkv_b.reshape(MLA_KV_LORA, H, MLA_NOPE + MLA_V)
    wkv = jnp.concatenate([wkv[:, :, :MLA_NOPE].reshape(MLA_KV_LORA, -1),
                           wkv[:, :, MLA_NOPE:].reshape(MLA_KV_LORA, -1)], axis=-1).astype(BF16)
    cq, sq, ck, sk = _rope_lane_tables(L)
    tok = lambda w: pl.BlockSpec((1, tl, w), lambda b, i: (b, i, 0))
    pos = lambda w: pl.BlockSpec((tl, w), lambda b, i: (i, 0))
    sds = lambda w: jax.ShapeDtypeStruct((B, L, w), BF16)
    dqk = 2 * LANES
    nk = L // tk
    q, k, v = pl.pallas_call(
        _mla_proj_kernel,
        grid=(B, L // tl),
        in_specs=[tok(MISC_W), _full((1, MLA_Q_LORA)), _full((1, MLA_KV_LORA)),
                  _full(wq.shape), _full(wkv.shape), pos(H * LANES), pos(H * LANES), pos(LANES), pos(LANES)],
        out_specs=[tok(H * dqk), tok(H * dqk), tok(H * dqk)],
        out_shape=[sds(H * dqk), sds(H * dqk), sds(H * dqk)],
        compiler_params=_cparams("parallel", "parallel"),
        name="mla_proj",
    )(misc3, q_norm[None, :], kv_norm[None, :], wq, wkv, cq, sq, ck, sk)
    return pl.pallas_call(
        _attn_kernel,
        grid=(B, L // tq, nk),
        in_specs=[pl.BlockSpec((1, tq, H * dqk), lambda b, i, j: (b, i, 0)),
                  pl.BlockSpec((1, tk, H * dqk), lambda b, i, j: (b, j, 0)),
                  pl.BlockSpec((1, tk, H * dqk), lambda b, i, j: (b, j, 0))],
        out_specs=pl.BlockSpec((1, tq, H * MLA_V), lambda b, i, j: (b, i, 0)),
        out_shape=jax.ShapeDtypeStruct((B, L, H * MLA_V), BF16),
        scratch_shapes=([pltpu.VMEM((tq, 1), F32)] * H + [pltpu.VMEM((tq, 2 * MLA_V), F32)] * H),
        compiler_params=_cparams("parallel", "parallel", "arbitrary"),
        name="mla_attn",
    )(q, k, v)


_BETA_LANE = 64
_G_LANE = 72
CH = GDN_CHUNK
NDH = 2 * GDN_HEADS
GDN_STEP_CHUNKS = 8


def _gdn_prep_kernel(c_ref, m_ref, alog_ref, dtb_ref, qn_ref, kn_ref, bg_ref):
    for h in range(GDN_HEADS):
        sl = slice(h * GDN_DK, (h + 1) * GDN_DK)
        qh = c_ref[0, :, sl]
        qn_ref[0, :, sl] = (qh * lax.rsqrt(jnp.sum(qh * qh, axis=-1, keepdims=True) + NORM_EPS)
                            * (GDN_DK ** -0.5))
        kh = c_ref[0, :, GDN_HEADS * GDN_DK + h * GDN_DK:GDN_HEADS * GDN_DK + (h + 1) * GDN_DK]
        kn_ref[0, :, sl] = kh * lax.rsqrt(jnp.sum(kh * kh, axis=-1, keepdims=True) + NORM_EPS)
    m = m_ref[0]
    lane = lax.broadcasted_iota(jnp.int32, m.shape, 1)
    beta = jax.nn.sigmoid(m)
    x = m + dtb_ref[...]
    softplus = jnp.maximum(x, 0.0) + jnp.log1p(jnp.exp(-jnp.abs(x)))
    g = -jnp.exp(alog_ref[...]) * softplus
    is_beta = (lane >= _BETA_LANE) & (lane < _BETA_LANE + NDH)
    is_g = (lane >= _G_LANE) & (lane < _G_LANE + NDH)
    bg_ref[0] = jnp.where(is_beta, beta, jnp.where(is_g, g, 0.0))


def _split3(x):
    hi = x.astype(BF16)
    r = x - hi.astype(F32)
    mid = r.astype(BF16)
    lo = (r - mid.astype(F32)).astype(BF16)
    return hi, mid, lo


def _dot_nt(a, b):
    return lax.dot_general(a, b, (((1,), (1,)), ((), ())), preferred_element_type=F32)


def _gdn_chunk_kernel(q_ref, k_ref, bg_ref, a_ref, attn_ref, gc_ref):
    lane = lax.broadcasted_iota(jnp.int32, (CH, LANES), 1)
    is_g = (lane >= _G_LANE) & (lane < _G_LANE + NDH)
    ri = lax.broadcasted_iota(jnp.int32, (CH, CH), 0)
    ci = lax.broadcasted_iota(jnp.int32, (CH, CH), 1)
    lower = ri >= ci
    upper = ri <= ci
    tril = lower.astype(BF16)
    triu = upper.astype(BF16)
    for c in range(GDN_STEP_CHUNKS):
        rows = slice(c * CH, (c + 1) * CH)
        bg = bg_ref[0, rows, :]
        pieces = _split3(jnp.where(is_g, bg, 0.0))
        pre = sum(jnp.dot(tril, p, preferred_element_type=F32) for p in pieces)
        suf = sum(jnp.dot(triu, p, preferred_element_type=F32) for p in pieces)
        gc = jnp.where(lane >= _G_LANE + GDN_HEADS, suf, pre)
        gc_ref[0, rows, :] = gc
        gct = gc.T
        for d in range(2):
            causal = lower if d == 0 else upper
            strict = (ri > ci) if d == 0 else (ri < ci)
            for h in range(GDN_HEADS):
                dh = d * GDN_HEADS + h
                sl = slice(h * GDN_DK, (h + 1) * GDN_DK)
                kh = k_ref[0, rows, sl]
                kb = (kh * bg[:, _BETA_LANE + dh:_BETA_LANE + dh + 1]).astype(BF16)
                khb = kh.astype(BF16)
                diff = gc[:, _G_LANE + dh:_G_LANE + dh + 1] - gct[_G_LANE + dh:_G_LANE + dh + 1, :]
                dec = jnp.exp(jnp.where(causal, diff, -jnp.inf))
                a_ref[0, c, dh] = jnp.where(strict, _dot_nt(kb, khb) * dec, 0.0)
                attn_ref[0, c, dh] = (_dot_nt(q_ref[0, rows, sl].astype(BF16), khb) * dec).astype(BF16)


def _gdn_solve_kernel(a_ref, t_ref, *, nblk_fwd):
    bwd = pl.program_id(0) >= nblk_fwd
    t_ref[...] = jnp.zeros(t_ref.shape, F32)

    @pl.when(jnp.logical_not(bwd))
    def _():
        _solve_triangular(a_ref, t_ref, reverse=False)

    @pl.when(bwd)
    def _():
        _solve_triangular(a_ref, t_ref, reverse=True)


def _solve_triangular(a_ref, t_ref, *, reverse):
    nblk = CH // 8
    sub8 = lax.broadcasted_iota(jnp.int32, (8, LANES), 0)
    for phase in range(nblk):
        rb = nblk - 1 - phase if reverse else phase
        groups = range(rb, nblk) if reverse else range(0, rb + 1)

        def row(it, carry, rb=rb, groups=groups):
            i = 8 * rb + (7 - it if reverse else it)
            acc = {cg: ((sub8 + 8 * cg == i).astype(F32) if cg == rb else jnp.zeros((8, LANES), F32))
                   for cg in groups}
            for jb in groups:
                ablk = a_ref[i, 8 * jb:8 * jb + 8, :]
                for jj in range(8):
                    arow = ablk[jj:jj + 1, :]
                    for cg in (range(jb, nblk) if reverse else range(0, jb + 1)):
                        acc[cg] = acc[cg] - arow * t_ref[8 * jb + jj, 8 * cg:8 * cg + 8, :]
            for cg in groups:
                t_ref[i, 8 * cg:8 * cg + 8, :] = acc[cg]
            return carry

        lax.fori_loop(0, 8, row, 0)


def _gdn_scan_kernel(qf_ref, kf_ref, vf_ref, bgf_ref, gcf_ref, tf_ref, af_ref,
                     qb_ref, kb_ref, vb_ref, bgb_ref, gcb_ref, tb_ref, ab_ref,
                     of_ref, ob_ref, *s_refs):
    @pl.when(pl.program_id(1) == 0)
    def _():
        for s_ref in s_refs:
            s_ref[...] = jnp.zeros(s_ref.shape, F32)

    dirs = ((qf_ref, kf_ref, vf_ref, bgf_ref, gcf_ref, tf_ref, af_ref, of_ref, CH - 1),
            (qb_ref, kb_ref, vb_ref, bgb_ref, gcb_ref, tb_ref, ab_ref, ob_ref, 0))
    for step in range(GDN_STEP_CHUNKS):
        probs = []
        for d, (q_ref, k_ref, v_ref, bg_ref, gc_ref, t_ref, a_ref, o_ref, last) in enumerate(dirs):
            c = step if d == 0 else GDN_STEP_CHUNKS - 1 - step
            rows = slice(c * CH, (c + 1) * CH)
            for h in range(GDN_HEADS):
                dh = d * GDN_HEADS + h
                sl = slice(h * GDN_DK, (h + 1) * GDN_DK)
                k, v = k_ref[0, rows, sl], v_ref[0, rows, sl]
                beta = bg_ref[0, rows, _BETA_LANE + dh:_BETA_LANE + dh + 1]
                gc = gc_ref[0, rows, _G_LANE + dh:_G_LANE + dh + 1]
                gl = gc_ref[0, c * CH + last:c * CH + last + 1, _G_LANE + dh:_G_LANE + dh + 1]
                egc = jnp.exp(gc)
                rhs = jnp.concatenate([v * beta, k * beta * egc], axis=-1).astype(BF16)
                sol = jnp.dot(t_ref[0, c, h], rhs, preferred_element_type=F32)
                s = s_refs[dh][...]
                probs.append(dict(sl=sl, rows=rows, o_ref=o_ref, s_ref=s_refs[dh], s=s, sb=s.astype(BF16),
                                  u=sol[:, :GDN_DV], w=sol[:, GDN_DV:].astype(BF16),
                                  qd=(q_ref[0, rows, sl] * egc).astype(BF16),
                                  a=a_ref[0, c, h].astype(BF16),
                                  kd=(k * jnp.exp(gl - gc)).astype(BF16), dec=jnp.exp(gl)))
        for p in probs:
            p["vn"] = (p["u"] - jnp.dot(p["w"], p["sb"], preferred_element_type=F32)).astype(BF16)
        for p in probs:
            p["o_ref"][0, p["rows"], p["sl"]] = (jnp.dot(p["qd"], p["sb"], preferred_element_type=F32)
                                                 + jnp.dot(p["a"], p["vn"], preferred_element_type=F32))
        for p in probs:
            p["s_ref"][...] = p["s"] * p["dec"] + lax.dot_general(
                p["kd"], p["vn"], (((0,), (0,)), ((), ())), preferred_element_type=F32)


def _gdn_out_kernel(of_ref, ob_ref, z_ref, n_ref, o_ref):
    z = z_ref[...].astype(F32)
    for h in range(GDN_HEADS):
        sl = slice(h * GDN_DV, (h + 1) * GDN_DV)
        o = of_ref[:, sl] + ob_ref[:, sl]
        y = o * lax.rsqrt(jnp.mean(o * o, axis=-1, keepdims=True) + NORM_EPS) * n_ref[...]
        zh = z[:, sl]
        o_ref[:, sl] = (y * (zh * jax.nn.sigmoid(zh))).astype(BF16)


def gdn_mixer(proj3, misc3, conv_w, a_log, dt_bias, out_norm, tl=512):
    B, L, _ = proj3.shape
    T = B * L
    N = L // CH
    H = GDN_HEADS
    tl = min(tl, L)
    qkv = dwconv3(proj3, 3, 1536, conv_w, jnp.zeros((1, 1536), F32), silu=True, out_dtype=F32)
    lane_vec = lambda p: jnp.zeros((1, LANES), F32).at[0, _G_LANE:_G_LANE + NDH].set(p.reshape(-1))
    tok = lambda w, c=0: pl.BlockSpec((1, tl, w), lambda b, i: (b, i, c))
    qn, kn, bg = pl.pallas_call(
        _gdn_prep_kernel,
        grid=(B, L // tl),
        in_specs=[tok(1536), tok(LANES, 3), _full((1, LANES)), _full((1, LANES))],
        out_specs=[tok(512), tok(512), tok(LANES)],
        out_shape=[jax.ShapeDtypeStruct((B, L, 512), F32), jax.ShapeDtypeStruct((B, L, 512), F32),
                   jax.ShapeDtypeStruct((B, L, LANES), F32)],
        compiler_params=_cparams("parallel", "parallel"),
        name="gdn_prep",
    )(qkv, misc3, lane_vec(a_log), lane_vec(dt_bias))

    CB = GDN_STEP_CHUNKS
    NB = N // CB
    chunk = lambda w, c=0: pl.BlockSpec((1, CB * CH, w), lambda b, n: (b, n, c))
    mats = pl.BlockSpec((1, CB, NDH, CH, CH), lambda b, n: (b, n, 0, 0, 0))
    mat_shape = jax.ShapeDtypeStruct((B, N, NDH, CH, CH), F32)
    a, attn, gc = pl.pallas_call(
        _gdn_chunk_kernel,
        grid=(B, NB),
        in_specs=[chunk(512), chunk(512), chunk(LANES)],
        out_specs=[mats, mats, chunk(LANES)],
        out_shape=[mat_shape, jax.ShapeDtypeStruct(mat_shape.shape, BF16),
                   jax.ShapeDtypeStruct((B, L, LANES), F32)],
        compiler_params=_cparams("parallel", "parallel"),
        name="gdn_chunk",
    )(qn, kn, bg)

    P = NDH * B * N
    at = a.transpose(3, 4, 2, 0, 1).reshape(CH, CH, P)
    tt = pl.pallas_call(
        functools.partial(_gdn_solve_kernel, nblk_fwd=P // LANES // 2),
        grid=(P // LANES,),
        in_specs=[pl.BlockSpec((CH, CH, LANES), lambda p: (0, 0, p))],
        out_specs=pl.BlockSpec((CH, CH, LANES), lambda p: (0, 0, p)),
        out_shape=jax.ShapeDtypeStruct((CH, CH, P), F32),
        compiler_params=_cparams("parallel"),
        name="gdn_solve",
    )(at)
    tmat = tt.reshape(CH, CH, NDH, B, N).transpose(3, 4, 2, 0, 1).astype(BF16)

    fwd = lambda w, c=0: pl.BlockSpec((1, CB * CH, w), lambda b, n: (b, n, c))
    bwd = lambda w, c=0: pl.BlockSpec((1, CB * CH, w), lambda b, n: (b, NB - 1 - n, c))
    mf = pl.BlockSpec((1, CB, H, CH, CH), lambda b, n: (b, n, 0, 0, 0))
    mb = pl.BlockSpec((1, CB, H, CH, CH), lambda b, n: (b, NB - 1 - n, 1, 0, 0))
    o_f, o_b = pl.pallas_call(
        _gdn_scan_kernel,
        grid=(B, NB),
        in_specs=[fwd(512), fwd(512), fwd(512, 2), fwd(LANES), fwd(LANES), mf, mf,
                  bwd(512), bwd(512), bwd(512, 2), bwd(LANES), bwd(LANES), mb, mb],
        out_specs=[fwd(512), bwd(512)],
        out_shape=[jax.ShapeDtypeStruct((B, L, 512), F32), jax.ShapeDtypeStruct((B, L, 512), F32)],
        scratch_shapes=[pltpu.VMEM((GDN_DK, GDN_DV), F32)] * NDH,
        compiler_params=_cparams("parallel", "arbitrary"),
        name="gdn_scan",
    )(qn, kn, qkv, bg, gc, tmat, attn, qn, kn, qkv, bg, gc, tmat, attn)

    tm = min(1024, T)
    row = lambda c=0: pl.BlockSpec((tm, 512), lambda i: (i, c))
    return pl.pallas_call(
        _gdn_out_kernel,
        grid=(T // tm,),
        in_specs=[row(), row(), row(12), _full((1, GDN_DV))],
        out_specs=row(),
        out_shape=jax.ShapeDtypeStruct((T, 512), BF16),
        compiler_params=_cparams("parallel"),
        name="gdn_out",
    )(o_f.reshape(T, 512), o_b.reshape(T, 512), proj3.reshape(T, MAIN_W), out_norm[None, :])


def _merge_kernel(x_ref, g_ref, oh_ref, om_ref, og_ref, wb_ref, wo_ref, n_ref, o_ref):
    merged = None
    for i, b_ref in enumerate((oh_ref, om_ref, og_ref)):
        gate = jax.nn.sigmoid(g_ref[:, i * D_MODEL:(i + 1) * D_MODEL].astype(F32))
        term = gate * jnp.dot(b_ref[...].astype(BF16), wb_ref[i], preferred_element_type=F32)
        merged = term if merged is None else merged + term
    y = jnp.dot(merged.astype(BF16), wo_ref[...], preferred_element_type=F32)
    r = lax.rsqrt(jnp.mean(y * y, axis=-1, keepdims=True) + NORM_EPS)
    o_ref[...] = x_ref[...] + y * r * n_ref[...]


def merge_out(x, proj, o_hy, o_mla, o_gdn, w_branch, w_out, norm_post, tm=512):
    T = x.shape[0]
    tm = min(tm, T)
    row = lambda w: pl.BlockSpec((tm, w), lambda i: (i, 0))
    return pl.pallas_call(
        _merge_kernel,
        grid=(T // tm,),
        in_specs=[row(D_MODEL), row(N_BRANCH * D_MODEL), row(BRANCH_W), row(BRANCH_W), row(BRANCH_W),
                  _full((N_BRANCH, BRANCH_W, D_MODEL)), _full((D_MODEL, D_MODEL)), _full((1, D_MODEL))],
        out_specs=row(D_MODEL),
        out_shape=jax.ShapeDtypeStruct((T, D_MODEL), F32),
        compiler_params=_cparams("parallel"),
        name="merge_out",
    )(x, proj, o_hy, o_mla, o_gdn, w_branch, w_out, norm_post)


def _ffn_kernel(x_ref, gpre_ref, wg_ref, wu_ref, wd_ref, gpost_ref, o_ref, h_ref, acc_ref):
    j = pl.program_id(1)

    @pl.when(j == 0)
    def _():
        x = x_ref[...]
        r = lax.rsqrt(jnp.mean(x * x, axis=-1, keepdims=True) + NORM_EPS)
        h_ref[...] = (x * r * gpre_ref[...]).astype(BF16)

    h = h_ref[...]
    gate = jnp.dot(h, wg_ref[...], preferred_element_type=F32)
    up = jnp.dot(h, wu_ref[...], preferred_element_type=F32)
    part = jnp.dot((gate * jax.nn.sigmoid(gate) * up).astype(BF16), wd_ref[...], preferred_element_type=F32)

    @pl.when(j == 0)
    def _():
        acc_ref[...] = part

    @pl.when(j > 0)
    def _():
        acc_ref[...] += part

    @pl.when(j == pl.num_programs(1) - 1)
    def _():
        f = acc_ref[...]
        r = lax.rsqrt(jnp.mean(f * f, axis=-1, keepdims=True) + NORM_EPS)
        o_ref[...] = x_ref[...] + f * r * gpost_ref[...]


def ffn(x, g_pre, w_gate, w_up, w_down, g_post, tm=512):
    T = x.shape[0]
    dff = w_gate.shape[1]
    tf = dff // 2
    tm = min(tm, T)
    return pl.pallas_call(
        _ffn_kernel,
        grid=(T // tm, dff // tf),
        in_specs=[pl.BlockSpec((tm, D_MODEL), lambda i, j: (i, 0)), _full((1, D_MODEL)),
                  pl.BlockSpec((D_MODEL, tf), lambda i, j: (0, j)),
                  pl.BlockSpec((D_MODEL, tf), lambda i, j: (0, j)),
                  pl.BlockSpec((tf, D_MODEL), lambda i, j: (j, 0)), _full((1, D_MODEL))],
        out_specs=pl.BlockSpec((tm, D_MODEL), lambda i, j: (i, 0)),
        out_shape=jax.ShapeDtypeStruct((T, D_MODEL), F32),
        scratch_shapes=[pltpu.VMEM((tm, D_MODEL), BF16), pltpu.VMEM((tm, D_MODEL), F32)],
        compiler_params=_cparams("parallel", "arbitrary"),
        name="ffn",
    )(x, g_pre, w_gate, w_up, w_down, g_post)


def _split_w_in(w_in):
    cols = lambda off, n: w_in[:, off:off + n]
    main = jnp.concatenate([cols(_OFF_GATE, N_BRANCH * D_MODEL), cols(_OFF_HY, 1536),
                            cols(_OFF_GQKV, 1536), cols(_OFF_GZ, 512)], axis=-1).astype(BF16)
    half = MLA_ROPE // 2
    kpe = _OFF_MKV + MLA_KV_LORA
    zeros = lambda n: jnp.zeros((D_MODEL, n), w_in.dtype)
    misc = jnp.concatenate([cols(_OFF_MQ, MLA_Q_LORA), cols(_OFF_MKV, MLA_KV_LORA),
                            cols(kpe, MLA_ROPE), cols(_OFF_GB, 8), cols(_OFF_GA, 8), zeros(48),
                            cols(kpe + half, half), cols(kpe, half), zeros(64)], axis=-1).astype(BF16)
    return main, misc


def trunk_layer(x, norm_mix_pre, norm_mix_post, norm_ffn_pre, norm_ffn_post, w_in,
                hy_conv_w, hy_conv_b, hy_ffn_w1, hy_ffn_b1, hy_sin_freq, hy_ffn_w2, hy_ffn_b2,
                hy_ffn_w3, hy_skip, mla_q_norm, mla_wq_b, mla_kv_norm, mla_wkv_b,
                gdn_conv_w, gdn_a_log, gdn_dt_bias, gdn_out_norm,
                w_branch, w_out, w_gate, w_up, w_down):
    B, L, D = x.shape
    T = B * L
    xt = x.reshape(T, D)
    w_main, w_misc = _split_w_in(w_in)
    g_pre = norm_mix_pre[None, :]
    proj = norm_mm(xt, g_pre, w_main, tm=2048, tn=512, out_dtype=BF16)
    misc = norm_mm(xt, g_pre, w_misc, tm=1024, tn=MISC_W, out_dtype=F32)
    proj3 = proj.reshape(B, L, MAIN_W)
    misc3 = misc.reshape(B, L, MISC_W)
    o_hy = hyena_mixer(proj3, hy_conv_w, hy_conv_b, hy_ffn_w1, hy_ffn_b1, hy_sin_freq,
                       hy_ffn_w2, hy_ffn_b2, hy_ffn_w3, hy_skip)
    o_mla = mla_mixer(misc3, mla_q_norm, mla_wq_b, mla_kv_norm, mla_wkv_b).reshape(T, BRANCH_W)
    o_gdn = gdn_mixer(proj3, misc3, gdn_conv_w, gdn_a_log, gdn_dt_bias, gdn_out_norm).reshape(T, BRANCH_W)
    xt = merge_out(xt, proj, o_hy, o_mla, o_gdn, w_branch.astype(BF16), w_out.astype(BF16),
                   norm_mix_post[None, :])
    xt = ffn(xt, norm_ffn_pre[None, :], w_gate.astype(BF16), w_up.astype(BF16), w_down.astype(BF16),
             norm_ffn_post[None, :])
    return xt.reshape(B, L, D)


def kernel(x_prompt, x_sample, norm_mix_pre, norm_mix_post, norm_ffn_pre, norm_ffn_post, w_in,
           hy_conv_w, hy_conv_b, hy_ffn_w1, hy_ffn_b1, hy_sin_freq, hy_ffn_w2, hy_ffn_b2,
           hy_ffn_w3, hy_skip, mla_q_norm, mla_wq_b, mla_kv_norm, mla_wkv_b,
           gdn_conv_w, gdn_a_log, gdn_dt_bias, gdn_out_norm,
           w_branch, w_out, w_gate, w_up, w_down):
    weights = (norm_mix_pre, norm_mix_post, norm_ffn_pre, norm_ffn_post, w_in,
               hy_conv_w, hy_conv_b, hy_ffn_w1, hy_ffn_b1, hy_sin_freq, hy_ffn_w2, hy_ffn_b2,
               hy_ffn_w3, hy_skip, mla_q_norm, mla_wq_b, mla_kv_norm, mla_wkv_b,
               gdn_conv_w, gdn_a_log, gdn_dt_bias, gdn_out_norm,
               w_branch, w_out, w_gate, w_up, w_down)

    def run_trunk(x):
        for layer in range(DEPTH):
            x = trunk_layer(x, *[w[layer] for w in weights])
        return x

    return (run_trunk(x_prompt), run_trunk(x_sample))
```

```python
import functools
import math

import jax
import jax.numpy as jnp
from jax import lax
from jax.experimental import pallas as pl
from jax.experimental.pallas import tpu as pltpu
import numpy as np

F32 = jnp.float32
BF16 = jnp.bfloat16

D_MODEL = 1024
DEPTH = 2
BRANCH_W = 512
N_BRANCH = 3
HY_W = BRANCH_W
HY_ORDER = 2
HY_EMB = 33
HY_FAST_PCT = 0.3
HY_SLOW_PCT = 1.5
HY_TARGET = 1e-2
MLA_HEADS = 4
MLA_NOPE = 128
MLA_ROPE = 64
MLA_V = 128
MLA_Q_LORA = 256
MLA_KV_LORA = 128
ROPE_THETA = 10000.0
GDN_HEADS = 4
GDN_DK = 128
GDN_DV = 128
GDN_CHUNK = 64
NORM_EPS = 1e-6

_OFF_HY = 0
_OFF_MQ = _OFF_HY + (HY_ORDER + 1) * HY_W
_OFF_MKV = _OFF_MQ + MLA_Q_LORA
_OFF_GQKV = _OFF_MKV + MLA_KV_LORA + MLA_ROPE
_OFF_GZ = _OFF_GQKV + GDN_HEADS * (2 * GDN_DK + GDN_DV)
_OFF_GB = _OFF_GZ + GDN_HEADS * GDN_DV
_OFF_GA = _OFF_GB + 2 * GDN_HEADS
_OFF_GATE = _OFF_GA + 2 * GDN_HEADS
_D_IN = _OFF_GATE + N_BRANCH * D_MODEL

MAIN_W = 3072 + 1536 + 1536 + 512
MISC_W = 640

LANES = 128
VMEM_LIMIT_BYTES = 56 * 1024 * 1024


def _cparams(*sem):
    return pltpu.CompilerParams(dimension_semantics=sem, vmem_limit_bytes=VMEM_LIMIT_BYTES)


def _full(shape):
    nd = len(shape)
    return pl.BlockSpec(shape, lambda *_: (0,) * nd)


def _norm_mm_kernel(x_ref, g_ref, w_ref, o_ref, h_ref):
    @pl.when(pl.program_id(1) == 0)
    def _():
        x = x_ref[...]
        r = lax.rsqrt(jnp.mean(x * x, axis=-1, keepdims=True) + NORM_EPS)
        h_ref[...] = (x * r * g_ref[...]).astype(BF16)

    o_ref[...] = jnp.dot(h_ref[...], w_ref[...], preferred_element_type=F32).astype(o_ref.dtype)


def norm_mm(x, g, w, tm, tn, out_dtype):
    T, D = x.shape
    N = w.shape[1]
    tm = min(tm, T)
    return pl.pallas_call(
        _norm_mm_kernel,
        grid=(T // tm, N // tn),
        in_specs=[pl.BlockSpec((tm, D), lambda i, j: (i, 0)),
                  pl.BlockSpec((1, D), lambda i, j: (0, 0)),
                  pl.BlockSpec((D, tn), lambda i, j: (0, j))],
        out_specs=pl.BlockSpec((tm, tn), lambda i, j: (i, j)),
        out_shape=jax.ShapeDtypeStruct((T, N), out_dtype),
        scratch_shapes=[pltpu.VMEM((tm, D), BF16)],
        compiler_params=_cparams("parallel", "arbitrary"),
        name="norm_mm",
    )(x, g, w)


def _dwconv_kernel(x_ref, p_ref, n_ref, w_ref, b_ref, o_ref, *, silu):
    i = pl.program_id(1)
    last = pl.num_programs(1) - 1
    x = x_ref[0].astype(F32)
    tl = x.shape[0]
    halo = p_ref.shape[1]
    prev_row = jnp.where(i > 0, p_ref[0, halo - 1:halo, :].astype(F32), 0.0)
    next_row = jnp.where(i < last, n_ref[0, 0:1, :].astype(F32), 0.0)
    rows = lax.broadcasted_iota(jnp.int32, x.shape, 0)
    x_dn = jnp.where(rows == 0, prev_row, pltpu.roll(x, 1, axis=0))
    x_up = jnp.where(rows == tl - 1, next_row, pltpu.roll(x, tl - 1, axis=0))
    out = x_dn * w_ref[0:1, :] + x * w_ref[1:2, :] + x_up * w_ref[2:3, :] + b_ref[...]
    if silu:
        out = out * jax.nn.sigmoid(out)
    o_ref[0] = out.astype(o_ref.dtype)


def dwconv3(x, col_blk, cw, w, b, silu, out_dtype, tl=512):
    B, L, _ = x.shape
    tl = min(tl, L)
    halo = 8 * (4 // x.dtype.itemsize)
    nsub = tl // halo
    lastblk = L // halo - 1
    return pl.pallas_call(
        functools.partial(_dwconv_kernel, silu=silu),
        grid=(B, L // tl),
        in_specs=[pl.BlockSpec((1, tl, cw), lambda b_, i: (b_, i, col_blk)),
                  pl.BlockSpec((1, halo, cw), lambda b_, i: (b_, jnp.maximum(i * nsub - 1, 0), col_blk)),
                  pl.BlockSpec((1, halo, cw), lambda b_, i: (b_, jnp.minimum((i + 1) * nsub, lastblk), col_blk)),
                  _full((3, cw)), _full((1, cw))],
        out_specs=pl.BlockSpec((1, tl, cw), lambda b_, i: (b_, i, 0)),
        out_shape=jax.ShapeDtypeStruct((B, L, cw), out_dtype),
        compiler_params=_cparams("parallel", "parallel"),
        name="dwconv3",
    )(x, x, x, w, b)


def _tile_lanes(t, c):
    return t if c == LANES else jnp.concatenate([t] * (c // LANES), axis=-1)


def _fft_s1_kernel(f_ref, x_ref, twr_ref, twi_ref, o_ref, *, nb, n1):
    c = x_ref.shape[-1]
    for t in range(nb):
        a = jnp.dot(f_ref[...], x_ref[0, t], preferred_element_type=F32)
        ar, ai = a[:n1], a[n1:]
        twr = _tile_lanes(twr_ref[t], c)
        twi = _tile_lanes(twi_ref[t], c)
        o_ref[0, t, :n1, :] = (ar * twr + ai * twi).astype(BF16)
        o_ref[0, t, n1:, :] = (ai * twr - ar * twi).astype(BF16)


def _fft_s2_kernel(f_ref, finv_ref, b_ref, kr_ref, ki_ref, o_ref, *, nb, n2):
    for t in range(nb):
        x = jnp.dot(f_ref[t], b_ref[0, t], preferred_element_type=F32)
        xr, xi = x[:n2], x[n2:]
        kr, ki = kr_ref[t].astype(F32), ki_ref[t].astype(F32)
        y = jnp.concatenate([xr * kr - xi * ki, xr * ki + xi * kr], axis=0).astype(BF16)
        o_ref[0, t] = jnp.dot(finv_ref[t], y, preferred_element_type=F32).astype(BF16)


def _fft_mm_kernel(g_ref, d_ref, o_ref, *, nb):
    for t in range(nb):
        o_ref[0, t] = jnp.dot(g_ref[...], d_ref[0, t], preferred_element_type=F32).astype(o_ref.dtype)


def _dft_tables(L):
    N = 2 * L
    n2 = 128 if N >= 4096 else 16
    n1 = N // n2
    n1h = n1 // 2
    k1 = np.arange(n1)[:, None]
    m1 = np.arange(n1h)[None, :]
    th = 2.0 * np.pi * ((k1 * m1) % n1) / n1
    c, s = np.cos(th), np.sin(th)
    f1 = np.block([[c, s], [-s, c]])
    g1 = np.block([[c.T, -s.T], [s.T, c.T]]) / N
    as_bf16 = lambda m: jnp.asarray(m, dtype=F32).astype(BF16)
    k = (jnp.arange(n1, dtype=jnp.int32)[:, None, None]
         + n1 * jnp.arange(n2, dtype=jnp.int32)[None, :, None])
    ang = ((k * jnp.arange(n2, dtype=jnp.int32)[None, None, :]) % N).astype(F32) * (2.0 * math.pi / N)
    c2, s2 = jnp.cos(ang), jnp.sin(ang)
    f2 = jnp.concatenate([jnp.concatenate([c2, s2], axis=2),
                          jnp.concatenate([-s2, c2], axis=2)], axis=1).astype(BF16)
    return n1, n2, as_bf16(f1), as_bf16(g1), f2, f2.transpose(0, 2, 1)


def _twiddles(n1, n2):
    N = n1 * n2
    prod = (jnp.arange(n2, dtype=jnp.int32)[:, None] * jnp.arange(n1, dtype=jnp.int32)[None, :]) % N
    ang = prod.astype(F32) * (2.0 * math.pi / N)
    shape = (n2, n1, LANES)
    return (jnp.broadcast_to(jnp.cos(ang)[:, :, None], shape),
            jnp.broadcast_to(jnp.sin(ang)[:, :, None], shape))


def fft_long_conv(z, kspec, order, tabs):
    B, L, C = z.shape
    n1, n2, f1, g1, f2, f2inv = tabs
    n1h = n1 // 2
    P = B // 2
    nb1 = min(8, n2)
    nb2 = min(8, n1)
    zt = z.astype(BF16).reshape(P, 2, n1h, n2, C).transpose(0, 3, 1, 2, 4).reshape(P, n2, 2 * n1h, C)
    a = pl.pallas_call(
        functools.partial(_fft_mm_kernel, nb=nb1),
        grid=(n2 // nb1, P),
        in_specs=[_full((2 * n1, 2 * n1h)),
                  pl.BlockSpec((1, nb1, 2 * n1h, C), lambda j, p: (p, j, 0, 0))],
        out_specs=pl.BlockSpec((1, nb1, 2 * n1, C), lambda j, p: (p, j, 0, 0)),
        out_shape=jax.ShapeDtypeStruct((P, n2, 2 * n1, C), BF16),
        compiler_params=_cparams("parallel", "parallel"),
        name="fft_s1",
    )(f1, zt)
    bt = a.reshape(P, n2, 2, n1, C).transpose(0, 3, 2, 1, 4).reshape(P, n1, 2 * n2, C)
    d = pl.pallas_call(
        functools.partial(_fft_s2_kernel, nb=nb2, n2=n2),
        grid=(n1 // nb2, P),
        in_specs=[pl.BlockSpec((nb2, 2 * n2, 2 * n2), lambda j, p: (j, 0, 0)),
                  pl.BlockSpec((nb2, 2 * n2, 2 * n2), lambda j, p: (j, 0, 0)),
                  pl.BlockSpec((1, nb2, 2 * n2, C), lambda j, p: (p, j, 0, 0)),
                  pl.BlockSpec((nb2, n2, C), lambda j, p: (j, 0, order)),
                  pl.BlockSpec((nb2, n2, C), lambda j, p: (j, 1, order))],
        out_specs=pl.BlockSpec((1, nb2, 2 * n2, C), lambda j, p: (p, j, 0, 0)),
        out_shape=jax.ShapeDtypeStruct((P, n1, 2 * n2, C), BF16),
        compiler_params=_cparams("parallel", "parallel"),
        name="fft_s2",
    )(f2, f2inv, bt, kspec, kspec)
    dt = d.reshape(P, n1, 2, n2, C).transpose(0, 3, 2, 1, 4).reshape(P, n2, 2 * n1, C)
    y = pl.pallas_call(
        functools.partial(_fft_mm_kernel, nb=nb1),
        grid=(n2 // nb1, P),
        in_specs=[_full((2 * n1h, 2 * n1)),
                  pl.BlockSpec((1, nb1, 2 * n1, C), lambda j, p: (p, j, 0, 0))],
        out_specs=pl.BlockSpec((1, nb1, 2 * n1h, C), lambda j, p: (p, j, 0, 0)),
        out_shape=jax.ShapeDtypeStruct((P, n2, 2 * n1h, C), BF16),
        compiler_params=_cparams("parallel", "parallel"),
        name="fft_s3",
    )(g1, dt)
    return y.reshape(P, n2, 2, n1h, C).transpose(0, 2, 3, 1, 4).reshape(B, L, C)


def _hy_gate_kernel(g_ref, c_ref, z_ref, s_ref, o_ref):
    o_ref[...] = (g_ref[...].astype(F32) * (c_ref[...].astype(F32) + s_ref[...] * z_ref[...].astype(F32))
                  ).astype(o_ref.dtype)


def hy_gate(u, gate_blk, conv, zsrc, z_blk, skip, out_dtype, tl=1024):
    T, C = conv.shape
    tl = min(tl, T)
    return pl.pallas_call(
        _hy_gate_kernel,
        grid=(T // tl,),
        in_specs=[pl.BlockSpec((tl, C), lambda i: (i, gate_blk)),
                  pl.BlockSpec((tl, C), lambda i: (i, 0)),
                  pl.BlockSpec((tl, C), lambda i: (i, z_blk)),
                  _full((1, C))],
        out_specs=pl.BlockSpec((tl, C), lambda i: (i, 0)),
        out_shape=jax.ShapeDtypeStruct((T, C), out_dtype),
        compiler_params=_cparams("parallel"),
        name="hy_gate",
    )(u, conv, zsrc, skip)


def hyena_filter_spectrum(L, n1, n2, tw1r, tw1i, w1, b1, freq, w2, b2, w3):
    t = jnp.linspace(0.0, 1.0, L, dtype=F32)[:, None]
    bands = (HY_EMB - 1) // 2
    wpos = (2.0 * math.pi / L) * jnp.arange(L, dtype=F32)[:, None]
    fr = jnp.linspace(1e-4, bands - 1, bands, dtype=F32)[None, :]
    feats = jnp.concatenate([t, jnp.cos(fr * wpos), -jnp.sin(fr * wpos)], axis=-1)
    deltas = jnp.abs(jnp.linspace(math.log(HY_TARGET) / HY_SLOW_PCT,
                                  math.log(HY_TARGET) / HY_FAST_PCT, HY_W, dtype=F32))
    feats2 = jnp.concatenate([feats, feats[0:1], feats[:0:-1]], axis=0)
    feats2 = jnp.pad(feats2, ((0, 0), (0, LANES - HY_EMB)))
    w1p = jnp.pad(w1, ((0, LANES - HY_EMB), (0, 0)))
    hid = w3.shape[0]
    oc = HY_ORDER * HY_W
    w3d = w3.reshape(hid, HY_ORDER, 2, HY_W).transpose(2, 0, 1, 3).reshape(2, hid, oc)
    tl = min(512, L)
    nh = L // tl
    kern, asum = pl.pallas_call(
        functools.partial(_filter_taps_kernel, zero_tile=nh),
        grid=(2 * nh,),
        in_specs=[pl.BlockSpec((tl, LANES), lambda i: (i, 0)), _full((LANES, hid)), _full((1, hid)),
                  _full((1, hid)), _full((hid, hid)), _full((1, hid)),
                  pl.BlockSpec((1, hid, oc), lambda i: (i // nh, 0, 0)), _full((1, oc))],
        out_specs=[pl.BlockSpec((tl, oc), lambda i: (i, 0)), _full((1, oc))],
        out_shape=[jax.ShapeDtypeStruct((2 * L, oc), F32), jax.ShapeDtypeStruct((1, oc), F32)],
        compiler_params=_cparams("arbitrary"),
        name="filter_taps",
    )(feats2, w1p, b1, freq, w2, b2, w3d, jnp.tile(deltas, HY_ORDER)[None, :])
    return filter_dft(kern, asum, n1, n2, tw1r, tw1i)


def _filter_taps_kernel(f_ref, w1_ref, b1_ref, fq_ref, w2_ref, b2_ref, w3_ref, dl_ref, k_ref, s_ref, *,
                        zero_tile):
    i = pl.program_id(0)
    hp = lax.Precision.HIGHEST
    f = f_ref[...]
    h = jnp.sin(fq_ref[...] * (jnp.dot(f, w1_ref[...], precision=hp, preferred_element_type=F32)
                               + b1_ref[...]))
    h = jnp.sin(fq_ref[...] * (jnp.dot(h, w2_ref[...], precision=hp, preferred_element_type=F32)
                               + b2_ref[...]))
    taps = jnp.dot(h, w3_ref[0], precision=hp, preferred_element_type=F32)
    taps = taps * jnp.exp(-f[:, 0:1] * dl_ref[...])
    row = lax.broadcasted_iota(jnp.int32, taps.shape, 0)
    taps = jnp.where((i == zero_tile) & (row == 0), 0.0, taps)
    k_ref[...] = taps

    @pl.when(i == 0)
    def _():
        s_ref[...] = jnp.zeros(s_ref.shape, F32)

    s_ref[...] += jnp.sum(jnp.abs(taps), axis=0, keepdims=True)


def _spec_s2_kernel(f_ref, b_ref, s_ref, o_ref, *, nb):
    for t in range(nb):
        o_ref[t] = (jnp.dot(f_ref[...], b_ref[0, t], preferred_element_type=F32) / s_ref[...]).astype(BF16)


def filter_dft(kern, asum, n1, n2, tw1r, tw1i):
    N, C = kern.shape
    ct = 512
    k1 = np.arange(n1)
    th = 2.0 * np.pi * ((k1[:, None] * k1[None, :]) % n1) / n1
    f1 = np.concatenate([np.cos(th), -np.sin(th)], axis=0)
    a2 = np.arange(n2)
    ph = 2.0 * np.pi * ((a2[:, None] * a2[None, :]) % n2) / n2
    f2 = np.block([[np.cos(ph), np.sin(ph)], [-np.sin(ph), np.cos(ph)]])

    as_bf16 = lambda m: jnp.asarray(m, dtype=F32).astype(BF16)
    nb1 = min(8, n2)
    nb2 = min(8, n1)
    xt = kern.astype(BF16).reshape(n1, n2, C).transpose(1, 0, 2)[None]
    a = pl.pallas_call(
        functools.partial(_fft_s1_kernel, nb=nb1, n1=n1),
        grid=(n2 // nb1, C // ct),
        in_specs=[_full((2 * n1, n1)),
                  pl.BlockSpec((1, nb1, n1, ct), lambda j, c: (0, j, 0, c)),
                  pl.BlockSpec((nb1, n1, LANES), lambda j, c: (j, 0, 0)),
                  pl.BlockSpec((nb1, n1, LANES), lambda j, c: (j, 0, 0))],
        out_specs=pl.BlockSpec((1, nb1, 2 * n1, ct), lambda j, c: (0, j, 0, c)),
        out_shape=jax.ShapeDtypeStruct((1, n2, 2 * n1, C), BF16),
        compiler_params=_cparams("parallel", "parallel"),
        name="spec_s1",
    )(as_bf16(f1), xt, tw1r, tw1i)
    bt = a.reshape(n2, 2, n1, C).transpose(2, 1, 0, 3).reshape(1, n1, 2 * n2, C)
    return pl.pallas_call(
        functools.partial(_spec_s2_kernel, nb=nb2),
        grid=(n1 // nb2, C // ct),
        in_specs=[_full((2 * n2, 2 * n2)),
                  pl.BlockSpec((1, nb2, 2 * n2, ct), lambda j, c: (0, j, 0, c)),
                  pl.BlockSpec((1, ct), lambda j, c: (0, c))],
        out_specs=pl.BlockSpec((nb2, 2 * n2, ct), lambda j, c: (j, 0, c)),
        out_shape=jax.ShapeDtypeStruct((n1, 2 * n2, C), BF16),
        compiler_params=_cparams("parallel", "parallel"),
        name="spec_s2",
    )(as_bf16(f2), bt, asum)


def hyena_mixer(proj3, conv_w, conv_b, w1, b1, freq, w2, b2, w3, skip):
    B, L, _ = proj3.shape
    T = B * L
    u = dwconv3(proj3, 2, 1536, conv_w, conv_b[None, :], silu=False, out_dtype=BF16).reshape(T, 1536)
    n1, n2, f1, g1, f2, f2inv = _dft_tables(L)
    tw1r, tw1i = _twiddles(n1, n2)
    tabs = (n1, n2, f1, g1, f2, f2inv)
    kspec = hyena_filter_spectrum(L, n1, n2, tw1r, tw1i, w1, b1[None, :], freq[None, :], w2, b2[None, :], w3)
    conv = fft_long_conv(u[:, 1024:1536].reshape(B, L, HY_W), kspec, 0, tabs).reshape(T, HY_W)
    z1 = hy_gate(u, 0, conv, u, 2, skip[0:1], BF16)
    conv = fft_long_conv(z1.reshape(B, L, HY_W), kspec, 1, tabs).reshape(T, HY_W)
    return hy_gate(u, 1, conv, z1, 0, skip[1:2], BF16)


def _mla_proj_kernel(m_ref, qg_ref, kg_ref, wq_ref, wkv_ref, cq_ref, sq_ref, ck_ref, sk_ref,
                     q_ref, k_ref, v_ref):
    m = m_ref[0]
    ql = m[:, 0:256]
    r = lax.rsqrt(jnp.mean(ql * ql, axis=-1, keepdims=True) + NORM_EPS)
    q = jnp.dot((ql * r * qg_ref[...]).astype(BF16), wq_ref[...], preferred_element_type=F32)
    ckv = m[:, 256:384]
    r = lax.rsqrt(jnp.mean(ckv * ckv, axis=-1, keepdims=True) + NORM_EPS)
    kv = jnp.dot((ckv * r * kg_ref[...]).astype(BF16), wkv_ref[...], preferred_element_type=F32)
    q_pe = ((q[:, 512:1024] * cq_ref[...] + q[:, 1024:1536] * sq_ref[...]) * ATTN_Q_SCALE).astype(BF16)
    k_pe = (m[:, 384:512] * ck_ref[...] + m[:, 512:640] * sk_ref[...]).astype(BF16)
    for h in range(MLA_HEADS):
        sl = slice(h * LANES, (h + 1) * LANES)
        q_ref[0, :, 2 * h * LANES:(2 * h + 1) * LANES] = (q[:, sl] * ATTN_Q_SCALE).astype(BF16)
        q_ref[0, :, (2 * h + 1) * LANES:(2 * h + 2) * LANES] = q_pe[:, sl]
        k_ref[0, :, 2 * h * LANES:(2 * h + 1) * LANES] = kv[:, sl].astype(BF16)
        k_ref[0, :, (2 * h + 1) * LANES:(2 * h + 2) * LANES] = k_pe
        v_ref[0, :, 2 * h * LANES:(2 * h + 1) * LANES] = kv[:, 512 + h * MLA_V:512 + (h + 1) * MLA_V].astype(BF16)
        v_ref[0, :, (2 * h + 1) * LANES:(2 * h + 2) * LANES] = jnp.ones((m.shape[0], LANES), BF16)


ATTN_Q_SCALE = (MLA_NOPE + MLA_ROPE) ** -0.5 * math.log2(math.e)
ATTN_ROW_PARTS = 2


def _attn_kernel(q_ref, k_ref, v_ref, o_ref, *scratch):
    ki = pl.program_id(2)
    dqk = 2 * LANES
    m_refs, acc_refs = scratch[:MLA_HEADS], scratch[MLA_HEADS:]

    @pl.when(ki == 0)
    def _():
        for m_ref, acc_ref in zip(m_refs, acc_refs):
            m_ref[...] = jnp.full(m_ref.shape, -jnp.inf, F32)
            acc_ref[...] = jnp.zeros(acc_ref.shape, F32)

    part = q_ref.shape[1] // ATTN_ROW_PARTS
    chains = [(h, slice(h * dqk, (h + 1) * dqk), slice(r * part, (r + 1) * part))
              for h in range(MLA_HEADS) for r in range(ATTN_ROW_PARTS)]
    scores = [lax.dot_general(q_ref[0, rows, sl], k_ref[0, :, sl], (((1,), (1,)), ((), ())),
                              preferred_element_type=F32) for _, sl, rows in chains]
    weights = []
    for (h, sl, rows), s in zip(chains, scores):
        m_prev = m_refs[h][rows, :]
        m_new = jnp.maximum(m_prev, jnp.max(s, axis=-1, keepdims=True))
        m_refs[h][rows, :] = m_new
        weights.append((jnp.exp2(m_prev - m_new), jnp.exp2(s - m_new).astype(BF16)))
    for (h, sl, rows), (alpha, p) in zip(chains, weights):
        acc_refs[h][rows, :] = (alpha * acc_refs[h][rows, :]
                                + jnp.dot(p, v_ref[0, :, sl], preferred_element_type=F32))

    @pl.when(ki == pl.num_programs(2) - 1)
    def _():
        for h, acc_ref in enumerate(acc_refs):
            o_ref[0, :, h * MLA_V:(h + 1) * MLA_V] = (acc_ref[:, :MLA_V] / acc_ref[:, MLA_V:]).astype(BF16)


def _rope_lane_tables(L):
    half = MLA_ROPE // 2
    inv = ROPE_THETA ** (-jnp.arange(half, dtype=F32) / half)
    ang = jnp.arange(L, dtype=F32)[:, None] * inv[None, :]
    cos, sin = jnp.cos(ang), jnp.sin(ang)
    zeros = jnp.zeros((L, LANES - MLA_ROPE), F32)
    ck = jnp.concatenate([cos, cos, zeros], axis=-1)
    sk = jnp.concatenate([-sin, sin, zeros], axis=-1)
    return jnp.tile(ck, (1, MLA_HEADS)), jnp.tile(sk, (1, MLA_HEADS)), ck, sk


def mla_mixer(misc3, q_norm, wq_b, kv_norm, wkv_b, tl=512, tq=1024, tk=1024):
    B, L, _ = misc3.shape
    H = MLA_HEADS
    tl, tq, tk = min(tl, L), min(tq, L), min(tk, L)
    half = MLA_ROPE // 2
    wq = wq_b.reshape(MLA_Q_LORA, H, MLA_NOPE + MLA_ROPE)
    x1 = wq[:, :, MLA_NOPE:MLA_NOPE + half]
    x2 = wq[:, :, MLA_NOPE + half:]
    zpad = jnp.zeros((MLA_Q_LORA, H, LANES - MLA_ROPE), wq.dtype)
    wq = jnp.concatenate([wq[:, :, :MLA_NOPE].reshape(MLA_Q_LORA, -1),
                          jnp.concatenate([x1, x2, zpad], axis=-1).reshape(MLA_Q_LORA, -1),
                          jnp.concatenate([x2, x1, zpad], axis=-1).reshape(MLA_Q_LORA, -1)],
                         axis=-1).astype(BF16)
    wkv = wkv_b.reshape(MLA_KV_LORA, H, MLA_NOPE + MLA_V)
    wkv = jnp.concatenate([wkv[:, :, :MLA_NOPE].reshape(MLA_KV_LORA, -1),
                           wkv[:, :, MLA_NOPE:].reshape(MLA_KV_LORA, -1)], axis=-1).astype(BF16)
    cq, sq, ck, sk = _rope_lane_tables(L)
    tok = lambda w: pl.BlockSpec((1, tl, w), lambda b, i: (b, i, 0))
    pos = lambda w: pl.BlockSpec((tl, w), lambda b, i: (i, 0))
    sds = lambda w: jax.ShapeDtypeStruct((B, L, w), BF16)
    dqk = 2 * LANES
    nk = L // tk
    q, k, v = pl.pallas_call(
        _mla_proj_kernel,
        grid=(B, L // tl),
        in_specs=[tok(MISC_W), _full((1, MLA_Q_LORA)), _full((1, MLA_KV_LORA)),
                  _full(wq.shape), _full(wkv.shape), pos(H * LANES), pos(H * LANES), pos(LANES), pos(LANES)],
        out_specs=[tok(H * dqk), tok(H * dqk), tok(H * dqk)],
        out_shape=[sds(H * dqk), sds(H * dqk), sds(H * dqk)],
        compiler_params=_cparams("parallel", "parallel"),
        name="mla_proj",
    )(misc3, q_norm[None, :], kv_norm[None, :], wq, wkv, cq, sq, ck, sk)
    return pl.pallas_call(
        _attn_kernel,
        grid=(B, L // tq, nk),
        in_specs=[pl.BlockSpec((1, tq, H * dqk), lambda b, i, j: (b, i, 0)),
                  pl.BlockSpec((1, tk, H * dqk), lambda b, i, j: (b, j, 0)),
                  pl.BlockSpec((1, tk, H * dqk), lambda b, i, j: (b, j, 0))],
        out_specs=pl.BlockSpec((1, tq, H * MLA_V), lambda b, i, j: (b, i, 0)),
        out_shape=jax.ShapeDtypeStruct((B, L, H * MLA_V), BF16),
        scratch_shapes=([pltpu.VMEM((tq, 1), F32)] * H + [pltpu.VMEM((tq, 2 * MLA_V), F32)] * H),
        compiler_params=_cparams("parallel", "parallel", "arbitrary"),
        name="mla_attn",
    )(q, k, v)


_BETA_LANE = 64
_G_LANE = 72
CH = GDN_CHUNK
NDH = 2 * GDN_HEADS
GDN_STEP_CHUNKS = 8


def _gdn_prep_kernel(c_ref, m_ref, alog_ref, dtb_ref, qn_ref, kn_ref, bg_ref):
    for h in range(GDN_HEADS):
        sl = slice(h * GDN_DK, (h + 1) * GDN_DK)
        qh = c_ref[0, :, sl]
        qn_ref[0, :, sl] = (qh * lax.rsqrt(jnp.sum(qh * qh, axis=-1, keepdims=True) + NORM_EPS)
                            * (GDN_DK ** -0.5))
        kh = c_ref[0, :, GDN_HEADS * GDN_DK + h * GDN_DK:GDN_HEADS * GDN_DK + (h + 1) * GDN_DK]
        kn_ref[0, :, sl] = kh * lax.rsqrt(jnp.sum(kh * kh, axis=-1, keepdims=True) + NORM_EPS)
    m = m_ref[0]
    lane = lax.broadcasted_iota(jnp.int32, m.shape, 1)
    beta = jax.nn.sigmoid(m)
    x = m + dtb_ref[...]
    softplus = jnp.maximum(x, 0.0) + jnp.log1p(jnp.exp(-jnp.abs(x)))
    g = -jnp.exp(alog_ref[...]) * softplus
    is_beta = (lane >= _BETA_LANE) & (lane < _BETA_LANE + NDH)
    is_g = (lane >= _G_LANE) & (lane < _G_LANE + NDH)
    bg_ref[0] = jnp.where(is_beta, beta, jnp.where(is_g, g, 0.0))


def _split3(x):
    hi = x.astype(BF16)
    r = x - hi.astype(F32)
    mid = r.astype(BF16)
    lo = (r - mid.astype(F32)).astype(BF16)
    return hi, mid, lo


def _dot_nt(a, b):
    return lax.dot_general(a, b, (((1,), (1,)), ((), ())), preferred_element_type=F32)


def _gdn_chunk_kernel(q_ref, k_ref, bg_ref, a_ref, attn_ref, gc_ref):
    lane = lax.broadcasted_iota(jnp.int32, (CH, LANES), 1)
    is_g = (lane >= _G_LANE) & (lane < _G_LANE + NDH)
    ri = lax.broadcasted_iota(jnp.int32, (CH, CH), 0)
    ci = lax.broadcasted_iota(jnp.int32, (CH, CH), 1)
    lower = ri >= ci
    upper = ri <= ci
    tril = lower.astype(BF16)
    triu = upper.astype(BF16)
    for c in range(GDN_STEP_CHUNKS):
        rows = slice(c * CH, (c + 1) * CH)
        bg = bg_ref[0, rows, :]
        pieces = _split3(jnp.where(is_g, bg, 0.0))
        pre = sum(jnp.dot(tril, p, preferred_element_type=F32) for p in pieces)
        suf = sum(jnp.dot(triu, p, preferred_element_type=F32) for p in pieces)
        gc = jnp.where(lane >= _G_LANE + GDN_HEADS, suf, pre)
        gc_ref[0, rows, :] = gc
        gct = gc.T
        for d in range(2):
            causal = lower if d == 0 else upper
            strict = (ri > ci) if d == 0 else (ri < ci)
            for h in range(GDN_HEADS):
                dh = d * GDN_HEADS + h
                sl = slice(h * GDN_DK, (h + 1) * GDN_DK)
                kh = k_ref[0, rows, sl]
                kb = (kh * bg[:, _BETA_LANE + dh:_BETA_LANE + dh + 1]).astype(BF16)
                khb = kh.astype(BF16)
                diff = gc[:, _G_LANE + dh:_G_LANE + dh + 1] - gct[_G_LANE + dh:_G_LANE + dh + 1, :]
                dec = jnp.exp(jnp.where(causal, diff, -jnp.inf))
                a_ref[0, c, dh] = jnp.where(strict, _dot_nt(kb, khb) * dec, 0.0)
                attn_ref[0, c, dh] = (_dot_nt(q_ref[0, rows, sl].astype(BF16), khb) * dec).astype(BF16)


def _gdn_solve_kernel(a_ref, t_ref, *, nblk_fwd):
    bwd = pl.program_id(0) >= nblk_fwd
    t_ref[...] = jnp.zeros(t_ref.shape, F32)

    @pl.when(jnp.logical_not(bwd))
    def _():
        _solve_triangular(a_ref, t_ref, reverse=False)

    @pl.when(bwd)
    def _():
        _solve_triangular(a_ref, t_ref, reverse=True)


def _solve_triangular(a_ref, t_ref, *, reverse):
    nblk = CH // 8
    sub8 = lax.broadcasted_iota(jnp.int32, (8, LANES), 0)
    for phase in range(nblk):
        rb = nblk - 1 - phase if reverse else phase
        groups = range(rb, nblk) if reverse else range(0, rb + 1)

        def row(it, carry, rb=rb, groups=groups):
            i = 8 * rb + (7 - it if reverse else it)
            acc = {cg: ((sub8 + 8 * cg == i).astype(F32) if cg == rb else jnp.zeros((8, LANES), F32))
                   for cg in groups}
            for jb in groups:
                ablk = a_ref[i, 8 * jb:8 * jb + 8, :]
                for jj in range(8):
                    arow = ablk[jj:jj + 1, :]
                    for cg in (range(jb, nblk) if reverse else range(0, jb + 1)):
                        acc[cg] = acc[cg] - arow * t_ref[8 * jb + jj, 8 * cg:8 * cg + 8, :]
            for cg in groups:
                t_ref[i, 8 * cg:8 * cg + 8, :] = acc[cg]
            return carry

        lax.fori_loop(0, 8, row, 0)


def _gdn_scan_kernel(qf_ref, kf_ref, vf_ref, bgf_ref, gcf_ref, tf_ref, af_ref,
                     qb_ref, kb_ref, vb_ref, bgb_ref, gcb_ref, tb_ref, ab_ref,
                     of_ref, ob_ref, *s_refs):
    @pl.when(pl.program_id(1) == 0)
    def _():
        for s_ref in s_refs:
            s_ref[...] = jnp.zeros(s_ref.shape, F32)

    dirs = ((qf_ref, kf_ref, vf_ref, bgf_ref, gcf_ref, tf_ref, af_ref, of_ref, CH - 1),
            (qb_ref, kb_ref, vb_ref, bgb_ref, gcb_ref, tb_ref, ab_ref, ob_ref, 0))
    for step in range(GDN_STEP_CHUNKS):
        probs = []
        for d, (q_ref, k_ref, v_ref, bg_ref, gc_ref, t_ref, a_ref, o_ref, last) in enumerate(dirs):
            c = step if d == 0 else GDN_STEP_CHUNKS - 1 - step
            rows = slice(c * CH, (c + 1) * CH)
            for h in range(GDN_HEADS):
                dh = d * GDN_HEADS + h
                sl = slice(h * GDN_DK, (h + 1) * GDN_DK)
                k, v = k_ref[0, rows, sl], v_ref[0, rows, sl]
                beta = bg_ref[0, rows, _BETA_LANE + dh:_BETA_LANE + dh + 1]
                gc = gc_ref[0, rows, _G_LANE + dh:_G_LANE + dh + 1]
                gl = gc_ref[0, c * CH + last:c * CH + last + 1, _G_LANE + dh:_G_LANE + dh + 1]
                egc = jnp.exp(gc)
                rhs = jnp.concatenate([v * beta, k * beta * egc], axis=-1).astype(BF16)
                sol = jnp.dot(t_ref[0, c, h], rhs, preferred_element_type=F32)
                s = s_refs[dh][...]
                probs.append(dict(sl=sl, rows=rows, o_ref=o_ref, s_ref=s_refs[dh], s=s, sb=s.astype(BF16),
                                  u=sol[:, :GDN_DV], w=sol[:, GDN_DV:].astype(BF16),
                                  qd=(q_ref[0, rows, sl] * egc).astype(BF16),
                                  a=a_ref[0, c, h].astype(BF16),
                                  kd=(k * jnp.exp(gl - gc)).astype(BF16), dec=jnp.exp(gl)))
        for p in probs:
            p["vn"] = (p["u"] - jnp.dot(p["w"], p["sb"], preferred_element_type=F32)).astype(BF16)
        for p in probs:
            p["o_ref"][0, p["rows"], p["sl"]] = (jnp.dot(p["qd"], p["sb"], preferred_element_type=F32)
                                                 + jnp.dot(p["a"], p["vn"], preferred_element_type=F32))
        for p in probs:
            p["s_ref"][...] = p["s"] * p["dec"] + lax.dot_general(
                p["kd"], p["vn"], (((0,), (0,)), ((), ())), preferred_element_type=F32)


def _gdn_out_kernel(of_ref, ob_ref, z_ref, n_ref, o_ref):
    z = z_ref[...].astype(F32)
    for h in range(GDN_HEADS):
        sl = slice(h * GDN_DV, (h + 1) * GDN_DV)
        o = of_ref[:, sl] + ob_ref[:, sl]
        y = o * lax.rsqrt(jnp.mean(o * o, axis=-1, keepdims=True) + NORM_EPS) * n_ref[...]
        zh = z[:, sl]
        o_ref[:, sl] = (y * (zh * jax.nn.sigmoid(zh))).astype(BF16)


def gdn_mixer(proj3, misc3, conv_w, a_log, dt_bias, out_norm, tl=512):
    B, L, _ = proj3.shape
    T = B * L
    N = L // CH
    H = GDN_HEADS
    tl = min(tl, L)
    qkv = dwconv3(proj3, 3, 1536, conv_w, jnp.zeros((1, 1536), F32), silu=True, out_dtype=F32)
    lane_vec = lambda p: jnp.zeros((1, LANES), F32).at[0, _G_LANE:_G_LANE + NDH].set(p.reshape(-1))
    tok = lambda w, c=0: pl.BlockSpec((1, tl, w), lambda b, i: (b, i, c))
    qn, kn, bg = pl.pallas_call(
        _gdn_prep_kernel,
        grid=(B, L // tl),
        in_specs=[tok(1536), tok(LANES, 3), _full((1, LANES)), _full((1, LANES))],
        out_specs=[tok(512), tok(512), tok(LANES)],
        out_shape=[jax.ShapeDtypeStruct((B, L, 512), F32), jax.ShapeDtypeStruct((B, L, 512), F32),
                   jax.ShapeDtypeStruct((B, L, LANES), F32)],
        compiler_params=_cparams("parallel", "parallel"),
        name="gdn_prep",
    )(qkv, misc3, lane_vec(a_log), lane_vec(dt_bias))

    CB = GDN_STEP_CHUNKS
    NB = N // CB
    chunk = lambda w, c=0: pl.BlockSpec((1, CB * CH, w), lambda b, n: (b, n, c))
    mats = pl.BlockSpec((1, CB, NDH, CH, CH), lambda b, n: (b, n, 0, 0, 0))
    mat_shape = jax.ShapeDtypeStruct((B, N, NDH, CH, CH), F32)
    a, attn, gc = pl.pallas_call(
        _gdn_chunk_kernel,
        grid=(B, NB),
        in_specs=[chunk(512), chunk(512), chunk(LANES)],
        out_specs=[mats, mats, chunk(LANES)],
        out_shape=[mat_shape, jax.ShapeDtypeStruct(mat_shape.shape, BF16),
                   jax.ShapeDtypeStruct((B, L, LANES), F32)],
        compiler_params=_cparams("parallel", "parallel"),
        name="gdn_chunk",
    )(qn, kn, bg)

    P = NDH * B * N
    at = a.transpose(3, 4, 2, 0, 1).reshape(CH, CH, P)
    tt = pl.pallas_call(
        functools.partial(_gdn_solve_kernel, nblk_fwd=P // LANES // 2),
        grid=(P // LANES,),
        in_specs=[pl.BlockSpec((CH, CH, LANES), lambda p: (0, 0, p))],
        out_specs=pl.BlockSpec((CH, CH, LANES), lambda p: (0, 0, p)),
        out_shape=jax.ShapeDtypeStruct((CH, CH, P), F32),
        compiler_params=_cparams("parallel"),
        name="gdn_solve",
    )(at)
    tmat = tt.reshape(CH, CH, NDH, B, N).transpose(3, 4, 2, 0, 1).astype(BF16)

    fwd = lambda w, c=0: pl.BlockSpec((1, CB * CH, w), lambda b, n: (b, n, c))
    bwd = lambda w, c=0: pl.BlockSpec((1, CB * CH, w), lambda b, n: (b, NB - 1 - n, c))
    mf = pl.BlockSpec((1, CB, H, CH, CH), lambda b, n: (b, n, 0, 0, 0))
    mb = pl.BlockSpec((1, CB, H, CH, CH), lambda b, n: (b, NB - 1 - n, 1, 0, 0))
    o_f, o_b = pl.pallas_call(
        _gdn_scan_kernel,
        grid=(B, NB),
        in_specs=[fwd(512), fwd(512), fwd(512, 2), fwd(LANES), fwd(LANES), mf, mf,
                  bwd(512), bwd(512), bwd(512, 2), bwd(LANES), bwd(LANES), mb, mb],
        out_specs=[fwd(512), bwd(512)],
        out_shape=[jax.ShapeDtypeStruct((B, L, 512), F32), jax.ShapeDtypeStruct((B, L, 512), F32)],
        scratch_shapes=[pltpu.VMEM((GDN_DK, GDN_DV), F32)] * NDH,
        compiler_params=_cparams("parallel", "arbitrary"),
        name="gdn_scan",
    )(qn, kn, qkv, bg, gc, tmat, attn, qn, kn, qkv, bg, gc, tmat, attn)

    tm = min(1024, T)
    row = lambda c=0: pl.BlockSpec((tm, 512), lambda i: (i, c))
    return pl.pallas_call(
        _gdn_out_kernel,
        grid=(T // tm,),
        in_specs=[row(), row(), row(12), _full((1, GDN_DV))],
        out_specs=row(),
        out_shape=jax.ShapeDtypeStruct((T, 512), BF16),
        compiler_params=_cparams("parallel"),
        name="gdn_out",
    )(o_f.reshape(T, 512), o_b.reshape(T, 512), proj3.reshape(T, MAIN_W), out_norm[None, :])


def _merge_kernel(x_ref, g_ref, oh_ref, om_ref, og_ref, wb_ref, wo_ref, n_ref, o_ref):
    merged = None
    for i, b_ref in enumerate((oh_ref, om_ref, og_ref)):
        gate = jax.nn.sigmoid(g_ref[:, i * D_MODEL:(i + 1) * D_MODEL].astype(F32))
        term = gate * jnp.dot(b_ref[...].astype(BF16), wb_ref[i], preferred_element_type=F32)
        merged = term if merged is None else merged + term
    y = jnp.dot(merged.astype(BF16), wo_ref[...], preferred_element_type=F32)
    r = lax.rsqrt(jnp.mean(y * y, axis=-1, keepdims=True) + NORM_EPS)
    o_ref[...] = x_ref[...] + y * r * n_ref[...]


def merge_out(x, proj, o_hy, o_mla, o_gdn, w_branch, w_out, norm_post, tm=512):
    T = x.shape[0]
    tm = min(tm, T)
    row = lambda w: pl.BlockSpec((tm, w), lambda i: (i, 0))
    return pl.pallas_call(
        _merge_kernel,
        grid=(T // tm,),
        in_specs=[row(D_MODEL), row(N_BRANCH * D_MODEL), row(BRANCH_W), row(BRANCH_W), row(BRANCH_W),
                  _full((N_BRANCH, BRANCH_W, D_MODEL)), _full((D_MODEL, D_MODEL)), _full((1, D_MODEL))],
        out_specs=row(D_MODEL),
        out_shape=jax.ShapeDtypeStruct((T, D_MODEL), F32),
        compiler_params=_cparams("parallel"),
        name="merge_out",
    )(x, proj, o_hy, o_mla, o_gdn, w_branch, w_out, norm_post)


FFN_ROW_PARTS = 2


def _ffn_kernel(x_ref, gpre_ref, wg_ref, wu_ref, wd_ref, gpost_ref, o_ref, h_ref, acc_ref):
    j = pl.program_id(1)

    @pl.when(j == 0)
    def _():
        x = x_ref[...]
        r = lax.rsqrt(jnp.mean(x * x, axis=-1, keepdims=True) + NORM_EPS)
        h_ref[...] = (x * r * gpre_ref[...]).astype(BF16)
        acc_ref[...] = jnp.zeros(acc_ref.shape, F32)

    part_rows = h_ref.shape[0] // FFN_ROW_PARTS
    parts = [slice(r * part_rows, (r + 1) * part_rows) for r in range(FFN_ROW_PARTS)]
    gates = [jnp.dot(h_ref[rows, :], wg_ref[...], preferred_element_type=F32) for rows in parts]
    ups = [jnp.dot(h_ref[rows, :], wu_ref[...], preferred_element_type=F32) for rows in parts]
    acts = [(g * jax.nn.sigmoid(g) * u).astype(BF16) for g, u in zip(gates, ups)]
    for rows, a in zip(parts, acts):
        acc_ref[rows, :] += jnp.dot(a, wd_ref[...], preferred_element_type=F32)

    @pl.when(j == pl.num_programs(1) - 1)
    def _():
        f = acc_ref[...]
        r = lax.rsqrt(jnp.mean(f * f, axis=-1, keepdims=True) + NORM_EPS)
        o_ref[...] = x_ref[...] + f * r * gpost_ref[...]


def ffn(x, g_pre, w_gate, w_up, w_down, g_post, tm=512):
    T = x.shape[0]
    dff = w_gate.shape[1]
    tf = dff // 2
    tm = min(tm, T)
    return pl.pallas_call(
        _ffn_kernel,
        grid=(T // tm, dff // tf),
        in_specs=[pl.BlockSpec((tm, D_MODEL), lambda i, j: (i, 0)), _full((1, D_MODEL)),
                  pl.BlockSpec((D_MODEL, tf), lambda i, j: (0, j)),
                  pl.BlockSpec((D_MODEL, tf), lambda i, j: (0, j)),
                  pl.BlockSpec((tf, D_MODEL), lambda i, j: (j, 0)), _full((1, D_MODEL))],
        out_specs=pl.BlockSpec((tm, D_MODEL), lambda i, j: (i, 0)),
        out_shape=jax.ShapeDtypeStruct((T, D_MODEL), F32),
        scratch_shapes=[pltpu.VMEM((tm, D_MODEL), BF16), pltpu.VMEM((tm, D_MODEL), F32)],
        compiler_params=_cparams("parallel", "arbitrary"),
        name="ffn",
    )(x, g_pre, w_gate, w_up, w_down, g_post)


def _split_w_in(w_in):
    cols = lambda off, n: w_in[:, off:off + n]
    main = jnp.concatenate([cols(_OFF_GATE, N_BRANCH * D_MODEL), cols(_OFF_HY, 1536),
                            cols(_OFF_GQKV, 1536), cols(_OFF_GZ, 512)], axis=-1).astype(BF16)
    half = MLA_ROPE // 2
    kpe = _OFF_MKV + MLA_KV_LORA
    zeros = lambda n: jnp.zeros((D_MODEL, n), w_in.dtype)
    misc = jnp.concatenate([cols(_OFF_MQ, MLA_Q_LORA), cols(_OFF_MKV, MLA_KV_LORA),
                            cols(kpe, MLA_ROPE), cols(_OFF_GB, 8), cols(_OFF_GA, 8), zeros(48),
                            cols(kpe + half, half), cols(kpe, half), zeros(64)], axis=-1).astype(BF16)
    return main, misc


def trunk_layer(x, norm_mix_pre, norm_mix_post, norm_ffn_pre, norm_ffn_post, w_in,
                hy_conv_w, hy_conv_b, hy_ffn_w1, hy_ffn_b1, hy_sin_freq, hy_ffn_w2, hy_ffn_b2,
                hy_ffn_w3, hy_skip, mla_q_norm, mla_wq_b, mla_kv_norm, mla_wkv_b,
                gdn_conv_w, gdn_a_log, gdn_dt_bias, gdn_out_norm,
                w_branch, w_out, w_gate, w_up, w_down):
    B, L, D = x.shape
    T = B * L
    xt = x.reshape(T, D)
    w_main, w_misc = _split_w_in(w_in)
    g_pre = norm_mix_pre[None, :]
    proj = norm_mm(xt, g_pre, w_main, tm=2048, tn=512, out_dtype=BF16)
    misc = norm_mm(xt, g_pre, w_misc, tm=1024, tn=MISC_W, out_dtype=F32)
    proj3 = proj.reshape(B, L, MAIN_W)
    misc3 = misc.reshape(B, L, MISC_W)
    o_hy = hyena_mixer(proj3, hy_conv_w, hy_conv_b, hy_ffn_w1, hy_ffn_b1, hy_sin_freq,
                       hy_ffn_w2, hy_ffn_b2, hy_ffn_w3, hy_skip)
    o_mla = mla_mixer(misc3, mla_q_norm, mla_wq_b, mla_kv_norm, mla_wkv_b).reshape(T, BRANCH_W)
    o_gdn = gdn_mixer(proj3, misc3, gdn_conv_w, gdn_a_log, gdn_dt_bias, gdn_out_norm).reshape(T, BRANCH_W)
    xt = merge_out(xt, proj, o_hy, o_mla, o_gdn, w_branch.astype(BF16), w_out.astype(BF16),
                   norm_mix_post[None, :])
    xt = ffn(xt, norm_ffn_pre[None, :], w_gate.astype(BF16), w_up.astype(BF16), w_down.astype(BF16),
             norm_ffn_post[None, :])
    return xt.reshape(B, L, D)


def kernel(x_prompt, x_sample, norm_mix_pre, norm_mix_post, norm_ffn_pre, norm_ffn_post, w_in,
           hy_conv_w, hy_conv_b, hy_ffn_w1, hy_ffn_b1, hy_sin_freq, hy_ffn_w2, hy_ffn_b2,
           hy_ffn_w3, hy_skip, mla_q_norm, mla_wq_b, mla_kv_norm, mla_wkv_b,
           gdn_conv_w, gdn_a_log, gdn_dt_bias, gdn_out_norm,
           w_branch, w_out, w_gate, w_up, w_down):
    weights = (norm_mix_pre, norm_mix_post, norm_ffn_pre, norm_ffn_post, w_in,
               hy_conv_w, hy_conv_b, hy_ffn_w1, hy_ffn_b1, hy_sin_freq, hy_ffn_w2, hy_ffn_b2,
               hy_ffn_w3, hy_skip, mla_q_norm, mla_wq_b, mla_kv_norm, mla_wkv_b,
               gdn_conv_w, gdn_a_log, gdn_dt_bias, gdn_out_norm,
               w_branch, w_out, w_gate, w_up, w_down)

    def run_trunk(x):
        for layer in range(DEPTH):
            x = trunk_layer(x, *[w[layer] for w in weights])
        return x

    return (run_trunk(x_prompt), run_trunk(x_sample))
```

```python
import functools
import math

import jax
import jax.numpy as jnp
from jax import lax
from jax.experimental import pallas as pl
from jax.experimental.pallas import tpu as pltpu
import numpy as np

F32 = jnp.float32
BF16 = jnp.bfloat16

D_MODEL = 1024
DEPTH = 2
BRANCH_W = 512
N_BRANCH = 3
HY_W = BRANCH_W
HY_ORDER = 2
HY_EMB = 33
HY_FAST_PCT = 0.3
HY_SLOW_PCT = 1.5
HY_TARGET = 1e-2
MLA_HEADS = 4
MLA_NOPE = 128
MLA_ROPE = 64
MLA_V = 128
MLA_Q_LORA = 256
MLA_KV_LORA = 128
ROPE_THETA = 10000.0
GDN_HEADS = 4
GDN_DK = 128
GDN_DV = 128
GDN_CHUNK = 64
NORM_EPS = 1e-6

_OFF_HY = 0
_OFF_MQ = _OFF_HY + (HY_ORDER + 1) * HY_W
_OFF_MKV = _OFF_MQ + MLA_Q_LORA
_OFF_GQKV = _OFF_MKV + MLA_KV_LORA + MLA_ROPE
_OFF_GZ = _OFF_GQKV + GDN_HEADS * (2 * GDN_DK + GDN_DV)
_OFF_GB = _OFF_GZ + GDN_HEADS * GDN_DV
_OFF_GA = _OFF_GB + 2 * GDN_HEADS
_OFF_GATE = _OFF_GA + 2 * GDN_HEADS
_D_IN = _OFF_GATE + N_BRANCH * D_MODEL

MAIN_W = 3072 + 1536 + 1536 + 512
MISC_W = 1024

LANES = 128
VMEM_LIMIT_BYTES = 56 * 1024 * 1024


def _cparams(*sem):
    return pltpu.CompilerParams(dimension_semantics=sem, vmem_limit_bytes=VMEM_LIMIT_BYTES)


def _full(shape):
    nd = len(shape)
    return pl.BlockSpec(shape, lambda *_: (0,) * nd)


def _norm_mm_kernel(x_ref, g_ref, w_ref, main_ref, misc_ref, h_ref, *, n_main):
    j = pl.program_id(1)

    @pl.when(j == 0)
    def _():
        x = x_ref[...]
        r = lax.rsqrt(jnp.mean(x * x, axis=-1, keepdims=True) + NORM_EPS)
        h_ref[...] = (x * r * g_ref[...]).astype(BF16)

    res = jnp.dot(h_ref[...], w_ref[...], preferred_element_type=F32)

    @pl.when(j < n_main)
    def _():
        main_ref[...] = res.astype(main_ref.dtype)

    @pl.when(j >= n_main)
    def _():
        misc_ref[...] = res


def norm_mm(x, g, w, tm, tn):
    T, D = x.shape
    tm = min(tm, T)
    n_main = MAIN_W // tn
    n_misc = MISC_W // tn
    return pl.pallas_call(
        functools.partial(_norm_mm_kernel, n_main=n_main),
        grid=(T // tm, n_main + n_misc),
        in_specs=[pl.BlockSpec((tm, D), lambda i, j: (i, 0)),
                  pl.BlockSpec((1, D), lambda i, j: (0, 0)),
                  pl.BlockSpec((D, tn), lambda i, j: (0, j))],
        out_specs=[pl.BlockSpec((tm, tn), lambda i, j: (i, jnp.minimum(j, n_main - 1))),
                   pl.BlockSpec((tm, tn), lambda i, j: (i, jnp.maximum(j - n_main, 0)))],
        out_shape=[jax.ShapeDtypeStruct((T, MAIN_W), BF16), jax.ShapeDtypeStruct((T, MISC_W), F32)],
        scratch_shapes=[pltpu.VMEM((tm, D), BF16)],
        compiler_params=_cparams("parallel", "arbitrary"),
        name="norm_mm",
    )(x, g, w)


def _conv3_rows(x_ref, p_ref, n_ref, w_ref):
    i = pl.program_id(1)
    last = pl.num_programs(1) - 1
    x = x_ref[0].astype(F32)
    tl = x.shape[0]
    halo = p_ref.shape[1]
    prev_row = jnp.where(i > 0, p_ref[0, halo - 1:halo, :].astype(F32), 0.0)
    next_row = jnp.where(i < last, n_ref[0, 0:1, :].astype(F32), 0.0)
    rows = lax.broadcasted_iota(jnp.int32, x.shape, 0)
    x_dn = jnp.where(rows == 0, prev_row, pltpu.roll(x, 1, axis=0))
    x_up = jnp.where(rows == tl - 1, next_row, pltpu.roll(x, tl - 1, axis=0))
    return x_dn * w_ref[0:1, :] + x * w_ref[1:2, :] + x_up * w_ref[2:3, :]


def _conv3_specs(x, col_blk, cw, tl):
    halo = 8 * (4 // x.dtype.itemsize)
    nsub = tl // halo
    lastblk = x.shape[1] // halo - 1
    return [pl.BlockSpec((1, tl, cw), lambda b_, i: (b_, i, col_blk)),
            pl.BlockSpec((1, halo, cw), lambda b_, i: (b_, jnp.maximum(i * nsub - 1, 0), col_blk)),
            pl.BlockSpec((1, halo, cw), lambda b_, i: (b_, jnp.minimum((i + 1) * nsub, lastblk), col_blk))]


def _dwconv_kernel(x_ref, p_ref, n_ref, w_ref, b_ref, o_ref):
    o_ref[0] = (_conv3_rows(x_ref, p_ref, n_ref, w_ref) + b_ref[...]).astype(o_ref.dtype)


def dwconv3(x, col_blk, cw, w, b, out_dtype, tl=512):
    B, L, _ = x.shape
    tl = min(tl, L)
    return pl.pallas_call(
        _dwconv_kernel,
        grid=(B, L // tl),
        in_specs=[*_conv3_specs(x, col_blk, cw, tl), _full((3, cw)), _full((1, cw))],
        out_specs=pl.BlockSpec((1, tl, cw), lambda b_, i: (b_, i, 0)),
        out_shape=jax.ShapeDtypeStruct((B, L, cw), out_dtype),
        compiler_params=_cparams("parallel", "parallel"),
        name="dwconv3",
    )(x, x, x, w, b)


def _tile_lanes(t, c):
    return t if c == LANES else jnp.concatenate([t] * (c // LANES), axis=-1)


def _fft_s1_kernel(f_ref, x_ref, twr_ref, twi_ref, o_ref, *, nb, n1):
    c = x_ref.shape[-1]
    for t in range(nb):
        a = jnp.dot(f_ref[...], x_ref[0, t], preferred_element_type=F32)
        ar, ai = a[:n1], a[n1:]
        twr = _tile_lanes(twr_ref[t], c)
        twi = _tile_lanes(twi_ref[t], c)
        o_ref[0, t, :n1, :] = (ar * twr + ai * twi).astype(BF16)
        o_ref[0, t, n1:, :] = (ai * twr - ar * twi).astype(BF16)


def _fft_s2_kernel(f_ref, finv_ref, b_ref, kr_ref, ki_ref, o_ref, *, nb, n2):
    for t in range(nb):
        x = jnp.dot(f_ref[t], b_ref[0, t], preferred_element_type=F32)
        xr, xi = x[:n2], x[n2:]
        kr, ki = kr_ref[t].astype(F32), ki_ref[t].astype(F32)
        y = jnp.concatenate([xr * kr - xi * ki, xr * ki + xi * kr], axis=0).astype(BF16)
        o_ref[0, t] = jnp.dot(finv_ref[t], y, preferred_element_type=F32).astype(BF16)


def _fft_mm_kernel(g_ref, d_ref, o_ref, *, nb):
    for t in range(nb):
        o_ref[0, t] = jnp.dot(g_ref[...], d_ref[0, t], preferred_element_type=F32).astype(o_ref.dtype)


def _dft_tables(L):
    N = 2 * L
    n2 = 128 if N >= 4096 else 16
    n1 = N // n2
    n1h = n1 // 2
    k1 = np.arange(n1)[:, None]
    m1 = np.arange(n1h)[None, :]
    th = 2.0 * np.pi * ((k1 * m1) % n1) / n1
    c, s = np.cos(th), np.sin(th)
    f1 = np.block([[c, s], [-s, c]])
    g1 = np.block([[c.T, -s.T], [s.T, c.T]]) / N
    as_bf16 = lambda m: jnp.asarray(m, dtype=F32).astype(BF16)
    k = (jnp.arange(n1, dtype=jnp.int32)[:, None, None]
         + n1 * jnp.arange(n2, dtype=jnp.int32)[None, :, None])
    ang = ((k * jnp.arange(n2, dtype=jnp.int32)[None, None, :]) % N).astype(F32) * (2.0 * math.pi / N)
    c2, s2 = jnp.cos(ang), jnp.sin(ang)
    f2 = jnp.concatenate([jnp.concatenate([c2, s2], axis=2),
                          jnp.concatenate([-s2, c2], axis=2)], axis=1).astype(BF16)
    return n1, n2, as_bf16(f1), as_bf16(g1), f2, f2.transpose(0, 2, 1)


def _twiddles(n1, n2):
    N = n1 * n2
    prod = (jnp.arange(n2, dtype=jnp.int32)[:, None] * jnp.arange(n1, dtype=jnp.int32)[None, :]) % N
    ang = prod.astype(F32) * (2.0 * math.pi / N)
    shape = (n2, n1, LANES)
    return (jnp.broadcast_to(jnp.cos(ang)[:, :, None], shape),
            jnp.broadcast_to(jnp.sin(ang)[:, :, None], shape))


def fft_long_conv(z, kspec, order, tabs):
    B, L, C = z.shape
    n1, n2, f1, g1, f2, f2inv = tabs
    n1h = n1 // 2
    P = B // 2
    nb1 = min(8, n2)
    nb2 = min(8, n1)
    zt = z.astype(BF16).reshape(P, 2, n1h, n2, C).transpose(0, 3, 1, 2, 4).reshape(P, n2, 2 * n1h, C)
    a = pl.pallas_call(
        functools.partial(_fft_mm_kernel, nb=nb1),
        grid=(n2 // nb1, P),
        in_specs=[_full((2 * n1, 2 * n1h)),
                  pl.BlockSpec((1, nb1, 2 * n1h, C), lambda j, p: (p, j, 0, 0))],
        out_specs=pl.BlockSpec((1, nb1, 2 * n1, C), lambda j, p: (p, j, 0, 0)),
        out_shape=jax.ShapeDtypeStruct((P, n2, 2 * n1, C), BF16),
        compiler_params=_cparams("parallel", "parallel"),
        name="fft_s1",
    )(f1, zt)
    bt = a.reshape(P, n2, 2, n1, C).transpose(0, 3, 2, 1, 4).reshape(P, n1, 2 * n2, C)
    d = pl.pallas_call(
        functools.partial(_fft_s2_kernel, nb=nb2, n2=n2),
        grid=(n1 // nb2, P),
        in_specs=[pl.BlockSpec((nb2, 2 * n2, 2 * n2), lambda j, p: (j, 0, 0)),
                  pl.BlockSpec((nb2, 2 * n2, 2 * n2), lambda j, p: (j, 0, 0)),
                  pl.BlockSpec((1, nb2, 2 * n2, C), lambda j, p: (p, j, 0, 0)),
                  pl.BlockSpec((nb2, n2, C), lambda j, p: (j, 0, order)),
                  pl.BlockSpec((nb2, n2, C), lambda j, p: (j, 1, order))],
        out_specs=pl.BlockSpec((1, nb2, 2 * n2, C), lambda j, p: (p, j, 0, 0)),
        out_shape=jax.ShapeDtypeStruct((P, n1, 2 * n2, C), BF16),
        compiler_params=_cparams("parallel", "parallel"),
        name="fft_s2",
    )(f2, f2inv, bt, kspec, kspec)
    dt = d.reshape(P, n1, 2, n2, C).transpose(0, 3, 2, 1, 4).reshape(P, n2, 2 * n1, C)
    y = pl.pallas_call(
        functools.partial(_fft_mm_kernel, nb=nb1),
        grid=(n2 // nb1, P),
        in_specs=[_full((2 * n1h, 2 * n1)),
                  pl.BlockSpec((1, nb1, 2 * n1, C), lambda j, p: (p, j, 0, 0))],
        out_specs=pl.BlockSpec((1, nb1, 2 * n1h, C), lambda j, p: (p, j, 0, 0)),
        out_shape=jax.ShapeDtypeStruct((P, n2, 2 * n1h, C), BF16),
        compiler_params=_cparams("parallel", "parallel"),
        name="fft_s3",
    )(g1, dt)
    return y.reshape(P, n2, 2, n1h, C).transpose(0, 2, 3, 1, 4).reshape(B, L, C)


def _hy_gate_kernel(g_ref, c_ref, z_ref, s_ref, o_ref):
    o_ref[...] = (g_ref[...].astype(F32) * (c_ref[...].astype(F32) + s_ref[...] * z_ref[...].astype(F32))
                  ).astype(o_ref.dtype)


def hy_gate(u, gate_blk, conv, zsrc, z_blk, skip, out_dtype, tl=1024):
    T, C = conv.shape
    tl = min(tl, T)
    return pl.pallas_call(
        _hy_gate_kernel,
        grid=(T // tl,),
        in_specs=[pl.BlockSpec((tl, C), lambda i: (i, gate_blk)),
                  pl.BlockSpec((tl, C), lambda i: (i, 0)),
                  pl.BlockSpec((tl, C), lambda i: (i, z_blk)),
                  _full((1, C))],
        out_specs=pl.BlockSpec((tl, C), lambda i: (i, 0)),
        out_shape=jax.ShapeDtypeStruct((T, C), out_dtype),
        compiler_params=_cparams("parallel"),
        name="hy_gate",
    )(u, conv, zsrc, skip)


def hyena_filter_spectrum(L, n1, n2, tw1r, tw1i, w1, b1, freq, w2, b2, w3):
    t = jnp.linspace(0.0, 1.0, L, dtype=F32)[:, None]
    bands = (HY_EMB - 1) // 2
    wpos = (2.0 * math.pi / L) * jnp.arange(L, dtype=F32)[:, None]
    fr = jnp.linspace(1e-4, bands - 1, bands, dtype=F32)[None, :]
    feats = jnp.concatenate([t, jnp.cos(fr * wpos), -jnp.sin(fr * wpos)], axis=-1)
    deltas = jnp.abs(jnp.linspace(math.log(HY_TARGET) / HY_SLOW_PCT,
                                  math.log(HY_TARGET) / HY_FAST_PCT, HY_W, dtype=F32))
    feats2 = jnp.concatenate([feats, feats[0:1], feats[:0:-1]], axis=0)
    feats2 = jnp.pad(feats2, ((0, 0), (0, LANES - HY_EMB)))
    w1p = jnp.pad(w1, ((0, LANES - HY_EMB), (0, 0)))
    hid = w3.shape[0]
    oc = HY_ORDER * HY_W
    w3d = w3.reshape(hid, HY_ORDER, 2, HY_W).transpose(2, 0, 1, 3).reshape(2, hid, oc)
    tl = min(512, L)
    nh = L // tl
    kern, asum = pl.pallas_call(
        functools.partial(_filter_taps_kernel, zero_tile=nh),
        grid=(2 * nh,),
        in_specs=[pl.BlockSpec((tl, LANES), lambda i: (i, 0)), _full((LANES, hid)), _full((1, hid)),
                  _full((1, hid)), _full((hid, hid)), _full((1, hid)),
                  pl.BlockSpec((1, hid, oc), lambda i: (i // nh, 0, 0)), _full((1, oc))],
        out_specs=[pl.BlockSpec((tl, oc), lambda i: (i, 0)), _full((1, oc))],
        out_shape=[jax.ShapeDtypeStruct((2 * L, oc), F32), jax.ShapeDtypeStruct((1, oc), F32)],
        compiler_params=_cparams("arbitrary"),
        name="filter_taps",
    )(feats2, w1p, b1, freq, w2, b2, w3d, jnp.tile(deltas, HY_ORDER)[None, :])
    return filter_dft(kern, asum, n1, n2, tw1r, tw1i)


def _filter_taps_kernel(f_ref, w1_ref, b1_ref, fq_ref, w2_ref, b2_ref, w3_ref, dl_ref, k_ref, s_ref, *,
                        zero_tile):
    i = pl.program_id(0)
    hp = lax.Precision.HIGHEST
    f = f_ref[...]
    h = jnp.sin(fq_ref[...] * (jnp.dot(f, w1_ref[...], precision=hp, preferred_element_type=F32)
                               + b1_ref[...]))
    h = jnp.sin(fq_ref[...] * (jnp.dot(h, w2_ref[...], precision=hp, preferred_element_type=F32)
                               + b2_ref[...]))
    taps = jnp.dot(h, w3_ref[0], precision=hp, preferred_element_type=F32)
    taps = taps * jnp.exp(-f[:, 0:1] * dl_ref[...])
    row = lax.broadcasted_iota(jnp.int32, taps.shape, 0)
    taps = jnp.where((i == zero_tile) & (row == 0), 0.0, taps)
    k_ref[...] = taps

    @pl.when(i == 0)
    def _():
        s_ref[...] = jnp.zeros(s_ref.shape, F32)

    s_ref[...] += jnp.sum(jnp.abs(taps), axis=0, keepdims=True)


def _spec_s2_kernel(f_ref, b_ref, s_ref, o_ref, *, nb):
    for t in range(nb):
        o_ref[t] = (jnp.dot(f_ref[...], b_ref[0, t], preferred_element_type=F32) / s_ref[...]).astype(BF16)


def filter_dft(kern, asum, n1, n2, tw1r, tw1i):
    N, C = kern.shape
    ct = 512
    k1 = np.arange(n1)
    th = 2.0 * np.pi * ((k1[:, None] * k1[None, :]) % n1) / n1
    f1 = np.concatenate([np.cos(th), -np.sin(th)], axis=0)
    a2 = np.arange(n2)
    ph = 2.0 * np.pi * ((a2[:, None] * a2[None, :]) % n2) / n2
    f2 = np.block([[np.cos(ph), np.sin(ph)], [-np.sin(ph), np.cos(ph)]])

    as_bf16 = lambda m: jnp.asarray(m, dtype=F32).astype(BF16)
    nb1 = min(8, n2)
    nb2 = min(8, n1)
    xt = kern.astype(BF16).reshape(n1, n2, C).transpose(1, 0, 2)[None]
    a = pl.pallas_call(
        functools.partial(_fft_s1_kernel, nb=nb1, n1=n1),
        grid=(n2 // nb1, C // ct),
        in_specs=[_full((2 * n1, n1)),
                  pl.BlockSpec((1, nb1, n1, ct), lambda j, c: (0, j, 0, c)),
                  pl.BlockSpec((nb1, n1, LANES), lambda j, c: (j, 0, 0)),
                  pl.BlockSpec((nb1, n1, LANES), lambda j, c: (j, 0, 0))],
        out_specs=pl.BlockSpec((1, nb1, 2 * n1, ct), lambda j, c: (0, j, 0, c)),
        out_shape=jax.ShapeDtypeStruct((1, n2, 2 * n1, C), BF16),
        compiler_params=_cparams("parallel", "parallel"),
        name="spec_s1",
    )(as_bf16(f1), xt, tw1r, tw1i)
    bt = a.reshape(n2, 2, n1, C).transpose(2, 1, 0, 3).reshape(1, n1, 2 * n2, C)
    return pl.pallas_call(
        functools.partial(_spec_s2_kernel, nb=nb2),
        grid=(n1 // nb2, C // ct),
        in_specs=[_full((2 * n2, 2 * n2)),
                  pl.BlockSpec((1, nb2, 2 * n2, ct), lambda j, c: (0, j, 0, c)),
                  pl.BlockSpec((1, ct), lambda j, c: (0, c))],
        out_specs=pl.BlockSpec((nb2, 2 * n2, ct), lambda j, c: (j, 0, c)),
        out_shape=jax.ShapeDtypeStruct((n1, 2 * n2, C), BF16),
        compiler_params=_cparams("parallel", "parallel"),
        name="spec_s2",
    )(as_bf16(f2), bt, asum)


def hyena_mixer(proj3, conv_w, conv_b, w1, b1, freq, w2, b2, w3, skip):
    B, L, _ = proj3.shape
    T = B * L
    u = dwconv3(proj3, 2, 1536, conv_w, conv_b[None, :], out_dtype=BF16).reshape(T, 1536)
    n1, n2, f1, g1, f2, f2inv = _dft_tables(L)
    tw1r, tw1i = _twiddles(n1, n2)
    tabs = (n1, n2, f1, g1, f2, f2inv)
    kspec = hyena_filter_spectrum(L, n1, n2, tw1r, tw1i, w1, b1[None, :], freq[None, :], w2, b2[None, :], w3)
    conv = fft_long_conv(u[:, 1024:1536].reshape(B, L, HY_W), kspec, 0, tabs).reshape(T, HY_W)
    z1 = hy_gate(u, 0, conv, u, 2, skip[0:1], BF16)
    conv = fft_long_conv(z1.reshape(B, L, HY_W), kspec, 1, tabs).reshape(T, HY_W)
    return hy_gate(u, 1, conv, z1, 0, skip[1:2], BF16)


def _mla_proj_kernel(m_ref, qg_ref, kg_ref, wq_ref, wkv_ref, cq_ref, sq_ref, ck_ref, sk_ref,
                     q_ref, k_ref, v_ref):
    m = m_ref[0]
    ql = m[:, 0:256]
    r = lax.rsqrt(jnp.mean(ql * ql, axis=-1, keepdims=True) + NORM_EPS)
    q = jnp.dot((ql * r * qg_ref[...]).astype(BF16), wq_ref[...], preferred_element_type=F32)
    ckv = m[:, 256:384]
    r = lax.rsqrt(jnp.mean(ckv * ckv, axis=-1, keepdims=True) + NORM_EPS)
    kv = jnp.dot((ckv * r * kg_ref[...]).astype(BF16), wkv_ref[...], preferred_element_type=F32)
    q_pe = ((q[:, 512:1024] * cq_ref[...] + q[:, 1024:1536] * sq_ref[...]) * ATTN_Q_SCALE).astype(BF16)
    k_pe = (m[:, 384:512] * ck_ref[...] + m[:, 512:640] * sk_ref[...]).astype(BF16)
    for h in range(MLA_HEADS):
        sl = slice(h * LANES, (h + 1) * LANES)
        q_ref[0, :, 2 * h * LANES:(2 * h + 1) * LANES] = (q[:, sl] * ATTN_Q_SCALE).astype(BF16)
        q_ref[0, :, (2 * h + 1) * LANES:(2 * h + 2) * LANES] = q_pe[:, sl]
        k_ref[0, :, 2 * h * LANES:(2 * h + 1) * LANES] = kv[:, sl].astype(BF16)
        k_ref[0, :, (2 * h + 1) * LANES:(2 * h + 2) * LANES] = k_pe
        v_ref[0, :, 2 * h * LANES:(2 * h + 1) * LANES] = kv[:, 512 + h * MLA_V:512 + (h + 1) * MLA_V].astype(BF16)
        v_ref[0, :, (2 * h + 1) * LANES:(2 * h + 2) * LANES] = jnp.ones((m.shape[0], LANES), BF16)


ATTN_Q_SCALE = (MLA_NOPE + MLA_ROPE) ** -0.5 * math.log2(math.e)
ATTN_ROW_PARTS = 2


def _attn_kernel(q_ref, k_ref, v_ref, o_ref, *scratch):
    ki = pl.program_id(2)
    dqk = 2 * LANES
    m_refs, acc_refs = scratch[:MLA_HEADS], scratch[MLA_HEADS:]

    @pl.when(ki == 0)
    def _():
        for m_ref, acc_ref in zip(m_refs, acc_refs):
            m_ref[...] = jnp.full(m_ref.shape, -jnp.inf, F32)
            acc_ref[...] = jnp.zeros(acc_ref.shape, F32)

    part = q_ref.shape[1] // ATTN_ROW_PARTS
    chains = [(h, slice(h * dqk, (h + 1) * dqk), slice(r * part, (r + 1) * part))
              for h in range(MLA_HEADS) for r in range(ATTN_ROW_PARTS)]
    scores = [lax.dot_general(q_ref[0, rows, sl], k_ref[0, :, sl], (((1,), (1,)), ((), ())),
                              preferred_element_type=F32) for _, sl, rows in chains]
    weights = []
    for (h, sl, rows), s in zip(chains, scores):
        m_prev = m_refs[h][rows, :]
        m_new = jnp.maximum(m_prev, jnp.max(s, axis=-1, keepdims=True))
        m_refs[h][rows, :] = m_new
        weights.append((jnp.exp2(m_prev - m_new), jnp.exp2(s - m_new).astype(BF16)))
    for (h, sl, rows), (alpha, p) in zip(chains, weights):
        acc_refs[h][rows, :] = (alpha * acc_refs[h][rows, :]
                                + jnp.dot(p, v_ref[0, :, sl], preferred_element_type=F32))

    @pl.when(ki == pl.num_programs(2) - 1)
    def _():
        for h, acc_ref in enumerate(acc_refs):
            o_ref[0, :, h * MLA_V:(h + 1) * MLA_V] = (acc_ref[:, :MLA_V] / acc_ref[:, MLA_V:]).astype(BF16)


def _rope_lane_tables(L):
    half = MLA_ROPE // 2
    inv = ROPE_THETA ** (-jnp.arange(half, dtype=F32) / half)
    ang = jnp.arange(L, dtype=F32)[:, None] * inv[None, :]
    cos, sin = jnp.cos(ang), jnp.sin(ang)
    zeros = jnp.zeros((L, LANES - MLA_ROPE), F32)
    ck = jnp.concatenate([cos, cos, zeros], axis=-1)
    sk = jnp.concatenate([-sin, sin, zeros], axis=-1)
    return jnp.tile(ck, (1, MLA_HEADS)), jnp.tile(sk, (1, MLA_HEADS)), ck, sk


def mla_mixer(misc3, q_norm, wq_b, kv_norm, wkv_b, tl=512, tq=1024, tk=1024):
    B, L, _ = misc3.shape
    H = MLA_HEADS
    tl, tq, tk = min(tl, L), min(tq, L), min(tk, L)
    half = MLA_ROPE // 2
    wq = wq_b.reshape(MLA_Q_LORA, H, MLA_NOPE + MLA_ROPE)
    x1 = wq[:, :, MLA_NOPE:MLA_NOPE + half]
    x2 = wq[:, :, MLA_NOPE + half:]
    zpad = jnp.zeros((MLA_Q_LORA, H, LANES - MLA_ROPE), wq.dtype)
    wq = jnp.concatenate([wq[:, :, :MLA_NOPE].reshape(MLA_Q_LORA, -1),
                          jnp.concatenate([x1, x2, zpad], axis=-1).reshape(MLA_Q_LORA, -1),
                          jnp.concatenate([x2, x1, zpad], axis=-1).reshape(MLA_Q_LORA, -1)],
                         axis=-1).astype(BF16)
    wkv = wkv_b.reshape(MLA_KV_LORA, H, MLA_NOPE + MLA_V)
    wkv = jnp.concatenate([wkv[:, :, :MLA_NOPE].reshape(MLA_KV_LORA, -1),
                           wkv[:, :, MLA_NOPE:].reshape(MLA_KV_LORA, -1)], axis=-1).astype(BF16)
    cq, sq, ck, sk = _rope_lane_tables(L)
    tok = lambda w: pl.BlockSpec((1, tl, w), lambda b, i: (b, i, 0))
    pos = lambda w: pl.BlockSpec((tl, w), lambda b, i: (i, 0))
    sds = lambda w: jax.ShapeDtypeStruct((B, L, w), BF16)
    dqk = 2 * LANES
    nk = L // tk
    q, k, v = pl.pallas_call(
        _mla_proj_kernel,
        grid=(B, L // tl),
        in_specs=[tok(MISC_W), _full((1, MLA_Q_LORA)), _full((1, MLA_KV_LORA)),
                  _full(wq.shape), _full(wkv.shape), pos(H * LANES), pos(H * LANES), pos(LANES), pos(LANES)],
        out_specs=[tok(H * dqk), tok(H * dqk), tok(H * dqk)],
        out_shape=[sds(H * dqk), sds(H * dqk), sds(H * dqk)],
        compiler_params=_cparams("parallel", "parallel"),
        name="mla_proj",
    )(misc3, q_norm[None, :], kv_norm[None, :], wq, wkv, cq, sq, ck, sk)
    return pl.pallas_call(
        _attn_kernel,
        grid=(B, L // tq, nk),
        in_specs=[pl.BlockSpec((1, tq, H * dqk), lambda b, i, j: (b, i, 0)),
                  pl.BlockSpec((1, tk, H * dqk), lambda b, i, j: (b, j, 0)),
                  pl.BlockSpec((1, tk, H * dqk), lambda b, i, j: (b, j, 0))],
        out_specs=pl.BlockSpec((1, tq, H * MLA_V), lambda b, i, j: (b, i, 0)),
        out_shape=jax.ShapeDtypeStruct((B, L, H * MLA_V), BF16),
        scratch_shapes=([pltpu.VMEM((tq, 1), F32)] * H + [pltpu.VMEM((tq, 2 * MLA_V), F32)] * H),
        compiler_params=_cparams("parallel", "parallel", "arbitrary"),
        name="mla_attn",
    )(q, k, v)


_BETA_LANE = 64
_G_LANE = 72
CH = GDN_CHUNK
NDH = 2 * GDN_HEADS
GDN_STEP_CHUNKS = 8


def _gdn_prep_kernel(x_ref, p_ref, n_ref, w_ref, m_ref, alog_ref, dtb_ref, qn_ref, kn_ref, v_ref, bg_ref):
    c = _conv3_rows(x_ref, p_ref, n_ref, w_ref)
    c = c * jax.nn.sigmoid(c)
    nqk = GDN_HEADS * GDN_DK
    for h in range(GDN_HEADS):
        sl = slice(h * GDN_DK, (h + 1) * GDN_DK)
        qh = c[:, sl]
        qn_ref[0, :, sl] = (qh * lax.rsqrt(jnp.sum(qh * qh, axis=-1, keepdims=True) + NORM_EPS)
                            * (GDN_DK ** -0.5))
        kh = c[:, nqk + h * GDN_DK:nqk + (h + 1) * GDN_DK]
        kn_ref[0, :, sl] = kh * lax.rsqrt(jnp.sum(kh * kh, axis=-1, keepdims=True) + NORM_EPS)
    v_ref[0] = c[:, 2 * nqk:]
    m = m_ref[0]
    lane = lax.broadcasted_iota(jnp.int32, m.shape, 1)
    beta = jax.nn.sigmoid(m)
    x = m + dtb_ref[...]
    softplus = jnp.maximum(x, 0.0) + jnp.log1p(jnp.exp(-jnp.abs(x)))
    g = -jnp.exp(alog_ref[...]) * softplus
    is_beta = (lane >= _BETA_LANE) & (lane < _BETA_LANE + NDH)
    is_g = (lane >= _G_LANE) & (lane < _G_LANE + NDH)
    bg_ref[0] = jnp.where(is_beta, beta, jnp.where(is_g, g, 0.0))


def _split3(x):
    hi = x.astype(BF16)
    r = x - hi.astype(F32)
    mid = r.astype(BF16)
    lo = (r - mid.astype(F32)).astype(BF16)
    return hi, mid, lo


def _dot_nt(a, b):
    return lax.dot_general(a, b, (((1,), (1,)), ((), ())), preferred_element_type=F32)


def _gdn_chunk_kernel(q_ref, k_ref, bg_ref, a_ref, attn_ref, gc_ref):
    lane = lax.broadcasted_iota(jnp.int32, (CH, LANES), 1)
    is_g = (lane >= _G_LANE) & (lane < _G_LANE + NDH)
    ri = lax.broadcasted_iota(jnp.int32, (CH, CH), 0)
    ci = lax.broadcasted_iota(jnp.int32, (CH, CH), 1)
    lower = ri >= ci
    upper = ri <= ci
    tril = lower.astype(BF16)
    triu = upper.astype(BF16)
    for c in range(GDN_STEP_CHUNKS):
        rows = slice(c * CH, (c + 1) * CH)
        bg = bg_ref[0, rows, :]
        pieces = _split3(jnp.where(is_g, bg, 0.0))
        pre = sum(jnp.dot(tril, p, preferred_element_type=F32) for p in pieces)
        suf = sum(jnp.dot(triu, p, preferred_element_type=F32) for p in pieces)
        gc = jnp.where(lane >= _G_LANE + GDN_HEADS, suf, pre)
        gc_ref[0, rows, :] = gc
        gct = gc.T
        for d in range(2):
            causal = lower if d == 0 else upper
            strict = (ri > ci) if d == 0 else (ri < ci)
            for h in range(GDN_HEADS):
                dh = d * GDN_HEADS + h
                sl = slice(h * GDN_DK, (h + 1) * GDN_DK)
                kh = k_ref[0, rows, sl]
                kb = (kh * bg[:, _BETA_LANE + dh:_BETA_LANE + dh + 1]).astype(BF16)
                khb = kh.astype(BF16)
                diff = gc[:, _G_LANE + dh:_G_LANE + dh + 1] - gct[_G_LANE + dh:_G_LANE + dh + 1, :]
                dec = jnp.exp(jnp.where(causal, diff, -jnp.inf))
                a_ref[0, c, dh] = jnp.where(strict, _dot_nt(kb, khb) * dec, 0.0)
                attn_ref[0, c, dh] = (_dot_nt(q_ref[0, rows, sl].astype(BF16), khb) * dec).astype(BF16)


def _gdn_solve_kernel(a_ref, t_ref, *, nblk_fwd):
    bwd = pl.program_id(0) >= nblk_fwd
    t_ref[...] = jnp.zeros(t_ref.shape, F32)

    @pl.when(jnp.logical_not(bwd))
    def _():
        _solve_triangular(a_ref, t_ref, reverse=False)

    @pl.when(bwd)
    def _():
        _solve_triangular(a_ref, t_ref, reverse=True)


def _solve_triangular(a_ref, t_ref, *, reverse):
    nblk = CH // 8
    sub8 = lax.broadcasted_iota(jnp.int32, (8, LANES), 0)
    for phase in range(nblk):
        rb = nblk - 1 - phase if reverse else phase
        groups = range(rb, nblk) if reverse else range(0, rb + 1)

        def row(it, carry, rb=rb, groups=groups):
            i = 8 * rb + (7 - it if reverse else it)
            acc = {cg: ((sub8 + 8 * cg == i).astype(F32) if cg == rb else jnp.zeros((8, LANES), F32))
                   for cg in groups}
            for jb in groups:
                ablk = a_ref[i, 8 * jb:8 * jb + 8, :]
                for jj in range(8):
                    arow = ablk[jj:jj + 1, :]
                    for cg in (range(jb, nblk) if reverse else range(0, jb + 1)):
                        acc[cg] = acc[cg] - arow * t_ref[8 * jb + jj, 8 * cg:8 * cg + 8, :]
            for cg in groups:
                t_ref[i, 8 * cg:8 * cg + 8, :] = acc[cg]
            return carry

        lax.fori_loop(0, 8, row, 0)


def _gdn_scan_kernel(qf_ref, kf_ref, vf_ref, bgf_ref, gcf_ref, tf_ref, af_ref,
                     qb_ref, kb_ref, vb_ref, bgb_ref, gcb_ref, tb_ref, ab_ref,
                     of_ref, ob_ref, *s_refs):
    @pl.when(pl.program_id(1) == 0)
    def _():
        for s_ref in s_refs:
            s_ref[...] = jnp.zeros(s_ref.shape, F32)

    dirs = ((qf_ref, kf_ref, vf_ref, bgf_ref, gcf_ref, tf_ref, af_ref, of_ref, CH - 1),
            (qb_ref, kb_ref, vb_ref, bgb_ref, gcb_ref, tb_ref, ab_ref, ob_ref, 0))
    for step in range(GDN_STEP_CHUNKS):
        probs = []
        for d, (q_ref, k_ref, v_ref, bg_ref, gc_ref, t_ref, a_ref, o_ref, last) in enumerate(dirs):
            c = step if d == 0 else GDN_STEP_CHUNKS - 1 - step
            rows = slice(c * CH, (c + 1) * CH)
            for h in range(GDN_HEADS):
                dh = d * GDN_HEADS + h
                sl = slice(h * GDN_DK, (h + 1) * GDN_DK)
                k, v = k_ref[0, rows, sl], v_ref[0, rows, sl]
                beta = bg_ref[0, rows, _BETA_LANE + dh:_BETA_LANE + dh + 1]
                gc = gc_ref[0, rows, _G_LANE + dh:_G_LANE + dh + 1]
                gl = gc_ref[0, c * CH + last:c * CH + last + 1, _G_LANE + dh:_G_LANE + dh + 1]
                egc = jnp.exp(gc)
                rhs = jnp.concatenate([v * beta, k * beta * egc], axis=-1).astype(BF16)
                sol = jnp.dot(t_ref[0, c, h], rhs, preferred_element_type=F32)
                s = s_refs[dh][...]
                probs.append(dict(sl=sl, rows=rows, o_ref=o_ref, s_ref=s_refs[dh], s=s, sb=s.astype(BF16),
                                  u=sol[:, :GDN_DV], w=sol[:, GDN_DV:].astype(BF16),
                                  qd=(q_ref[0, rows, sl] * egc).astype(BF16),
                                  a=a_ref[0, c, h].astype(BF16),
                                  kd=(k * jnp.exp(gl - gc)).astype(BF16), dec=jnp.exp(gl)))
        for p in probs:
            p["vn"] = (p["u"] - jnp.dot(p["w"], p["sb"], preferred_element_type=F32)).astype(BF16)
        for p in probs:
            p["o_ref"][0, p["rows"], p["sl"]] = (jnp.dot(p["qd"], p["sb"], preferred_element_type=F32)
                                                 + jnp.dot(p["a"], p["vn"], preferred_element_type=F32))
        for p in probs:
            p["s_ref"][...] = p["s"] * p["dec"] + lax.dot_general(
                p["kd"], p["vn"], (((0,), (0,)), ((), ())), preferred_element_type=F32)


def _gdn_out_kernel(of_ref, ob_ref, z_ref, n_ref, o_ref):
    z = z_ref[...].astype(F32)
    for h in range(GDN_HEADS):
        sl = slice(h * GDN_DV, (h + 1) * GDN_DV)
        o = of_ref[:, sl] + ob_ref[:, sl]
        y = o * lax.rsqrt(jnp.mean(o * o, axis=-1, keepdims=True) + NORM_EPS) * n_ref[...]
        zh = z[:, sl]
        o_ref[:, sl] = (y * (zh * jax.nn.sigmoid(zh))).astype(BF16)


def gdn_mixer(proj3, misc3, conv_w, a_log, dt_bias, out_norm, tl=512):
    B, L, _ = proj3.shape
    T = B * L
    N = L // CH
    H = GDN_HEADS
    tl = min(tl, L)
    lane_vec = lambda p: jnp.zeros((1, LANES), F32).at[0, _G_LANE:_G_LANE + NDH].set(p.reshape(-1))
    tok = lambda w, c=0: pl.BlockSpec((1, tl, w), lambda b, i: (b, i, c))
    f32_tok = lambda w: jax.ShapeDtypeStruct((B, L, w), F32)
    qn, kn, vv, bg = pl.pallas_call(
        _gdn_prep_kernel,
        grid=(B, L // tl),
        in_specs=[*_conv3_specs(proj3, 3, 1536, tl), _full((3, 1536)), tok(LANES, 3),
                  _full((1, LANES)), _full((1, LANES))],
        out_specs=[tok(512), tok(512), tok(512), tok(LANES)],
        out_shape=[f32_tok(512), f32_tok(512), f32_tok(512), f32_tok(LANES)],
        compiler_params=_cparams("parallel", "parallel"),
        name="gdn_prep",
    )(proj3, proj3, proj3, conv_w, misc3, lane_vec(a_log), lane_vec(dt_bias))

    CB = GDN_STEP_CHUNKS
    NB = N // CB
    chunk = lambda w, c=0: pl.BlockSpec((1, CB * CH, w), lambda b, n: (b, n, c))
    mats = pl.BlockSpec((1, CB, NDH, CH, CH), lambda b, n: (b, n, 0, 0, 0))
    mat_shape = jax.ShapeDtypeStruct((B, N, NDH, CH, CH), F32)
    a, attn, gc = pl.pallas_call(
        _gdn_chunk_kernel,
        grid=(B, NB),
        in_specs=[chunk(512), chunk(512), chunk(LANES)],
        out_specs=[mats, mats, chunk(LANES)],
        out_shape=[mat_shape, jax.ShapeDtypeStruct(mat_shape.shape, BF16),
                   jax.ShapeDtypeStruct((B, L, LANES), F32)],
        compiler_params=_cparams("parallel", "parallel"),
        name="gdn_chunk",
    )(qn, kn, bg)

    P = NDH * B * N
    at = a.transpose(3, 4, 2, 0, 1).reshape(CH, CH, P)
    tt = pl.pallas_call(
        functools.partial(_gdn_solve_kernel, nblk_fwd=P // LANES // 2),
        grid=(P // LANES,),
        in_specs=[pl.BlockSpec((CH, CH, LANES), lambda p: (0, 0, p))],
        out_specs=pl.BlockSpec((CH, CH, LANES), lambda p: (0, 0, p)),
        out_shape=jax.ShapeDtypeStruct((CH, CH, P), F32),
        compiler_params=_cparams("parallel"),
        name="gdn_solve",
    )(at)
    tmat = tt.reshape(CH, CH, NDH, B, N).transpose(3, 4, 2, 0, 1).astype(BF16)

    fwd = lambda w, c=0: pl.BlockSpec((1, CB * CH, w), lambda b, n: (b, n, c))
    bwd = lambda w, c=0: pl.BlockSpec((1, CB * CH, w), lambda b, n: (b, NB - 1 - n, c))
    mf = pl.BlockSpec((1, CB, H, CH, CH), lambda b, n: (b, n, 0, 0, 0))
    mb = pl.BlockSpec((1, CB, H, CH, CH), lambda b, n: (b, NB - 1 - n, 1, 0, 0))
    o_f, o_b = pl.pallas_call(
        _gdn_scan_kernel,
        grid=(B, NB),
        in_specs=[fwd(512), fwd(512), fwd(512), fwd(LANES), fwd(LANES), mf, mf,
                  bwd(512), bwd(512), bwd(512), bwd(LANES), bwd(LANES), mb, mb],
        out_specs=[fwd(512), bwd(512)],
        out_shape=[jax.ShapeDtypeStruct((B, L, 512), F32), jax.ShapeDtypeStruct((B, L, 512), F32)],
        scratch_shapes=[pltpu.VMEM((GDN_DK, GDN_DV), F32)] * NDH,
        compiler_params=_cparams("parallel", "arbitrary"),
        name="gdn_scan",
    )(qn, kn, vv, bg, gc, tmat, attn, qn, kn, vv, bg, gc, tmat, attn)

    tm = min(1024, T)
    row = lambda c=0: pl.BlockSpec((tm, 512), lambda i: (i, c))
    return pl.pallas_call(
        _gdn_out_kernel,
        grid=(T // tm,),
        in_specs=[row(), row(), row(12), _full((1, GDN_DV))],
        out_specs=row(),
        out_shape=jax.ShapeDtypeStruct((T, 512), BF16),
        compiler_params=_cparams("parallel"),
        name="gdn_out",
    )(o_f.reshape(T, 512), o_b.reshape(T, 512), proj3.reshape(T, MAIN_W), out_norm[None, :])


def _merge_kernel(x_ref, g_ref, oh_ref, om_ref, og_ref, wb_ref, wo_ref, n_ref, o_ref):
    merged = None
    for i, b_ref in enumerate((oh_ref, om_ref, og_ref)):
        gate = jax.nn.sigmoid(g_ref[:, i * D_MODEL:(i + 1) * D_MODEL].astype(F32))
        term = gate * jnp.dot(b_ref[...].astype(BF16), wb_ref[i], preferred_element_type=F32)
        merged = term if merged is None else merged + term
    y = jnp.dot(merged.astype(BF16), wo_ref[...], preferred_element_type=F32)
    r = lax.rsqrt(jnp.mean(y * y, axis=-1, keepdims=True) + NORM_EPS)
    o_ref[...] = x_ref[...] + y * r * n_ref[...]


def merge_out(x, proj, o_hy, o_mla, o_gdn, w_branch, w_out, norm_post, tm=512):
    T = x.shape[0]
    tm = min(tm, T)
    row = lambda w: pl.BlockSpec((tm, w), lambda i: (i, 0))
    return pl.pallas_call(
        _merge_kernel,
        grid=(T // tm,),
        in_specs=[row(D_MODEL), row(N_BRANCH * D_MODEL), row(BRANCH_W), row(BRANCH_W), row(BRANCH_W),
                  _full((N_BRANCH, BRANCH_W, D_MODEL)), _full((D_MODEL, D_MODEL)), _full((1, D_MODEL))],
        out_specs=row(D_MODEL),
        out_shape=jax.ShapeDtypeStruct((T, D_MODEL), F32),
        compiler_params=_cparams("parallel"),
        name="merge_out",
    )(x, proj, o_hy, o_mla, o_gdn, w_branch, w_out, norm_post)


FFN_ROW_PARTS = 2


def _ffn_kernel(x_ref, gpre_ref, wg_ref, wu_ref, wd_ref, gpost_ref, o_ref, h_ref, acc_ref):
    j = pl.program_id(1)

    @pl.when(j == 0)
    def _():
        x = x_ref[...]
        r = lax.rsqrt(jnp.mean(x * x, axis=-1, keepdims=True) + NORM_EPS)
        h_ref[...] = (x * r * gpre_ref[...]).astype(BF16)
        acc_ref[...] = jnp.zeros(acc_ref.shape, F32)

    part_rows = h_ref.shape[0] // FFN_ROW_PARTS
    parts = [slice(r * part_rows, (r + 1) * part_rows) for r in range(FFN_ROW_PARTS)]
    gates = [jnp.dot(h_ref[rows, :], wg_ref[...], preferred_element_type=F32) for rows in parts]
    ups = [jnp.dot(h_ref[rows, :], wu_ref[...], preferred_element_type=F32) for rows in parts]
    acts = [(g * jax.nn.sigmoid(g) * u).astype(BF16) for g, u in zip(gates, ups)]
    for rows, a in zip(parts, acts):
        acc_ref[rows, :] += jnp.dot(a, wd_ref[...], preferred_element_type=F32)

    @pl.when(j == pl.num_programs(1) - 1)
    def _():
        f = acc_ref[...]
        r = lax.rsqrt(jnp.mean(f * f, axis=-1, keepdims=True) + NORM_EPS)
        o_ref[...] = x_ref[...] + f * r * gpost_ref[...]


def ffn(x, g_pre, w_gate, w_up, w_down, g_post, tm=512):
    T = x.shape[0]
    dff = w_gate.shape[1]
    tf = dff // 2
    tm = min(tm, T)
    return pl.pallas_call(
        _ffn_kernel,
        grid=(T // tm, dff // tf),
        in_specs=[pl.BlockSpec((tm, D_MODEL), lambda i, j: (i, 0)), _full((1, D_MODEL)),
                  pl.BlockSpec((D_MODEL, tf), lambda i, j: (0, j)),
                  pl.BlockSpec((D_MODEL, tf), lambda i, j: (0, j)),
                  pl.BlockSpec((tf, D_MODEL), lambda i, j: (j, 0)), _full((1, D_MODEL))],
        out_specs=pl.BlockSpec((tm, D_MODEL), lambda i, j: (i, 0)),
        out_shape=jax.ShapeDtypeStruct((T, D_MODEL), F32),
        scratch_shapes=[pltpu.VMEM((tm, D_MODEL), BF16), pltpu.VMEM((tm, D_MODEL), F32)],
        compiler_params=_cparams("parallel", "arbitrary"),
        name="ffn",
    )(x, g_pre, w_gate, w_up, w_down, g_post)


def _split_w_in(w_in):
    cols = lambda off, n: w_in[:, off:off + n]
    main = jnp.concatenate([cols(_OFF_GATE, N_BRANCH * D_MODEL), cols(_OFF_HY, 1536),
                            cols(_OFF_GQKV, 1536), cols(_OFF_GZ, 512)], axis=-1).astype(BF16)
    half = MLA_ROPE // 2
    kpe = _OFF_MKV + MLA_KV_LORA
    zeros = lambda n: jnp.zeros((D_MODEL, n), w_in.dtype)
    misc = jnp.concatenate([cols(_OFF_MQ, MLA_Q_LORA), cols(_OFF_MKV, MLA_KV_LORA),
                            cols(kpe, MLA_ROPE), cols(_OFF_GB, 8), cols(_OFF_GA, 8), zeros(48),
                            cols(kpe + half, half), cols(kpe, half), zeros(MISC_W - 576)],
                           axis=-1).astype(BF16)
    return jnp.concatenate([main, misc], axis=-1)


def trunk_layer(x, norm_mix_pre, norm_mix_post, norm_ffn_pre, norm_ffn_post, w_in,
                hy_conv_w, hy_conv_b, hy_ffn_w1, hy_ffn_b1, hy_sin_freq, hy_ffn_w2, hy_ffn_b2,
                hy_ffn_w3, hy_skip, mla_q_norm, mla_wq_b, mla_kv_norm, mla_wkv_b,
                gdn_conv_w, gdn_a_log, gdn_dt_bias, gdn_out_norm,
                w_branch, w_out, w_gate, w_up, w_down):
    B, L, D = x.shape
    T = B * L
    xt = x.reshape(T, D)
    proj, misc = norm_mm(xt, norm_mix_pre[None, :], _split_w_in(w_in), tm=2048, tn=512)
    proj3 = proj.reshape(B, L, MAIN_W)
    misc3 = misc.reshape(B, L, MISC_W)
    o_hy = hyena_mixer(proj3, hy_conv_w, hy_conv_b, hy_ffn_w1, hy_ffn_b1, hy_sin_freq,
                       hy_ffn_w2, hy_ffn_b2, hy_ffn_w3, hy_skip)
    o_mla = mla_mixer(misc3, mla_q_norm, mla_wq_b, mla_kv_norm, mla_wkv_b).reshape(T, BRANCH_W)
    o_gdn = gdn_mixer(proj3, misc3, gdn_conv_w, gdn_a_log, gdn_dt_bias, gdn_out_norm).reshape(T, BRANCH_W)
    xt = merge_out(xt, proj, o_hy, o_mla, o_gdn, w_branch.astype(BF16), w_out.astype(BF16),
                   norm_mix_post[None, :])
    xt = ffn(xt, norm_ffn_pre[None, :], w_gate.astype(BF16), w_up.astype(BF16), w_down.astype(BF16),
             norm_ffn_post[None, :])
    return xt.reshape(B, L, D)


def kernel(x_prompt, x_sample, norm_mix_pre, norm_mix_post, norm_ffn_pre, norm_ffn_post, w_in,
           hy_conv_w, hy_conv_b, hy_ffn_w1, hy_ffn_b1, hy_sin_freq, hy_ffn_w2, hy_ffn_b2,
           hy_ffn_w3, hy_skip, mla_q_norm, mla_wq_b, mla_kv_norm, mla_wkv_b,
           gdn_conv_w, gdn_a_log, gdn_dt_bias, gdn_out_norm,
           w_branch, w_out, w_gate, w_up, w_down):
    weights = (norm_mix_pre, norm_mix_post, norm_ffn_pre, norm_ffn_post, w_in,
               hy_conv_w, hy_conv_b, hy_ffn_w1, hy_ffn_b1, hy_sin_freq, hy_ffn_w2, hy_ffn_b2,
               hy_ffn_w3, hy_skip, mla_q_norm, mla_wq_b, mla_kv_norm, mla_wkv_b,
               gdn_conv_w, gdn_a_log, gdn_dt_bias, gdn_out_norm,
               w_branch, w_out, w_gate, w_up, w_down)

    def run_trunk(x):
        for layer in range(DEPTH):
            x = trunk_layer(x, *[w[layer] for w in weights])
        return x

    return (run_trunk(x_prompt), run_trunk(x_sample))
```

```python
import functools
import math

import jax
import jax.numpy as jnp
from jax import lax
from jax.experimental import pallas as pl
from jax.experimental.pallas import tpu as pltpu
import numpy as np

F32 = jnp.float32
BF16 = jnp.bfloat16

D_MODEL = 1024
DEPTH = 2
BRANCH_W = 512
N_BRANCH = 3
HY_W = BRANCH_W
HY_ORDER = 2
HY_EMB = 33
HY_FAST_PCT = 0.3
HY_SLOW_PCT = 1.5
HY_TARGET = 1e-2
MLA_HEADS = 4
MLA_NOPE = 128
MLA_ROPE = 64
MLA_V = 128
MLA_Q_LORA = 256
MLA_KV_LORA = 128
ROPE_THETA = 10000.0
GDN_HEADS = 4
GDN_DK = 128
GDN_DV = 128
GDN_CHUNK = 64
NORM_EPS = 1e-6

_OFF_HY = 0
_OFF_MQ = _OFF_HY + (HY_ORDER + 1) * HY_W
_OFF_MKV = _OFF_MQ + MLA_Q_LORA
_OFF_GQKV = _OFF_MKV + MLA_KV_LORA + MLA_ROPE
_OFF_GZ = _OFF_GQKV + GDN_HEADS * (2 * GDN_DK + GDN_DV)
_OFF_GB = _OFF_GZ + GDN_HEADS * GDN_DV
_OFF_GA = _OFF_GB + 2 * GDN_HEADS
_OFF_GATE = _OFF_GA + 2 * GDN_HEADS
_D_IN = _OFF_GATE + N_BRANCH * D_MODEL

MAIN_W = 3072 + 1536 + 1536 + 512
MISC_W = 512

LANES = 128
VMEM_LIMIT_BYTES = 56 * 1024 * 1024


def _cparams(*sem):
    return pltpu.CompilerParams(dimension_semantics=sem, vmem_limit_bytes=VMEM_LIMIT_BYTES)


def _full(shape):
    nd = len(shape)
    return pl.BlockSpec(shape, lambda *_: (0,) * nd)


NORM_ROW_PARTS = 4


def _norm_mm_kernel(x_ref, g_ref, w_ref, main_ref, misc_ref, h_ref, *, n_main):
    j = pl.program_id(1)

    @pl.when(j == 0)
    def _():
        part = x_ref.shape[0] // NORM_ROW_PARTS
        for r in range(NORM_ROW_PARTS):
            rows = slice(r * part, (r + 1) * part)
            x = x_ref[rows, :]
            inv = lax.rsqrt(jnp.mean(x * x, axis=-1, keepdims=True) + NORM_EPS)
            h = (x * inv * g_ref[...]).astype(BF16)
            h_ref[rows, :] = h
            main_ref[rows, :] = jnp.dot(h, w_ref[...], preferred_element_type=F32).astype(main_ref.dtype)

    @pl.when((j > 0) & (j < n_main))
    def _():
        main_ref[...] = jnp.dot(h_ref[...], w_ref[...], preferred_element_type=F32).astype(main_ref.dtype)

    @pl.when(j >= n_main)
    def _():
        misc_ref[...] = jnp.dot(h_ref[...], w_ref[...], preferred_element_type=F32)


def norm_mm(x, g, w, tm, tn):
    T, D = x.shape
    tm = min(tm, T)
    n_main = MAIN_W // tn
    n_misc = MISC_W // tn
    return pl.pallas_call(
        functools.partial(_norm_mm_kernel, n_main=n_main),
        grid=(T // tm, n_main + n_misc),
        in_specs=[pl.BlockSpec((tm, D), lambda i, j: (i, 0)),
                  pl.BlockSpec((1, D), lambda i, j: (0, 0)),
                  pl.BlockSpec((D, tn), lambda i, j: (0, j))],
        out_specs=[pl.BlockSpec((tm, tn), lambda i, j: (i, jnp.minimum(j, n_main - 1))),
                   pl.BlockSpec((tm, tn), lambda i, j: (i, jnp.maximum(j - n_main, 0)))],
        out_shape=[jax.ShapeDtypeStruct((T, MAIN_W), BF16), jax.ShapeDtypeStruct((T, MISC_W), F32)],
        scratch_shapes=[pltpu.VMEM((tm, D), BF16)],
        compiler_params=_cparams("parallel", "arbitrary"),
        name="norm_mm",
    )(x, g, w)


def _conv3_rows(x_ref, p_ref, n_ref, w_ref):
    i = pl.program_id(1)
    last = pl.num_programs(1) - 1
    x = x_ref[0].astype(F32)
    tl = x.shape[0]
    halo = p_ref.shape[1]
    prev_row = jnp.where(i > 0, p_ref[0, halo - 1:halo, :].astype(F32), 0.0)
    next_row = jnp.where(i < last, n_ref[0, 0:1, :].astype(F32), 0.0)
    rows = lax.broadcasted_iota(jnp.int32, x.shape, 0)
    x_dn = jnp.where(rows == 0, prev_row, pltpu.roll(x, 1, axis=0))
    x_up = jnp.where(rows == tl - 1, next_row, pltpu.roll(x, tl - 1, axis=0))
    return x_dn * w_ref[0:1, :] + x * w_ref[1:2, :] + x_up * w_ref[2:3, :]


def _conv3_specs(x, col_blk, cw, tl):
    halo = 8 * (4 // x.dtype.itemsize)
    nsub = tl // halo
    lastblk = x.shape[1] // halo - 1
    return [pl.BlockSpec((1, tl, cw), lambda b_, i: (b_, i, col_blk)),
            pl.BlockSpec((1, halo, cw), lambda b_, i: (b_, jnp.maximum(i * nsub - 1, 0), col_blk)),
            pl.BlockSpec((1, halo, cw), lambda b_, i: (b_, jnp.minimum((i + 1) * nsub, lastblk), col_blk))]


def _dwconv_kernel(x_ref, p_ref, n_ref, w_ref, b_ref, o_ref):
    o_ref[0] = (_conv3_rows(x_ref, p_ref, n_ref, w_ref) + b_ref[...]).astype(o_ref.dtype)


def dwconv3(x, col_blk, cw, w, b, out_dtype, tl=512):
    B, L, _ = x.shape
    tl = min(tl, L)
    return pl.pallas_call(
        _dwconv_kernel,
        grid=(B, L // tl),
        in_specs=[*_conv3_specs(x, col_blk, cw, tl), _full((3, cw)), _full((1, cw))],
        out_specs=pl.BlockSpec((1, tl, cw), lambda b_, i: (b_, i, 0)),
        out_shape=jax.ShapeDtypeStruct((B, L, cw), out_dtype),
        compiler_params=_cparams("parallel", "parallel"),
        name="dwconv3",
    )(x, x, x, w, b)


def _tile_lanes(t, c):
    return t if c == LANES else jnp.concatenate([t] * (c // LANES), axis=-1)


def _fft_s1_kernel(f_ref, x_ref, twr_ref, twi_ref, o_ref, *, nb, n1):
    c = x_ref.shape[-1]
    for t in range(nb):
        a = jnp.dot(f_ref[...], x_ref[0, t], preferred_element_type=F32)
        ar, ai = a[:n1], a[n1:]
        twr = _tile_lanes(twr_ref[t], c)
        twi = _tile_lanes(twi_ref[t], c)
        o_ref[0, t, :n1, :] = (ar * twr + ai * twi).astype(BF16)
        o_ref[0, t, n1:, :] = (ai * twr - ar * twi).astype(BF16)


def _fft_s2_kernel(f_ref, finv_ref, b_ref, kr_ref, ki_ref, o_ref, *, nb, n2):
    for t in range(nb):
        x = jnp.dot(f_ref[t], b_ref[0, t], preferred_element_type=F32)
        xr, xi = x[:n2], x[n2:]
        kr, ki = kr_ref[t].astype(F32), ki_ref[t].astype(F32)
        y = jnp.concatenate([xr * kr - xi * ki, xr * ki + xi * kr], axis=0).astype(BF16)
        o_ref[0, t] = jnp.dot(finv_ref[t], y, preferred_element_type=F32).astype(BF16)


def _fft_mm_kernel(g_ref, d_ref, o_ref, *, nb):
    for t in range(nb):
        o_ref[0, t] = jnp.dot(g_ref[...], d_ref[0, t], preferred_element_type=F32).astype(o_ref.dtype)


def _dft_tables(L):
    N = 2 * L
    n2 = 128 if N >= 4096 else 16
    n1 = N // n2
    n1h = n1 // 2
    k1 = np.arange(n1)[:, None]
    m1 = np.arange(n1h)[None, :]
    th = 2.0 * np.pi * ((k1 * m1) % n1) / n1
    c, s = np.cos(th), np.sin(th)
    f1 = np.block([[c, s], [-s, c]])
    g1 = np.block([[c.T, -s.T], [s.T, c.T]]) / N
    as_bf16 = lambda m: jnp.asarray(m, dtype=F32).astype(BF16)
    k = (jnp.arange(n1, dtype=jnp.int32)[:, None, None]
         + n1 * jnp.arange(n2, dtype=jnp.int32)[None, :, None])
    ang = ((k * jnp.arange(n2, dtype=jnp.int32)[None, None, :]) % N).astype(F32) * (2.0 * math.pi / N)
    c2, s2 = jnp.cos(ang), jnp.sin(ang)
    f2 = jnp.concatenate([jnp.concatenate([c2, s2], axis=2),
                          jnp.concatenate([-s2, c2], axis=2)], axis=1).astype(BF16)
    return n1, n2, as_bf16(f1), as_bf16(g1), f2, f2.transpose(0, 2, 1)


def _twiddles(n1, n2):
    N = n1 * n2
    prod = (jnp.arange(n2, dtype=jnp.int32)[:, None] * jnp.arange(n1, dtype=jnp.int32)[None, :]) % N
    ang = prod.astype(F32) * (2.0 * math.pi / N)
    shape = (n2, n1, LANES)
    return (jnp.broadcast_to(jnp.cos(ang)[:, :, None], shape),
            jnp.broadcast_to(jnp.sin(ang)[:, :, None], shape))


def fft_long_conv(z, kspec, order, tabs):
    B, L, C = z.shape
    n1, n2, f1, g1, f2, f2inv = tabs
    n1h = n1 // 2
    P = B // 2
    nb1 = min(8, n2)
    nb2 = min(8, n1)
    zt = z.astype(BF16).reshape(P, 2, n1h, n2, C).transpose(0, 3, 1, 2, 4).reshape(P, n2, 2 * n1h, C)
    a = pl.pallas_call(
        functools.partial(_fft_mm_kernel, nb=nb1),
        grid=(n2 // nb1, P),
        in_specs=[_full((2 * n1, 2 * n1h)),
                  pl.BlockSpec((1, nb1, 2 * n1h, C), lambda j, p: (p, j, 0, 0))],
        out_specs=pl.BlockSpec((1, nb1, 2 * n1, C), lambda j, p: (p, j, 0, 0)),
        out_shape=jax.ShapeDtypeStruct((P, n2, 2 * n1, C), BF16),
        compiler_params=_cparams("parallel", "parallel"),
        name="fft_s1",
    )(f1, zt)
    bt = a.reshape(P, n2, 2, n1, C).transpose(0, 3, 2, 1, 4).reshape(P, n1, 2 * n2, C)
    d = pl.pallas_call(
        functools.partial(_fft_s2_kernel, nb=nb2, n2=n2),
        grid=(n1 // nb2, P),
        in_specs=[pl.BlockSpec((nb2, 2 * n2, 2 * n2), lambda j, p: (j, 0, 0)),
                  pl.BlockSpec((nb2, 2 * n2, 2 * n2), lambda j, p: (j, 0, 0)),
                  pl.BlockSpec((1, nb2, 2 * n2, C), lambda j, p: (p, j, 0, 0)),
                  pl.BlockSpec((nb2, n2, C), lambda j, p: (j, 0, order)),
                  pl.BlockSpec((nb2, n2, C), lambda j, p: (j, 1, order))],
        out_specs=pl.BlockSpec((1, nb2, 2 * n2, C), lambda j, p: (p, j, 0, 0)),
        out_shape=jax.ShapeDtypeStruct((P, n1, 2 * n2, C), BF16),
        compiler_params=_cparams("parallel", "parallel"),
        name="fft_s2",
    )(f2, f2inv, bt, kspec, kspec)
    dt = d.reshape(P, n1, 2, n2, C).transpose(0, 3, 2, 1, 4).reshape(P, n2, 2 * n1, C)
    y = pl.pallas_call(
        functools.partial(_fft_mm_kernel, nb=nb1),
        grid=(n2 // nb1, P),
        in_specs=[_full((2 * n1h, 2 * n1)),
                  pl.BlockSpec((1, nb1, 2 * n1, C), lambda j, p: (p, j, 0, 0))],
        out_specs=pl.BlockSpec((1, nb1, 2 * n1h, C), lambda j, p: (p, j, 0, 0)),
        out_shape=jax.ShapeDtypeStruct((P, n2, 2 * n1h, C), BF16),
        compiler_params=_cparams("parallel", "parallel"),
        name="fft_s3",
    )(g1, dt)
    return y.reshape(P, n2, 2, n1h, C).transpose(0, 2, 3, 1, 4).reshape(B, L, C)


def _hy_gate_kernel(g_ref, c_ref, z_ref, s_ref, o_ref):
    o_ref[...] = (g_ref[...].astype(F32) * (c_ref[...].astype(F32) + s_ref[...] * z_ref[...].astype(F32))
                  ).astype(o_ref.dtype)


def hy_gate(u, gate_blk, conv, zsrc, z_blk, skip, out_dtype, tl=1024):
    T, C = conv.shape
    tl = min(tl, T)
    return pl.pallas_call(
        _hy_gate_kernel,
        grid=(T // tl,),
        in_specs=[pl.BlockSpec((tl, C), lambda i: (i, gate_blk)),
                  pl.BlockSpec((tl, C), lambda i: (i, 0)),
                  pl.BlockSpec((tl, C), lambda i: (i, z_blk)),
                  _full((1, C))],
        out_specs=pl.BlockSpec((tl, C), lambda i: (i, 0)),
        out_shape=jax.ShapeDtypeStruct((T, C), out_dtype),
        compiler_params=_cparams("parallel"),
        name="hy_gate",
    )(u, conv, zsrc, skip)


def hyena_filter_spectrum(L, n1, n2, tw1r, tw1i, w1, b1, freq, w2, b2, w3):
    t = jnp.linspace(0.0, 1.0, L, dtype=F32)[:, None]
    bands = (HY_EMB - 1) // 2
    wpos = (2.0 * math.pi / L) * jnp.arange(L, dtype=F32)[:, None]
    fr = jnp.linspace(1e-4, bands - 1, bands, dtype=F32)[None, :]
    feats = jnp.concatenate([t, jnp.cos(fr * wpos), -jnp.sin(fr * wpos)], axis=-1)
    deltas = jnp.abs(jnp.linspace(math.log(HY_TARGET) / HY_SLOW_PCT,
                                  math.log(HY_TARGET) / HY_FAST_PCT, HY_W, dtype=F32))
    feats2 = jnp.concatenate([feats, feats[0:1], feats[:0:-1]], axis=0)
    feats2 = jnp.pad(feats2, ((0, 0), (0, LANES - HY_EMB)))
    w1p = jnp.pad(w1, ((0, LANES - HY_EMB), (0, 0)))
    hid = w3.shape[0]
    oc = HY_ORDER * HY_W
    w3d = w3.reshape(hid, HY_ORDER, 2, HY_W).transpose(2, 0, 1, 3).reshape(2, hid, oc)
    tl = min(512, L)
    nh = L // tl
    kern, asum = pl.pallas_call(
        functools.partial(_filter_taps_kernel, zero_tile=nh),
        grid=(2 * nh,),
        in_specs=[pl.BlockSpec((tl, LANES), lambda i: (i, 0)), _full((LANES, hid)), _full((1, hid)),
                  _full((1, hid)), _full((hid, hid)), _full((1, hid)),
                  pl.BlockSpec((1, hid, oc), lambda i: (i // nh, 0, 0)), _full((1, oc))],
        out_specs=[pl.BlockSpec((tl, oc), lambda i: (i, 0)), _full((1, oc))],
        out_shape=[jax.ShapeDtypeStruct((2 * L, oc), F32), jax.ShapeDtypeStruct((1, oc), F32)],
        compiler_params=_cparams("arbitrary"),
        name="filter_taps",
    )(feats2, w1p, b1, freq, w2, b2, w3d, jnp.tile(deltas, HY_ORDER)[None, :])
    return filter_dft(kern, asum, n1, n2, tw1r, tw1i)


def _filter_taps_kernel(f_ref, w1_ref, b1_ref, fq_ref, w2_ref, b2_ref, w3_ref, dl_ref, k_ref, s_ref, *,
                        zero_tile):
    i = pl.program_id(0)
    hp = lax.Precision.HIGHEST
    f = f_ref[...]
    h = jnp.sin(fq_ref[...] * (jnp.dot(f, w1_ref[...], precision=hp, preferred_element_type=F32)
                               + b1_ref[...]))
    h = jnp.sin(fq_ref[...] * (jnp.dot(h, w2_ref[...], precision=hp, preferred_element_type=F32)
                               + b2_ref[...]))
    taps = jnp.dot(h, w3_ref[0], precision=hp, preferred_element_type=F32)
    taps = taps * jnp.exp(-f[:, 0:1] * dl_ref[...])
    row = lax.broadcasted_iota(jnp.int32, taps.shape, 0)
    taps = jnp.where((i == zero_tile) & (row == 0), 0.0, taps)
    k_ref[...] = taps

    @pl.when(i == 0)
    def _():
        s_ref[...] = jnp.zeros(s_ref.shape, F32)

    s_ref[...] += jnp.sum(jnp.abs(taps), axis=0, keepdims=True)


def _spec_s2_kernel(f_ref, b_ref, s_ref, o_ref, *, nb):
    for t in range(nb):
        o_ref[t] = (jnp.dot(f_ref[...], b_ref[0, t], preferred_element_type=F32) / s_ref[...]).astype(BF16)


def filter_dft(kern, asum, n1, n2, tw1r, tw1i):
    N, C = kern.shape
    ct = 512
    k1 = np.arange(n1)
    th = 2.0 * np.pi * ((k1[:, None] * k1[None, :]) % n1) / n1
    f1 = np.concatenate([np.cos(th), -np.sin(th)], axis=0)
    a2 = np.arange(n2)
    ph = 2.0 * np.pi * ((a2[:, None] * a2[None, :]) % n2) / n2
    f2 = np.block([[np.cos(ph), np.sin(ph)], [-np.sin(ph), np.cos(ph)]])

    as_bf16 = lambda m: jnp.asarray(m, dtype=F32).astype(BF16)
    nb1 = min(8, n2)
    nb2 = min(8, n1)
    xt = kern.astype(BF16).reshape(n1, n2, C).transpose(1, 0, 2)[None]
    a = pl.pallas_call(
        functools.partial(_fft_s1_kernel, nb=nb1, n1=n1),
        grid=(n2 // nb1, C // ct),
        in_specs=[_full((2 * n1, n1)),
                  pl.BlockSpec((1, nb1, n1, ct), lambda j, c: (0, j, 0, c)),
                  pl.BlockSpec((nb1, n1, LANES), lambda j, c: (j, 0, 0)),
                  pl.BlockSpec((nb1, n1, LANES), lambda j, c: (j, 0, 0))],
        out_specs=pl.BlockSpec((1, nb1, 2 * n1, ct), lambda j, c: (0, j, 0, c)),
        out_shape=jax.ShapeDtypeStruct((1, n2, 2 * n1, C), BF16),
        compiler_params=_cparams("parallel", "parallel"),
        name="spec_s1",
    )(as_bf16(f1), xt, tw1r, tw1i)
    bt = a.reshape(n2, 2, n1, C).transpose(2, 1, 0, 3).reshape(1, n1, 2 * n2, C)
    return pl.pallas_call(
        functools.partial(_spec_s2_kernel, nb=nb2),
        grid=(n1 // nb2, C // ct),
        in_specs=[_full((2 * n2, 2 * n2)),
                  pl.BlockSpec((1, nb2, 2 * n2, ct), lambda j, c: (0, j, 0, c)),
                  pl.BlockSpec((1, ct), lambda j, c: (0, c))],
        out_specs=pl.BlockSpec((nb2, 2 * n2, ct), lambda j, c: (j, 0, c)),
        out_shape=jax.ShapeDtypeStruct((n1, 2 * n2, C), BF16),
        compiler_params=_cparams("parallel", "parallel"),
        name="spec_s2",
    )(as_bf16(f2), bt, asum)


def hyena_mixer(proj3, conv_w, conv_b, w1, b1, freq, w2, b2, w3, skip):
    B, L, _ = proj3.shape
    T = B * L
    u = dwconv3(proj3, 2, 1536, conv_w, conv_b[None, :], out_dtype=BF16).reshape(T, 1536)
    n1, n2, f1, g1, f2, f2inv = _dft_tables(L)
    tw1r, tw1i = _twiddles(n1, n2)
    tabs = (n1, n2, f1, g1, f2, f2inv)
    kspec = hyena_filter_spectrum(L, n1, n2, tw1r, tw1i, w1, b1[None, :], freq[None, :], w2, b2[None, :], w3)
    conv = fft_long_conv(u[:, 1024:1536].reshape(B, L, HY_W), kspec, 0, tabs).reshape(T, HY_W)
    z1 = hy_gate(u, 0, conv, u, 2, skip[0:1], BF16)
    conv = fft_long_conv(z1.reshape(B, L, HY_W), kspec, 1, tabs).reshape(T, HY_W)
    return hy_gate(u, 1, conv, z1, 0, skip[1:2], BF16)


def _mla_proj_kernel(m_ref, qg_ref, kg_ref, wq_ref, wkv_ref, cq_ref, sq_ref, ck_ref, sk_ref,
                     q_ref, k_ref, v_ref):
    m = m_ref[0]
    ql = m[:, 0:256]
    r = lax.rsqrt(jnp.mean(ql * ql, axis=-1, keepdims=True) + NORM_EPS)
    q = jnp.dot((ql * r * qg_ref[...]).astype(BF16), wq_ref[...], preferred_element_type=F32)
    ckv = m[:, 256:384]
    r = lax.rsqrt(jnp.mean(ckv * ckv, axis=-1, keepdims=True) + NORM_EPS)
    kv = jnp.dot((ckv * r * kg_ref[...]).astype(BF16), wkv_ref[...], preferred_element_type=F32)
    q_pe = ((q[:, 512:1024] * cq_ref[...] + q[:, 1024:1536] * sq_ref[...]) * ATTN_Q_SCALE).astype(BF16)
    kp = m[:, 384:512]
    half = MLA_ROPE // 2
    lane = lax.broadcasted_iota(jnp.int32, kp.shape, 1)
    kp_swapped = jnp.where(lane < half, pltpu.roll(kp, LANES - half, axis=1), pltpu.roll(kp, half, axis=1))
    k_pe = (kp * ck_ref[...] + kp_swapped * sk_ref[...]).astype(BF16)
    for h in range(MLA_HEADS):
        sl = slice(h * LANES, (h + 1) * LANES)
        q_ref[0, :, 2 * h * LANES:(2 * h + 1) * LANES] = (q[:, sl] * ATTN_Q_SCALE).astype(BF16)
        q_ref[0, :, (2 * h + 1) * LANES:(2 * h + 2) * LANES] = q_pe[:, sl]
        k_ref[0, :, 2 * h * LANES:(2 * h + 1) * LANES] = kv[:, sl].astype(BF16)
        k_ref[0, :, (2 * h + 1) * LANES:(2 * h + 2) * LANES] = k_pe
        v_ref[0, :, 2 * h * LANES:(2 * h + 1) * LANES] = kv[:, 512 + h * MLA_V:512 + (h + 1) * MLA_V].astype(BF16)
        v_ref[0, :, (2 * h + 1) * LANES:(2 * h + 2) * LANES] = jnp.ones((m.shape[0], LANES), BF16)


ATTN_Q_SCALE = (MLA_NOPE + MLA_ROPE) ** -0.5 * math.log2(math.e)
ATTN_ROW_PARTS = 2


def _attn_kernel(q_ref, k_ref, v_ref, o_ref, *scratch):
    ki = pl.program_id(2)
    dqk = 2 * LANES
    m_refs, acc_refs = scratch[:MLA_HEADS], scratch[MLA_HEADS:]

    @pl.when(ki == 0)
    def _():
        for m_ref, acc_ref in zip(m_refs, acc_refs):
            m_ref[...] = jnp.full(m_ref.shape, -jnp.inf, F32)
            acc_ref[...] = jnp.zeros(acc_ref.shape, F32)

    part = q_ref.shape[1] // ATTN_ROW_PARTS
    chains = [(h, slice(h * dqk, (h + 1) * dqk), slice(r * part, (r + 1) * part))
              for h in range(MLA_HEADS) for r in range(ATTN_ROW_PARTS)]
    scores = [lax.dot_general(q_ref[0, rows, sl], k_ref[0, :, sl], (((1,), (1,)), ((), ())),
                              preferred_element_type=F32) for _, sl, rows in chains]
    weights = []
    for (h, sl, rows), s in zip(chains, scores):
        m_prev = m_refs[h][rows, :]
        m_new = jnp.maximum(m_prev, jnp.max(s, axis=-1, keepdims=True))
        m_refs[h][rows, :] = m_new
        weights.append((jnp.exp2(m_prev - m_new), jnp.exp2(s - m_new).astype(BF16)))
    for (h, sl, rows), (alpha, p) in zip(chains, weights):
        acc_refs[h][rows, :] = (alpha * acc_refs[h][rows, :]
                                + jnp.dot(p, v_ref[0, :, sl], preferred_element_type=F32))

    @pl.when(ki == pl.num_programs(2) - 1)
    def _():
        for h, acc_ref in enumerate(acc_refs):
            o_ref[0, :, h * MLA_V:(h + 1) * MLA_V] = (acc_ref[:, :MLA_V] / acc_ref[:, MLA_V:]).astype(BF16)


def _rope_lane_tables(L):
    half = MLA_ROPE // 2
    inv = ROPE_THETA ** (-jnp.arange(half, dtype=F32) / half)
    ang = jnp.arange(L, dtype=F32)[:, None] * inv[None, :]
    cos, sin = jnp.cos(ang), jnp.sin(ang)
    zeros = jnp.zeros((L, LANES - MLA_ROPE), F32)
    ck = jnp.concatenate([cos, cos, zeros], axis=-1)
    sk = jnp.concatenate([-sin, sin, zeros], axis=-1)
    return jnp.tile(ck, (1, MLA_HEADS)), jnp.tile(sk, (1, MLA_HEADS)), ck, sk


def mla_mixer(misc3, q_norm, wq_b, kv_norm, wkv_b, tl=512, tq=1024, tk=1024):
    B, L, _ = misc3.shape
    H = MLA_HEADS
    tl, tq, tk = min(tl, L), min(tq, L), min(tk, L)
    half = MLA_ROPE // 2
    wq = wq_b.reshape(MLA_Q_LORA, H, MLA_NOPE + MLA_ROPE)
    x1 = wq[:, :, MLA_NOPE:MLA_NOPE + half]
    x2 = wq[:, :, MLA_NOPE + half:]
    zpad = jnp.zeros((MLA_Q_LORA, H, LANES - MLA_ROPE), wq.dtype)
    wq = jnp.concatenate([wq[:, :, :MLA_NOPE].reshape(MLA_Q_LORA, -1),
                          jnp.concatenate([x1, x2, zpad], axis=-1).reshape(MLA_Q_LORA, -1),
                          jnp.concatenate([x2, x1, zpad], axis=-1).reshape(MLA_Q_LORA, -1)],
                         axis=-1).astype(BF16)
    wkv = wkv_b.reshape(MLA_KV_LORA, H, MLA_NOPE + MLA_V)
    wkv = jnp.concatenate([wkv[:, :, :MLA_NOPE].reshape(MLA_KV_LORA, -1),
                           wkv[:, :, MLA_NOPE:].reshape(MLA_KV_LORA, -1)], axis=-1).astype(BF16)
    cq, sq, ck, sk = _rope_lane_tables(L)
    tok = lambda w: pl.BlockSpec((1, tl, w), lambda b, i: (b, i, 0))
    pos = lambda w: pl.BlockSpec((tl, w), lambda b, i: (i, 0))
    sds = lambda w: jax.ShapeDtypeStruct((B, L, w), BF16)
    dqk = 2 * LANES
    nk = L // tk
    q, k, v = pl.pallas_call(
        _mla_proj_kernel,
        grid=(B, L // tl),
        in_specs=[tok(MISC_W), _full((1, MLA_Q_LORA)), _full((1, MLA_KV_LORA)),
                  _full(wq.shape), _full(wkv.shape), pos(H * LANES), pos(H * LANES), pos(LANES), pos(LANES)],
        out_specs=[tok(H * dqk), tok(H * dqk), tok(H * dqk)],
        out_shape=[sds(H * dqk), sds(H * dqk), sds(H * dqk)],
        compiler_params=_cparams("parallel", "parallel"),
        name="mla_proj",
    )(misc3, q_norm[None, :], kv_norm[None, :], wq, wkv, cq, sq, ck, sk)
    return pl.pallas_call(
        _attn_kernel,
        grid=(B, L // tq, nk),
        in_specs=[pl.BlockSpec((1, tq, H * dqk), lambda b, i, j: (b, i, 0)),
                  pl.BlockSpec((1, tk, H * dqk), lambda b, i, j: (b, j, 0)),
                  pl.BlockSpec((1, tk, H * dqk), lambda b, i, j: (b, j, 0))],
        out_specs=pl.BlockSpec((1, tq, H * MLA_V), lambda b, i, j: (b, i, 0)),
        out_shape=jax.ShapeDtypeStruct((B, L, H * MLA_V), BF16),
        scratch_shapes=([pltpu.VMEM((tq, 1), F32)] * H + [pltpu.VMEM((tq, 2 * MLA_V), F32)] * H),
        compiler_params=_cparams("parallel", "parallel", "arbitrary"),
        name="mla_attn",
    )(q, k, v)


_BETA_LANE = 64
_G_LANE = 72
CH = GDN_CHUNK
NDH = 2 * GDN_HEADS
GDN_STEP_CHUNKS = 8


def _gdn_prep_kernel(x_ref, p_ref, n_ref, w_ref, m_ref, alog_ref, dtb_ref, qn_ref, kn_ref, v_ref, bg_ref):
    c = _conv3_rows(x_ref, p_ref, n_ref, w_ref)
    c = c * jax.nn.sigmoid(c)
    nqk = GDN_HEADS * GDN_DK
    for h in range(GDN_HEADS):
        sl = slice(h * GDN_DK, (h + 1) * GDN_DK)
        qh = c[:, sl]
        qn_ref[0, :, sl] = (qh * lax.rsqrt(jnp.sum(qh * qh, axis=-1, keepdims=True) + NORM_EPS)
                            * (GDN_DK ** -0.5))
        kh = c[:, nqk + h * GDN_DK:nqk + (h + 1) * GDN_DK]
        kn_ref[0, :, sl] = kh * lax.rsqrt(jnp.sum(kh * kh, axis=-1, keepdims=True) + NORM_EPS)
    v_ref[0] = c[:, 2 * nqk:]
    m = m_ref[0]
    lane = lax.broadcasted_iota(jnp.int32, m.shape, 1)
    beta = jax.nn.sigmoid(m)
    x = m + dtb_ref[...]
    softplus = jnp.maximum(x, 0.0) + jnp.log1p(jnp.exp(-jnp.abs(x)))
    g = -jnp.exp(alog_ref[...]) * softplus
    is_beta = (lane >= _BETA_LANE) & (lane < _BETA_LANE + NDH)
    is_g = (lane >= _G_LANE) & (lane < _G_LANE + NDH)
    bg_ref[0] = jnp.where(is_beta, beta, jnp.where(is_g, g, 0.0))


def _split3(x):
    hi = x.astype(BF16)
    r = x - hi.astype(F32)
    mid = r.astype(BF16)
    lo = (r - mid.astype(F32)).astype(BF16)
    return hi, mid, lo


def _dot_nt(a, b):
    return lax.dot_general(a, b, (((1,), (1,)), ((), ())), preferred_element_type=F32)


def _gdn_chunk_kernel(q_ref, k_ref, bg_ref, a_ref, attn_ref, gc_ref):
    lane = lax.broadcasted_iota(jnp.int32, (CH, LANES), 1)
    is_g = (lane >= _G_LANE) & (lane < _G_LANE + NDH)
    ri = lax.broadcasted_iota(jnp.int32, (CH, CH), 0)
    ci = lax.broadcasted_iota(jnp.int32, (CH, CH), 1)
    lower = ri >= ci
    upper = ri <= ci
    tril = lower.astype(BF16)
    triu = upper.astype(BF16)
    for c in range(GDN_STEP_CHUNKS):
        rows = slice(c * CH, (c + 1) * CH)
        bg = bg_ref[0, rows, :]
        pieces = _split3(jnp.where(is_g, bg, 0.0))
        pre = sum(jnp.dot(tril, p, preferred_element_type=F32) for p in pieces)
        suf = sum(jnp.dot(triu, p, preferred_element_type=F32) for p in pieces)
        gc = jnp.where(lane >= _G_LANE + GDN_HEADS, suf, pre)
        gc_ref[0, rows, :] = gc
        gct = gc.T
        for d in range(2):
            causal = lower if d == 0 else upper
            strict = (ri > ci) if d == 0 else (ri < ci)
            for h in range(GDN_HEADS):
                dh = d * GDN_HEADS + h
                sl = slice(h * GDN_DK, (h + 1) * GDN_DK)
                kh = k_ref[0, rows, sl]
                kb = (kh * bg[:, _BETA_LANE + dh:_BETA_LANE + dh + 1]).astype(BF16)
                khb = kh.astype(BF16)
                diff = gc[:, _G_LANE + dh:_G_LANE + dh + 1] - gct[_G_LANE + dh:_G_LANE + dh + 1, :]
                dec = jnp.exp(jnp.where(causal, diff, -jnp.inf))
                a_ref[0, c, dh] = jnp.where(strict, _dot_nt(kb, khb) * dec, 0.0)
                attn_ref[0, c, dh] = (_dot_nt(q_ref[0, rows, sl].astype(BF16), khb) * dec).astype(BF16)


def _gdn_solve_kernel(a_ref, t_ref, *, nblk_fwd):
    bwd = pl.program_id(0) >= nblk_fwd
    t_ref[...] = jnp.zeros(t_ref.shape, F32)

    @pl.when(jnp.logical_not(bwd))
    def _():
        _solve_triangular(a_ref, t_ref, reverse=False)

    @pl.when(bwd)
    def _():
        _solve_triangular(a_ref, t_ref, reverse=True)


def _solve_triangular(a_ref, t_ref, *, reverse):
    nblk = CH // 8
    sub8 = lax.broadcasted_iota(jnp.int32, (8, LANES), 0)
    for phase in range(nblk):
        rb = nblk - 1 - phase if reverse else phase
        groups = range(rb, nblk) if reverse else range(0, rb + 1)

        def row(it, carry, rb=rb, groups=groups):
            i = 8 * rb + (7 - it if reverse else it)
            acc = {cg: ((sub8 + 8 * cg == i).astype(F32) if cg == rb else jnp.zeros((8, LANES), F32))
                   for cg in groups}
            for jb in groups:
                ablk = a_ref[i, 8 * jb:8 * jb + 8, :]
                for jj in range(8):
                    arow = ablk[jj:jj + 1, :]
                    for cg in (range(jb, nblk) if reverse else range(0, jb + 1)):
                        acc[cg] = acc[cg] - arow * t_ref[8 * jb + jj, 8 * cg:8 * cg + 8, :]
            for cg in groups:
                t_ref[i, 8 * cg:8 * cg + 8, :] = acc[cg]
            return carry

        lax.fori_loop(0, 8, row, 0)


def _gdn_scan_kernel(qf_ref, kf_ref, vf_ref, bgf_ref, gcf_ref, tf_ref, af_ref,
                     qb_ref, kb_ref, vb_ref, bgb_ref, gcb_ref, tb_ref, ab_ref,
                     of_ref, ob_ref, *s_refs):
    @pl.when(pl.program_id(1) == 0)
    def _():
        for s_ref in s_refs:
            s_ref[...] = jnp.zeros(s_ref.shape, F32)

    dirs = ((qf_ref, kf_ref, vf_ref, bgf_ref, gcf_ref, tf_ref, af_ref, of_ref, CH - 1),
            (qb_ref, kb_ref, vb_ref, bgb_ref, gcb_ref, tb_ref, ab_ref, ob_ref, 0))
    for step in range(GDN_STEP_CHUNKS):
        probs = []
        for d, (q_ref, k_ref, v_ref, bg_ref, gc_ref, t_ref, a_ref, o_ref, last) in enumerate(dirs):
            c = step if d == 0 else GDN_STEP_CHUNKS - 1 - step
            rows = slice(c * CH, (c + 1) * CH)
            for h in range(GDN_HEADS):
                dh = d * GDN_HEADS + h
                sl = slice(h * GDN_DK, (h + 1) * GDN_DK)
                k, v = k_ref[0, rows, sl], v_ref[0, rows, sl]
                beta = bg_ref[0, rows, _BETA_LANE + dh:_BETA_LANE + dh + 1]
                gc = gc_ref[0, rows, _G_LANE + dh:_G_LANE + dh + 1]
                gl = gc_ref[0, c * CH + last:c * CH + last + 1, _G_LANE + dh:_G_LANE + dh + 1]
                egc = jnp.exp(gc)
                rhs = jnp.concatenate([v * beta, k * beta * egc], axis=-1).astype(BF16)
                sol = jnp.dot(t_ref[0, c, h], rhs, preferred_element_type=F32)
                s = s_refs[dh][...]
                probs.append(dict(sl=sl, rows=rows, o_ref=o_ref, s_ref=s_refs[dh], s=s, sb=s.astype(BF16),
                                  u=sol[:, :GDN_DV], w=sol[:, GDN_DV:].astype(BF16),
                                  qd=(q_ref[0, rows, sl] * egc).astype(BF16),
                                  a=a_ref[0, c, h].astype(BF16),
                                  kd=(k * jnp.exp(gl - gc)).astype(BF16), dec=jnp.exp(gl)))
        for p in probs:
            p["vn"] = (p["u"] - jnp.dot(p["w"], p["sb"], preferred_element_type=F32)).astype(BF16)
        for p in probs:
            p["o_ref"][0, p["rows"], p["sl"]] = (jnp.dot(p["qd"], p["sb"], preferred_element_type=F32)
                                                 + jnp.dot(p["a"], p["vn"], preferred_element_type=F32))
        for p in probs:
            p["s_ref"][...] = p["s"] * p["dec"] + lax.dot_general(
                p["kd"], p["vn"], (((0,), (0,)), ((), ())), preferred_element_type=F32)


def _gdn_out_kernel(of_ref, ob_ref, z_ref, n_ref, o_ref):
    z = z_ref[...].astype(F32)
    for h in range(GDN_HEADS):
        sl = slice(h * GDN_DV, (h + 1) * GDN_DV)
        o = of_ref[:, sl] + ob_ref[:, sl]
        y = o * lax.rsqrt(jnp.mean(o * o, axis=-1, keepdims=True) + NORM_EPS) * n_ref[...]
        zh = z[:, sl]
        o_ref[:, sl] = (y * (zh * jax.nn.sigmoid(zh))).astype(BF16)


def gdn_mixer(proj3, misc3, conv_w, a_log, dt_bias, out_norm, tl=512):
    B, L, _ = proj3.shape
    T = B * L
    N = L // CH
    H = GDN_HEADS
    tl = min(tl, L)
    lane_vec = lambda p: jnp.zeros((1, LANES), F32).at[0, _G_LANE:_G_LANE + NDH].set(p.reshape(-1))
    tok = lambda w, c=0: pl.BlockSpec((1, tl, w), lambda b, i: (b, i, c))
    f32_tok = lambda w: jax.ShapeDtypeStruct((B, L, w), F32)
    qn, kn, vv, bg = pl.pallas_call(
        _gdn_prep_kernel,
        grid=(B, L // tl),
        in_specs=[*_conv3_specs(proj3, 3, 1536, tl), _full((3, 1536)), tok(LANES, 3),
                  _full((1, LANES)), _full((1, LANES))],
        out_specs=[tok(512), tok(512), tok(512), tok(LANES)],
        out_shape=[f32_tok(512), f32_tok(512), f32_tok(512), f32_tok(LANES)],
        compiler_params=_cparams("parallel", "parallel"),
        name="gdn_prep",
    )(proj3, proj3, proj3, conv_w, misc3, lane_vec(a_log), lane_vec(dt_bias))

    CB = GDN_STEP_CHUNKS
    NB = N // CB
    chunk = lambda w, c=0: pl.BlockSpec((1, CB * CH, w), lambda b, n: (b, n, c))
    mats = pl.BlockSpec((1, CB, NDH, CH, CH), lambda b, n: (b, n, 0, 0, 0))
    mat_shape = jax.ShapeDtypeStruct((B, N, NDH, CH, CH), F32)
    a, attn, gc = pl.pallas_call(
        _gdn_chunk_kernel,
        grid=(B, NB),
        in_specs=[chunk(512), chunk(512), chunk(LANES)],
        out_specs=[mats, mats, chunk(LANES)],
        out_shape=[mat_shape, jax.ShapeDtypeStruct(mat_shape.shape, BF16),
                   jax.ShapeDtypeStruct((B, L, LANES), F32)],
        compiler_params=_cparams("parallel", "parallel"),
        name="gdn_chunk",
    )(qn, kn, bg)

    P = NDH * B * N
    at = a.transpose(3, 4, 2, 0, 1).reshape(CH, CH, P)
    tt = pl.pallas_call(
        functools.partial(_gdn_solve_kernel, nblk_fwd=P // LANES // 2),
        grid=(P // LANES,),
        in_specs=[pl.BlockSpec((CH, CH, LANES), lambda p: (0, 0, p))],
        out_specs=pl.BlockSpec((CH, CH, LANES), lambda p: (0, 0, p)),
        out_shape=jax.ShapeDtypeStruct((CH, CH, P), F32),
        compiler_params=_cparams("parallel"),
        name="gdn_solve",
    )(at)
    tmat = tt.reshape(CH, CH, NDH, B, N).transpose(3, 4, 2, 0, 1).astype(BF16)

    fwd = lambda w, c=0: pl.BlockSpec((1, CB * CH, w), lambda b, n: (b, n, c))
    bwd = lambda w, c=0: pl.BlockSpec((1, CB * CH, w), lambda b, n: (b, NB - 1 - n, c))
    mf = pl.BlockSpec((1, CB, H, CH, CH), lambda b, n: (b, n, 0, 0, 0))
    mb = pl.BlockSpec((1, CB, H, CH, CH), lambda b, n: (b, NB - 1 - n, 1, 0, 0))
    o_f, o_b = pl.pallas_call(
        _gdn_scan_kernel,
        grid=(B, NB),
        in_specs=[fwd(512), fwd(512), fwd(512), fwd(LANES), fwd(LANES), mf, mf,
                  bwd(512), bwd(512), bwd(512), bwd(LANES), bwd(LANES), mb, mb],
        out_specs=[fwd(512), bwd(512)],
        out_shape=[jax.ShapeDtypeStruct((B, L, 512), F32), jax.ShapeDtypeStruct((B, L, 512), F32)],
        scratch_shapes=[pltpu.VMEM((GDN_DK, GDN_DV), F32)] * NDH,
        compiler_params=_cparams("parallel", "arbitrary"),
        name="gdn_scan",
    )(qn, kn, vv, bg, gc, tmat, attn, qn, kn, vv, bg, gc, tmat, attn)

    tm = min(1024, T)
    row = lambda c=0: pl.BlockSpec((tm, 512), lambda i: (i, c))
    return pl.pallas_call(
        _gdn_out_kernel,
        grid=(T // tm,),
        in_specs=[row(), row(), row(12), _full((1, GDN_DV))],
        out_specs=row(),
        out_shape=jax.ShapeDtypeStruct((T, 512), BF16),
        compiler_params=_cparams("parallel"),
        name="gdn_out",
    )(o_f.reshape(T, 512), o_b.reshape(T, 512), proj3.reshape(T, MAIN_W), out_norm[None, :])


def _merge_kernel(x_ref, g_ref, oh_ref, om_ref, og_ref, wb_ref, wo_ref, n_ref, o_ref):
    merged = None
    for i, b_ref in enumerate((oh_ref, om_ref, og_ref)):
        gate = jax.nn.sigmoid(g_ref[:, i * D_MODEL:(i + 1) * D_MODEL].astype(F32))
        term = gate * jnp.dot(b_ref[...].astype(BF16), wb_ref[i], preferred_element_type=F32)
        merged = term if merged is None else merged + term
    y = jnp.dot(merged.astype(BF16), wo_ref[...], preferred_element_type=F32)
    r = lax.rsqrt(jnp.mean(y * y, axis=-1, keepdims=True) + NORM_EPS)
    o_ref[...] = x_ref[...] + y * r * n_ref[...]


def merge_out(x, proj, o_hy, o_mla, o_gdn, w_branch, w_out, norm_post, tm=512):
    T = x.shape[0]
    tm = min(tm, T)
    row = lambda w: pl.BlockSpec((tm, w), lambda i: (i, 0))
    return pl.pallas_call(
        _merge_kernel,
        grid=(T // tm,),
        in_specs=[row(D_MODEL), row(N_BRANCH * D_MODEL), row(BRANCH_W), row(BRANCH_W), row(BRANCH_W),
                  _full((N_BRANCH, BRANCH_W, D_MODEL)), _full((D_MODEL, D_MODEL)), _full((1, D_MODEL))],
        out_specs=row(D_MODEL),
        out_shape=jax.ShapeDtypeStruct((T, D_MODEL), F32),
        compiler_params=_cparams("parallel"),
        name="merge_out",
    )(x, proj, o_hy, o_mla, o_gdn, w_branch, w_out, norm_post)


FFN_ROW_PARTS = 2


def _ffn_kernel(x_ref, gpre_ref, wg_ref, wu_ref, wd_ref, gpost_ref, o_ref, h_ref, acc_ref):
    j = pl.program_id(1)

    @pl.when(j == 0)
    def _():
        x = x_ref[...]
        r = lax.rsqrt(jnp.mean(x * x, axis=-1, keepdims=True) + NORM_EPS)
        h_ref[...] = (x * r * gpre_ref[...]).astype(BF16)
        acc_ref[...] = jnp.zeros(acc_ref.shape, F32)

    part_rows = h_ref.shape[0] // FFN_ROW_PARTS
    parts = [slice(r * part_rows, (r + 1) * part_rows) for r in range(FFN_ROW_PARTS)]
    gates = [jnp.dot(h_ref[rows, :], wg_ref[...], preferred_element_type=F32) for rows in parts]
    ups = [jnp.dot(h_ref[rows, :], wu_ref[...], preferred_element_type=F32) for rows in parts]
    acts = [(g * jax.nn.sigmoid(g) * u).astype(BF16) for g, u in zip(gates, ups)]
    for rows, a in zip(parts, acts):
        acc_ref[rows, :] += jnp.dot(a, wd_ref[...], preferred_element_type=F32)

    @pl.when(j == pl.num_programs(1) - 1)
    def _():
        f = acc_ref[...]
        r = lax.rsqrt(jnp.mean(f * f, axis=-1, keepdims=True) + NORM_EPS)
        o_ref[...] = x_ref[...] + f * r * gpost_ref[...]


def ffn(x, g_pre, w_gate, w_up, w_down, g_post, tm=512):
    T = x.shape[0]
    dff = w_gate.shape[1]
    tf = dff // 2
    tm = min(tm, T)
    return pl.pallas_call(
        _ffn_kernel,
        grid=(T // tm, dff // tf),
        in_specs=[pl.BlockSpec((tm, D_MODEL), lambda i, j: (i, 0)), _full((1, D_MODEL)),
                  pl.BlockSpec((D_MODEL, tf), lambda i, j: (0, j)),
                  pl.BlockSpec((D_MODEL, tf), lambda i, j: (0, j)),
                  pl.BlockSpec((tf, D_MODEL), lambda i, j: (j, 0)), _full((1, D_MODEL))],
        out_specs=pl.BlockSpec((tm, D_MODEL), lambda i, j: (i, 0)),
        out_shape=jax.ShapeDtypeStruct((T, D_MODEL), F32),
        scratch_shapes=[pltpu.VMEM((tm, D_MODEL), BF16), pltpu.VMEM((tm, D_MODEL), F32)],
        compiler_params=_cparams("parallel", "arbitrary"),
        name="ffn",
    )(x, g_pre, w_gate, w_up, w_down, g_post)


def _split_w_in(w_in):
    cols = lambda off, n: w_in[:, off:off + n]
    main = jnp.concatenate([cols(_OFF_GATE, N_BRANCH * D_MODEL), cols(_OFF_HY, 1536),
                            cols(_OFF_GQKV, 1536), cols(_OFF_GZ, 512)], axis=-1).astype(BF16)
    half = MLA_ROPE // 2
    kpe = _OFF_MKV + MLA_KV_LORA
    zeros = lambda n: jnp.zeros((D_MODEL, n), w_in.dtype)
    misc = jnp.concatenate([cols(_OFF_MQ, MLA_Q_LORA), cols(_OFF_MKV, MLA_KV_LORA),
                            cols(kpe, MLA_ROPE), cols(_OFF_GB, 8), cols(_OFF_GA, 8), zeros(MISC_W - 464)],
                           axis=-1).astype(BF16)
    return jnp.concatenate([main, misc], axis=-1)


def trunk_layer(x, norm_mix_pre, norm_mix_post, norm_ffn_pre, norm_ffn_post, w_in,
                hy_conv_w, hy_conv_b, hy_ffn_w1, hy_ffn_b1, hy_sin_freq, hy_ffn_w2, hy_ffn_b2,
                hy_ffn_w3, hy_skip, mla_q_norm, mla_wq_b, mla_kv_norm, mla_wkv_b,
                gdn_conv_w, gdn_a_log, gdn_dt_bias, gdn_out_norm,
                w_branch, w_out, w_gate, w_up, w_down):
    B, L, D = x.shape
    T = B * L
    xt = x.reshape(T, D)
    proj, misc = norm_mm(xt, norm_mix_pre[None, :], _split_w_in(w_in), tm=2048, tn=512)
    proj3 = proj.reshape(B, L, MAIN_W)
    misc3 = misc.reshape(B, L, MISC_W)
    o_hy = hyena_mixer(proj3, hy_conv_w, hy_conv_b, hy_ffn_w1, hy_ffn_b1, hy_sin_freq,
                       hy_ffn_w2, hy_ffn_b2, hy_ffn_w3, hy_skip)
    o_mla = mla_mixer(misc3, mla_q_norm, mla_wq_b, mla_kv_norm, mla_wkv_b).reshape(T, BRANCH_W)
    o_gdn = gdn_mixer(proj3, misc3, gdn_conv_w, gdn_a_log, gdn_dt_bias, gdn_out_norm).reshape(T, BRANCH_W)
    xt = merge_out(xt, proj, o_hy, o_mla, o_gdn, w_branch.astype(BF16), w_out.astype(BF16),
                   norm_mix_post[None, :])
    xt = ffn(xt, norm_ffn_pre[None, :], w_gate.astype(BF16), w_up.astype(BF16), w_down.astype(BF16),
             norm_ffn_post[None, :])
    return xt.reshape(B, L, D)


def kernel(x_prompt, x_sample, norm_mix_pre, norm_mix_post, norm_ffn_pre, norm_ffn_post, w_in,
           hy_conv_w, hy_conv_b, hy_ffn_w1, hy_ffn_b1, hy_sin_freq, hy_ffn_w2, hy_ffn_b2,
           hy_ffn_w3, hy_skip, mla_q_norm, mla_wq_b, mla_kv_norm, mla_wkv_b,
           gdn_conv_w, gdn_a_log, gdn_dt_bias, gdn_out_norm,
           w_branch, w_out, w_gate, w_up, w_down):
    weights = (norm_mix_pre, norm_mix_post, norm_ffn_pre, norm_ffn_post, w_in,
               hy_conv_w, hy_conv_b, hy_ffn_w1, hy_ffn_b1, hy_sin_freq, hy_ffn_w2, hy_ffn_b2,
               hy_ffn_w3, hy_skip, mla_q_norm, mla_wq_b, mla_kv_norm, mla_wkv_b,
               gdn_conv_w, gdn_a_log, gdn_dt_bias, gdn_out_norm,
               w_branch, w_out, w_gate, w_up, w_down)

    def run_trunk(x):
        for layer in range(DEPTH):
            x = trunk_layer(x, *[w[layer] for w in weights])
        return x

    return (run_trunk(x_prompt), run_trunk(x_sample))
```

```python
import functools
import math

import jax
import jax.numpy as jnp
from jax import lax
from jax.experimental import pallas as pl
from jax.experimental.pallas import tpu as pltpu
import numpy as np

F32 = jnp.float32
BF16 = jnp.bfloat16

D_MODEL = 1024
DEPTH = 2
BRANCH_W = 512
N_BRANCH = 3
HY_W = BRANCH_W
HY_ORDER = 2
HY_EMB = 33
HY_FAST_PCT = 0.3
HY_SLOW_PCT = 1.5
HY_TARGET = 1e-2
MLA_HEADS = 4
MLA_NOPE = 128
MLA_ROPE = 64
MLA_V = 128
MLA_Q_LORA = 256
MLA_KV_LORA = 128
ROPE_THETA = 10000.0
GDN_HEADS = 4
GDN_DK = 128
GDN_DV = 128
GDN_CHUNK = 64
NORM_EPS = 1e-6

_OFF_HY = 0
_OFF_MQ = _OFF_HY + (HY_ORDER + 1) * HY_W
_OFF_MKV = _OFF_MQ + MLA_Q_LORA
_OFF_GQKV = _OFF_MKV + MLA_KV_LORA + MLA_ROPE
_OFF_GZ = _OFF_GQKV + GDN_HEADS * (2 * GDN_DK + GDN_DV)
_OFF_GB = _OFF_GZ + GDN_HEADS * GDN_DV
_OFF_GA = _OFF_GB + 2 * GDN_HEADS
_OFF_GATE = _OFF_GA + 2 * GDN_HEADS
_D_IN = _OFF_GATE + N_BRANCH * D_MODEL

MAIN_W = 3072 + 1536 + 1536 + 512
MISC_W = 512

LANES = 128
VMEM_LIMIT_BYTES = 56 * 1024 * 1024


def _cparams(*sem):
    return pltpu.CompilerParams(dimension_semantics=sem, vmem_limit_bytes=VMEM_LIMIT_BYTES)


def _full(shape):
    nd = len(shape)
    return pl.BlockSpec(shape, lambda *_: (0,) * nd)


NORM_ROW_PARTS = 4


def _norm_mm_kernel(x_ref, g_ref, w_ref, main_ref, misc_ref, h_ref, *, n_main):
    j = pl.program_id(1)

    @pl.when(j == 0)
    def _():
        part = x_ref.shape[0] // NORM_ROW_PARTS
        for r in range(NORM_ROW_PARTS):
            rows = slice(r * part, (r + 1) * part)
            x = x_ref[rows, :]
            inv = lax.rsqrt(jnp.mean(x * x, axis=-1, keepdims=True) + NORM_EPS)
            h = (x * inv * g_ref[...]).astype(BF16)
            h_ref[rows, :] = h
            main_ref[rows, :] = jnp.dot(h, w_ref[...], preferred_element_type=F32).astype(main_ref.dtype)

    @pl.when((j > 0) & (j < n_main))
    def _():
        main_ref[...] = jnp.dot(h_ref[...], w_ref[...], preferred_element_type=F32).astype(main_ref.dtype)

    @pl.when(j >= n_main)
    def _():
        misc_ref[...] = jnp.dot(h_ref[...], w_ref[...], preferred_element_type=F32)


def norm_mm(x, g, w, tm, tn):
    T, D = x.shape
    tm = min(tm, T)
    n_main = MAIN_W // tn
    n_misc = MISC_W // tn
    return pl.pallas_call(
        functools.partial(_norm_mm_kernel, n_main=n_main),
        grid=(T // tm, n_main + n_misc),
        in_specs=[pl.BlockSpec((tm, D), lambda i, j: (i, 0)),
                  pl.BlockSpec((1, D), lambda i, j: (0, 0)),
                  pl.BlockSpec((D, tn), lambda i, j: (0, j))],
        out_specs=[pl.BlockSpec((tm, tn), lambda i, j: (i, jnp.minimum(j, n_main - 1))),
                   pl.BlockSpec((tm, tn), lambda i, j: (i, jnp.maximum(j - n_main, 0)))],
        out_shape=[jax.ShapeDtypeStruct((T, MAIN_W), BF16), jax.ShapeDtypeStruct((T, MISC_W), F32)],
        scratch_shapes=[pltpu.VMEM((tm, D), BF16)],
        compiler_params=_cparams("parallel", "arbitrary"),
        name="norm_mm",
    )(x, g, w)


def _conv3_rows(x_ref, p_ref, n_ref, w_ref):
    i = pl.program_id(1)
    last = pl.num_programs(1) - 1
    x = x_ref[0].astype(F32)
    tl = x.shape[0]
    halo = p_ref.shape[1]
    prev_row = jnp.where(i > 0, p_ref[0, halo - 1:halo, :].astype(F32), 0.0)
    next_row = jnp.where(i < last, n_ref[0, 0:1, :].astype(F32), 0.0)
    rows = lax.broadcasted_iota(jnp.int32, x.shape, 0)
    x_dn = jnp.where(rows == 0, prev_row, pltpu.roll(x, 1, axis=0))
    x_up = jnp.where(rows == tl - 1, next_row, pltpu.roll(x, tl - 1, axis=0))
    return x_dn * w_ref[0:1, :] + x * w_ref[1:2, :] + x_up * w_ref[2:3, :]


def _conv3_specs(x, col_blk, cw, tl):
    halo = 8 * (4 // x.dtype.itemsize)
    nsub = tl // halo
    lastblk = x.shape[1] // halo - 1
    return [pl.BlockSpec((1, tl, cw), lambda b_, i: (b_, i, col_blk)),
            pl.BlockSpec((1, halo, cw), lambda b_, i: (b_, jnp.maximum(i * nsub - 1, 0), col_blk)),
            pl.BlockSpec((1, halo, cw), lambda b_, i: (b_, jnp.minimum((i + 1) * nsub, lastblk), col_blk))]


def _dwconv_kernel(x_ref, p_ref, n_ref, w_ref, b_ref, o_ref):
    o_ref[0] = (_conv3_rows(x_ref, p_ref, n_ref, w_ref) + b_ref[...]).astype(o_ref.dtype)


def dwconv3(x, col_blk, cw, w, b, out_dtype, tl=512):
    B, L, _ = x.shape
    tl = min(tl, L)
    return pl.pallas_call(
        _dwconv_kernel,
        grid=(B, L // tl),
        in_specs=[*_conv3_specs(x, col_blk, cw, tl), _full((3, cw)), _full((1, cw))],
        out_specs=pl.BlockSpec((1, tl, cw), lambda b_, i: (b_, i, 0)),
        out_shape=jax.ShapeDtypeStruct((B, L, cw), out_dtype),
        compiler_params=_cparams("parallel", "parallel"),
        name="dwconv3",
    )(x, x, x, w, b)


def _tile_lanes(t, c):
    return t if c == LANES else jnp.concatenate([t] * (c // LANES), axis=-1)


def _fft_s1_kernel(f_ref, x_ref, twr_ref, twi_ref, o_ref, *, nb, n1):
    c = x_ref.shape[-1]
    for t in range(nb):
        a = jnp.dot(f_ref[...], x_ref[0, t], preferred_element_type=F32)
        ar, ai = a[:n1], a[n1:]
        twr = _tile_lanes(twr_ref[t], c)
        twi = _tile_lanes(twi_ref[t], c)
        o_ref[0, t, :n1, :] = (ar * twr + ai * twi).astype(BF16)
        o_ref[0, t, n1:, :] = (ai * twr - ar * twi).astype(BF16)


def _fft_s2_kernel(f_ref, finv_ref, b_ref, kr_ref, ki_ref, o_ref, *, nb, n2):
    for t in range(nb):
        x = jnp.dot(f_ref[t], b_ref[0, t], preferred_element_type=F32)
        xr, xi = x[:n2], x[n2:]
        kr, ki = kr_ref[t].astype(F32), ki_ref[t].astype(F32)
        y = jnp.concatenate([xr * kr - xi * ki, xr * ki + xi * kr], axis=0).astype(BF16)
        o_ref[0, t] = jnp.dot(finv_ref[t], y, preferred_element_type=F32).astype(BF16)


def _fft_mm_kernel(g_ref, d_ref, o_ref, *, nb):
    for t in range(nb):
        o_ref[0, t] = jnp.dot(g_ref[...], d_ref[0, t], preferred_element_type=F32).astype(o_ref.dtype)


def _dft_tables(L):
    N = 2 * L
    n2 = 128 if N >= 4096 else 16
    n1 = N // n2
    n1h = n1 // 2
    k1 = np.arange(n1)[:, None]
    m1 = np.arange(n1h)[None, :]
    th = 2.0 * np.pi * ((k1 * m1) % n1) / n1
    c, s = np.cos(th), np.sin(th)
    f1 = np.block([[c, s], [-s, c]])
    g1 = np.block([[c.T, -s.T], [s.T, c.T]]) / N
    as_bf16 = lambda m: jnp.asarray(m, dtype=F32).astype(BF16)
    k = (jnp.arange(n1, dtype=jnp.int32)[:, None, None]
         + n1 * jnp.arange(n2, dtype=jnp.int32)[None, :, None])
    ang = ((k * jnp.arange(n2, dtype=jnp.int32)[None, None, :]) % N).astype(F32) * (2.0 * math.pi / N)
    c2, s2 = jnp.cos(ang), jnp.sin(ang)
    f2 = jnp.concatenate([jnp.concatenate([c2, s2], axis=2),
                          jnp.concatenate([-s2, c2], axis=2)], axis=1).astype(BF16)
    return n1, n2, as_bf16(f1), as_bf16(g1), f2, f2.transpose(0, 2, 1)


def _twiddles(n1, n2):
    N = n1 * n2
    prod = (jnp.arange(n2, dtype=jnp.int32)[:, None] * jnp.arange(n1, dtype=jnp.int32)[None, :]) % N
    ang = prod.astype(F32) * (2.0 * math.pi / N)
    shape = (n2, n1, LANES)
    return (jnp.broadcast_to(jnp.cos(ang)[:, :, None], shape),
            jnp.broadcast_to(jnp.sin(ang)[:, :, None], shape))


def fft_long_conv(z, kspec, order, tabs):
    B, L, C = z.shape
    n1, n2, f1, g1, f2, f2inv = tabs
    n1h = n1 // 2
    P = B // 2
    nb1 = min(8, n2)
    nb2 = min(8, n1)
    zt = z.astype(BF16).reshape(P, 2, n1h, n2, C).transpose(0, 3, 1, 2, 4).reshape(P, n2, 2 * n1h, C)
    a = pl.pallas_call(
        functools.partial(_fft_mm_kernel, nb=nb1),
        grid=(n2 // nb1, P),
        in_specs=[_full((2 * n1, 2 * n1h)),
                  pl.BlockSpec((1, nb1, 2 * n1h, C), lambda j, p: (p, j, 0, 0))],
        out_specs=pl.BlockSpec((1, nb1, 2 * n1, C), lambda j, p: (p, j, 0, 0)),
        out_shape=jax.ShapeDtypeStruct((P, n2, 2 * n1, C), BF16),
        compiler_params=_cparams("parallel", "parallel"),
        name="fft_s1",
    )(f1, zt)
    bt = a.reshape(P, n2, 2, n1, C).transpose(0, 3, 2, 1, 4).reshape(P, n1, 2 * n2, C)
    d = pl.pallas_call(
        functools.partial(_fft_s2_kernel, nb=nb2, n2=n2),
        grid=(n1 // nb2, P),
        in_specs=[pl.BlockSpec((nb2, 2 * n2, 2 * n2), lambda j, p: (j, 0, 0)),
                  pl.BlockSpec((nb2, 2 * n2, 2 * n2), lambda j, p: (j, 0, 0)),
                  pl.BlockSpec((1, nb2, 2 * n2, C), lambda j, p: (p, j, 0, 0)),
                  pl.BlockSpec((nb2, n2, C), lambda j, p: (j, 0, order)),
                  pl.BlockSpec((nb2, n2, C), lambda j, p: (j, 1, order))],
        out_specs=pl.BlockSpec((1, nb2, 2 * n2, C), lambda j, p: (p, j, 0, 0)),
        out_shape=jax.ShapeDtypeStruct((P, n1, 2 * n2, C), BF16),
        compiler_params=_cparams("parallel", "parallel"),
        name="fft_s2",
    )(f2, f2inv, bt, kspec, kspec)
    dt = d.reshape(P, n1, 2, n2, C).transpose(0, 3, 2, 1, 4).reshape(P, n2, 2 * n1, C)
    y = pl.pallas_call(
        functools.partial(_fft_mm_kernel, nb=nb1),
        grid=(n2 // nb1, P),
        in_specs=[_full((2 * n1h, 2 * n1)),
                  pl.BlockSpec((1, nb1, 2 * n1, C), lambda j, p: (p, j, 0, 0))],
        out_specs=pl.BlockSpec((1, nb1, 2 * n1h, C), lambda j, p: (p, j, 0, 0)),
        out_shape=jax.ShapeDtypeStruct((P, n2, 2 * n1h, C), BF16),
        compiler_params=_cparams("parallel", "parallel"),
        name="fft_s3",
    )(g1, dt)
    return y.reshape(P, n2, 2, n1h, C).transpose(0, 2, 3, 1, 4).reshape(B, L, C)


def _hy_gate_kernel(g_ref, c_ref, z_ref, s_ref, o_ref):
    o_ref[...] = (g_ref[...].astype(F32) * (c_ref[...].astype(F32) + s_ref[...] * z_ref[...].astype(F32))
                  ).astype(o_ref.dtype)


def hy_gate(u, gate_blk, conv, zsrc, z_blk, skip, out_dtype, tl=1024):
    T, C = conv.shape
    tl = min(tl, T)
    return pl.pallas_call(
        _hy_gate_kernel,
        grid=(T // tl,),
        in_specs=[pl.BlockSpec((tl, C), lambda i: (i, gate_blk)),
                  pl.BlockSpec((tl, C), lambda i: (i, 0)),
                  pl.BlockSpec((tl, C), lambda i: (i, z_blk)),
                  _full((1, C))],
        out_specs=pl.BlockSpec((tl, C), lambda i: (i, 0)),
        out_shape=jax.ShapeDtypeStruct((T, C), out_dtype),
        compiler_params=_cparams("parallel"),
        name="hy_gate",
    )(u, conv, zsrc, skip)


def hyena_filter_spectrum(L, n1, n2, tw1r, tw1i, w1, b1, freq, w2, b2, w3):
    t = jnp.linspace(0.0, 1.0, L, dtype=F32)[:, None]
    bands = (HY_EMB - 1) // 2
    wpos = (2.0 * math.pi / L) * jnp.arange(L, dtype=F32)[:, None]
    fr = jnp.linspace(1e-4, bands - 1, bands, dtype=F32)[None, :]
    feats = jnp.concatenate([t, jnp.cos(fr * wpos), -jnp.sin(fr * wpos)], axis=-1)
    deltas = jnp.abs(jnp.linspace(math.log(HY_TARGET) / HY_SLOW_PCT,
                                  math.log(HY_TARGET) / HY_FAST_PCT, HY_W, dtype=F32))
    feats2 = jnp.concatenate([feats, feats[0:1], feats[:0:-1]], axis=0)
    feats2 = jnp.pad(feats2, ((0, 0), (0, LANES - HY_EMB)))
    w1p = jnp.pad(w1, ((0, LANES - HY_EMB), (0, 0)))
    hid = w3.shape[0]
    oc = HY_ORDER * HY_W
    w3d = w3.reshape(hid, HY_ORDER, 2, HY_W).transpose(2, 0, 1, 3).reshape(2, hid, oc)
    tl = min(512, L)
    nh = L // tl
    kern, asum = pl.pallas_call(
        functools.partial(_filter_taps_kernel, zero_tile=nh),
        grid=(2 * nh,),
        in_specs=[pl.BlockSpec((tl, LANES), lambda i: (i, 0)), _full((LANES, hid)), _full((1, hid)),
                  _full((1, hid)), _full((hid, hid)), _full((1, hid)),
                  pl.BlockSpec((1, hid, oc), lambda i: (i // nh, 0, 0)), _full((1, oc))],
        out_specs=[pl.BlockSpec((tl, oc), lambda i: (i, 0)), _full((1, oc))],
        out_shape=[jax.ShapeDtypeStruct((2 * L, oc), F32), jax.ShapeDtypeStruct((1, oc), F32)],
        compiler_params=_cparams("arbitrary"),
        name="filter_taps",
    )(feats2, w1p, b1, freq, w2, b2, w3d, jnp.tile(deltas, HY_ORDER)[None, :])
    return filter_dft(kern, asum, n1, n2, tw1r, tw1i)


def _filter_taps_kernel(f_ref, w1_ref, b1_ref, fq_ref, w2_ref, b2_ref, w3_ref, dl_ref, k_ref, s_ref, *,
                        zero_tile):
    i = pl.program_id(0)
    hp = lax.Precision.HIGHEST
    f = f_ref[...]
    h = jnp.sin(fq_ref[...] * (jnp.dot(f, w1_ref[...], precision=hp, preferred_element_type=F32)
                               + b1_ref[...]))
    h = jnp.sin(fq_ref[...] * (jnp.dot(h, w2_ref[...], precision=hp, preferred_element_type=F32)
                               + b2_ref[...]))
    taps = jnp.dot(h, w3_ref[0], precision=hp, preferred_element_type=F32)
    taps = taps * jnp.exp(-f[:, 0:1] * dl_ref[...])
    row = lax.broadcasted_iota(jnp.int32, taps.shape, 0)
    taps = jnp.where((i == zero_tile) & (row == 0), 0.0, taps)
    k_ref[...] = taps

    @pl.when(i == 0)
    def _():
        s_ref[...] = jnp.zeros(s_ref.shape, F32)

    s_ref[...] += jnp.sum(jnp.abs(taps), axis=0, keepdims=True)


def _spec_s2_kernel(f_ref, b_ref, s_ref, o_ref, *, nb):
    for t in range(nb):
        o_ref[t] = (jnp.dot(f_ref[...], b_ref[0, t], preferred_element_type=F32) / s_ref[...]).astype(BF16)


def filter_dft(kern, asum, n1, n2, tw1r, tw1i):
    N, C = kern.shape
    ct = 512
    k1 = np.arange(n1)
    th = 2.0 * np.pi * ((k1[:, None] * k1[None, :]) % n1) / n1
    f1 = np.concatenate([np.cos(th), -np.sin(th)], axis=0)
    a2 = np.arange(n2)
    ph = 2.0 * np.pi * ((a2[:, None] * a2[None, :]) % n2) / n2
    f2 = np.block([[np.cos(ph), np.sin(ph)], [-np.sin(ph), np.cos(ph)]])

    as_bf16 = lambda m: jnp.asarray(m, dtype=F32).astype(BF16)
    nb1 = min(8, n2)
    nb2 = min(8, n1)
    xt = kern.astype(BF16).reshape(n1, n2, C).transpose(1, 0, 2)[None]
    a = pl.pallas_call(
        functools.partial(_fft_s1_kernel, nb=nb1, n1=n1),
        grid=(n2 // nb1, C // ct),
        in_specs=[_full((2 * n1, n1)),
                  pl.BlockSpec((1, nb1, n1, ct), lambda j, c: (0, j, 0, c)),
                  pl.BlockSpec((nb1, n1, LANES), lambda j, c: (j, 0, 0)),
                  pl.BlockSpec((nb1, n1, LANES), lambda j, c: (j, 0, 0))],
        out_specs=pl.BlockSpec((1, nb1, 2 * n1, ct), lambda j, c: (0, j, 0, c)),
        out_shape=jax.ShapeDtypeStruct((1, n2, 2 * n1, C), BF16),
        compiler_params=_cparams("parallel", "parallel"),
        name="spec_s1",
    )(as_bf16(f1), xt, tw1r, tw1i)
    bt = a.reshape(n2, 2, n1, C).transpose(2, 1, 0, 3).reshape(1, n1, 2 * n2, C)
    return pl.pallas_call(
        functools.partial(_spec_s2_kernel, nb=nb2),
        grid=(n1 // nb2, C // ct),
        in_specs=[_full((2 * n2, 2 * n2)),
                  pl.BlockSpec((1, nb2, 2 * n2, ct), lambda j, c: (0, j, 0, c)),
                  pl.BlockSpec((1, ct), lambda j, c: (0, c))],
        out_specs=pl.BlockSpec((nb2, 2 * n2, ct), lambda j, c: (j, 0, c)),
        out_shape=jax.ShapeDtypeStruct((n1, 2 * n2, C), BF16),
        compiler_params=_cparams("parallel", "parallel"),
        name="spec_s2",
    )(as_bf16(f2), bt, asum)


def hyena_mixer(proj3, conv_w, conv_b, w1, b1, freq, w2, b2, w3, skip):
    B, L, _ = proj3.shape
    T = B * L
    u = dwconv3(proj3, 2, 1536, conv_w, conv_b[None, :], out_dtype=BF16).reshape(T, 1536)
    n1, n2, f1, g1, f2, f2inv = _dft_tables(L)
    tw1r, tw1i = _twiddles(n1, n2)
    tabs = (n1, n2, f1, g1, f2, f2inv)
    kspec = hyena_filter_spectrum(L, n1, n2, tw1r, tw1i, w1, b1[None, :], freq[None, :], w2, b2[None, :], w3)
    conv = fft_long_conv(u[:, 1024:1536].reshape(B, L, HY_W), kspec, 0, tabs).reshape(T, HY_W)
    z1 = hy_gate(u, 0, conv, u, 2, skip[0:1], BF16)
    conv = fft_long_conv(z1.reshape(B, L, HY_W), kspec, 1, tabs).reshape(T, HY_W)
    return hy_gate(u, 1, conv, z1, 0, skip[1:2], BF16)


def _mla_proj_kernel(m_ref, qg_ref, kg_ref, wq_ref, wkv_ref, cq_ref, sq_ref, ck_ref, sk_ref,
                     q_ref, k_ref, v_ref):
    m = m_ref[0]
    ql = m[:, 0:256]
    r = lax.rsqrt(jnp.mean(ql * ql, axis=-1, keepdims=True) + NORM_EPS)
    q = jnp.dot((ql * r * qg_ref[...]).astype(BF16), wq_ref[...], preferred_element_type=F32)
    ckv = m[:, 256:384]
    r = lax.rsqrt(jnp.mean(ckv * ckv, axis=-1, keepdims=True) + NORM_EPS)
    kv = jnp.dot((ckv * r * kg_ref[...]).astype(BF16), wkv_ref[...], preferred_element_type=F32)
    q_pe = ((q[:, 512:1024] * cq_ref[...] + q[:, 1024:1536] * sq_ref[...]) * ATTN_Q_SCALE).astype(BF16)
    kp = m[:, 384:512]
    half = MLA_ROPE // 2
    lane = lax.broadcasted_iota(jnp.int32, kp.shape, 1)
    kp_swapped = jnp.where(lane < half, pltpu.roll(kp, LANES - half, axis=1), pltpu.roll(kp, half, axis=1))
    k_pe = (kp * ck_ref[...] + kp_swapped * sk_ref[...]).astype(BF16)
    for h in range(MLA_HEADS):
        sl = slice(h * LANES, (h + 1) * LANES)
        q_ref[0, :, 2 * h * LANES:(2 * h + 1) * LANES] = (q[:, sl] * ATTN_Q_SCALE).astype(BF16)
        q_ref[0, :, (2 * h + 1) * LANES:(2 * h + 2) * LANES] = q_pe[:, sl]
        k_ref[0, :, 2 * h * LANES:(2 * h + 1) * LANES] = kv[:, sl].astype(BF16)
        k_ref[0, :, (2 * h + 1) * LANES:(2 * h + 2) * LANES] = k_pe
        v_ref[0, :, 2 * h * LANES:(2 * h + 1) * LANES] = kv[:, 512 + h * MLA_V:512 + (h + 1) * MLA_V].astype(BF16)
        v_ref[0, :, (2 * h + 1) * LANES:(2 * h + 2) * LANES] = jnp.ones((m.shape[0], LANES), BF16)


ATTN_Q_SCALE = (MLA_NOPE + MLA_ROPE) ** -0.5 * math.log2(math.e)
ATTN_ROW_PARTS = 2


def _attn_kernel(q_ref, k_ref, v_ref, o_ref, *scratch):
    ki = pl.program_id(2)
    dqk = 2 * LANES
    m_refs, acc_refs = scratch[:MLA_HEADS], scratch[MLA_HEADS:]

    @pl.when(ki == 0)
    def _():
        for m_ref, acc_ref in zip(m_refs, acc_refs):
            m_ref[...] = jnp.full(m_ref.shape, -jnp.inf, F32)
            acc_ref[...] = jnp.zeros(acc_ref.shape, F32)

    part = q_ref.shape[1] // ATTN_ROW_PARTS
    chains = [(h, slice(h * dqk, (h + 1) * dqk), slice(r * part, (r + 1) * part))
              for h in range(MLA_HEADS) for r in range(ATTN_ROW_PARTS)]
    scores = [lax.dot_general(q_ref[0, rows, sl], k_ref[0, :, sl], (((1,), (1,)), ((), ())),
                              preferred_element_type=F32) for _, sl, rows in chains]
    weights = []
    for (h, sl, rows), s in zip(chains, scores):
        m_prev = m_refs[h][rows, :]
        m_new = jnp.maximum(m_prev, jnp.max(s, axis=-1, keepdims=True))
        m_refs[h][rows, :] = m_new
        weights.append((jnp.exp2(m_prev - m_new), jnp.exp2(s - m_new).astype(BF16)))
    for (h, sl, rows), (alpha, p) in zip(chains, weights):
        acc_refs[h][rows, :] = (alpha * acc_refs[h][rows, :]
                                + jnp.dot(p, v_ref[0, :, sl], preferred_element_type=F32))

    @pl.when(ki == pl.num_programs(2) - 1)
    def _():
        for h, acc_ref in enumerate(acc_refs):
            o_ref[0, :, h * MLA_V:(h + 1) * MLA_V] = (acc_ref[:, :MLA_V] / acc_ref[:, MLA_V:]).astype(BF16)


def _rope_lane_tables(L):
    half = MLA_ROPE // 2
    inv = ROPE_THETA ** (-jnp.arange(half, dtype=F32) / half)
    ang = jnp.arange(L, dtype=F32)[:, None] * inv[None, :]
    cos, sin = jnp.cos(ang), jnp.sin(ang)
    zeros = jnp.zeros((L, LANES - MLA_ROPE), F32)
    ck = jnp.concatenate([cos, cos, zeros], axis=-1)
    sk = jnp.concatenate([-sin, sin, zeros], axis=-1)
    return jnp.tile(ck, (1, MLA_HEADS)), jnp.tile(sk, (1, MLA_HEADS)), ck, sk


def mla_mixer(misc3, q_norm, wq_b, kv_norm, wkv_b, tl=512, tq=1024, tk=1024):
    B, L, _ = misc3.shape
    H = MLA_HEADS
    tl, tq, tk = min(tl, L), min(tq, L), min(tk, L)
    half = MLA_ROPE // 2
    wq = wq_b.reshape(MLA_Q_LORA, H, MLA_NOPE + MLA_ROPE)
    x1 = wq[:, :, MLA_NOPE:MLA_NOPE + half]
    x2 = wq[:, :, MLA_NOPE + half:]
    zpad = jnp.zeros((MLA_Q_LORA, H, LANES - MLA_ROPE), wq.dtype)
    wq = jnp.concatenate([wq[:, :, :MLA_NOPE].reshape(MLA_Q_LORA, -1),
                          jnp.concatenate([x1, x2, zpad], axis=-1).reshape(MLA_Q_LORA, -1),
                          jnp.concatenate([x2, x1, zpad], axis=-1).reshape(MLA_Q_LORA, -1)],
                         axis=-1).astype(BF16)
    wkv = wkv_b.reshape(MLA_KV_LORA, H, MLA_NOPE + MLA_V)
    wkv = jnp.concatenate([wkv[:, :, :MLA_NOPE].reshape(MLA_KV_LORA, -1),
                           wkv[:, :, MLA_NOPE:].reshape(MLA_KV_LORA, -1)], axis=-1).astype(BF16)
    cq, sq, ck, sk = _rope_lane_tables(L)
    tok = lambda w: pl.BlockSpec((1, tl, w), lambda b, i: (b, i, 0))
    pos = lambda w: pl.BlockSpec((tl, w), lambda b, i: (i, 0))
    sds = lambda w: jax.ShapeDtypeStruct((B, L, w), BF16)
    dqk = 2 * LANES
    nk = L // tk
    q, k, v = pl.pallas_call(
        _mla_proj_kernel,
        grid=(B, L // tl),
        in_specs=[tok(MISC_W), _full((1, MLA_Q_LORA)), _full((1, MLA_KV_LORA)),
                  _full(wq.shape), _full(wkv.shape), pos(H * LANES), pos(H * LANES), pos(LANES), pos(LANES)],
        out_specs=[tok(H * dqk), tok(H * dqk), tok(H * dqk)],
        out_shape=[sds(H * dqk), sds(H * dqk), sds(H * dqk)],
        compiler_params=_cparams("parallel", "parallel"),
        name="mla_proj",
    )(misc3, q_norm[None, :], kv_norm[None, :], wq, wkv, cq, sq, ck, sk)
    return pl.pallas_call(
        _attn_kernel,
        grid=(B, L // tq, nk),
        in_specs=[pl.BlockSpec((1, tq, H * dqk), lambda b, i, j: (b, i, 0)),
                  pl.BlockSpec((1, tk, H * dqk), lambda b, i, j: (b, j, 0)),
                  pl.BlockSpec((1, tk, H * dqk), lambda b, i, j: (b, j, 0))],
        out_specs=pl.BlockSpec((1, tq, H * MLA_V), lambda b, i, j: (b, i, 0)),
        out_shape=jax.ShapeDtypeStruct((B, L, H * MLA_V), BF16),
        scratch_shapes=([pltpu.VMEM((tq, 1), F32)] * H + [pltpu.VMEM((tq, 2 * MLA_V), F32)] * H),
        compiler_params=_cparams("parallel", "parallel", "arbitrary"),
        name="mla_attn",
    )(q, k, v)


_BETA_LANE = 64
_G_LANE = 72
CH = GDN_CHUNK
NDH = 2 * GDN_HEADS
GDN_STEP_CHUNKS = 8


def _gdn_prep_kernel(x_ref, p_ref, n_ref, w_ref, m_ref, alog_ref, dtb_ref, qn_ref, kn_ref, v_ref, bg_ref):
    c = _conv3_rows(x_ref, p_ref, n_ref, w_ref)
    c = c * jax.nn.sigmoid(c)
    nqk = GDN_HEADS * GDN_DK
    for h in range(GDN_HEADS):
        sl = slice(h * GDN_DK, (h + 1) * GDN_DK)
        qh = c[:, sl]
        qn_ref[0, :, sl] = (qh * lax.rsqrt(jnp.sum(qh * qh, axis=-1, keepdims=True) + NORM_EPS)
                            * (GDN_DK ** -0.5))
        kh = c[:, nqk + h * GDN_DK:nqk + (h + 1) * GDN_DK]
        kn_ref[0, :, sl] = kh * lax.rsqrt(jnp.sum(kh * kh, axis=-1, keepdims=True) + NORM_EPS)
    v_ref[0] = c[:, 2 * nqk:]
    m = m_ref[0]
    lane = lax.broadcasted_iota(jnp.int32, m.shape, 1)
    beta = jax.nn.sigmoid(m)
    x = m + dtb_ref[...]
    softplus = jnp.maximum(x, 0.0) + jnp.log1p(jnp.exp(-jnp.abs(x)))
    g = -jnp.exp(alog_ref[...]) * softplus
    is_beta = (lane >= _BETA_LANE) & (lane < _BETA_LANE + NDH)
    is_g = (lane >= _G_LANE) & (lane < _G_LANE + NDH)
    bg_ref[0] = jnp.where(is_beta, beta, jnp.where(is_g, g, 0.0))


def _split3(x):
    hi = x.astype(BF16)
    r = x - hi.astype(F32)
    mid = r.astype(BF16)
    lo = (r - mid.astype(F32)).astype(BF16)
    return hi, mid, lo


def _dot_nt(a, b):
    return lax.dot_general(a, b, (((1,), (1,)), ((), ())), preferred_element_type=F32)


def _gdn_chunk_kernel(q_ref, k_ref, bg_ref, a_ref, attn_ref, gc_ref):
    lane = lax.broadcasted_iota(jnp.int32, (CH, LANES), 1)
    is_g = (lane >= _G_LANE) & (lane < _G_LANE + NDH)
    ri = lax.broadcasted_iota(jnp.int32, (CH, CH), 0)
    ci = lax.broadcasted_iota(jnp.int32, (CH, CH), 1)
    lower = ri >= ci
    upper = ri <= ci
    tril = lower.astype(BF16)
    triu = upper.astype(BF16)
    for c in range(GDN_STEP_CHUNKS):
        rows = slice(c * CH, (c + 1) * CH)
        bg = bg_ref[0, rows, :]
        pieces = _split3(jnp.where(is_g, bg, 0.0))
        pre = sum(jnp.dot(tril, p, preferred_element_type=F32) for p in pieces)
        suf = sum(jnp.dot(triu, p, preferred_element_type=F32) for p in pieces)
        gc = jnp.where(lane >= _G_LANE + GDN_HEADS, suf, pre)
        gc_ref[0, rows, :] = gc
        gct = gc.T
        for d in range(2):
            causal = lower if d == 0 else upper
            strict = (ri > ci) if d == 0 else (ri < ci)
            for h in range(GDN_HEADS):
                dh = d * GDN_HEADS + h
                sl = slice(h * GDN_DK, (h + 1) * GDN_DK)
                kh = k_ref[0, rows, sl]
                kb = (kh * bg[:, _BETA_LANE + dh:_BETA_LANE + dh + 1]).astype(BF16)
                khb = kh.astype(BF16)
                diff = gc[:, _G_LANE + dh:_G_LANE + dh + 1] - gct[_G_LANE + dh:_G_LANE + dh + 1, :]
                dec = jnp.exp(jnp.where(causal, diff, -jnp.inf))
                a_ref[0, c, dh] = jnp.where(strict, _dot_nt(kb, khb) * dec, 0.0)
                attn_ref[0, c, dh] = (_dot_nt(q_ref[0, rows, sl].astype(BF16), khb) * dec).astype(BF16)


def _gdn_solve_kernel(a_ref, t_ref, *, nblk_fwd):
    bwd = pl.program_id(0) >= nblk_fwd
    t_ref[...] = jnp.zeros(t_ref.shape, F32)

    @pl.when(jnp.logical_not(bwd))
    def _():
        _solve_triangular(a_ref, t_ref, reverse=False)

    @pl.when(bwd)
    def _():
        _solve_triangular(a_ref, t_ref, reverse=True)


def _solve_triangular(a_ref, t_ref, *, reverse):
    nblk = CH // 8
    sub8 = lax.broadcasted_iota(jnp.int32, (8, LANES), 0)
    for phase in range(nblk):
        rb = nblk - 1 - phase if reverse else phase
        groups = range(rb, nblk) if reverse else range(0, rb + 1)

        def row(it, carry, rb=rb, groups=groups):
            i = 8 * rb + (7 - it if reverse else it)
            acc = {cg: ((sub8 + 8 * cg == i).astype(F32) if cg == rb else jnp.zeros((8, LANES), F32))
                   for cg in groups}
            for jb in groups:
                ablk = a_ref[i, 8 * jb:8 * jb + 8, :]
                for jj in range(8):
                    arow = ablk[jj:jj + 1, :]
                    for cg in (range(jb, nblk) if reverse else range(0, jb + 1)):
                        acc[cg] = acc[cg] - arow * t_ref[8 * jb + jj, 8 * cg:8 * cg + 8, :]
            for cg in groups:
                t_ref[i, 8 * cg:8 * cg + 8, :] = acc[cg]
            return carry

        lax.fori_loop(0, 8, row, 0)


def _gdn_scan_kernel(qf_ref, kf_ref, vf_ref, bgf_ref, gcf_ref, tf_ref, af_ref,
                     qb_ref, kb_ref, vb_ref, bgb_ref, gcb_ref, tb_ref, ab_ref,
                     of_ref, ob_ref, *s_refs):
    @pl.when(pl.program_id(1) == 0)
    def _():
        for s_ref in s_refs:
            s_ref[...] = jnp.zeros(s_ref.shape, F32)

    dirs = ((qf_ref, kf_ref, vf_ref, bgf_ref, gcf_ref, tf_ref, af_ref, of_ref, CH - 1),
            (qb_ref, kb_ref, vb_ref, bgb_ref, gcb_ref, tb_ref, ab_ref, ob_ref, 0))
    for step in range(GDN_STEP_CHUNKS):
        probs = []
        for d, (q_ref, k_ref, v_ref, bg_ref, gc_ref, t_ref, a_ref, o_ref, last) in enumerate(dirs):
            c = step if d == 0 else GDN_STEP_CHUNKS - 1 - step
            rows = slice(c * CH, (c + 1) * CH)
            for h in range(GDN_HEADS):
                dh = d * GDN_HEADS + h
                sl = slice(h * GDN_DK, (h + 1) * GDN_DK)
                k, v = k_ref[0, rows, sl], v_ref[0, rows, sl]
                beta = bg_ref[0, rows, _BETA_LANE + dh:_BETA_LANE + dh + 1]
                gc = gc_ref[0, rows, _G_LANE + dh:_G_LANE + dh + 1]
                gl = gc_ref[0, c * CH + last:c * CH + last + 1, _G_LANE + dh:_G_LANE + dh + 1]
                egc = jnp.exp(gc)
                rhs = jnp.concatenate([v * beta, k * beta * egc], axis=-1).astype(BF16)
                sol = jnp.dot(t_ref[0, c, h], rhs, preferred_element_type=F32)
                s = s_refs[dh][...]
                probs.append(dict(sl=sl, rows=rows, o_ref=o_ref, s_ref=s_refs[dh], s=s, sb=s.astype(BF16),
                                  u=sol[:, :GDN_DV], w=sol[:, GDN_DV:].astype(BF16),
                                  qd=(q_ref[0, rows, sl] * egc).astype(BF16),
                                  a=a_ref[0, c, h].astype(BF16),
                                  kd=(k * jnp.exp(gl - gc)).astype(BF16), dec=jnp.exp(gl)))
        for p in probs:
            p["vn"] = (p["u"] - jnp.dot(p["w"], p["sb"], preferred_element_type=F32)).astype(BF16)
        for p in probs:
            p["o_ref"][0, p["rows"], p["sl"]] = (jnp.dot(p["qd"], p["sb"], preferred_element_type=F32)
                                                 + jnp.dot(p["a"], p["vn"], preferred_element_type=F32))
        for p in probs:
            p["s_ref"][...] = p["s"] * p["dec"] + lax.dot_general(
                p["kd"], p["vn"], (((0,), (0,)), ((), ())), preferred_element_type=F32)


def _gdn_out_kernel(of_ref, ob_ref, z_ref, n_ref, o_ref):
    z = z_ref[...].astype(F32)
    for h in range(GDN_HEADS):
        sl = slice(h * GDN_DV, (h + 1) * GDN_DV)
        o = of_ref[:, sl] + ob_ref[:, sl]
        y = o * lax.rsqrt(jnp.mean(o * o, axis=-1, keepdims=True) + NORM_EPS) * n_ref[...]
        zh = z[:, sl]
        o_ref[:, sl] = (y * (zh * jax.nn.sigmoid(zh))).astype(BF16)


def gdn_mixer(proj3, misc3, conv_w, a_log, dt_bias, out_norm, tl=512):
    B, L, _ = proj3.shape
    T = B * L
    N = L // CH
    H = GDN_HEADS
    tl = min(tl, L)
    lane_vec = lambda p: jnp.zeros((1, LANES), F32).at[0, _G_LANE:_G_LANE + NDH].set(p.reshape(-1))
    tok = lambda w, c=0: pl.BlockSpec((1, tl, w), lambda b, i: (b, i, c))
    f32_tok = lambda w: jax.ShapeDtypeStruct((B, L, w), F32)
    qn, kn, vv, bg = pl.pallas_call(
        _gdn_prep_kernel,
        grid=(B, L // tl),
        in_specs=[*_conv3_specs(proj3, 3, 1536, tl), _full((3, 1536)), tok(LANES, 3),
                  _full((1, LANES)), _full((1, LANES))],
        out_specs=[tok(512), tok(512), tok(512), tok(LANES)],
        out_shape=[f32_tok(512), f32_tok(512), f32_tok(512), f32_tok(LANES)],
        compiler_params=_cparams("parallel", "parallel"),
        name="gdn_prep",
    )(proj3, proj3, proj3, conv_w, misc3, lane_vec(a_log), lane_vec(dt_bias))

    CB = GDN_STEP_CHUNKS
    NB = N // CB
    chunk = lambda w, c=0: pl.BlockSpec((1, CB * CH, w), lambda b, n: (b, n, c))
    mats = pl.BlockSpec((1, CB, NDH, CH, CH), lambda b, n: (b, n, 0, 0, 0))
    mat_shape = jax.ShapeDtypeStruct((B, N, NDH, CH, CH), F32)
    a, attn, gc = pl.pallas_call(
        _gdn_chunk_kernel,
        grid=(B, NB),
        in_specs=[chunk(512), chunk(512), chunk(LANES)],
        out_specs=[mats, mats, chunk(LANES)],
        out_shape=[mat_shape, jax.ShapeDtypeStruct(mat_shape.shape, BF16),
                   jax.ShapeDtypeStruct((B, L, LANES), F32)],
        compiler_params=_cparams("parallel", "parallel"),
        name="gdn_chunk",
    )(qn, kn, bg)

    P = NDH * B * N
    at = a.transpose(3, 4, 2, 0, 1).reshape(CH, CH, P)
    tt = pl.pallas_call(
        functools.partial(_gdn_solve_kernel, nblk_fwd=P // LANES // 2),
        grid=(P // LANES,),
        in_specs=[pl.BlockSpec((CH, CH, LANES), lambda p: (0, 0, p))],
        out_specs=pl.BlockSpec((CH, CH, LANES), lambda p: (0, 0, p)),
        out_shape=jax.ShapeDtypeStruct((CH, CH, P), F32),
        compiler_params=_cparams("parallel"),
        name="gdn_solve",
    )(at)
    tmat = tt.reshape(CH, CH, NDH, B, N).transpose(3, 4, 2, 0, 1).astype(BF16)

    fwd = lambda w, c=0: pl.BlockSpec((1, CB * CH, w), lambda b, n: (b, n, c))
    bwd = lambda w, c=0: pl.BlockSpec((1, CB * CH, w), lambda b, n: (b, NB - 1 - n, c))
    mf = pl.BlockSpec((1, CB, H, CH, CH), lambda b, n: (b, n, 0, 0, 0))
    mb = pl.BlockSpec((1, CB, H, CH, CH), lambda b, n: (b, NB - 1 - n, 1, 0, 0))
    o_f, o_b = pl.pallas_call(
        _gdn_scan_kernel,
        grid=(B, NB),
        in_specs=[fwd(512), fwd(512), fwd(512), fwd(LANES), fwd(LANES), mf, mf,
                  bwd(512), bwd(512), bwd(512), bwd(LANES), bwd(LANES), mb, mb],
        out_specs=[fwd(512), bwd(512)],
        out_shape=[jax.ShapeDtypeStruct((B, L, 512), F32), jax.ShapeDtypeStruct((B, L, 512), F32)],
        scratch_shapes=[pltpu.VMEM((GDN_DK, GDN_DV), F32)] * NDH,
        compiler_params=_cparams("parallel", "arbitrary"),
        name="gdn_scan",
    )(qn, kn, vv, bg, gc, tmat, attn, qn, kn, vv, bg, gc, tmat, attn)

    tm = min(1024, T)
    row = lambda c=0: pl.BlockSpec((tm, 512), lambda i: (i, c))
    return pl.pallas_call(
        _gdn_out_kernel,
        grid=(T // tm,),
        in_specs=[row(), row(), row(12), _full((1, GDN_DV))],
        out_specs=row(),
        out_shape=jax.ShapeDtypeStruct((T, 512), BF16),
        compiler_params=_cparams("parallel"),
        name="gdn_out",
    )(o_f.reshape(T, 512), o_b.reshape(T, 512), proj3.reshape(T, MAIN_W), out_norm[None, :])


MERGE_ROW_PARTS = 2


def _merge_kernel(x_ref, g_ref, oh_ref, om_ref, og_ref, wb_ref, wo_ref, n_ref, o_ref):
    part_rows = x_ref.shape[0] // MERGE_ROW_PARTS
    parts = [slice(r * part_rows, (r + 1) * part_rows) for r in range(MERGE_ROW_PARTS)]
    merged = []
    for rows in parts:
        acc = None
        for i, b_ref in enumerate((oh_ref, om_ref, og_ref)):
            gate = jax.nn.sigmoid(g_ref[rows, i * D_MODEL:(i + 1) * D_MODEL].astype(F32))
            term = gate * jnp.dot(b_ref[rows, :], wb_ref[i], preferred_element_type=F32)
            acc = term if acc is None else acc + term
        merged.append(acc.astype(BF16))
    ys = [jnp.dot(mrg, wo_ref[...], preferred_element_type=F32) for mrg in merged]
    for rows, y in zip(parts, ys):
        r = lax.rsqrt(jnp.mean(y * y, axis=-1, keepdims=True) + NORM_EPS)
        o_ref[rows, :] = x_ref[rows, :] + y * r * n_ref[...]


def merge_out(x, proj, o_hy, o_mla, o_gdn, w_branch, w_out, norm_post, tm=512):
    T = x.shape[0]
    tm = min(tm, T)
    row = lambda w: pl.BlockSpec((tm, w), lambda i: (i, 0))
    return pl.pallas_call(
        _merge_kernel,
        grid=(T // tm,),
        in_specs=[row(D_MODEL), row(N_BRANCH * D_MODEL), row(BRANCH_W), row(BRANCH_W), row(BRANCH_W),
                  _full((N_BRANCH, BRANCH_W, D_MODEL)), _full((D_MODEL, D_MODEL)), _full((1, D_MODEL))],
        out_specs=row(D_MODEL),
        out_shape=jax.ShapeDtypeStruct((T, D_MODEL), F32),
        compiler_params=_cparams("parallel"),
        name="merge_out",
    )(x, proj, o_hy, o_mla, o_gdn, w_branch, w_out, norm_post)


FFN_ROW_PARTS = 2


def _ffn_kernel(x_ref, gpre_ref, wg_ref, wu_ref, wd_ref, gpost_ref, o_ref, h_ref, acc_ref):
    j = pl.program_id(1)
    part_rows = h_ref.shape[0] // FFN_ROW_PARTS
    parts = [slice(r * part_rows, (r + 1) * part_rows) for r in range(FFN_ROW_PARTS)]

    def swiglu_partial(hs, first):
        gates = [jnp.dot(h, wg_ref[...], preferred_element_type=F32) for h in hs]
        ups = [jnp.dot(h, wu_ref[...], preferred_element_type=F32) for h in hs]
        acts = [(g * jax.nn.sigmoid(g) * u).astype(BF16) for g, u in zip(gates, ups)]
        for rows, a in zip(parts, acts):
            part = jnp.dot(a, wd_ref[...], preferred_element_type=F32)
            acc_ref[rows, :] = part if first else acc_ref[rows, :] + part

    @pl.when(j == 0)
    def _():
        hs = []
        for rows in parts:
            x = x_ref[rows, :]
            r = lax.rsqrt(jnp.mean(x * x, axis=-1, keepdims=True) + NORM_EPS)
            h = (x * r * gpre_ref[...]).astype(BF16)
            h_ref[rows, :] = h
            hs.append(h)
        swiglu_partial(hs, True)

    @pl.when(j > 0)
    def _():
        swiglu_partial([h_ref[rows, :] for rows in parts], False)

    @pl.when(j == pl.num_programs(1) - 1)
    def _():
        f = acc_ref[...]
        r = lax.rsqrt(jnp.mean(f * f, axis=-1, keepdims=True) + NORM_EPS)
        o_ref[...] = x_ref[...] + f * r * gpost_ref[...]


def ffn(x, g_pre, w_gate, w_up, w_down, g_post, tm=512):
    T = x.shape[0]
    dff = w_gate.shape[1]
    tf = dff // 2
    tm = min(tm, T)
    return pl.pallas_call(
        _ffn_kernel,
        grid=(T // tm, dff // tf),
        in_specs=[pl.BlockSpec((tm, D_MODEL), lambda i, j: (i, 0)), _full((1, D_MODEL)),
                  pl.BlockSpec((D_MODEL, tf), lambda i, j: (0, j)),
                  pl.BlockSpec((D_MODEL, tf), lambda i, j: (0, j)),
                  pl.BlockSpec((tf, D_MODEL), lambda i, j: (j, 0)), _full((1, D_MODEL))],
        out_specs=pl.BlockSpec((tm, D_MODEL), lambda i, j: (i, 0)),
        out_shape=jax.ShapeDtypeStruct((T, D_MODEL), F32),
        scratch_shapes=[pltpu.VMEM((tm, D_MODEL), BF16), pltpu.VMEM((tm, D_MODEL), F32)],
        compiler_params=_cparams("parallel", "arbitrary"),
        name="ffn",
    )(x, g_pre, w_gate, w_up, w_down, g_post)


def _split_w_in(w_in):
    cols = lambda off, n: w_in[:, off:off + n]
    main = jnp.concatenate([cols(_OFF_GATE, N_BRANCH * D_MODEL), cols(_OFF_HY, 1536),
                            cols(_OFF_GQKV, 1536), cols(_OFF_GZ, 512)], axis=-1).astype(BF16)
    half = MLA_ROPE // 2
    kpe = _OFF_MKV + MLA_KV_LORA
    zeros = lambda n: jnp.zeros((D_MODEL, n), w_in.dtype)
    misc = jnp.concatenate([cols(_OFF_MQ, MLA_Q_LORA), cols(_OFF_MKV, MLA_KV_LORA),
                            cols(kpe, MLA_ROPE), cols(_OFF_GB, 8), cols(_OFF_GA, 8), zeros(MISC_W - 464)],
                           axis=-1).astype(BF16)
    return jnp.concatenate([main, misc], axis=-1)


def trunk_layer(x, norm_mix_pre, norm_mix_post, norm_ffn_pre, norm_ffn_post, w_in,
                hy_conv_w, hy_conv_b, hy_ffn_w1, hy_ffn_b1, hy_sin_freq, hy_ffn_w2, hy_ffn_b2,
                hy_ffn_w3, hy_skip, mla_q_norm, mla_wq_b, mla_kv_norm, mla_wkv_b,
                gdn_conv_w, gdn_a_log, gdn_dt_bias, gdn_out_norm,
                w_branch, w_out, w_gate, w_up, w_down):
    B, L, D = x.shape
    T = B * L
    xt = x.reshape(T, D)
    proj, misc = norm_mm(xt, norm_mix_pre[None, :], _split_w_in(w_in), tm=2048, tn=512)
    proj3 = proj.reshape(B, L, MAIN_W)
    misc3 = misc.reshape(B, L, MISC_W)
    o_hy = hyena_mixer(proj3, hy_conv_w, hy_conv_b, hy_ffn_w1, hy_ffn_b1, hy_sin_freq,
                       hy_ffn_w2, hy_ffn_b2, hy_ffn_w3, hy_skip)
    o_mla = mla_mixer(misc3, mla_q_norm, mla_wq_b, mla_kv_norm, mla_wkv_b).reshape(T, BRANCH_W)
    o_gdn = gdn_mixer(proj3, misc3, gdn_conv_w, gdn_a_log, gdn_dt_bias, gdn_out_norm).reshape(T, BRANCH_W)
    xt = merge_out(xt, proj, o_hy, o_mla, o_gdn, w_branch.astype(BF16), w_out.astype(BF16),
                   norm_mix_post[None, :])
    xt = ffn(xt, norm_ffn_pre[None, :], w_gate.astype(BF16), w_up.astype(BF16), w_down.astype(BF16),
             norm_ffn_post[None, :])
    return xt.reshape(B, L, D)


def kernel(x_prompt, x_sample, norm_mix_pre, norm_mix_post, norm_ffn_pre, norm_ffn_post, w_in,
           hy_conv_w, hy_conv_b, hy_ffn_w1, hy_ffn_b1, hy_sin_freq, hy_ffn_w2, hy_ffn_b2,
           hy_ffn_w3, hy_skip, mla_q_norm, mla_wq_b, mla_kv_norm, mla_wkv_b,
           gdn_conv_w, gdn_a_log, gdn_dt_bias, gdn_out_norm,
           w_branch, w_out, w_gate, w_up, w_down):
    weights = (norm_mix_pre, norm_mix_post, norm_ffn_pre, norm_ffn_post, w_in,
               hy_conv_w, hy_conv_b, hy_ffn_w1, hy_ffn_b1, hy_sin_freq, hy_ffn_w2, hy_ffn_b2,
               hy_ffn_w3, hy_skip, mla_q_norm, mla_wq_b, mla_kv_norm, mla_wkv_b,
               gdn_conv_w, gdn_a_log, gdn_dt_bias, gdn_out_norm,
               w_branch, w_out, w_gate, w_up, w_down)

    def run_trunk(x):
        for layer in range(DEPTH):
            x = trunk_layer(x, *[w[layer] for w in weights])
        return x

    return (run_trunk(x_prompt), run_trunk(x_sample))
```
